```python
import math
import jax
import jax.numpy as jnp
from jax import lax
import numpy as np

D_MODEL = 1024
BATCH = 8
SEQ = 4096
DEPTH = 4

CHUNK = 64
Q_BLOCK = 128
EPS = 1e-6
MEM_LEN = 256

SSD_D_INNER = 1024
SSD_HEAD_DIM = 64
SSD_N_HEADS = SSD_D_INNER // SSD_HEAD_DIM
SSD_N_GROUPS = 4
SSD_HEADS_PER_GROUP = SSD_N_HEADS // SSD_N_GROUPS
SSD_D_STATE = 128
SSD_CONV = 4
SSD_XBC = SSD_D_INNER + 2 * SSD_N_GROUPS * SSD_D_STATE

CONV_D = 1024
CONV_K = 31

MLA_HEADS = 8
MLA_Q_RANK = 384
MLA_KV_RANK = 256
MLA_NOPE = 128
MLA_ROPE = 64
MLA_V = 128
MLA_QK = MLA_NOPE + MLA_ROPE
ROPE_THETA = 10000.0

N_BRANCH = 3
IN_SIZES = (SSD_D_INNER, SSD_XBC, SSD_N_HEADS, 2 * CONV_D, MLA_Q_RANK, MLA_KV_RANK + MLA_ROPE, N_BRANCH * D_MODEL)
IN_WIDTH = sum(IN_SIZES)

X_HEADS = 4
X_HEAD_DIM = D_MODEL // X_HEADS

FFN_HIDDEN = -(-(8 * D_MODEL) // (3 * 256)) * 256

kernel_name = 'hybrid_ssd_conformer_mla_block'


def split_cols(x, sizes):
    parts, start = [], 0
    for n in sizes:
        parts.append(x[..., start:start + n])
        start += n
    return parts


def rms_norm(x, g):
    xf = x.astype(jnp.float32)
    y = xf * lax.rsqrt(jnp.mean(xf * xf, axis=-1, keepdims=True) + EPS)
    return (y * g.astype(jnp.float32)).astype(x.dtype)


def layer_norm(x, g, b):
    xf = x.astype(jnp.float32)
    mu = jnp.mean(xf, axis=-1, keepdims=True)
    xc = xf - mu
    var = jnp.mean(xc * xc, axis=-1, keepdims=True)
    y = xc * lax.rsqrt(var + EPS) * g.astype(jnp.float32) + b.astype(jnp.float32)
    return y.astype(x.dtype)


def causal_depthwise_conv(x, w, b):
    k, c = w.shape
    y = lax.conv_general_dilated(x, w[:, None, :].astype(x.dtype), window_strides=(1,), padding=[(k - 1, 0)],
                                 dimension_numbers=('NWC', 'WIO', 'NWC'), feature_group_count=c)
    return y + b


def rope_tables(positions, dim):
    inv = ROPE_THETA ** (-jnp.arange(0, dim, 2, dtype=jnp.float32) / dim)
    ang = positions.astype(jnp.float32)[..., None] * inv
    return jnp.cos(ang), jnp.sin(ang)


def apply_rope(x, cos, sin):
    x1, x2 = jnp.split(x.astype(jnp.float32), 2, axis=-1)
    return jnp.concatenate([x1 * cos - x2 * sin, x2 * cos + x1 * sin], axis=-1).astype(x.dtype)


def segsum(a):
    l = a.shape[-1]
    cs = jnp.cumsum(a, axis=-1)
    seg = cs[..., :, None] - cs[..., None, :]
    mask = jnp.tril(jnp.ones((l, l), dtype=bool))
    return jnp.where(mask, seg, -jnp.inf)


def ssd_chunked_scan(xh, dt, A, Bg, Cg):
    b, s, G, R, P = xh.shape
    N = Bg.shape[-1]
    nc = s // CHUNK
    X = (xh * dt[..., None]).reshape(b, nc, CHUNK, G, R, P)
    a = (dt * A).reshape(b, nc, CHUNK, G, R).transpose(0, 3, 4, 1, 2)
    Bc = Bg.reshape(b, nc, CHUNK, G, N)
    Cc = Cg.reshape(b, nc, CHUNK, G, N)
    a_cs = jnp.cumsum(a, axis=-1)
    decay = jnp.exp(segsum(a))
    cb = jnp.einsum('bclgn,bcsgn->bgcls', Cc, Bc)
    y_diag = jnp.einsum('bgrcls,bcsgrp->bclgrp', cb[:, :, None] * decay, X)
    decay_states = jnp.exp(a_cs[..., -1:] - a_cs)
    states = jnp.einsum('bclgn,bgrcl,bclgrp->bcgrpn', Bc, decay_states, X)
    chunk_decay = jnp.exp(a_cs[..., -1])

    def step(h, inp):
        st, dec = inp
        return h * dec[..., None, None] + st, h

    h0 = jnp.zeros((b, G, R, P, N), dtype=X.dtype)
    _, prev = lax.scan(step, h0, (states.transpose(1, 0, 2, 3, 4, 5), chunk_decay.transpose(3, 0, 1, 2)))
    y_off = jnp.einsum('bclgn,cbgrpn,bgrcl->bclgrp', Cc, prev, jnp.exp(a_cs))
    return (y_diag + y_off).reshape(b, s, G, R, P)


def ssd_mixer(z, xbc, dt_raw, conv_w, conv_b, dt_bias, a_log, d_skip, norm_g, w_proj):
    b, s, _ = z.shape
    G, R, P, N = SSD_N_GROUPS, SSD_HEADS_PER_GROUP, SSD_HEAD_DIM, SSD_D_STATE
    xbc = jax.nn.silu(causal_depthwise_conv(xbc, conv_w, conv_b))
    xs, bm, cm = split_cols(xbc, (SSD_D_INNER, G * N, G * N))
    xh = xs.astype(jnp.float32).reshape(b, s, G, R, P)
    Bg = bm.astype(jnp.float32).reshape(b, s, G, N)
    Cg = cm.astype(jnp.float32).reshape(b, s, G, N)
    dt = jax.nn.softplus(dt_raw.astype(jnp.float32) + dt_bias.astype(jnp.float32)).reshape(b, s, G, R)
    A = -jnp.exp(a_log.astype(jnp.float32)).reshape(G, R)
    y = ssd_chunked_scan(xh, dt, A, Bg, Cg) + d_skip.astype(jnp.float32).reshape(G, R)[..., None] * xh
    y = y.reshape(b, s, SSD_D_INNER) * jax.nn.silu(z.astype(jnp.float32))
    yg = y.reshape(b, s, G, SSD_D_INNER // G)
    yg = yg * lax.rsqrt(jnp.mean(yg * yg, axis=-1, keepdims=True) + EPS)
    y = yg.reshape(b, s, SSD_D_INNER) * norm_g.astype(jnp.float32)
    return y.astype(z.dtype) @ w_proj


def conformer_conv_module(glu_in, dw_w, dw_b, ln_g, ln_b, w_pw):
    a, g = jnp.split(glu_in, 2, axis=-1)
    v = a * jax.nn.sigmoid(g)
    v = causal_depthwise_conv(v, dw_w, dw_b)
    v = jax.nn.silu(layer_norm(v, ln_g, ln_b))
    return v @ w_pw


def block_causal_attention(q, k, v):
    b, s, h, dq = q.shape
    dv = v.shape[-1]
    nb = s // Q_BLOCK
    scale = dq ** -0.5
    kt = k.transpose(0, 2, 1, 3)
    vt = v.transpose(0, 2, 1, 3)
    k_chunk = jnp.arange(s) // CHUNK
    qb = q.reshape(b, nb, Q_BLOCK, h, dq).transpose(1, 0, 3, 2, 4)

    def one_block(args):
        qblk, i = args
        q_chunk = (i * Q_BLOCK + jnp.arange(Q_BLOCK)) // CHUNK
        sc = jnp.einsum('bhqd,bhkd->bhqk', qblk, kt, preferred_element_type=jnp.float32) * scale
        sc = jnp.where(k_chunk[None, :] <= q_chunk[:, None], sc, -jnp.inf)
        p = jax.nn.softmax(sc, axis=-1)
        return jnp.einsum('bhqk,bhkd->bhqd', p.astype(v.dtype), vt)

    o = lax.map(one_block, (qb, jnp.arange(nb)))
    return o.transpose(1, 0, 3, 2, 4).reshape(b, s, h, dv)


def mla_mixer(q_lat, kv_lat, cos, sin, q_a_g, w_q_b, kv_a_g, w_kv_b, q_norm_g, k_norm_g, w_o):
    b, s, _ = q_lat.shape
    q = (rms_norm(q_lat, q_a_g) @ w_q_b).reshape(b, s, MLA_HEADS, MLA_QK)
    c_kv, k_rope = kv_lat[..., :MLA_KV_RANK], kv_lat[..., MLA_KV_RANK:]
    kv = (rms_norm(c_kv, kv_a_g) @ w_kv_b).reshape(b, s, MLA_HEADS, MLA_NOPE + MLA_V)
    k_nope, v = kv[..., :MLA_NOPE], kv[..., MLA_NOPE:]
    q_nope = rms_norm(q[..., :MLA_NOPE], q_norm_g[:MLA_NOPE])
    q_rope = apply_rope(rms_norm(q[..., MLA_NOPE:], q_norm_g[MLA_NOPE:]), cos[:, :, None], sin[:, :, None])
    k_nope = rms_norm(k_nope, k_norm_g[:MLA_NOPE])
    k_rope = apply_rope(rms_norm(k_rope, k_norm_g[MLA_NOPE:]), cos, sin)
    qf = jnp.concatenate([q_nope, q_rope], axis=-1)
    kf = jnp.concatenate([k_nope, jnp.broadcast_to(k_rope[:, :, None], (b, s, MLA_HEADS, MLA_ROPE))], axis=-1)
    o = block_causal_attention(qf, kf, v)
    return o.reshape(b, s, MLA_HEADS * MLA_V) @ w_o


def memory_cross_attention(h, mem_n, w_q, w_kv, q_norm_g, k_norm_g, w_o):
    b, s, _ = h.shape
    m = mem_n.shape[1]
    q = rms_norm((h @ w_q).reshape(b, s, X_HEADS, X_HEAD_DIM), q_norm_g)
    kv = (mem_n @ w_kv).reshape(b, m, 2, X_HEADS, X_HEAD_DIM)
    k = rms_norm(kv[:, :, 0], k_norm_g)
    v = kv[:, :, 1]
    sc = jnp.einsum('bqhd,bkhd->bhqk', q, k, preferred_element_type=jnp.float32) * (X_HEAD_DIM ** -0.5)
    p = jax.nn.softmax(sc, axis=-1)
    o = jnp.einsum('bhqk,bkhd->bqhd', p.astype(v.dtype), v)
    return o.reshape(b, s, D_MODEL) @ w_o


def swiglu_ffn(h, w_in, w_out):
    gate, up = jnp.split(h @ w_in, 2, axis=-1)
    return (jax.nn.silu(gate) * up) @ w_out


def _fwd_setup_inputs(seed: int = 0) -> dict:
    key = jax.random.key(seed)
    ks = iter(jax.random.split(key, 64))
    f32 = jnp.float32
    L = DEPTH

    def normal(shape, std):
        return jax.random.normal(next(ks), shape, f32) * std

    def dense(shape, fan_in, scale=1.0):
        return normal(shape, scale * fan_in ** -0.5)

    def gain(shape):
        return 1.0 + normal(shape, 0.02)

    def small(shape):
        return normal(shape, 0.02)

    out_scale = 0.5
    x = normal((BATCH, SEQ, D_MODEL), 1.0)
    mem = normal((BATCH, MEM_LEN, D_MODEL), 1.0)
    start = jax.random.randint(next(ks), (BATCH, 1), 0, 100000, dtype=jnp.int32)
    positions = start + jnp.arange(SEQ, dtype=jnp.int32)[None, :]
    dt0 = jnp.exp(jax.random.uniform(next(ks), (L, SSD_N_HEADS), f32, math.log(1e-3), math.log(1e-1)))
    ssd_dt_bias = dt0 + jnp.log(-jnp.expm1(-dt0))
    ssd_a_log = jnp.log(jax.random.uniform(next(ks), (L, SSD_N_HEADS), f32, 1.0, 16.0))
    return {
        'x': x,
        'mem': mem,
        'positions': positions,
        'mix_norm_g': gain((L, D_MODEL)),
        'w_in': dense((L, D_MODEL, IN_WIDTH), D_MODEL),
        'ssd_conv_w': dense((L, SSD_CONV, SSD_XBC), SSD_CONV),
        'ssd_conv_b': small((L, SSD_XBC)),
        'ssd_dt_bias': ssd_dt_bias,
        'ssd_a_log': ssd_a_log,
        'ssd_d': gain((L, SSD_N_HEADS)),
        'ssd_norm_g': gain((L, SSD_D_INNER)),
        'ssd_w_out': dense((L, SSD_D_INNER, D_MODEL), SSD_D_INNER),
        'conv_dw_w': dense((L, CONV_K, CONV_D), CONV_K),
        'conv_dw_b': small((L, CONV_D)),
        'conv_ln_g': gain((L, CONV_D)),
        'conv_ln_b': small((L, CONV_D)),
        'conv_w_out': dense((L, CONV_D, D_MODEL), CONV_D),
        'mla_q_a_g': gain((L, MLA_Q_RANK)),
        'mla_w_q_b': dense((L, MLA_Q_RANK, MLA_HEADS * MLA_QK), MLA_Q_RANK),
        'mla_kv_a_g': gain((L, MLA_KV_RANK)),
        'mla_w_kv_b': dense((L, MLA_KV_RANK, MLA_HEADS * (MLA_NOPE + MLA_V)), MLA_KV_RANK),
        'mla_q_norm_g': gain((L, MLA_QK)),
        'mla_k_norm_g': gain((L, MLA_QK)),
        'mla_w_o': dense((L, MLA_HEADS * MLA_V, D_MODEL), MLA_HEADS * MLA_V),
        'gate_b': small((L, N_BRANCH, D_MODEL)),
        'w_out': dense((L, D_MODEL, D_MODEL), D_MODEL, out_scale),
        'xattn_norm_g': gain((L, D_MODEL)),
        'mem_norm_g': gain((L, D_MODEL)),
        'xattn_w_q': dense((L, D_MODEL, D_MODEL), D_MODEL),
        'xattn_w_kv': dense((L, D_MODEL, 2 * D_MODEL), D_MODEL),
        'xattn_q_norm_g': gain((L, X_HEAD_DIM)),
        'xattn_k_norm_g': gain((L, X_HEAD_DIM)),
        'xattn_w_o': dense((L, D_MODEL, D_MODEL), D_MODEL, out_scale),
        'ffn_norm_g': gain((L, D_MODEL)),
        'ffn_w_in': dense((L, D_MODEL, 2 * FFN_HIDDEN), D_MODEL),
        'ffn_w_out': dense((L, FFN_HIDDEN, D_MODEL), FFN_HIDDEN, out_scale),
    }


def _fwd_reference(x, mem, positions, mix_norm_g, w_in, ssd_conv_w, ssd_conv_b, ssd_dt_bias, ssd_a_log, ssd_d,
              ssd_norm_g, ssd_w_out, conv_dw_w, conv_dw_b, conv_ln_g, conv_ln_b, conv_w_out, mla_q_a_g,
              mla_w_q_b, mla_kv_a_g, mla_w_kv_b, mla_q_norm_g, mla_k_norm_g, mla_w_o, gate_b, w_out,
              xattn_norm_g, mem_norm_g, xattn_w_q, xattn_w_kv, xattn_q_norm_g, xattn_k_norm_g, xattn_w_o,
              ffn_norm_g, ffn_w_in, ffn_w_out):
    b, s, _ = x.shape
    cos, sin = rope_tables(positions, MLA_ROPE)
    for l in range(DEPTH):
        u = rms_norm(x, mix_norm_g[l])
        z, xbc, dt_raw, glu_in, q_lat, kv_lat, gate_logits = split_cols(u @ w_in[l], IN_SIZES)
        y_ssd = ssd_mixer(z, xbc, dt_raw, ssd_conv_w[l], ssd_conv_b[l], ssd_dt_bias[l], ssd_a_log[l],
                          ssd_d[l], ssd_norm_g[l], ssd_w_out[l])
        y_conv = conformer_conv_module(glu_in, conv_dw_w[l], conv_dw_b[l], conv_ln_g[l], conv_ln_b[l],
                                       conv_w_out[l])
        y_mla = mla_mixer(q_lat, kv_lat, cos, sin, mla_q_a_g[l], mla_w_q_b[l], mla_kv_a_g[l], mla_w_kv_b[l],
                          mla_q_norm_g[l], mla_k_norm_g[l], mla_w_o[l])
        gates = jax.nn.sigmoid((gate_logits + gate_b[l].reshape(-1)).astype(jnp.float32))
        gates = gates.astype(x.dtype).reshape(b, s, N_BRANCH, D_MODEL)
        merged = gates[:, :, 0] * y_ssd + gates[:, :, 1] * y_conv + gates[:, :, 2] * y_mla
        x = x + merged @ w_out[l]
        x = x + memory_cross_attention(rms_norm(x, xattn_norm_g[l]), rms_norm(mem, mem_norm_g[l]),
                                       xattn_w_q[l], xattn_w_kv[l], xattn_q_norm_g[l], xattn_k_norm_g[l],
                                       xattn_w_o[l])
        x = x + swiglu_ffn(rms_norm(x, ffn_norm_g[l]), ffn_w_in[l], ffn_w_out[l])
    return x


import jax as _jax
import jax.numpy as _jnp

TWIN_FORMAT = 'train_step'
FWD_PARAMS = ['x', 'mem', 'positions', 'mix_norm_g', 'w_in', 'ssd_conv_w', 'ssd_conv_b', 'ssd_dt_bias', 'ssd_a_log', 'ssd_d', 'ssd_norm_g', 'ssd_w_out', 'conv_dw_w', 'conv_dw_b', 'conv_ln_g', 'conv_ln_b', 'conv_w_out', 'mla_q_a_g', 'mla_w_q_b', 'mla_kv_a_g', 'mla_w_kv_b', 'mla_q_norm_g', 'mla_k_norm_g', 'mla_w_o', 'gate_b', 'w_out', 'xattn_norm_g', 'mem_norm_g', 'xattn_w_q', 'xattn_w_kv', 'xattn_q_norm_g', 'xattn_k_norm_g', 'xattn_w_o', 'ffn_norm_g', 'ffn_w_in', 'ffn_w_out']
TWIN_WEIGHTS = ['mix_norm_g', 'w_in', 'ssd_conv_w', 'ssd_conv_b', 'ssd_dt_bias', 'ssd_a_log', 'ssd_d', 'ssd_norm_g', 'ssd_w_out', 'conv_dw_w', 'conv_dw_b', 'conv_ln_g', 'conv_ln_b', 'conv_w_out', 'mla_q_a_g', 'mla_w_q_b', 'mla_kv_a_g', 'mla_w_kv_b', 'mla_q_norm_g', 'mla_k_norm_g', 'mla_w_o', 'gate_b', 'w_out', 'xattn_norm_g', 'mem_norm_g', 'xattn_w_q', 'xattn_w_kv', 'xattn_q_norm_g', 'xattn_k_norm_g', 'xattn_w_o', 'ffn_norm_g', 'ffn_w_in', 'ffn_w_out']
TWIN_DIFF_INPUT = 'x'
TWIN_INPUTS = ['x', 'mem', 'positions', 'mix_norm_g', 'w_in', 'ssd_conv_w', 'ssd_conv_b', 'ssd_dt_bias', 'ssd_a_log', 'ssd_d', 'ssd_norm_g', 'ssd_w_out', 'conv_dw_w', 'conv_dw_b', 'conv_ln_g', 'conv_ln_b', 'conv_w_out', 'mla_q_a_g', 'mla_w_q_b', 'mla_kv_a_g', 'mla_w_kv_b', 'mla_q_norm_g', 'mla_k_norm_g', 'mla_w_o', 'gate_b', 'w_out', 'xattn_norm_g', 'mem_norm_g', 'xattn_w_q', 'xattn_w_kv', 'xattn_q_norm_g', 'xattn_k_norm_g', 'xattn_w_o', 'ffn_norm_g', 'ffn_w_in', 'ffn_w_out', 'loss_target', 'm_mix_norm_g', 'm_w_in', 'm_ssd_conv_w', 'm_ssd_conv_b', 'm_ssd_dt_bias', 'm_ssd_a_log', 'm_ssd_d', 'm_ssd_norm_g', 'm_ssd_w_out', 'm_conv_dw_w', 'm_conv_dw_b', 'm_conv_ln_g', 'm_conv_ln_b', 'm_conv_w_out', 'm_mla_q_a_g', 'm_mla_w_q_b', 'm_mla_kv_a_g', 'm_mla_w_kv_b', 'm_mla_q_norm_g', 'm_mla_k_norm_g', 'm_mla_w_o', 'm_gate_b', 'm_w_out', 'm_xattn_norm_g', 'm_mem_norm_g', 'm_xattn_w_q', 'm_xattn_w_kv', 'm_xattn_q_norm_g', 'm_xattn_k_norm_g', 'm_xattn_w_o', 'm_ffn_norm_g', 'm_ffn_w_in', 'm_ffn_w_out', 'v_mix_norm_g', 'v_w_in', 'v_ssd_conv_w', 'v_ssd_conv_b', 'v_ssd_dt_bias', 'v_ssd_a_log', 'v_ssd_d', 'v_ssd_norm_g', 'v_ssd_w_out', 'v_conv_dw_w', 'v_conv_dw_b', 'v_conv_ln_g', 'v_conv_ln_b', 'v_conv_w_out', 'v_mla_q_a_g', 'v_mla_w_q_b', 'v_mla_kv_a_g', 'v_mla_w_kv_b', 'v_mla_q_norm_g', 'v_mla_k_norm_g', 'v_mla_w_o', 'v_gate_b', 'v_w_out', 'v_xattn_norm_g', 'v_mem_norm_g', 'v_xattn_w_q', 'v_xattn_w_kv', 'v_xattn_q_norm_g', 'v_xattn_k_norm_g', 'v_xattn_w_o', 'v_ffn_norm_g', 'v_ffn_w_in', 'v_ffn_w_out']
TWIN_OUTPUTS = ['loss', 'grad_x', 'grad_mix_norm_g', 'grad_w_in', 'grad_ssd_conv_w', 'grad_ssd_conv_b', 'grad_ssd_dt_bias', 'grad_ssd_a_log', 'grad_ssd_d', 'grad_ssd_norm_g', 'grad_ssd_w_out', 'grad_conv_dw_w', 'grad_conv_dw_b', 'grad_conv_ln_g', 'grad_conv_ln_b', 'grad_conv_w_out', 'grad_mla_q_a_g', 'grad_mla_w_q_b', 'grad_mla_kv_a_g', 'grad_mla_w_kv_b', 'grad_mla_q_norm_g', 'grad_mla_k_norm_g', 'grad_mla_w_o', 'grad_gate_b', 'grad_w_out', 'grad_xattn_norm_g', 'grad_mem_norm_g', 'grad_xattn_w_q', 'grad_xattn_w_kv', 'grad_xattn_q_norm_g', 'grad_xattn_k_norm_g', 'grad_xattn_w_o', 'grad_ffn_norm_g', 'grad_ffn_w_in', 'grad_ffn_w_out', 'delta_mix_norm_g', 'delta_w_in', 'delta_ssd_conv_w', 'delta_ssd_conv_b', 'delta_ssd_dt_bias', 'delta_ssd_a_log', 'delta_ssd_d', 'delta_ssd_norm_g', 'delta_ssd_w_out', 'delta_conv_dw_w', 'delta_conv_dw_b', 'delta_conv_ln_g', 'delta_conv_ln_b', 'delta_conv_w_out', 'delta_mla_q_a_g', 'delta_mla_w_q_b', 'delta_mla_kv_a_g', 'delta_mla_w_kv_b', 'delta_mla_q_norm_g', 'delta_mla_k_norm_g', 'delta_mla_w_o', 'delta_gate_b', 'delta_w_out', 'delta_xattn_norm_g', 'delta_mem_norm_g', 'delta_xattn_w_q', 'delta_xattn_w_kv', 'delta_xattn_q_norm_g', 'delta_xattn_k_norm_g', 'delta_xattn_w_o', 'delta_ffn_norm_g', 'delta_ffn_w_in', 'delta_ffn_w_out', 'new_m_mix_norm_g', 'new_m_w_in', 'new_m_ssd_conv_w', 'new_m_ssd_conv_b', 'new_m_ssd_dt_bias', 'new_m_ssd_a_log', 'new_m_ssd_d', 'new_m_ssd_norm_g', 'new_m_ssd_w_out', 'new_m_conv_dw_w', 'new_m_conv_dw_b', 'new_m_conv_ln_g', 'new_m_conv_ln_b', 'new_m_conv_w_out', 'new_m_mla_q_a_g', 'new_m_mla_w_q_b', 'new_m_mla_kv_a_g', 'new_m_mla_w_kv_b', 'new_m_mla_q_norm_g', 'new_m_mla_k_norm_g', 'new_m_mla_w_o', 'new_m_gate_b', 'new_m_w_out', 'new_m_xattn_norm_g', 'new_m_mem_norm_g', 'new_m_xattn_w_q', 'new_m_xattn_w_kv', 'new_m_xattn_q_norm_g', 'new_m_xattn_k_norm_g', 'new_m_xattn_w_o', 'new_m_ffn_norm_g', 'new_m_ffn_w_in', 'new_m_ffn_w_out', 'new_v_mix_norm_g', 'new_v_w_in', 'new_v_ssd_conv_w', 'new_v_ssd_conv_b', 'new_v_ssd_dt_bias', 'new_v_ssd_a_log', 'new_v_ssd_d', 'new_v_ssd_norm_g', 'new_v_ssd_w_out', 'new_v_conv_dw_w', 'new_v_conv_dw_b', 'new_v_conv_ln_g', 'new_v_conv_ln_b', 'new_v_conv_w_out', 'new_v_mla_q_a_g', 'new_v_mla_w_q_b', 'new_v_mla_kv_a_g', 'new_v_mla_w_kv_b', 'new_v_mla_q_norm_g', 'new_v_mla_k_norm_g', 'new_v_mla_w_o', 'new_v_gate_b', 'new_v_w_out', 'new_v_xattn_norm_g', 'new_v_mem_norm_g', 'new_v_xattn_w_q', 'new_v_xattn_w_kv', 'new_v_xattn_q_norm_g', 'new_v_xattn_k_norm_g', 'new_v_xattn_w_o', 'new_v_ffn_norm_g', 'new_v_ffn_w_in', 'new_v_ffn_w_out']
TWIN_LEAF_KINDS = {'loss': 'loss', 'grad_x': 'grad_x', 'grad_mix_norm_g': 'grad_w', 'grad_w_in': 'grad_w', 'grad_ssd_conv_w': 'grad_w', 'grad_ssd_conv_b': 'grad_w', 'grad_ssd_dt_bias': 'grad_w', 'grad_ssd_a_log': 'grad_w', 'grad_ssd_d': 'grad_w', 'grad_ssd_norm_g': 'grad_w', 'grad_ssd_w_out': 'grad_w', 'grad_conv_dw_w': 'grad_w', 'grad_conv_dw_b': 'grad_w', 'grad_conv_ln_g': 'grad_w', 'grad_conv_ln_b': 'grad_w', 'grad_conv_w_out': 'grad_w', 'grad_mla_q_a_g': 'grad_w', 'grad_mla_w_q_b': 'grad_w', 'grad_mla_kv_a_g': 'grad_w', 'grad_mla_w_kv_b': 'grad_w', 'grad_mla_q_norm_g': 'grad_w', 'grad_mla_k_norm_g': 'grad_w', 'grad_mla_w_o': 'grad_w', 'grad_gate_b': 'grad_w', 'grad_w_out': 'grad_w', 'grad_xattn_norm_g': 'grad_w', 'grad_mem_norm_g': 'grad_w', 'grad_xattn_w_q': 'grad_w', 'grad_xattn_w_kv': 'grad_w', 'grad_xattn_q_norm_g': 'grad_w', 'grad_xattn_k_norm_g': 'grad_w', 'grad_xattn_w_o': 'grad_w', 'grad_ffn_norm_g': 'grad_w', 'grad_ffn_w_in': 'grad_w', 'grad_ffn_w_out': 'grad_w', 'delta_mix_norm_g': 'delta_w', 'delta_w_in': 'delta_w', 'delta_ssd_conv_w': 'delta_w', 'delta_ssd_conv_b': 'delta_w', 'delta_ssd_dt_bias': 'delta_w', 'delta_ssd_a_log': 'delta_w', 'delta_ssd_d': 'delta_w', 'delta_ssd_norm_g': 'delta_w', 'delta_ssd_w_out': 'delta_w', 'delta_conv_dw_w': 'delta_w', 'delta_conv_dw_b': 'delta_w', 'delta_conv_ln_g': 'delta_w', 'delta_conv_ln_b': 'delta_w', 'delta_conv_w_out': 'delta_w', 'delta_mla_q_a_g': 'delta_w', 'delta_mla_w_q_b': 'delta_w', 'delta_mla_kv_a_g': 'delta_w', 'delta_mla_w_kv_b': 'delta_w', 'delta_mla_q_norm_g': 'delta_w', 'delta_mla_k_norm_g': 'delta_w', 'delta_mla_w_o': 'delta_w', 'delta_gate_b': 'delta_w', 'delta_w_out': 'delta_w', 'delta_xattn_norm_g': 'delta_w', 'delta_mem_norm_g': 'delta_w', 'delta_xattn_w_q': 'delta_w', 'delta_xattn_w_kv': 'delta_w', 'delta_xattn_q_norm_g': 'delta_w', 'delta_xattn_k_norm_g': 'delta_w', 'delta_xattn_w_o': 'delta_w', 'delta_ffn_norm_g': 'delta_w', 'delta_ffn_w_in': 'delta_w', 'delta_ffn_w_out': 'delta_w', 'new_m_mix_norm_g': 'new_m', 'new_m_w_in': 'new_m', 'new_m_ssd_conv_w': 'new_m', 'new_m_ssd_conv_b': 'new_m', 'new_m_ssd_dt_bias': 'new_m', 'new_m_ssd_a_log': 'new_m', 'new_m_ssd_d': 'new_m', 'new_m_ssd_norm_g': 'new_m', 'new_m_ssd_w_out': 'new_m', 'new_m_conv_dw_w': 'new_m', 'new_m_conv_dw_b': 'new_m', 'new_m_conv_ln_g': 'new_m', 'new_m_conv_ln_b': 'new_m', 'new_m_conv_w_out': 'new_m', 'new_m_mla_q_a_g': 'new_m', 'new_m_mla_w_q_b': 'new_m', 'new_m_mla_kv_a_g': 'new_m', 'new_m_mla_w_kv_b': 'new_m', 'new_m_mla_q_norm_g': 'new_m', 'new_m_mla_k_norm_g': 'new_m', 'new_m_mla_w_o': 'new_m', 'new_m_gate_b': 'new_m', 'new_m_w_out': 'new_m', 'new_m_xattn_norm_g': 'new_m', 'new_m_mem_norm_g': 'new_m', 'new_m_xattn_w_q': 'new_m', 'new_m_xattn_w_kv': 'new_m', 'new_m_xattn_q_norm_g': 'new_m', 'new_m_xattn_k_norm_g': 'new_m', 'new_m_xattn_w_o': 'new_m', 'new_m_ffn_norm_g': 'new_m', 'new_m_ffn_w_in': 'new_m', 'new_m_ffn_w_out': 'new_m', 'new_v_mix_norm_g': 'new_v', 'new_v_w_in': 'new_v', 'new_v_ssd_conv_w': 'new_v', 'new_v_ssd_conv_b': 'new_v', 'new_v_ssd_dt_bias': 'new_v', 'new_v_ssd_a_log': 'new_v', 'new_v_ssd_d': 'new_v', 'new_v_ssd_norm_g': 'new_v', 'new_v_ssd_w_out': 'new_v', 'new_v_conv_dw_w': 'new_v', 'new_v_conv_dw_b': 'new_v', 'new_v_conv_ln_g': 'new_v', 'new_v_conv_ln_b': 'new_v', 'new_v_conv_w_out': 'new_v', 'new_v_mla_q_a_g': 'new_v', 'new_v_mla_w_q_b': 'new_v', 'new_v_mla_kv_a_g': 'new_v', 'new_v_mla_w_kv_b': 'new_v', 'new_v_mla_q_norm_g': 'new_v', 'new_v_mla_k_norm_g': 'new_v', 'new_v_mla_w_o': 'new_v', 'new_v_gate_b': 'new_v', 'new_v_w_out': 'new_v', 'new_v_xattn_norm_g': 'new_v', 'new_v_mem_norm_g': 'new_v', 'new_v_xattn_w_q': 'new_v', 'new_v_xattn_w_kv': 'new_v', 'new_v_xattn_q_norm_g': 'new_v', 'new_v_xattn_k_norm_g': 'new_v', 'new_v_xattn_w_o': 'new_v', 'new_v_ffn_norm_g': 'new_v', 'new_v_ffn_w_in': 'new_v', 'new_v_ffn_w_out': 'new_v'}


def _forward(args):
    return _fwd_reference(*[args[k] for k in FWD_PARAMS])


def _output_shape():
    out = _jax.eval_shape(lambda: _forward(_fwd_setup_inputs(0)))
    return out.shape, out.dtype

N_MICROBATCH = 1
ADAM_LR = 0.001
ADAM_B1 = 0.9
ADAM_B2 = 0.999
ADAM_EPS = 1e-08
ADAM_WD = 0.01
ADAM_STEP = 10
PER_EXAMPLE_BATCH_AXIS = {'x': 0, 'mem': 0, 'positions': 0, 'loss_target': 0}
SHARED_INPUTS = []
_WEIGHT_DTYPES = {'mix_norm_g': _jnp.float32, 'w_in': _jnp.float32, 'ssd_conv_w': _jnp.float32, 'ssd_conv_b': _jnp.float32, 'ssd_dt_bias': _jnp.float32, 'ssd_a_log': _jnp.float32, 'ssd_d': _jnp.float32, 'ssd_norm_g': _jnp.float32, 'ssd_w_out': _jnp.float32, 'conv_dw_w': _jnp.float32, 'conv_dw_b': _jnp.float32, 'conv_ln_g': _jnp.float32, 'conv_ln_b': _jnp.float32, 'conv_w_out': _jnp.float32, 'mla_q_a_g': _jnp.float32, 'mla_w_q_b': _jnp.float32, 'mla_kv_a_g': _jnp.float32, 'mla_w_kv_b': _jnp.float32, 'mla_q_norm_g': _jnp.float32, 'mla_k_norm_g': _jnp.float32, 'mla_w_o': _jnp.float32, 'gate_b': _jnp.float32, 'w_out': _jnp.float32, 'xattn_norm_g': _jnp.float32, 'mem_norm_g': _jnp.float32, 'xattn_w_q': _jnp.float32, 'xattn_w_kv': _jnp.float32, 'xattn_q_norm_g': _jnp.float32, 'xattn_k_norm_g': _jnp.float32, 'xattn_w_o': _jnp.float32, 'ffn_norm_g': _jnp.float32, 'ffn_w_in': _jnp.float32, 'ffn_w_out': _jnp.float32}
MOMENT_SCALE = {'mix_norm_g': 3.214351e-01, 'w_in': 7.675080e-02, 'ssd_conv_w': 1.404021e-01, 'ssd_conv_b': 4.627545e-01, 'ssd_dt_bias': 2.670302e-01, 'ssd_a_log': 1.005464e+00, 'ssd_d': 1.285920e+00, 'ssd_norm_g': 3.176414e+00, 'ssd_w_out': 2.747400e-01, 'conv_dw_w': 9.827498e-02, 'conv_dw_b': 1.361454e+00, 'conv_ln_g': 1.168003e+00, 'conv_ln_b': 1.042264e+00, 'conv_w_out': 2.773940e-01, 'mla_q_a_g': 2.686392e-02, 'mla_w_q_b': 1.348197e-02, 'mla_kv_a_g': 2.074032e-01, 'mla_w_kv_b': 6.124684e-02, 'mla_q_norm_g': 5.814932e-02, 'mla_k_norm_g': 5.828812e-02, 'mla_w_o': 9.040143e-02, 'gate_b': 5.011596e-01, 'w_out': 7.112320e-01, 'xattn_norm_g': 2.694323e-02, 'mem_norm_g': 1.034238e-01, 'xattn_w_q': 2.696378e-02, 'xattn_w_kv': 5.297537e-02, 'xattn_q_norm_g': 2.839841e-01, 'xattn_k_norm_g': 2.833526e-01, 'xattn_w_o': 1.462312e-01, 'ffn_norm_g': 6.081327e+00, 'ffn_w_in': 8.877163e-02, 'ffn_w_out': 2.874490e-01}


def _to_microbatches(a, axis):
    t = _jnp.moveaxis(a, axis, 0)
    t = t.reshape((N_MICROBATCH, t.shape[0] // N_MICROBATCH) + t.shape[1:])
    return _jnp.moveaxis(t, 1, axis + 1)


def setup_inputs(seed: int = 0) -> dict:
    inp = _fwd_setup_inputs(seed)
    key = _jax.random.fold_in(_jax.random.key(seed), 7919)
    shape, _ = _output_shape()
    out = dict(inp)
    out["loss_target"] = _jax.random.normal(_jax.random.fold_in(key, 0), shape, _jnp.float32)
    for i, name in enumerate(TWIN_WEIGHTS):
        w = inp[name].astype(_jnp.float32)
        if MOMENT_SCALE is None:
            s = _jnp.sqrt(_jnp.mean(_jnp.square(w)) + 1e-30)
        else:
            s = MOMENT_SCALE[name]
        km, kv = _jax.random.split(_jax.random.fold_in(key, i + 1))
        out[name] = w
        out["m_" + name] = s * _jax.random.normal(km, w.shape, _jnp.float32)
        out["v_" + name] = (s * s) * _jax.random.uniform(kv, w.shape, _jnp.float32, 0.5, 1.5)
    if N_MICROBATCH > 1:
        for name, axis in PER_EXAMPLE_BATCH_AXIS.items():
            out[name] = _to_microbatches(out[name], axis)
    return {'x': out['x'], 'mem': out['mem'], 'positions': out['positions'], 'mix_norm_g': out['mix_norm_g'], 'w_in': out['w_in'], 'ssd_conv_w': out['ssd_conv_w'], 'ssd_conv_b': out['ssd_conv_b'], 'ssd_dt_bias': out['ssd_dt_bias'], 'ssd_a_log': out['ssd_a_log'], 'ssd_d': out['ssd_d'], 'ssd_norm_g': out['ssd_norm_g'], 'ssd_w_out': out['ssd_w_out'], 'conv_dw_w': out['conv_dw_w'], 'conv_dw_b': out['conv_dw_b'], 'conv_ln_g': out['conv_ln_g'], 'conv_ln_b': out['conv_ln_b'], 'conv_w_out': out['conv_w_out'], 'mla_q_a_g': out['mla_q_a_g'], 'mla_w_q_b': out['mla_w_q_b'], 'mla_kv_a_g': out['mla_kv_a_g'], 'mla_w_kv_b': out['mla_w_kv_b'], 'mla_q_norm_g': out['mla_q_norm_g'], 'mla_k_norm_g': out['mla_k_norm_g'], 'mla_w_o': out['mla_w_o'], 'gate_b': out['gate_b'], 'w_out': out['w_out'], 'xattn_norm_g': out['xattn_norm_g'], 'mem_norm_g': out['mem_norm_g'], 'xattn_w_q': out['xattn_w_q'], 'xattn_w_kv': out['xattn_w_kv'], 'xattn_q_norm_g': out['xattn_q_norm_g'], 'xattn_k_norm_g': out['xattn_k_norm_g'], 'xattn_w_o': out['xattn_w_o'], 'ffn_norm_g': out['ffn_norm_g'], 'ffn_w_in': out['ffn_w_in'], 'ffn_w_out': out['ffn_w_out'], 'loss_target': out['loss_target'], 'm_mix_norm_g': out['m_mix_norm_g'], 'm_w_in': out['m_w_in'], 'm_ssd_conv_w': out['m_ssd_conv_w'], 'm_ssd_conv_b': out['m_ssd_conv_b'], 'm_ssd_dt_bias': out['m_ssd_dt_bias'], 'm_ssd_a_log': out['m_ssd_a_log'], 'm_ssd_d': out['m_ssd_d'], 'm_ssd_norm_g': out['m_ssd_norm_g'], 'm_ssd_w_out': out['m_ssd_w_out'], 'm_conv_dw_w': out['m_conv_dw_w'], 'm_conv_dw_b': out['m_conv_dw_b'], 'm_conv_ln_g': out['m_conv_ln_g'], 'm_conv_ln_b': out['m_conv_ln_b'], 'm_conv_w_out': out['m_conv_w_out'], 'm_mla_q_a_g': out['m_mla_q_a_g'], 'm_mla_w_q_b': out['m_mla_w_q_b'], 'm_mla_kv_a_g': out['m_mla_kv_a_g'], 'm_mla_w_kv_b': out['m_mla_w_kv_b'], 'm_mla_q_norm_g': out['m_mla_q_norm_g'], 'm_mla_k_norm_g': out['m_mla_k_norm_g'], 'm_mla_w_o': out['m_mla_w_o'], 'm_gate_b': out['m_gate_b'], 'm_w_out': out['m_w_out'], 'm_xattn_norm_g': out['m_xattn_norm_g'], 'm_mem_norm_g': out['m_mem_norm_g'], 'm_xattn_w_q': out['m_xattn_w_q'], 'm_xattn_w_kv': out['m_xattn_w_kv'], 'm_xattn_q_norm_g': out['m_xattn_q_norm_g'], 'm_xattn_k_norm_g': out['m_xattn_k_norm_g'], 'm_xattn_w_o': out['m_xattn_w_o'], 'm_ffn_norm_g': out['m_ffn_norm_g'], 'm_ffn_w_in': out['m_ffn_w_in'], 'm_ffn_w_out': out['m_ffn_w_out'], 'v_mix_norm_g': out['v_mix_norm_g'], 'v_w_in': out['v_w_in'], 'v_ssd_conv_w': out['v_ssd_conv_w'], 'v_ssd_conv_b': out['v_ssd_conv_b'], 'v_ssd_dt_bias': out['v_ssd_dt_bias'], 'v_ssd_a_log': out['v_ssd_a_log'], 'v_ssd_d': out['v_ssd_d'], 'v_ssd_norm_g': out['v_ssd_norm_g'], 'v_ssd_w_out': out['v_ssd_w_out'], 'v_conv_dw_w': out['v_conv_dw_w'], 'v_conv_dw_b': out['v_conv_dw_b'], 'v_conv_ln_g': out['v_conv_ln_g'], 'v_conv_ln_b': out['v_conv_ln_b'], 'v_conv_w_out': out['v_conv_w_out'], 'v_mla_q_a_g': out['v_mla_q_a_g'], 'v_mla_w_q_b': out['v_mla_w_q_b'], 'v_mla_kv_a_g': out['v_mla_kv_a_g'], 'v_mla_w_kv_b': out['v_mla_w_kv_b'], 'v_mla_q_norm_g': out['v_mla_q_norm_g'], 'v_mla_k_norm_g': out['v_mla_k_norm_g'], 'v_mla_w_o': out['v_mla_w_o'], 'v_gate_b': out['v_gate_b'], 'v_w_out': out['v_w_out'], 'v_xattn_norm_g': out['v_xattn_norm_g'], 'v_mem_norm_g': out['v_mem_norm_g'], 'v_xattn_w_q': out['v_xattn_w_q'], 'v_xattn_w_kv': out['v_xattn_w_kv'], 'v_xattn_q_norm_g': out['v_xattn_q_norm_g'], 'v_xattn_k_norm_g': out['v_xattn_k_norm_g'], 'v_xattn_w_o': out['v_xattn_w_o'], 'v_ffn_norm_g': out['v_ffn_norm_g'], 'v_ffn_w_in': out['v_ffn_w_in'], 'v_ffn_w_out': out['v_ffn_w_out']}


def _loss(weights, diff, rest, loss_target):
    with _jax.named_scope("forward"):
        args = {**rest, TWIN_DIFF_INPUT: diff, **{k: w.astype(_WEIGHT_DTYPES[k]) for k, w in weights.items()}}
        y = _forward(args)
    with _jax.named_scope("loss_head"):
        err = _jnp.square(y.astype(_jnp.float32) - loss_target)
        return 0.5 * _jnp.sum(_jnp.mean(err, axis=-1)) if err.ndim else 0.5 * err


def _adamw(w, g, m, v):
    m = ADAM_B1 * m + (1.0 - ADAM_B1) * g
    v = ADAM_B2 * v + (1.0 - ADAM_B2) * _jnp.square(g)
    m_hat = m / (1.0 - ADAM_B1 ** ADAM_STEP)
    v_hat = v / (1.0 - ADAM_B2 ** ADAM_STEP)
    delta = -ADAM_LR * (m_hat / (_jnp.sqrt(v_hat) + ADAM_EPS) + ADAM_WD * w)
    return delta, m, v


def reference(x, mem, positions, mix_norm_g, w_in, ssd_conv_w, ssd_conv_b, ssd_dt_bias, ssd_a_log, ssd_d, ssd_norm_g, ssd_w_out, conv_dw_w, conv_dw_b, conv_ln_g, conv_ln_b, conv_w_out, mla_q_a_g, mla_w_q_b, mla_kv_a_g, mla_w_kv_b, mla_q_norm_g, mla_k_norm_g, mla_w_o, gate_b, w_out, xattn_norm_g, mem_norm_g, xattn_w_q, xattn_w_kv, xattn_q_norm_g, xattn_k_norm_g, xattn_w_o, ffn_norm_g, ffn_w_in, ffn_w_out, loss_target, m_mix_norm_g, m_w_in, m_ssd_conv_w, m_ssd_conv_b, m_ssd_dt_bias, m_ssd_a_log, m_ssd_d, m_ssd_norm_g, m_ssd_w_out, m_conv_dw_w, m_conv_dw_b, m_conv_ln_g, m_conv_ln_b, m_conv_w_out, m_mla_q_a_g, m_mla_w_q_b, m_mla_kv_a_g, m_mla_w_kv_b, m_mla_q_norm_g, m_mla_k_norm_g, m_mla_w_o, m_gate_b, m_w_out, m_xattn_norm_g, m_mem_norm_g, m_xattn_w_q, m_xattn_w_kv, m_xattn_q_norm_g, m_xattn_k_norm_g, m_xattn_w_o, m_ffn_norm_g, m_ffn_w_in, m_ffn_w_out, v_mix_norm_g, v_w_in, v_ssd_conv_w, v_ssd_conv_b, v_ssd_dt_bias, v_ssd_a_log, v_ssd_d, v_ssd_norm_g, v_ssd_w_out, v_conv_dw_w, v_conv_dw_b, v_conv_ln_g, v_conv_ln_b, v_conv_w_out, v_mla_q_a_g, v_mla_w_q_b, v_mla_kv_a_g, v_mla_w_kv_b, v_mla_q_norm_g, v_mla_k_norm_g, v_mla_w_o, v_gate_b, v_w_out, v_xattn_norm_g, v_mem_norm_g, v_xattn_w_q, v_xattn_w_kv, v_xattn_q_norm_g, v_xattn_k_norm_g, v_xattn_w_o, v_ffn_norm_g, v_ffn_w_in, v_ffn_w_out):
    given = dict(x=x, mem=mem, positions=positions, mix_norm_g=mix_norm_g, w_in=w_in, ssd_conv_w=ssd_conv_w, ssd_conv_b=ssd_conv_b, ssd_dt_bias=ssd_dt_bias, ssd_a_log=ssd_a_log, ssd_d=ssd_d, ssd_norm_g=ssd_norm_g, ssd_w_out=ssd_w_out, conv_dw_w=conv_dw_w, conv_dw_b=conv_dw_b, conv_ln_g=conv_ln_g, conv_ln_b=conv_ln_b, conv_w_out=conv_w_out, mla_q_a_g=mla_q_a_g, mla_w_q_b=mla_w_q_b, mla_kv_a_g=mla_kv_a_g, mla_w_kv_b=mla_w_kv_b, mla_q_norm_g=mla_q_norm_g, mla_k_norm_g=mla_k_norm_g, mla_w_o=mla_w_o, gate_b=gate_b, w_out=w_out, xattn_norm_g=xattn_norm_g, mem_norm_g=mem_norm_g, xattn_w_q=xattn_w_q, xattn_w_kv=xattn_w_kv, xattn_q_norm_g=xattn_q_norm_g, xattn_k_norm_g=xattn_k_norm_g, xattn_w_o=xattn_w_o, ffn_norm_g=ffn_norm_g, ffn_w_in=ffn_w_in, ffn_w_out=ffn_w_out, loss_target=loss_target, m_mix_norm_g=m_mix_norm_g, m_w_in=m_w_in, m_ssd_conv_w=m_ssd_conv_w, m_ssd_conv_b=m_ssd_conv_b, m_ssd_dt_bias=m_ssd_dt_bias, m_ssd_a_log=m_ssd_a_log, m_ssd_d=m_ssd_d, m_ssd_norm_g=m_ssd_norm_g, m_ssd_w_out=m_ssd_w_out, m_conv_dw_w=m_conv_dw_w, m_conv_dw_b=m_conv_dw_b, m_conv_ln_g=m_conv_ln_g, m_conv_ln_b=m_conv_ln_b, m_conv_w_out=m_conv_w_out, m_mla_q_a_g=m_mla_q_a_g, m_mla_w_q_b=m_mla_w_q_b, m_mla_kv_a_g=m_mla_kv_a_g, m_mla_w_kv_b=m_mla_w_kv_b, m_mla_q_norm_g=m_mla_q_norm_g, m_mla_k_norm_g=m_mla_k_norm_g, m_mla_w_o=m_mla_w_o, m_gate_b=m_gate_b, m_w_out=m_w_out, m_xattn_norm_g=m_xattn_norm_g, m_mem_norm_g=m_mem_norm_g, m_xattn_w_q=m_xattn_w_q, m_xattn_w_kv=m_xattn_w_kv, m_xattn_q_norm_g=m_xattn_q_norm_g, m_xattn_k_norm_g=m_xattn_k_norm_g, m_xattn_w_o=m_xattn_w_o, m_ffn_norm_g=m_ffn_norm_g, m_ffn_w_in=m_ffn_w_in, m_ffn_w_out=m_ffn_w_out, v_mix_norm_g=v_mix_norm_g, v_w_in=v_w_in, v_ssd_conv_w=v_ssd_conv_w, v_ssd_conv_b=v_ssd_conv_b, v_ssd_dt_bias=v_ssd_dt_bias, v_ssd_a_log=v_ssd_a_log, v_ssd_d=v_ssd_d, v_ssd_norm_g=v_ssd_norm_g, v_ssd_w_out=v_ssd_w_out, v_conv_dw_w=v_conv_dw_w, v_conv_dw_b=v_conv_dw_b, v_conv_ln_g=v_conv_ln_g, v_conv_ln_b=v_conv_ln_b, v_conv_w_out=v_conv_w_out, v_mla_q_a_g=v_mla_q_a_g, v_mla_w_q_b=v_mla_w_q_b, v_mla_kv_a_g=v_mla_kv_a_g, v_mla_w_kv_b=v_mla_w_kv_b, v_mla_q_norm_g=v_mla_q_norm_g, v_mla_k_norm_g=v_mla_k_norm_g, v_mla_w_o=v_mla_w_o, v_gate_b=v_gate_b, v_w_out=v_w_out, v_xattn_norm_g=v_xattn_norm_g, v_mem_norm_g=v_mem_norm_g, v_xattn_w_q=v_xattn_w_q, v_xattn_w_kv=v_xattn_w_kv, v_xattn_q_norm_g=v_xattn_q_norm_g, v_xattn_k_norm_g=v_xattn_k_norm_g, v_xattn_w_o=v_xattn_w_o, v_ffn_norm_g=v_ffn_norm_g, v_ffn_w_in=v_ffn_w_in, v_ffn_w_out=v_ffn_w_out)
    weights = {n: given[n] for n in TWIN_WEIGHTS}
    shared = {n: given[n] for n in SHARED_INPUTS}
    per_example = {n: given[n] for n in ['x', 'mem', 'positions']}
    grad_fn = _jax.value_and_grad(_loss, argnums=(0, 1))

    def one_microbatch(ex, loss_target):
        ex = dict(ex)
        diff = ex.pop(TWIN_DIFF_INPUT)
        return grad_fn(weights, diff, {**shared, **ex}, loss_target)

    if N_MICROBATCH == 1:
        loss, (grad_w, grad_x) = one_microbatch(per_example, given["loss_target"])
    else:
        def body(carry, xs):
            loss_sum, grad_sum = carry
            l_k, (gw_k, gx_k) = one_microbatch(xs[0], xs[1])
            with _jax.named_scope("update"):
                return (loss_sum + l_k, _jax.tree.map(_jnp.add, grad_sum, gw_k)), gx_k

        init = (_jnp.zeros((), _jnp.float32), _jax.tree.map(_jnp.zeros_like, weights))
        (loss, grad_w), grad_x = _jax.lax.scan(body, init, (per_example, given["loss_target"]))
    with _jax.named_scope("update"):
        delta_w, new_m, new_v = {}, {}, {}
        for n in TWIN_WEIGHTS:
            delta_w[n], new_m[n], new_v[n] = _adamw(weights[n], grad_w[n], given["m_" + n], given["v_" + n])
    return (loss, grad_x, *[grad_w[n] for n in TWIN_WEIGHTS], *[delta_w[n] for n in TWIN_WEIGHTS],
            *[new_m[n] for n in TWIN_WEIGHTS], *[new_v[n] for n in TWIN_WEIGHTS])
```

```python
import functools

import jax
import jax.numpy as jnp
from jax import lax
from jax.experimental import pallas as pl
from jax.experimental.pallas import tpu as pltpu

F32 = jnp.float32
BF16 = jnp.bfloat16
HIGHEST = lax.Precision.HIGHEST
MESH_ID = pl.DeviceIdType.MESH

VMEM_LIMIT_BYTES = 56 * 1024 * 1024
LANES = 128

EPS = 1e-6
DEPTH = 4
D_MODEL = 1024
N_DEV = 8
SSD_HEADS = 16
SSD_HEAD_DIM = 64
SSD_STATE = 128
SSD_GROUPS = 4
SSD_INNER = 1024
SSD_TILE = 256
CONV_K = 31
SSD_CONV_K = 4
MLA_HEADS = 8
MLA_NOPE = 128
MLA_ROPE = 64
MLA_V = 128
MLA_Q_RANK = 384
MLA_KV_RANK = 256
ATT_CHUNK = 64
ROPE_THETA = 10000.0
X_HEADS = 4
X_HEAD_DIM = 256
FFN_HIDDEN = 2816
IN_SIZES = (1024, 2048, 16, 2048, 384, 320, 3072)

ADAM_LR = 0.001
ADAM_B1 = 0.9
ADAM_B2 = 0.999
ADAM_EPS = 1e-08
ADAM_WD = 0.01
ADAM_STEP = 10

ROW_TILE = 256
ATT_Q_TILE = 128
XATT_Q_TILE = 512

SHARDED = {
    "w_in": 1, "ssd_conv_w": 1, "ssd_w_out": 0, "conv_dw_w": 1, "conv_w_out": 0, "mla_w_q_b": 1, "mla_w_kv_b": 1,
    "mla_w_o": 0, "gate_b": 1, "w_out": 0, "xattn_w_q": 0, "xattn_w_kv": 1, "xattn_w_o": 0, "ffn_w_in": 1,
    "ffn_w_out": 0,
}
WEIGHTS = ["mix_norm_g", "w_in", "ssd_conv_w", "ssd_conv_b", "ssd_dt_bias", "ssd_a_log", "ssd_d", "ssd_norm_g",
           "ssd_w_out", "conv_dw_w", "conv_dw_b", "conv_ln_g", "conv_ln_b", "conv_w_out", "mla_q_a_g", "mla_w_q_b",
           "mla_kv_a_g", "mla_w_kv_b", "mla_q_norm_g", "mla_k_norm_g", "mla_w_o", "gate_b", "w_out", "xattn_norm_g",
           "mem_norm_g", "xattn_w_q", "xattn_w_kv", "xattn_q_norm_g", "xattn_k_norm_g", "xattn_w_o", "ffn_norm_g",
           "ffn_w_in", "ffn_w_out"]
REPLICATED = [n for n in WEIGHTS if n not in SHARDED]


def _params(sem=None):
    return pltpu.CompilerParams(dimension_semantics=sem, vmem_limit_bytes=VMEM_LIMIT_BYTES)


def _divisor(n, cap, mult):
    if n <= cap:
        return n
    for d in range(cap - cap % mult, 0, -mult):
        if n % d == 0:
            return d
    raise ValueError(f"no tile for {n}")


def _dg(a, b, ca, cb):
    return lax.dot_general(a.astype(BF16), b.astype(BF16), (((ca,), (cb,)), ((), ())), preferred_element_type=F32)


@jax.custom_vjp
def bdot(a, b):
    return _dg(a, b, 1, 0)


bdot.defvjp(lambda a, b: (_dg(a, b, 1, 0), (a, b)), lambda r, g: (_dg(g, r[1], 1, 1), _dg(r[0], g, 0, 0)))


@jax.custom_vjp
def bdot_nt(a, b):
    return _dg(a, b, 1, 1)


bdot_nt.defvjp(lambda a, b: (_dg(a, b, 1, 1), (a, b)), lambda r, g: (_dg(g, r[1], 1, 0), _dg(g, r[0], 0, 0)))


@jax.custom_vjp
def bdot_tn(a, b):
    return _dg(a, b, 0, 0)


bdot_tn.defvjp(lambda a, b: (_dg(a, b, 0, 0), (a, b)), lambda r, g: (_dg(r[1], g, 1, 1), _dg(r[0], g, 1, 0)))


def hdot(a, b):
    return jnp.dot(a, b, precision=HIGHEST, preferred_element_type=F32)


def _iota(shape, dim):
    return lax.broadcasted_iota(jnp.int32, shape, dim)


def _mm(name, a, b, ta=False, tb=False, res=None):
    m, k = (a.shape[1], a.shape[0]) if ta else a.shape
    n = b.shape[0] if tb else b.shape[1]
    tm = _divisor(m, 1024, LANES if ta else 8)
    tn = _divisor(n, 1536, LANES)
    tk = _divisor(k, 512, LANES)
    nk = k // tk
    dims = (((0 if ta else 1,), (1 if tb else 0,)), ((), ()))

    def body(*refs):
        if res is None:
            a_ref, b_ref, o_ref, acc = refs
        else:
            a_ref, b_ref, r_ref, o_ref, acc = refs
        kk = pl.program_id(2)

        @pl.when(kk == 0)
        def _():
            acc[...] = jnp.zeros_like(acc)

        acc[...] += lax.dot_general(a_ref[...].astype(BF16), b_ref[...].astype(BF16), dims,
                                    preferred_element_type=F32)

        @pl.when(kk == nk - 1)
        def _():
            o_ref[...] = acc[...] if res is None else acc[...] + r_ref[...]

    a_spec = pl.BlockSpec((tk, tm), lambda i, j, kk: (kk, i)) if ta else pl.BlockSpec((tm, tk), lambda i, j, kk: (i, kk))
    b_spec = pl.BlockSpec((tn, tk), lambda i, j, kk: (j, kk)) if tb else pl.BlockSpec((tk, tn), lambda i, j, kk: (kk, j))
    o_spec = pl.BlockSpec((tm, tn), lambda i, j, kk: (i, j))
    in_specs = [a_spec, b_spec] + ([] if res is None else [o_spec])
    args = (a, b) + (() if res is None else (res,))
    return pl.pallas_call(
        body, grid=(m // tm, n // tn, nk), in_specs=in_specs, out_specs=o_spec,
        out_shape=jax.ShapeDtypeStruct((m, n), F32), scratch_shapes=[pltpu.VMEM((tm, tn), F32)],
        compiler_params=_params(("parallel", "parallel", "arbitrary")), name=name)(*args)


def matmul(name, a, w, res=None):
    if res is None:
        @jax.custom_vjp
        def run(a, w):
            return _mm(name, a, w)

        def fwd(a, w):
            return run(a, w), (a, w)

        def bwd(r, g):
            return _mm(name + "_da", g, r[1], tb=True), _mm(name + "_dw", r[0], g, ta=True)

        run.defvjp(fwd, bwd)
        return run(a, w)

    @jax.custom_vjp
    def run_res(a, w, res):
        return _mm(name, a, w, res=res)

    def fwd_res(a, w, res):
        return run_res(a, w, res), (a, w)

    def bwd_res(r, g):
        return _mm(name + "_da", g, r[1], tb=True), _mm(name + "_dw", r[0], g, ta=True), g

    run_res.defvjp(fwd_res, bwd_res)
    return run_res(a, w, res)


def tmap(name, f, grid, ins, outs):
    arrays = [x[0] for x in ins]
    kinds = [x[3] for x in ins]
    in_specs = [pl.BlockSpec(x[1], x[2]) for x in ins]
    out_specs = [pl.BlockSpec(x[1], x[2]) for x in outs]
    out_shape = [jax.ShapeDtypeStruct(x[0], F32) for x in outs]
    n_in, n_out = len(ins), len(outs)
    didx = [k for k, kd in enumerate(kinds) if kd != "n"]

    def fwd_call(*arrs):
        def body(*refs):
            pids = (pl.program_id(0), pl.program_id(1))
            vals = f(pids, *[r[...] for r in refs[:n_in]])
            for r, v in zip(refs[n_in:], vals):
                r[...] = v

        return pl.pallas_call(body, grid=grid, in_specs=in_specs, out_specs=out_specs, out_shape=out_shape,
                              compiler_params=_params(("arbitrary", "arbitrary")), name=name)(*arrs)

    def bwd_call(arrs, cts):
        def body(*refs):
            o, i = pl.program_id(0), pl.program_id(1)
            vals = [r[...] for r in refs[:n_in]]

            def g(*dv):
                full = list(vals)
                for k, v in zip(didx, dv):
                    full[k] = v
                return tuple(f((o, i), *full))

            _, vjp = jax.vjp(g, *[vals[k] for k in didx])
            grads = vjp(tuple(r[...] for r in refs[n_in:n_in + n_out]))
            for k, gr, r in zip(didx, grads, refs[n_in + n_out:]):
                if kinds[k] == "t":
                    r[...] = gr
                else:
                    first = (i == 0) if kinds[k] == "ai" else jnp.logical_and(o == 0, i == 0)

                    @pl.when(first)
                    def _(r=r, gr=gr):
                        r[...] = gr

                    @pl.when(jnp.logical_not(first))
                    def _(r=r, gr=gr):
                        r[...] += gr

        g_specs = [in_specs[k] for k in didx]
        g_shape = [jax.ShapeDtypeStruct(arrs[k].shape, F32) for k in didx]
        return pl.pallas_call(body, grid=grid, in_specs=in_specs + out_specs, out_specs=g_specs, out_shape=g_shape,
                              compiler_params=_params(("arbitrary", "arbitrary")), name=name + "_bwd")(*arrs, *cts)

    @jax.custom_vjp
    def run(*arrs):
        return tuple(fwd_call(*arrs))

    def run_fwd(*arrs):
        return run(*arrs), arrs

    def run_bwd(arrs, cts):
        gs = bwd_call(arrs, cts)
        full = [None] * n_in
        for k, g in zip(didx, gs):
            full[k] = g
        return tuple(full)

    run.defvjp(run_fwd, run_bwd)
    return run(*arrays)


def _rows(arr, tile, kind="t"):
    return (arr, (tile, arr.shape[1]), lambda o, i: (i, 0), kind)


def _whole(arr, kind="ag"):
    return (arr, arr.shape, lambda o, i: (0, 0), kind)


def _row_out(t, n, tile):
    return ((t, n), (tile, n), lambda o, i: (i, 0))


def _rms(x, g, n=None):
    ms = jnp.sum(x * x, axis=-1, keepdims=True) / (x.shape[-1] if n is None else n)
    return x * lax.rsqrt(ms + EPS) * g


def rms_norm(name, x, g):
    t, n = x.shape
    tile = min(ROW_TILE, t)
    return tmap(name, lambda p, x, g: (_rms(x, g),), (1, t // tile), [_rows(x, tile), _whole(g)],
                [_row_out(t, n, tile)])[0]


def rms_norm_nograd_x(name, x, g):
    t, n = x.shape
    tile = min(ROW_TILE, t)
    return tmap(name, lambda p, x, g: (_rms(x, g),), (1, t // tile), [_rows(x, tile, "n"), _whole(g)],
                [_row_out(t, n, tile)])[0]


def _glu_f(p, glu):
    h = glu.shape[1] // 2
    return (glu[:, :h] * jax.nn.sigmoid(glu[:, h:]),)


def _ln_silu_f(p, v, g, b):
    mu = jnp.mean(v, axis=-1, keepdims=True)
    xc = v - mu
    var = jnp.mean(xc * xc, axis=-1, keepdims=True)
    return (jax.nn.silu(xc * lax.rsqrt(var + EPS) * g + b),)


def _ssd_gate_norm_f(p, y, z, g):
    v = y * jax.nn.silu(z)
    w = SSD_INNER // SSD_GROUPS
    parts = []
    for k in range(SSD_GROUPS):
        vg = v[:, k * w:(k + 1) * w]
        parts.append(vg * lax.rsqrt(jnp.mean(vg * vg, axis=-1, keepdims=True) + EPS))
    return (jnp.concatenate(parts, axis=1) * g,)


def _merge_f(p, gl, gb, y0, y1, y2):
    g = jax.nn.sigmoid(gl + gb)
    d = D_MODEL
    return (g[:, :d] * y0 + g[:, d:2 * d] * y1 + g[:, 2 * d:] * y2,)


def _swiglu_f(p, gate, up):
    return (jax.nn.silu(gate) * up,)


def _rot_matrix():
    r, c = _iota((LANES, LANES), 0), _iota((LANES, LANES), 1)
    h = MLA_ROPE // 2
    plus = jnp.logical_and(c >= h, jnp.logical_and(c < 2 * h, r == c - h))
    minus = jnp.logical_and(c < h, r == c + h)
    return plus.astype(F32) - minus.astype(F32)


def _rope(x, cosf, sinf):
    return x * cosf + hdot(x, _rot_matrix()) * sinf


def _k_nope_f(p, kn_raw, kg):
    return (_rms(kn_raw, kg[:, :MLA_NOPE]),)


def _k_rope_f(p, kr_raw, cosf, sinf, kg):
    return (_rope(_rms(kr_raw, kg[:, MLA_NOPE:], n=MLA_ROPE), cosf, sinf),)


def _softmax(s):
    m = jnp.max(s, axis=-1, keepdims=True)
    e = jnp.exp(s - m)
    return e / jnp.sum(e, axis=-1, keepdims=True)


def _mla_attn_f(p, qn_raw, qr_raw, cosf, sinf, kn, kr, v, qg):
    _, i = p
    tq, t = qn_raw.shape[0], kn.shape[0]
    qn = _rms(qn_raw, qg[:, :MLA_NOPE])
    qr = _rope(_rms(qr_raw, qg[:, MLA_NOPE:], n=MLA_ROPE), cosf, sinf)
    s = (bdot_nt(qn, kn) + bdot_nt(qr, kr)) * ((MLA_NOPE + MLA_ROPE) ** -0.5)
    q_chunk = (i * tq + _iota((tq, 1), 0)) // ATT_CHUNK
    k_chunk = _iota((1, t), 1) // ATT_CHUNK
    s = jnp.where(k_chunk <= q_chunk, s, -1e30)
    return (bdot(_softmax(s), v),)


def _xattn_f(p, q, k, v, qg, kg):
    s = bdot_nt(_rms(q, qg), _rms(k, kg)) * (X_HEAD_DIM ** -0.5)
    return (bdot(_softmax(s), v),)


def _shift_down(v, s, rows):
    return v if s == 0 else jnp.where(rows >= s, pltpu.roll(v, s, 0), 0.0)


def _shift_up(v, s, rows):
    t = v.shape[0]
    return v if s == 0 else jnp.where(rows < t - s, pltpu.roll(v, t - s, 0), 0.0)


def _dwconv_fwd(name, x, w, b):
    t, c = x.shape
    kw = w.shape[0]

    def body(x_ref, w_ref, b_ref, y_ref):
        x = x_ref[...]
        rows = _iota(x.shape, 0)
        acc = jnp.zeros_like(x) + b_ref[...]
        for k in range(kw):
            acc = acc + w_ref[k:k + 1, :] * _shift_down(x, kw - 1 - k, rows)
        y_ref[...] = acc

    col = lambda i: (0, i)
    return pl.pallas_call(
        body, grid=(c // LANES,),
        in_specs=[pl.BlockSpec((t, LANES), col), pl.BlockSpec((kw, LANES), col), pl.BlockSpec((1, LANES), col)],
        out_specs=pl.BlockSpec((t, LANES), col), out_shape=jax.ShapeDtypeStruct((t, c), F32),
        compiler_params=_params(("arbitrary",)), name=name)(x, w, b)


def _dwconv_bwd(name, x, w, dy):
    t, c = x.shape
    kw = w.shape[0]

    def body(x_ref, w_ref, dy_ref, dx_ref, dw_ref, db_ref):
        x, dy = x_ref[...], dy_ref[...]
        rows = _iota(x.shape, 0)
        dx = jnp.zeros_like(x)
        for k in range(kw):
            s = kw - 1 - k
            dx = dx + w_ref[k:k + 1, :] * _shift_up(dy, s, rows)
            dw_ref[k:k + 1, :] = jnp.sum(dy * _shift_down(x, s, rows), axis=0, keepdims=True)
        dx_ref[...] = dx
        db_ref[...] = jnp.sum(dy, axis=0, keepdims=True)

    col = lambda i: (0, i)
    big, wsp, bsp = pl.BlockSpec((t, LANES), col), pl.BlockSpec((kw, LANES), col), pl.BlockSpec((1, LANES), col)
    return pl.pallas_call(
        body, grid=(c // LANES,), in_specs=[big, wsp, big], out_specs=[big, wsp, bsp],
        out_shape=[jax.ShapeDtypeStruct((t, c), F32), jax.ShapeDtypeStruct((kw, c), F32),
                   jax.ShapeDtypeStruct((1, c), F32)],
        compiler_params=_params(("arbitrary",)), name=name)(x, w, dy)


def dwconv(name, x, w, b):
    @jax.custom_vjp
    def run(x, w, b):
        return _dwconv_fwd(name, x, w, b)

    def fwd(x, w, b):
        return run(x, w, b), (x, w)

    def bwd(r, g):
        return tuple(_dwconv_bwd(name + "_bwd", r[0], r[1], g))

    run.defvjp(fwd, bwd)
    return run(x, w, b)


def _ssd_tile(xc, dtr, dtb, alog, dsk, prev):
    ln = xc.shape[0]
    gw = SSD_INNER // SSD_GROUPS
    ns = SSD_STATE
    xs = jax.nn.silu(xc[:, :SSD_INNER])
    bm = jax.nn.silu(xc[:, SSD_INNER:SSD_INNER + SSD_GROUPS * ns])
    cm = jax.nn.silu(xc[:, SSD_INNER + SSD_GROUPS * ns:])
    dt = jax.nn.softplus(dtr + dtb)
    a = dt * (-jnp.exp(alog))
    expand = (_iota((LANES, SSD_INNER), 0) == _iota((LANES, SSD_INNER), 1) // SSD_HEAD_DIM).astype(F32)
    causal = _iota((ln, ln), 0) >= _iota((ln, ln), 1)
    acs_h = hdot(causal.astype(F32), a)
    acs_c = hdot(acs_h, expand)
    dt_c = hdot(dt, expand)

    def row_per_column(v):
        return jnp.mean(hdot(jnp.broadcast_to(v, (8, LANES)), expand), axis=0, keepdims=True)

    aend_c = row_per_column(jnp.sum(a, axis=0, keepdims=True))
    xdt = xs * dt_c
    to_end = xdt * jnp.exp(aend_c - acs_c)
    from_start = jnp.exp(acs_c)
    acs_ht = acs_h.T
    lane_h, sub_h = _iota((1, LANES), 1), _iota((LANES, 1), 0)
    head_of_col = _iota((1, gw), 1) // SSD_HEAD_DIM
    ys, states = [], []
    for g in range(SSD_GROUPS):
        cg = cm[:, g * ns:(g + 1) * ns]
        bg = bm[:, g * ns:(g + 1) * ns]
        cols = slice(g * gw, (g + 1) * gw)
        y = bdot(cg, prev[:, cols]) * from_start[:, cols]
        states.append(bdot_tn(bg, to_end[:, cols]))
        cb = bdot_nt(cg, bg)
        for r in range(gw // SSD_HEAD_DIM):
            h = g * (gw // SSD_HEAD_DIM) + r
            col = jnp.sum(jnp.where(lane_h == h, acs_h, 0.0), axis=1, keepdims=True)
            row = jnp.sum(jnp.where(sub_h == h, acs_ht, 0.0), axis=0, keepdims=True)
            decay = jnp.exp(jnp.where(causal, col - row, -1e30))
            y = y + jnp.where(head_of_col == r, bdot(cb * decay, xdt[:, cols]), 0.0)
        ys.append(y)
    y = jnp.concatenate(ys, axis=1) + row_per_column(dsk) * xs
    new = prev * jnp.exp(aend_c) + jnp.concatenate(states, axis=1)
    return y, new


def _ssd_fwd(name, xc, dtr, dtb, alog, dsk):
    t = xc.shape[0]
    ln = min(SSD_TILE, t)
    nt = t // ln

    def body(xc_ref, dtr_ref, dtb_ref, alog_ref, dsk_ref, y_ref, prev_ref, carry):
        @pl.when(pl.program_id(0) == 0)
        def _():
            carry[...] = jnp.zeros_like(carry)

        prev = carry[...]
        prev_ref[...] = prev
        y, new = _ssd_tile(xc_ref[...], dtr_ref[...], dtb_ref[...], alog_ref[...], dsk_ref[...], prev)
        y_ref[...] = y
        carry[...] = new

    row = lambda i: (i, 0)
    par = pl.BlockSpec((1, LANES), lambda i: (0, 0))
    return pl.pallas_call(
        body, grid=(nt,),
        in_specs=[pl.BlockSpec((ln, xc.shape[1]), row), pl.BlockSpec((ln, LANES), row), par, par, par],
        out_specs=[pl.BlockSpec((ln, SSD_INNER), row), pl.BlockSpec((None, SSD_STATE, SSD_INNER), lambda i: (i, 0, 0))],
        out_shape=[jax.ShapeDtypeStruct((t, SSD_INNER), F32), jax.ShapeDtypeStruct((nt, SSD_STATE, SSD_INNER), F32)],
        scratch_shapes=[pltpu.VMEM((SSD_STATE, SSD_INNER), F32)],
        compiler_params=_params(("arbitrary",)), name=name)(xc, dtr, dtb, alog, dsk)


def _ssd_bwd(name, xc, dtr, dtb, alog, dsk, prevs, dy):
    t = xc.shape[0]
    ln = min(SSD_TILE, t)
    nt = t // ln

    def body(xc_ref, dtr_ref, dtb_ref, alog_ref, dsk_ref, prev_ref, dy_ref, dxc_ref, ddtr_ref, ddtb_ref, dalog_ref,
             ddsk_ref, dcarry):
        i = pl.program_id(0)

        @pl.when(i == 0)
        def _():
            dcarry[...] = jnp.zeros_like(dcarry)

        _, vjp = jax.vjp(_ssd_tile, xc_ref[...], dtr_ref[...], dtb_ref[...], alog_ref[...], dsk_ref[...], prev_ref[...])
        dxc, ddtr, ddtb, dalog, ddsk, dprev = vjp((dy_ref[...], dcarry[...]))
        dxc_ref[...] = dxc
        ddtr_ref[...] = ddtr
        dcarry[...] = dprev
        for r, gr in ((ddtb_ref, ddtb), (dalog_ref, dalog), (ddsk_ref, ddsk)):
            @pl.when(i == 0)
            def _(r=r, gr=gr):
                r[...] = gr

            @pl.when(i != 0)
            def _(r=r, gr=gr):
                r[...] += gr

    row = lambda i: (nt - 1 - i, 0)
    par = pl.BlockSpec((1, LANES), lambda i: (0, 0))
    big, dts = pl.BlockSpec((ln, xc.shape[1]), row), pl.BlockSpec((ln, LANES), row)
    par_shape = jax.ShapeDtypeStruct((1, LANES), F32)
    return pl.pallas_call(
        body, grid=(nt,),
        in_specs=[big, dts, par, par, par, pl.BlockSpec((None, SSD_STATE, SSD_INNER), lambda i: (nt - 1 - i, 0, 0)),
                  pl.BlockSpec((ln, SSD_INNER), row)],
        out_specs=[big, dts, par, par, par],
        out_shape=[jax.ShapeDtypeStruct(xc.shape, F32), jax.ShapeDtypeStruct(dtr.shape, F32), par_shape, par_shape,
                   par_shape],
        scratch_shapes=[pltpu.VMEM((SSD_STATE, SSD_INNER), F32)],
        compiler_params=_params(("arbitrary",)), name=name)(xc, dtr, dtb, alog, dsk, prevs, dy)


def ssd_scan(name, xc, dtr, dtb, alog, dsk):
    @jax.custom_vjp
    def run(xc, dtr, dtb, alog, dsk):
        return _ssd_fwd(name, xc, dtr, dtb, alog, dsk)[0]

    def fwd(xc, dtr, dtb, alog, dsk):
        y, prevs = _ssd_fwd(name, xc, dtr, dtb, alog, dsk)
        return y, (xc, dtr, dtb, alog, dsk, prevs)

    def bwd(r, g):
        return tuple(_ssd_bwd(name + "_bwd", *r, g))

    run.defvjp(fwd, bwd)
    return run(xc, dtr, dtb, alog, dsk)


def loss_head(y, target):
    t, n = y.shape
    tile = min(ROW_TILE, t)

    def body(y_ref, t_ref, dy_ref, acc_ref):
        d = y_ref[...] - t_ref[...]
        dy_ref[...] = d * (1.0 / n)

        @pl.when(pl.program_id(0) == 0)
        def _():
            acc_ref[...] = jnp.zeros_like(acc_ref)

        acc_ref[...] += jnp.sum(d * d, axis=0, keepdims=True)

    row = pl.BlockSpec((tile, n), lambda i: (i, 0))
    dy, acc = pl.pallas_call(
        body, grid=(t // tile,), in_specs=[row, row], out_specs=[row, pl.BlockSpec((1, n), lambda i: (0, 0))],
        out_shape=[jax.ShapeDtypeStruct((t, n), F32), jax.ShapeDtypeStruct((1, n), F32)],
        compiler_params=_params(("arbitrary",)), name="loss_head")(y, target)
    return acc, dy


def adamw(name, w, g, m, v):
    shape = w.shape
    cols = shape[-1]
    rows = w.size // cols
    tile = _divisor(rows, 512, 8) if rows % 8 == 0 else rows

    def body(w_ref, g_ref, m_ref, v_ref, d_ref, nm_ref, nv_ref):
        g = g_ref[...]
        m = ADAM_B1 * m_ref[...] + (1.0 - ADAM_B1) * g
        v = ADAM_B2 * v_ref[...] + (1.0 - ADAM_B2) * jnp.square(g)
        m_hat = m / (1.0 - ADAM_B1 ** ADAM_STEP)
        v_hat = v / (1.0 - ADAM_B2 ** ADAM_STEP)
        d_ref[...] = -ADAM_LR * (m_hat / (jnp.sqrt(v_hat) + ADAM_EPS) + ADAM_WD * w_ref[...])
        nm_ref[...] = m
        nv_ref[...] = v

    spec = pl.BlockSpec((tile, cols), lambda i: (i, 0))
    two_d = jax.ShapeDtypeStruct((rows, cols), F32)
    outs = pl.pallas_call(body, grid=(rows // tile,), in_specs=[spec] * 4, out_specs=[spec] * 3, out_shape=[two_d] * 3,
                          compiler_params=_params(("arbitrary",)), name=name)(
        *[a.reshape(rows, cols) for a in (w, g, m, v)])
    return [o.reshape(shape) for o in outs]


HBM_SPEC = pl.BlockSpec(memory_space=pl.ANY)


def _position():
    return lax.axis_index("x"), lax.axis_index("y"), lax.axis_index("c")


def all_gather(name, shard):
    r, c_ = shard.shape

    def body(x_ref, out_ref, send_sems, recv_sems, local_sem):
        x, y, c = _position()
        me, sibling = (x, y, c), (x, y, 1 - c)
        chips = [(1 - x, y), (x, 1 - y), (1 - x, 1 - y)]

        def block(px, py, pc):
            return out_ref.at[4 * px + 2 * py + pc]

        def copy(k, blk, to, src=None):
            return pltpu.make_async_remote_copy(
                src_ref=block(*blk) if src is None else src, dst_ref=block(*blk), send_sem=send_sems.at[k],
                recv_sem=recv_sems.at[k], device_id=to, device_id_type=MESH_ID)

        mine = pltpu.make_async_copy(x_ref, block(*me), local_sem)
        mine.start()
        first = [copy(0, me, sibling, src=x_ref)]
        first += [copy(1 + j, me, (*chip, c), src=x_ref) for j, chip in enumerate(chips)]
        for cp in first:
            cp.start()
        passed = [copy(4 + j, (*chip, c), sibling) for j, chip in enumerate(chips)]
        for j, chip in enumerate(chips):
            copy(1 + j, (*chip, c), me).wait_recv()
            passed[j].start()
        copy(0, sibling, me).wait_recv()
        for j, chip in enumerate(chips):
            copy(4 + j, (*chip, 1 - c), me).wait_recv()
        for cp in first + passed:
            cp.wait_send()
        mine.wait()

    return pl.pallas_call(
        body, in_specs=[HBM_SPEC], out_specs=HBM_SPEC, out_shape=jax.ShapeDtypeStruct((N_DEV, r, c_), shard.dtype),
        scratch_shapes=[pltpu.SemaphoreType.DMA((7,)), pltpu.SemaphoreType.DMA((7,)), pltpu.SemaphoreType.DMA],
        name=name)(shard)


def pair_exchange(name, g):
    _, nchip, r, c_ = g.shape

    def body(g_ref, out_ref, send_sem, recv_sem):
        x, y, c = _position()
        cp = pltpu.make_async_remote_copy(src_ref=g_ref.at[1 - c], dst_ref=out_ref, send_sem=send_sem,
                                          recv_sem=recv_sem, device_id=(x, y, 1 - c), device_id_type=MESH_ID)
        cp.start()
        cp.wait()

    return pl.pallas_call(
        body, in_specs=[HBM_SPEC], out_specs=HBM_SPEC, out_shape=jax.ShapeDtypeStruct((nchip, r, c_), g.dtype),
        scratch_shapes=[pltpu.SemaphoreType.DMA, pltpu.SemaphoreType.DMA], name=name)(g)


def chip_exchange(name, p):
    _, r, c_ = p.shape

    def body(p_ref, out_ref, send_sems, recv_sems):
        x, y, c = _position()
        copies = []
        for j, (px, py) in enumerate([(1 - x, y), (x, 1 - y), (1 - x, 1 - y)]):
            cp = pltpu.make_async_remote_copy(
                src_ref=p_ref.at[2 * px + py], dst_ref=out_ref.at[j], send_sem=send_sems.at[j],
                recv_sem=recv_sems.at[j], device_id=(px, py, c), device_id_type=MESH_ID)
            cp.start()
            copies.append(cp)
        for cp in copies:
            cp.wait()

    return pl.pallas_call(
        body, in_specs=[HBM_SPEC], out_specs=HBM_SPEC, out_shape=jax.ShapeDtypeStruct((3, r, c_), p.dtype),
        scratch_shapes=[pltpu.SemaphoreType.DMA((3,)), pltpu.SemaphoreType.DMA((3,))], name=name)(p)


def _sum_tile(r):
    return _divisor(r, 512, 8)


def pair_reduce(name, g, got, my_c):
    _, nchip, r, c_ = g.shape
    tile = _sum_tile(r)

    def body(c_ref, g_ref, got_ref, o_ref):
        o_ref[...] = g_ref[...] + got_ref[...]

    return pl.pallas_call(
        body,
        grid_spec=pltpu.PrefetchScalarGridSpec(
            num_scalar_prefetch=1, grid=(nchip, r // tile),
            in_specs=[pl.BlockSpec((None, None, tile, c_), lambda k, i, cc: (cc[0], k, i, 0)),
                      pl.BlockSpec((None, tile, c_), lambda k, i, cc: (k, i, 0))],
            out_specs=pl.BlockSpec((None, tile, c_), lambda k, i, cc: (k, i, 0))),
        out_shape=jax.ShapeDtypeStruct((nchip, r, c_), F32),
        compiler_params=_params(("arbitrary", "arbitrary")), name=name)(my_c.reshape(1).astype(jnp.int32), g, got)


def chip_reduce(name, p, got, my_chip):
    _, r, c_ = p.shape
    tile = _sum_tile(r)

    def body(k_ref, p_ref, got_ref, o_ref):
        o_ref[...] = ((p_ref[...] + got_ref[0]) + got_ref[1]) + got_ref[2]

    return pl.pallas_call(
        body,
        grid_spec=pltpu.PrefetchScalarGridSpec(
            num_scalar_prefetch=1, grid=(r // tile,),
            in_specs=[pl.BlockSpec((None, tile, c_), lambda i, kk: (kk[0], i, 0)),
                      pl.BlockSpec((3, tile, c_), lambda i, kk: (0, i, 0))],
            out_specs=pl.BlockSpec((tile, c_), lambda i, kk: (i, 0))),
        out_shape=jax.ShapeDtypeStruct((r, c_), F32),
        compiler_params=_params(("arbitrary",)), name=name)(my_chip.reshape(1).astype(jnp.int32), p, got)


def sum_blocks(name, a):
    n, r, c_ = a.shape
    tile = _sum_tile(r)

    def body(a_ref, o_ref):
        acc = a_ref[0]
        for k in range(1, n):
            acc = acc + a_ref[k]
        o_ref[...] = acc

    return pl.pallas_call(
        body, grid=(r // tile,), in_specs=[pl.BlockSpec((n, tile, c_), lambda i: (0, i, 0))],
        out_specs=pl.BlockSpec((tile, c_), lambda i: (i, 0)), out_shape=jax.ShapeDtypeStruct((r, c_), F32),
        compiler_params=_params(("arbitrary",)), name=name)(a)


PACK_COLS = 1024


def _pack(arrays):
    flat = jnp.concatenate([a.reshape(-1) for a in arrays])
    rows = -(-flat.shape[0] // PACK_COLS)
    rows += -rows % 8
    return jnp.pad(flat, (0, rows * PACK_COLS - flat.shape[0])).reshape(rows, PACK_COLS)


def _pack_blocked(arrays):
    flat = jnp.concatenate([a.reshape(N_DEV, -1) for a in arrays], axis=1)
    rows = -(-flat.shape[1] // PACK_COLS)
    rows += -rows % 8
    return jnp.pad(flat, ((0, 0), (0, rows * PACK_COLS - flat.shape[1]))).reshape(N_DEV, rows, PACK_COLS)


def _unpack(buf, shapes, lead=()):
    flat = buf.reshape(lead + (-1,))
    out, off = [], 0
    for s in shapes:
        n = 1
        for d in s:
            n *= d
        out.append(flat[..., off:off + n].reshape(lead + tuple(s)))
        off += n
    return out


def _join_shards(blocks, axis):
    ax = axis + 1
    moved = jnp.moveaxis(blocks, 0, ax)
    s = moved.shape
    return moved.reshape(s[:ax] + (s[ax] * s[ax + 1],) + s[ax + 2:])


def _split_shards(full, axis):
    ax = axis + 1
    s = full.shape
    cut = full.reshape(s[:ax] + (N_DEV, s[ax] // N_DEV) + s[ax + 1:])
    return jnp.moveaxis(cut, ax, 0)


def _pad_cols(w, n):
    return jnp.pad(w, ((0, 0), (0, n - w.shape[1])))


def _prep_layer(w):
    o = [0]
    for n in IN_SIZES:
        o.append(o[-1] + n)
    w_in = w["w_in"]
    kv_lat = o[5]
    q = w["mla_w_q_b"].reshape(MLA_Q_RANK, MLA_HEADS, MLA_NOPE + MLA_ROPE)
    kv = w["mla_w_kv_b"].reshape(MLA_KV_RANK, MLA_HEADS, MLA_NOPE + MLA_V)

    def row(v):
        return v.reshape(1, -1)

    def norm_pair(g):
        return _pad_cols(row(g), 2 * LANES)

    return {
        "mix_norm_g": row(w["mix_norm_g"]),
        "w_z": w_in[:, o[0]:o[1]], "w_xbc": w_in[:, o[1]:o[2]], "w_dt": _pad_cols(w_in[:, o[2]:o[3]], LANES),
        "w_glu": w_in[:, o[3]:o[4]], "w_q": w_in[:, o[4]:o[5]], "w_ckv": w_in[:, kv_lat:kv_lat + MLA_KV_RANK],
        "w_kr": _pad_cols(w_in[:, kv_lat + MLA_KV_RANK:o[6]], LANES), "w_gate": w_in[:, o[6]:o[7]],
        "ssd_conv_w": w["ssd_conv_w"], "ssd_conv_b": row(w["ssd_conv_b"]),
        "ssd_dt_bias": _pad_cols(row(w["ssd_dt_bias"]), LANES), "ssd_a_log": _pad_cols(row(w["ssd_a_log"]), LANES),
        "ssd_d": _pad_cols(row(w["ssd_d"]), LANES), "ssd_norm_g": row(w["ssd_norm_g"]), "ssd_w_out": w["ssd_w_out"],
        "conv_dw_w": w["conv_dw_w"], "conv_dw_b": row(w["conv_dw_b"]), "conv_ln_g": row(w["conv_ln_g"]),
        "conv_ln_b": row(w["conv_ln_b"]), "conv_w_out": w["conv_w_out"],
        "mla_q_a_g": row(w["mla_q_a_g"]), "mla_kv_a_g": row(w["mla_kv_a_g"]),
        "w_qn": q[:, :, :MLA_NOPE].reshape(MLA_Q_RANK, -1),
        "w_qr": jnp.pad(q[:, :, MLA_NOPE:], ((0, 0), (0, 0), (0, LANES - MLA_ROPE))).reshape(MLA_Q_RANK, -1),
        "w_kn": kv[:, :, :MLA_NOPE].reshape(MLA_KV_RANK, -1), "w_v": kv[:, :, MLA_NOPE:].reshape(MLA_KV_RANK, -1),
        "mla_q_norm_g": norm_pair(w["mla_q_norm_g"]), "mla_k_norm_g": norm_pair(w["mla_k_norm_g"]),
        "mla_w_o": w["mla_w_o"], "gate_b": row(w["gate_b"]), "w_out": w["w_out"],
        "xattn_norm_g": row(w["xattn_norm_g"]), "mem_norm_g": row(w["mem_norm_g"]), "xattn_w_q": w["xattn_w_q"],
        "w_xk": w["xattn_w_kv"][:, :D_MODEL], "w_xv": w["xattn_w_kv"][:, D_MODEL:],
        "xattn_q_norm_g": row(w["xattn_q_norm_g"]), "xattn_k_norm_g": row(w["xattn_k_norm_g"]),
        "xattn_w_o": w["xattn_w_o"], "ffn_norm_g": row(w["ffn_norm_g"]),
        "w_ffn_gate": w["ffn_w_in"][:, :FFN_HIDDEN], "w_ffn_up": w["ffn_w_in"][:, FFN_HIDDEN:],
        "ffn_w_out": w["ffn_w_out"],
    }


def _unprep_grads(g):
    n_dt, n_kr = IN_SIZES[2], MLA_ROPE
    qn = g["w_qn"].reshape(MLA_Q_RANK, MLA_HEADS, MLA_NOPE)
    qr = g["w_qr"].reshape(MLA_Q_RANK, MLA_HEADS, LANES)[:, :, :MLA_ROPE]
    kn = g["w_kn"].reshape(MLA_KV_RANK, MLA_HEADS, MLA_NOPE)
    vv = g["w_v"].reshape(MLA_KV_RANK, MLA_HEADS, MLA_V)
    flat = lambda v: v.reshape(-1)
    return {
        "mix_norm_g": flat(g["mix_norm_g"]),
        "w_in": jnp.concatenate([g["w_z"], g["w_xbc"], g["w_dt"][:, :n_dt], g["w_glu"], g["w_q"], g["w_ckv"],
                                 g["w_kr"][:, :n_kr], g["w_gate"]], axis=1),
        "ssd_conv_w": g["ssd_conv_w"], "ssd_conv_b": flat(g["ssd_conv_b"]),
        "ssd_dt_bias": flat(g["ssd_dt_bias"])[:SSD_HEADS], "ssd_a_log": flat(g["ssd_a_log"])[:SSD_HEADS],
        "ssd_d": flat(g["ssd_d"])[:SSD_HEADS], "ssd_norm_g": flat(g["ssd_norm_g"]), "ssd_w_out": g["ssd_w_out"],
        "conv_dw_w": g["conv_dw_w"], "conv_dw_b": flat(g["conv_dw_b"]), "conv_ln_g": flat(g["conv_ln_g"]),
        "conv_ln_b": flat(g["conv_ln_b"]), "conv_w_out": g["conv_w_out"],
        "mla_q_a_g": flat(g["mla_q_a_g"]), "mla_w_q_b": jnp.concatenate([qn, qr], axis=2).reshape(MLA_Q_RANK, -1),
        "mla_kv_a_g": flat(g["mla_kv_a_g"]), "mla_w_kv_b": jnp.concatenate([kn, vv], axis=2).reshape(MLA_KV_RANK, -1),
        "mla_q_norm_g": flat(g["mla_q_norm_g"])[:MLA_NOPE + MLA_ROPE],
        "mla_k_norm_g": flat(g["mla_k_norm_g"])[:MLA_NOPE + MLA_ROPE],
        "mla_w_o": g["mla_w_o"], "gate_b": g["gate_b"].reshape(3, D_MODEL), "w_out": g["w_out"],
        "xattn_norm_g": flat(g["xattn_norm_g"]), "mem_norm_g": flat(g["mem_norm_g"]), "xattn_w_q": g["xattn_w_q"],
        "xattn_w_kv": jnp.concatenate([g["w_xk"], g["w_xv"]], axis=1),
        "xattn_q_norm_g": flat(g["xattn_q_norm_g"]), "xattn_k_norm_g": flat(g["xattn_k_norm_g"]),
        "xattn_w_o": g["xattn_w_o"], "ffn_norm_g": flat(g["ffn_norm_g"]),
        "ffn_w_in": jnp.concatenate([g["w_ffn_gate"], g["w_ffn_up"]], axis=1), "ffn_w_out": g["ffn_w_out"],
    }


def _layer(l, x, mem, cosf, sinf, w):
    t = x.shape[0]
    n = lambda s: f"l{l}_{s}"
    tile = min(ROW_TILE, t)
    grid = (1, t // tile)

    def rowwise(name, f, ins, width):
        return tmap(n(name), f, grid, ins, [_row_out(t, width, tile)])[0]

    u = rms_norm(n("mix_norm"), x, w["mix_norm_g"])
    z = matmul(n("in_z"), u, w["w_z"])
    xbc = matmul(n("in_xbc"), u, w["w_xbc"])
    dtr = matmul(n("in_dt"), u, w["w_dt"])
    glu = matmul(n("in_glu"), u, w["w_glu"])
    q_lat = matmul(n("in_q"), u, w["w_q"])
    c_kv = matmul(n("in_ckv"), u, w["w_ckv"])
    kr_raw = matmul(n("in_kr"), u, w["w_kr"])
    gate_logits = matmul(n("in_gate"), u, w["w_gate"])

    xc = dwconv(n("ssd_conv"), xbc, w["ssd_conv_w"], w["ssd_conv_b"])
    y_scan = ssd_scan(n("ssd_scan"), xc, dtr, w["ssd_dt_bias"], w["ssd_a_log"], w["ssd_d"])
    y_norm = rowwise("ssd_gate_norm", _ssd_gate_norm_f, [_rows(y_scan, tile), _rows(z, tile), _whole(w["ssd_norm_g"])],
                     SSD_INNER)
    y_ssd = matmul(n("ssd_out"), y_norm, w["ssd_w_out"])

    v = rowwise("glu", _glu_f, [_rows(glu, tile)], D_MODEL)
    v = dwconv(n("conv_dw"), v, w["conv_dw_w"], w["conv_dw_b"])
    v = rowwise("conv_ln_silu", _ln_silu_f, [_rows(v, tile), _whole(w["conv_ln_g"]), _whole(w["conv_ln_b"])], D_MODEL)
    y_conv = matmul(n("conv_out"), v, w["conv_w_out"])

    q_n = rms_norm(n("q_a_norm"), q_lat, w["mla_q_a_g"])
    qn_raw = matmul(n("q_nope"), q_n, w["w_qn"])
    qr_raw = matmul(n("q_rope"), q_n, w["w_qr"])
    c_n = rms_norm(n("kv_a_norm"), c_kv, w["mla_kv_a_g"])
    kn_raw = matmul(n("k_nope"), c_n, w["w_kn"])
    val = matmul(n("mla_v"), c_n, w["w_v"])
    head_rows = lambda arr, tl, kind="t": (arr, (tl, LANES), lambda o, i: (i, o), kind)
    head_full = lambda arr: (arr, (t, LANES), lambda o, i: (0, o), "ai")
    kn = tmap(n("k_nope_norm"), _k_nope_f, (MLA_HEADS, t // tile), [head_rows(kn_raw, tile), _whole(w["mla_k_norm_g"])],
              [((t, MLA_HEADS * MLA_NOPE), (tile, LANES), lambda o, i: (i, o))])[0]
    kr = rowwise("k_rope", _k_rope_f, [_rows(kr_raw, tile), _rows(cosf, tile, "n"), _rows(sinf, tile, "n"),
                                       _whole(w["mla_k_norm_g"])], LANES)
    tq = min(ATT_Q_TILE, t)
    att = tmap(n("mla_attn"), _mla_attn_f, (MLA_HEADS, t // tq),
               [head_rows(qn_raw, tq), head_rows(qr_raw, tq), _rows(cosf, tq, "n"), _rows(sinf, tq, "n"), head_full(kn),
                _whole(kr), head_full(val), _whole(w["mla_q_norm_g"])],
               [((t, MLA_HEADS * MLA_V), (tq, LANES), lambda o, i: (i, o))])[0]
    y_mla = matmul(n("mla_out"), att, w["mla_w_o"])

    merged = rowwise("merge", _merge_f, [_rows(gate_logits, tile), _whole(w["gate_b"]), _rows(y_ssd, tile),
                                         _rows(y_conv, tile), _rows(y_mla, tile)], D_MODEL)
    x = matmul(n("mix_out"), merged, w["w_out"], res=x)

    h = rms_norm(n("xattn_norm"), x, w["xattn_norm_g"])
    mem_n = rms_norm_nograd_x(n("mem_norm"), mem, w["mem_norm_g"])
    xq = matmul(n("xattn_q"), h, w["xattn_w_q"])
    xk = matmul(n("xattn_k"), mem_n, w["w_xk"])
    xv = matmul(n("xattn_v"), mem_n, w["w_xv"])
    m = mem.shape[0]
    txq = min(XATT_Q_TILE, t)
    kv_head = lambda arr: (arr, (m, X_HEAD_DIM), lambda o, i: (0, o), "ai")
    xo = tmap(n("xattn"), _xattn_f, (X_HEADS, t // txq),
              [(xq, (txq, X_HEAD_DIM), lambda o, i: (i, o), "t"), kv_head(xk), kv_head(xv),
               _whole(w["xattn_q_norm_g"]), _whole(w["xattn_k_norm_g"])],
              [((t, D_MODEL), (txq, X_HEAD_DIM), lambda o, i: (i, o))])[0]
    x = matmul(n("xattn_out"), xo, w["xattn_w_o"], res=x)

    h = rms_norm(n("ffn_norm"), x, w["ffn_norm_g"])
    gate = matmul(n("ffn_gate"), h, w["w_ffn_gate"])
    up = matmul(n("ffn_up"), h, w["w_ffn_up"])
    act = rowwise("swiglu", _swiglu_f, [_rows(gate, tile), _rows(up, tile)], FFN_HIDDEN)
    return matmul(n("ffn_out"), act, w["ffn_w_out"], res=x)


def _rope_tables(positions):
    inv = ROPE_THETA ** (-jnp.arange(0, MLA_ROPE, 2, dtype=F32) / MLA_ROPE)
    ang = positions.astype(F32)[:, None] * inv
    pad = jnp.zeros((positions.shape[0], LANES - MLA_ROPE), F32)
    cos, sin = jnp.cos(ang), jnp.sin(ang)
    return jnp.concatenate([cos, cos, pad], axis=1), jnp.concatenate([sin, sin, pad], axis=1)


def local_step(x, mem, positions, target, weights):
    cosf, sinf = _rope_tables(positions)
    prepped = [_prep_layer({k: v[l] for k, v in weights.items()}) for l in range(DEPTH)]

    def forward(x, prepped):
        for l in range(DEPTH):
            x = _layer(l, x, mem, cosf, sinf, prepped[l])
        return x

    y, pull = jax.vjp(forward, x, prepped)
    sq, dy = loss_head(y, target)
    gx, gp = pull(dy)
    per_layer = [_unprep_grads(g) for g in gp]
    grads = {k: jnp.stack([pl_[k] for pl_ in per_layer]) for k in WEIGHTS}
    return sq, gx, grads


def _step(x, mem, positions, loss_target, w, m, v):
    xi, yi, ci = _position()
    sharded = [n for n in WEIGHTS if n in SHARDED]

    shard_shapes = [w[n].shape for n in sharded]
    gathered = all_gather("weights_all_gather", _pack([w[n] for n in sharded]))
    blocks = _unpack(gathered, shard_shapes, lead=(N_DEV,))
    full = {n: w[n] for n in REPLICATED}
    for n, b in zip(sharded, blocks):
        full[n] = _join_shards(b, SHARDED[n])

    sq, gx, grads = local_step(x[0], mem[0], positions[0], loss_target[0], full)
    loss = lax.psum(0.5 * jnp.sum(sq) / D_MODEL, ("x", "y", "c"))

    blocked = _pack_blocked([_split_shards(grads[n], SHARDED[n]) for n in sharded])
    rows = blocked.shape[1]
    by_core = blocked.reshape(2, 2, 2, rows, PACK_COLS).transpose(2, 0, 1, 3, 4).reshape(2, 4, rows, PACK_COLS)
    from_sibling = pair_exchange("grads_pair_exchange", by_core)
    chip_partial = pair_reduce("grads_pair_reduce", by_core, from_sibling, ci)
    from_chips = chip_exchange("grads_chip_exchange", chip_partial)
    reduced = chip_reduce("grads_chip_reduce", chip_partial, from_chips, 2 * xi + yi)
    g_shard = dict(zip(sharded, _unpack(reduced, shard_shapes)))

    rep_shapes = [w[n].shape for n in REPLICATED]
    rep_all = all_gather("small_grads_all_gather", _pack([grads[n] for n in REPLICATED]))
    g_rep = dict(zip(REPLICATED, _unpack(sum_blocks("small_grads_sum", rep_all), rep_shapes)))

    out_g, out_d, out_m, out_v = [], [], [], []
    for n in WEIGHTS:
        g = g_shard[n] if n in SHARDED else g_rep[n]
        d, nm, nv = adamw("adamw_" + n, w[n], g, m[n], v[n])
        out_g.append(g)
        out_d.append(d)
        out_m.append(nm)
        out_v.append(nv)
    return (loss, gx[None], *out_g, *out_d, *out_m, *out_v)


def kernel(x, mem, positions, mix_norm_g, w_in, ssd_conv_w, ssd_conv_b, ssd_dt_bias, ssd_a_log, ssd_d, ssd_norm_g, ssd_w_out, conv_dw_w, conv_dw_b, conv_ln_g, conv_ln_b, conv_w_out, mla_q_a_g, mla_w_q_b, mla_kv_a_g, mla_w_kv_b, mla_q_norm_g, mla_k_norm_g, mla_w_o, gate_b, w_out, xattn_norm_g, mem_norm_g, xattn_w_q, xattn_w_kv, xattn_q_norm_g, xattn_k_norm_g, xattn_w_o, ffn_norm_g, ffn_w_in, ffn_w_out, loss_target, m_mix_norm_g, m_w_in, m_ssd_conv_w, m_ssd_conv_b, m_ssd_dt_bias, m_ssd_a_log, m_ssd_d, m_ssd_norm_g, m_ssd_w_out, m_conv_dw_w, m_conv_dw_b, m_conv_ln_g, m_conv_ln_b, m_conv_w_out, m_mla_q_a_g, m_mla_w_q_b, m_mla_kv_a_g, m_mla_w_kv_b, m_mla_q_norm_g, m_mla_k_norm_g, m_mla_w_o, m_gate_b, m_w_out, m_xattn_norm_g, m_mem_norm_g, m_xattn_w_q, m_xattn_w_kv, m_xattn_q_norm_g, m_xattn_k_norm_g, m_xattn_w_o, m_ffn_norm_g, m_ffn_w_in, m_ffn_w_out, v_mix_norm_g, v_w_in, v_ssd_conv_w, v_ssd_conv_b, v_ssd_dt_bias, v_ssd_a_log, v_ssd_d, v_ssd_norm_g, v_ssd_w_out, v_conv_dw_w, v_conv_dw_b, v_conv_ln_g, v_conv_ln_b, v_conv_w_out, v_mla_q_a_g, v_mla_w_q_b, v_mla_kv_a_g, v_mla_w_kv_b, v_mla_q_norm_g, v_mla_k_norm_g, v_mla_w_o, v_gate_b, v_w_out, v_xattn_norm_g, v_mem_norm_g, v_xattn_w_q, v_xattn_w_kv, v_xattn_q_norm_g, v_xattn_k_norm_g, v_xattn_w_o, v_ffn_norm_g, v_ffn_w_in, v_ffn_w_out):
    args = locals()
    w = {n: args[n] for n in WEIGHTS}
    m = {n: args["m_" + n] for n in WEIGHTS}
    v = {n: args["v_" + n] for n in WEIGHTS}
    return _step(x, mem, positions, loss_target, w, m, v)
```

```python
import functools

import jax
import jax.numpy as jnp
from jax import lax
from jax.experimental import pallas as pl
from jax.experimental.pallas import tpu as pltpu

F32 = jnp.float32
BF16 = jnp.bfloat16
HIGHEST = lax.Precision.HIGHEST
MESH_ID = pl.DeviceIdType.MESH

VMEM_LIMIT_BYTES = 56 * 1024 * 1024
LANES = 128

EPS = 1e-6
DEPTH = 4
D_MODEL = 1024
N_DEV = 8
SSD_HEADS = 16
SSD_HEAD_DIM = 64
SSD_STATE = 128
SSD_GROUPS = 4
SSD_INNER = 1024
SSD_TILE = 256
CONV_K = 31
SSD_CONV_K = 4
MLA_HEADS = 8
MLA_NOPE = 128
MLA_ROPE = 64
MLA_V = 128
MLA_Q_RANK = 384
MLA_KV_RANK = 256
ATT_CHUNK = 64
ROPE_THETA = 10000.0
X_HEADS = 4
X_HEAD_DIM = 256
FFN_HIDDEN = 2816
IN_SIZES = (1024, 2048, 16, 2048, 384, 320, 3072)

ADAM_LR = 0.001
ADAM_B1 = 0.9
ADAM_B2 = 0.999
ADAM_EPS = 1e-08
ADAM_WD = 0.01
ADAM_STEP = 10

ROW_TILE = 256
ATT_BLOCK = 256
XATT_Q_TILE = 512

SHARDED = {
    "w_in": 1, "ssd_conv_w": 1, "ssd_w_out": 0, "conv_dw_w": 1, "conv_w_out": 0, "mla_w_q_b": 1, "mla_w_kv_b": 1,
    "mla_w_o": 0, "gate_b": 1, "w_out": 0, "xattn_w_q": 0, "xattn_w_kv": 1, "xattn_w_o": 0, "ffn_w_in": 1,
    "ffn_w_out": 0,
}
WEIGHTS = ["mix_norm_g", "w_in", "ssd_conv_w", "ssd_conv_b", "ssd_dt_bias", "ssd_a_log", "ssd_d", "ssd_norm_g",
           "ssd_w_out", "conv_dw_w", "conv_dw_b", "conv_ln_g", "conv_ln_b", "conv_w_out", "mla_q_a_g", "mla_w_q_b",
           "mla_kv_a_g", "mla_w_kv_b", "mla_q_norm_g", "mla_k_norm_g", "mla_w_o", "gate_b", "w_out", "xattn_norm_g",
           "mem_norm_g", "xattn_w_q", "xattn_w_kv", "xattn_q_norm_g", "xattn_k_norm_g", "xattn_w_o", "ffn_norm_g",
           "ffn_w_in", "ffn_w_out"]
REPLICATED = [n for n in WEIGHTS if n not in SHARDED]


def _params(sem=None):
    return pltpu.CompilerParams(dimension_semantics=sem, vmem_limit_bytes=VMEM_LIMIT_BYTES)


def _divisor(n, cap, mult):
    if n <= cap:
        return n
    for d in range(cap - cap % mult, 0, -mult):
        if n % d == 0:
            return d
    raise ValueError(f"no tile for {n}")


def _dg(a, b, ca, cb):
    return lax.dot_general(a.astype(BF16), b.astype(BF16), (((ca,), (cb,)), ((), ())), preferred_element_type=F32)


@jax.custom_vjp
def bdot(a, b):
    return _dg(a, b, 1, 0)


bdot.defvjp(lambda a, b: (_dg(a, b, 1, 0), (a, b)), lambda r, g: (_dg(g, r[1], 1, 1), _dg(r[0], g, 0, 0)))


@jax.custom_vjp
def bdot_nt(a, b):
    return _dg(a, b, 1, 1)


bdot_nt.defvjp(lambda a, b: (_dg(a, b, 1, 1), (a, b)), lambda r, g: (_dg(g, r[1], 1, 0), _dg(g, r[0], 0, 0)))


@jax.custom_vjp
def bdot_tn(a, b):
    return _dg(a, b, 0, 0)


bdot_tn.defvjp(lambda a, b: (_dg(a, b, 0, 0), (a, b)), lambda r, g: (_dg(r[1], g, 1, 1), _dg(r[0], g, 1, 0)))


def hdot(a, b):
    return jnp.dot(a, b, precision=HIGHEST, preferred_element_type=F32)


def _iota(shape, dim):
    return lax.broadcasted_iota(jnp.int32, shape, dim)


def _mm(name, a, b, ta=False, tb=False, res=None):
    m, k = (a.shape[1], a.shape[0]) if ta else a.shape
    n = b.shape[0] if tb else b.shape[1]
    tm = _divisor(m, 1024, LANES if ta else 8)
    tn = _divisor(n, 1536, LANES)
    tk = _divisor(k, 512, LANES)
    nk = k // tk
    dims = (((0 if ta else 1,), (1 if tb else 0,)), ((), ()))

    def body(*refs):
        if res is None:
            a_ref, b_ref, o_ref, acc = refs
        else:
            a_ref, b_ref, r_ref, o_ref, acc = refs
        kk = pl.program_id(2)

        @pl.when(kk == 0)
        def _():
            acc[...] = jnp.zeros_like(acc)

        acc[...] += lax.dot_general(a_ref[...].astype(BF16), b_ref[...].astype(BF16), dims,
                                    preferred_element_type=F32)

        @pl.when(kk == nk - 1)
        def _():
            o_ref[...] = acc[...] if res is None else acc[...] + r_ref[...]

    a_spec = pl.BlockSpec((tk, tm), lambda i, j, kk: (kk, i)) if ta else pl.BlockSpec((tm, tk), lambda i, j, kk: (i, kk))
    b_spec = pl.BlockSpec((tn, tk), lambda i, j, kk: (j, kk)) if tb else pl.BlockSpec((tk, tn), lambda i, j, kk: (kk, j))
    o_spec = pl.BlockSpec((tm, tn), lambda i, j, kk: (i, j))
    in_specs = [a_spec, b_spec] + ([] if res is None else [o_spec])
    args = (a, b) + (() if res is None else (res,))
    return pl.pallas_call(
        body, grid=(m // tm, n // tn, nk), in_specs=in_specs, out_specs=o_spec,
        out_shape=jax.ShapeDtypeStruct((m, n), F32), scratch_shapes=[pltpu.VMEM((tm, tn), F32)],
        compiler_params=_params(("parallel", "parallel", "arbitrary")), name=name)(*args)


def matmul(name, a, w, res=None):
    if res is None:
        @jax.custom_vjp
        def run(a, w):
            return _mm(name, a, w)

        def fwd(a, w):
            return run(a, w), (a, w)

        def bwd(r, g):
            return _mm(name + "_da", g, r[1], tb=True), _mm(name + "_dw", r[0], g, ta=True)

        run.defvjp(fwd, bwd)
        return run(a, w)

    @jax.custom_vjp
    def run_res(a, w, res):
        return _mm(name, a, w, res=res)

    def fwd_res(a, w, res):
        return run_res(a, w, res), (a, w)

    def bwd_res(r, g):
        return _mm(name + "_da", g, r[1], tb=True), _mm(name + "_dw", r[0], g, ta=True), g

    run_res.defvjp(fwd_res, bwd_res)
    return run_res(a, w, res)


def tmap(name, f, grid, ins, outs):
    arrays = [x[0] for x in ins]
    kinds = [x[3] for x in ins]
    in_specs = [pl.BlockSpec(x[1], x[2]) for x in ins]
    out_specs = [pl.BlockSpec(x[1], x[2]) for x in outs]
    out_shape = [jax.ShapeDtypeStruct(x[0], F32) for x in outs]
    n_in, n_out = len(ins), len(outs)
    didx = [k for k, kd in enumerate(kinds) if kd != "n"]

    def fwd_call(*arrs):
        def body(*refs):
            pids = (pl.program_id(0), pl.program_id(1))
            vals = f(pids, *[r[...] for r in refs[:n_in]])
            for r, v in zip(refs[n_in:], vals):
                r[...] = v

        return pl.pallas_call(body, grid=grid, in_specs=in_specs, out_specs=out_specs, out_shape=out_shape,
                              compiler_params=_params(("arbitrary", "arbitrary")), name=name)(*arrs)

    def bwd_call(arrs, cts):
        def body(*refs):
            o, i = pl.program_id(0), pl.program_id(1)
            vals = [r[...] for r in refs[:n_in]]

            def g(*dv):
                full = list(vals)
                for k, v in zip(didx, dv):
                    full[k] = v
                return tuple(f((o, i), *full))

            _, vjp = jax.vjp(g, *[vals[k] for k in didx])
            grads = vjp(tuple(r[...] for r in refs[n_in:n_in + n_out]))
            for k, gr, r in zip(didx, grads, refs[n_in + n_out:]):
                if kinds[k] == "t":
                    r[...] = gr
                else:
                    first = (i == 0) if kinds[k] == "ai" else jnp.logical_and(o == 0, i == 0)

                    @pl.when(first)
                    def _(r=r, gr=gr):
                        r[...] = gr

                    @pl.when(jnp.logical_not(first))
                    def _(r=r, gr=gr):
                        r[...] += gr

        g_specs = [in_specs[k] for k in didx]
        g_shape = [jax.ShapeDtypeStruct(arrs[k].shape, F32) for k in didx]
        return pl.pallas_call(body, grid=grid, in_specs=in_specs + out_specs, out_specs=g_specs, out_shape=g_shape,
                              compiler_params=_params(("arbitrary", "arbitrary")), name=name + "_bwd")(*arrs, *cts)

    @jax.custom_vjp
    def run(*arrs):
        return tuple(fwd_call(*arrs))

    def run_fwd(*arrs):
        return run(*arrs), arrs

    def run_bwd(arrs, cts):
        gs = bwd_call(arrs, cts)
        full = [None] * n_in
        for k, g in zip(didx, gs):
            full[k] = g
        return tuple(full)

    run.defvjp(run_fwd, run_bwd)
    return run(*arrays)


def _rows(arr, tile, kind="t"):
    return (arr, (tile, arr.shape[1]), lambda o, i: (i, 0), kind)


def _whole(arr, kind="ag"):
    return (arr, arr.shape, lambda o, i: (0, 0), kind)


def _row_out(t, n, tile):
    return ((t, n), (tile, n), lambda o, i: (i, 0))


def _rms(x, g, n=None):
    ms = jnp.sum(x * x, axis=-1, keepdims=True) / (x.shape[-1] if n is None else n)
    return x * lax.rsqrt(ms + EPS) * g


def rms_norm(name, x, g):
    t, n = x.shape
    tile = min(ROW_TILE, t)
    return tmap(name, lambda p, x, g: (_rms(x, g),), (1, t // tile), [_rows(x, tile), _whole(g)],
                [_row_out(t, n, tile)])[0]


def rms_norm_nograd_x(name, x, g):
    t, n = x.shape
    tile = min(ROW_TILE, t)
    return tmap(name, lambda p, x, g: (_rms(x, g),), (1, t // tile), [_rows(x, tile, "n"), _whole(g)],
                [_row_out(t, n, tile)])[0]


def _glu_f(p, glu):
    h = glu.shape[1] // 2
    return (glu[:, :h] * jax.nn.sigmoid(glu[:, h:]),)


def _ln_silu_f(p, v, g, b):
    mu = jnp.mean(v, axis=-1, keepdims=True)
    xc = v - mu
    var = jnp.mean(xc * xc, axis=-1, keepdims=True)
    return (jax.nn.silu(xc * lax.rsqrt(var + EPS) * g + b),)


def _ssd_gate_norm_f(p, y, z, g):
    v = y * jax.nn.silu(z)
    w = SSD_INNER // SSD_GROUPS
    parts = []
    for k in range(SSD_GROUPS):
        vg = v[:, k * w:(k + 1) * w]
        parts.append(vg * lax.rsqrt(jnp.mean(vg * vg, axis=-1, keepdims=True) + EPS))
    return (jnp.concatenate(parts, axis=1) * g,)


def _merge_f(p, gl, gb, y0, y1, y2):
    g = jax.nn.sigmoid(gl + gb)
    d = D_MODEL
    return (g[:, :d] * y0 + g[:, d:2 * d] * y1 + g[:, 2 * d:] * y2,)


def _swiglu_f(p, gate, up):
    return (jax.nn.silu(gate) * up,)


def _rot_matrix():
    r, c = _iota((LANES, LANES), 0), _iota((LANES, LANES), 1)
    h = MLA_ROPE // 2
    plus = jnp.logical_and(c >= h, jnp.logical_and(c < 2 * h, r == c - h))
    minus = jnp.logical_and(c < h, r == c + h)
    return plus.astype(F32) - minus.astype(F32)


def _rope(x, cosf, sinf):
    return x * cosf + hdot(x, _rot_matrix()) * sinf


def _per_head(f, x):
    return jnp.concatenate([f(x[:, h * LANES:(h + 1) * LANES]) for h in range(x.shape[1] // LANES)], axis=1)


def _k_nope_f(p, kn_raw, kg):
    return (_per_head(lambda x: _rms(x, kg[:, :MLA_NOPE]), kn_raw),)


def _k_rope_f(p, kr_raw, cosf, sinf, kg):
    return (_rope(_rms(kr_raw, kg[:, MLA_NOPE:], n=MLA_ROPE), cosf, sinf),)


def _q_prep_f(p, qn_raw, qr_raw, cosf, sinf, qg):
    qn = _per_head(lambda x: _rms(x, qg[:, :MLA_NOPE]), qn_raw)
    qr = _per_head(lambda x: _rope(_rms(x, qg[:, MLA_NOPE:], n=MLA_ROPE), cosf, sinf), qr_raw)
    return qn, qr


def _softmax(s):
    m = jnp.max(s, axis=-1, keepdims=True)
    e = jnp.exp(s - m)
    return e / jnp.sum(e, axis=-1, keepdims=True)


def _xattn_f(p, q, k, v, qg, kg):
    s = bdot_nt(_rms(q, qg), _rms(k, kg)) * (X_HEAD_DIM ** -0.5)
    return (bdot(_softmax(s), v),)


ATT_SCALE = (MLA_NOPE + MLA_ROPE) ** -0.5
NT_DIMS = (((1,), (1,)), ((), ()))
NN_DIMS = (((1,), (0,)), ((), ()))
TN_DIMS = (((0,), (0,)), ((), ()))


def _att_specs(t, blk):
    q_spec = pl.BlockSpec((blk, LANES), lambda h, i: (i, h))
    k_spec = pl.BlockSpec((t, LANES), lambda h, i: (0, h))
    shared = pl.BlockSpec((t, LANES), lambda h, i: (0, 0))
    lse_spec = pl.BlockSpec((None, blk, 1), lambda h, i: (h, i, 0))
    return q_spec, k_spec, shared, lse_spec


def _diagonal_mask(blk):
    return (_iota((blk, blk), 1) // ATT_CHUNK) <= (_iota((blk, blk), 0) // ATT_CHUNK)


def _att_keys(kn_ref, kr_ref, j, blk):
    ks = pl.ds(pl.multiple_of(j * blk, blk), blk)
    return ks, jnp.concatenate([kn_ref[ks, :], kr_ref[ks, :]], axis=1).astype(BF16)


def _att_fwd(name, qn, qr, kn, kr, v):
    t, width = qn.shape
    heads = width // LANES
    blk = min(ATT_BLOCK, t)

    def body(qn_ref, qr_ref, kn_ref, kr_ref, v_ref, o_ref, lse_ref):
        i = pl.program_id(1)
        q = jnp.concatenate([qn_ref[...], qr_ref[...]], axis=1).astype(BF16)

        def block(j, carry, masked):
            m, l, acc = carry
            ks, k = _att_keys(kn_ref, kr_ref, j, blk)
            s = lax.dot_general(q, k, NT_DIMS, preferred_element_type=F32) * ATT_SCALE
            if masked:
                s = jnp.where(_diagonal_mask(blk), s, -1e30)
            m_new = jnp.maximum(m, jnp.max(s, axis=1, keepdims=True))
            alpha = jnp.exp(m - m_new)
            p = jnp.exp(s - m_new)
            l = alpha * l + jnp.sum(p, axis=1, keepdims=True)
            acc = alpha * acc + lax.dot_general(p.astype(BF16), v_ref[ks, :].astype(BF16), NN_DIMS,
                                                preferred_element_type=F32)
            return m_new, l, acc

        init = (jnp.full((blk, 1), -1e30, F32), jnp.zeros((blk, 1), F32), jnp.zeros((blk, LANES), F32))
        carry = lax.fori_loop(0, i, lambda j, c: block(j, c, False), init)
        m, l, acc = block(i, carry, True)
        o_ref[...] = acc / l
        lse_ref[...] = m + jnp.log(l)

    q_spec, k_spec, shared, lse_spec = _att_specs(t, blk)
    return pl.pallas_call(
        body, grid=(heads, t // blk), in_specs=[q_spec, q_spec, k_spec, shared, k_spec], out_specs=[q_spec, lse_spec],
        out_shape=[jax.ShapeDtypeStruct((t, width), F32), jax.ShapeDtypeStruct((heads, t, 1), F32)],
        compiler_params=_params(("arbitrary", "arbitrary")), name=name)(qn, qr, kn, kr, v)


def _att_bwd(name, qn, qr, kn, kr, v, o, lse, do):
    t, width = qn.shape
    heads = width // LANES
    blk = min(ATT_BLOCK, t)

    def body(qn_ref, qr_ref, kn_ref, kr_ref, v_ref, o_ref, lse_ref, do_ref, dqn_ref, dqr_ref, dkn_ref, dkr_ref,
             dv_ref):
        h, i = pl.program_id(0), pl.program_id(1)

        @pl.when(i == 0)
        def _():
            dkn_ref[...] = jnp.zeros_like(dkn_ref)
            dv_ref[...] = jnp.zeros_like(dv_ref)

        @pl.when(jnp.logical_and(h == 0, i == 0))
        def _():
            dkr_ref[...] = jnp.zeros_like(dkr_ref)

        q = jnp.concatenate([qn_ref[...], qr_ref[...]], axis=1).astype(BF16)
        do = do_ref[...]
        do16 = do.astype(BF16)
        delta = jnp.sum(do * o_ref[...], axis=1, keepdims=True)
        lse = lse_ref[...]

        def block(j, dq, masked):
            ks, k = _att_keys(kn_ref, kr_ref, j, blk)
            s = lax.dot_general(q, k, NT_DIMS, preferred_element_type=F32) * ATT_SCALE
            if masked:
                s = jnp.where(_diagonal_mask(blk), s, -1e30)
            p = jnp.exp(s - lse)
            dv_ref[ks, :] += lax.dot_general(p.astype(BF16), do16, TN_DIMS, preferred_element_type=F32)
            dp = lax.dot_general(do16, v_ref[ks, :].astype(BF16), NT_DIMS, preferred_element_type=F32)
            ds = (p * (dp - delta) * ATT_SCALE).astype(BF16)
            dk = lax.dot_general(ds, q, TN_DIMS, preferred_element_type=F32)
            dkn_ref[ks, :] += dk[:, :LANES]
            dkr_ref[ks, :] += dk[:, LANES:]
            return dq + lax.dot_general(ds, k, NN_DIMS, preferred_element_type=F32)

        dq = lax.fori_loop(0, i, lambda j, c: block(j, c, False), jnp.zeros((blk, 2 * LANES), F32))
        dq = block(i, dq, True)
        dqn_ref[...] = dq[:, :LANES]
        dqr_ref[...] = dq[:, LANES:]

    q_spec, k_spec, shared, lse_spec = _att_specs(t, blk)
    big, one = jax.ShapeDtypeStruct((t, width), F32), jax.ShapeDtypeStruct((t, LANES), F32)
    return pl.pallas_call(
        body, grid=(heads, t // blk),
        in_specs=[q_spec, q_spec, k_spec, shared, k_spec, q_spec, lse_spec, q_spec],
        out_specs=[q_spec, q_spec, k_spec, shared, k_spec], out_shape=[big, big, big, one, big],
        compiler_params=_params(("arbitrary", "arbitrary")), name=name)(qn, qr, kn, kr, v, o, lse, do)


def mla_attention(name, qn, qr, kn, kr, v):
    @jax.custom_vjp
    def run(qn, qr, kn, kr, v):
        return _att_fwd(name, qn, qr, kn, kr, v)[0]

    def fwd(qn, qr, kn, kr, v):
        o, lse = _att_fwd(name, qn, qr, kn, kr, v)
        return o, (qn, qr, kn, kr, v, o, lse)

    def bwd(r, g):
        return tuple(_att_bwd(name + "_bwd", *r, g))

    run.defvjp(fwd, bwd)
    return run(qn, qr, kn, kr, v)


def _shift_down(v, s, rows):
    return v if s == 0 else jnp.where(rows >= s, pltpu.roll(v, s, 0), 0.0)


def _shift_up(v, s, rows):
    t = v.shape[0]
    return v if s == 0 else jnp.where(rows < t - s, pltpu.roll(v, t - s, 0), 0.0)


def _dwconv_fwd(name, x, w, b):
    t, c = x.shape
    kw = w.shape[0]

    def body(x_ref, w_ref, b_ref, y_ref):
        x = x_ref[...]
        rows = _iota(x.shape, 0)
        acc = jnp.zeros_like(x) + b_ref[...]
        for k in range(kw):
            acc = acc + w_ref[k:k + 1, :] * _shift_down(x, kw - 1 - k, rows)
        y_ref[...] = acc

    col = lambda i: (0, i)
    return pl.pallas_call(
        body, grid=(c // LANES,),
        in_specs=[pl.BlockSpec((t, LANES), col), pl.BlockSpec((kw, LANES), col), pl.BlockSpec((1, LANES), col)],
        out_specs=pl.BlockSpec((t, LANES), col), out_shape=jax.ShapeDtypeStruct((t, c), F32),
        compiler_params=_params(("arbitrary",)), name=name)(x, w, b)


def _dwconv_bwd(name, x, w, dy):
    t, c = x.shape
    kw = w.shape[0]

    def body(x_ref, w_ref, dy_ref, dx_ref, dw_ref, db_ref):
        x, dy = x_ref[...], dy_ref[...]
        rows = _iota(x.shape, 0)
        dx = jnp.zeros_like(x)
        for k in range(kw):
            s = kw - 1 - k
            dx = dx + w_ref[k:k + 1, :] * _shift_up(dy, s, rows)
            dw_ref[k:k + 1, :] = jnp.sum(dy * _shift_down(x, s, rows), axis=0, keepdims=True)
        dx_ref[...] = dx
        db_ref[...] = jnp.sum(dy, axis=0, keepdims=True)

    col = lambda i: (0, i)
    big, wsp, bsp = pl.BlockSpec((t, LANES), col), pl.BlockSpec((kw, LANES), col), pl.BlockSpec((1, LANES), col)
    return pl.pallas_call(
        body, grid=(c // LANES,), in_specs=[big, wsp, big], out_specs=[big, wsp, bsp],
        out_shape=[jax.ShapeDtypeStruct((t, c), F32), jax.ShapeDtypeStruct((kw, c), F32),
                   jax.ShapeDtypeStruct((1, c), F32)],
        compiler_params=_params(("arbitrary",)), name=name)(x, w, dy)


def dwconv(name, x, w, b):
    @jax.custom_vjp
    def run(x, w, b):
        return _dwconv_fwd(name, x, w, b)

    def fwd(x, w, b):
        return run(x, w, b), (x, w)

    def bwd(r, g):
        return tuple(_dwconv_bwd(name + "_bwd", r[0], r[1], g))

    run.defvjp(fwd, bwd)
    return run(x, w, b)


def _ssd_tile(xc, dtr, dtb, alog, dsk, prev):
    ln = xc.shape[0]
    gw = SSD_INNER // SSD_GROUPS
    ns = SSD_STATE
    xs = jax.nn.silu(xc[:, :SSD_INNER])
    bm = jax.nn.silu(xc[:, SSD_INNER:SSD_INNER + SSD_GROUPS * ns])
    cm = jax.nn.silu(xc[:, SSD_INNER + SSD_GROUPS * ns:])
    dt = jax.nn.softplus(dtr + dtb)
    a = dt * (-jnp.exp(alog))
    expand = (_iota((LANES, SSD_INNER), 0) == _iota((LANES, SSD_INNER), 1) // SSD_HEAD_DIM).astype(F32)
    causal = _iota((ln, ln), 0) >= _iota((ln, ln), 1)
    acs_h = hdot(causal.astype(F32), a)
    acs_c = hdot(acs_h, expand)
    dt_c = hdot(dt, expand)

    def row_per_column(v):
        return jnp.mean(hdot(jnp.broadcast_to(v, (8, LANES)), expand), axis=0, keepdims=True)

    aend_c = row_per_column(jnp.sum(a, axis=0, keepdims=True))
    xdt = xs * dt_c
    to_end = xdt * jnp.exp(aend_c - acs_c)
    from_start = jnp.exp(acs_c)
    acs_ht = acs_h.T
    lane_h, sub_h = _iota((1, LANES), 1), _iota((LANES, 1), 0)
    head_of_col = _iota((1, gw), 1) // SSD_HEAD_DIM
    ys, states = [], []
    for g in range(SSD_GROUPS):
        cg = cm[:, g * ns:(g + 1) * ns]
        bg = bm[:, g * ns:(g + 1) * ns]
        cols = slice(g * gw, (g + 1) * gw)
        y = bdot(cg, prev[:, cols]) * from_start[:, cols]
        states.append(bdot_tn(bg, to_end[:, cols]))
        cb = bdot_nt(cg, bg)
        for r in range(gw // SSD_HEAD_DIM):
            h = g * (gw // SSD_HEAD_DIM) + r
            col = jnp.sum(jnp.where(lane_h == h, acs_h, 0.0), axis=1, keepdims=True)
            row = jnp.sum(jnp.where(sub_h == h, acs_ht, 0.0), axis=0, keepdims=True)
            decay = jnp.exp(jnp.where(causal, col - row, -1e30))
            y = y + jnp.where(head_of_col == r, bdot(cb * decay, xdt[:, cols]), 0.0)
        ys.append(y)
    y = jnp.concatenate(ys, axis=1) + row_per_column(dsk) * xs
    new = prev * jnp.exp(aend_c) + jnp.concatenate(states, axis=1)
    return y, new


def _ssd_fwd(name, xc, dtr, dtb, alog, dsk):
    t = xc.shape[0]
    ln = min(SSD_TILE, t)
    nt = t // ln

    def body(xc_ref, dtr_ref, dtb_ref, alog_ref, dsk_ref, y_ref, prev_ref, carry):
        @pl.when(pl.program_id(0) == 0)
        def _():
            carry[...] = jnp.zeros_like(carry)

        prev = carry[...]
        prev_ref[...] = prev
        y, new = _ssd_tile(xc_ref[...], dtr_ref[...], dtb_ref[...], alog_ref[...], dsk_ref[...], prev)
        y_ref[...] = y
        carry[...] = new

    row = lambda i: (i, 0)
    par = pl.BlockSpec((1, LANES), lambda i: (0, 0))
    return pl.pallas_call(
        body, grid=(nt,),
        in_specs=[pl.BlockSpec((ln, xc.shape[1]), row), pl.BlockSpec((ln, LANES), row), par, par, par],
        out_specs=[pl.BlockSpec((ln, SSD_INNER), row), pl.BlockSpec((None, SSD_STATE, SSD_INNER), lambda i: (i, 0, 0))],
        out_shape=[jax.ShapeDtypeStruct((t, SSD_INNER), F32), jax.ShapeDtypeStruct((nt, SSD_STATE, SSD_INNER), F32)],
        scratch_shapes=[pltpu.VMEM((SSD_STATE, SSD_INNER), F32)],
        compiler_params=_params(("arbitrary",)), name=name)(xc, dtr, dtb, alog, dsk)


def _ssd_bwd(name, xc, dtr, dtb, alog, dsk, prevs, dy):
    t = xc.shape[0]
    ln = min(SSD_TILE, t)
    nt = t // ln

    def body(xc_ref, dtr_ref, dtb_ref, alog_ref, dsk_ref, prev_ref, dy_ref, dxc_ref, ddtr_ref, ddtb_ref, dalog_ref,
             ddsk_ref, dcarry):
        i = pl.program_id(0)

        @pl.when(i == 0)
        def _():
            dcarry[...] = jnp.zeros_like(dcarry)

        _, vjp = jax.vjp(_ssd_tile, xc_ref[...], dtr_ref[...], dtb_ref[...], alog_ref[...], dsk_ref[...], prev_ref[...])
        dxc, ddtr, ddtb, dalog, ddsk, dprev = vjp((dy_ref[...], dcarry[...]))
        dxc_ref[...] = dxc
        ddtr_ref[...] = ddtr
        dcarry[...] = dprev
        for r, gr in ((ddtb_ref, ddtb), (dalog_ref, dalog), (ddsk_ref, ddsk)):
            @pl.when(i == 0)
            def _(r=r, gr=gr):
                r[...] = gr

            @pl.when(i != 0)
            def _(r=r, gr=gr):
                r[...] += gr

    row = lambda i: (nt - 1 - i, 0)
    par = pl.BlockSpec((1, LANES), lambda i: (0, 0))
    big, dts = pl.BlockSpec((ln, xc.shape[1]), row), pl.BlockSpec((ln, LANES), row)
    par_shape = jax.ShapeDtypeStruct((1, LANES), F32)
    return pl.pallas_call(
        body, grid=(nt,),
        in_specs=[big, dts, par, par, par, pl.BlockSpec((None, SSD_STATE, SSD_INNER), lambda i: (nt - 1 - i, 0, 0)),
                  pl.BlockSpec((ln, SSD_INNER), row)],
        out_specs=[big, dts, par, par, par],
        out_shape=[jax.ShapeDtypeStruct(xc.shape, F32), jax.ShapeDtypeStruct(dtr.shape, F32), par_shape, par_shape,
                   par_shape],
        scratch_shapes=[pltpu.VMEM((SSD_STATE, SSD_INNER), F32)],
        compiler_params=_params(("arbitrary",)), name=name)(xc, dtr, dtb, alog, dsk, prevs, dy)


def ssd_scan(name, xc, dtr, dtb, alog, dsk):
    @jax.custom_vjp
    def run(xc, dtr, dtb, alog, dsk):
        return _ssd_fwd(name, xc, dtr, dtb, alog, dsk)[0]

    def fwd(xc, dtr, dtb, alog, dsk):
        y, prevs = _ssd_fwd(name, xc, dtr, dtb, alog, dsk)
        return y, (xc, dtr, dtb, alog, dsk, prevs)

    def bwd(r, g):
        return tuple(_ssd_bwd(name + "_bwd", *r, g))

    run.defvjp(fwd, bwd)
    return run(xc, dtr, dtb, alog, dsk)


def loss_head(y, target):
    t, n = y.shape
    tile = min(ROW_TILE, t)

    def body(y_ref, t_ref, dy_ref, acc_ref):
        d = y_ref[...] - t_ref[...]
        dy_ref[...] = d * (1.0 / n)

        @pl.when(pl.program_id(0) == 0)
        def _():
            acc_ref[...] = jnp.zeros_like(acc_ref)

        acc_ref[...] += jnp.sum(d * d, axis=0, keepdims=True)

    row = pl.BlockSpec((tile, n), lambda i: (i, 0))
    dy, acc = pl.pallas_call(
        body, grid=(t // tile,), in_specs=[row, row], out_specs=[row, pl.BlockSpec((1, n), lambda i: (0, 0))],
        out_shape=[jax.ShapeDtypeStruct((t, n), F32), jax.ShapeDtypeStruct((1, n), F32)],
        compiler_params=_params(("arbitrary",)), name="loss_head")(y, target)
    return acc, dy


def adamw(name, w, g, m, v):
    shape = w.shape
    cols = shape[-1]
    rows = w.size // cols
    tile = _divisor(rows, 512, 8) if rows % 8 == 0 else rows

    def body(w_ref, g_ref, m_ref, v_ref, d_ref, nm_ref, nv_ref):
        g = g_ref[...]
        m = ADAM_B1 * m_ref[...] + (1.0 - ADAM_B1) * g
        v = ADAM_B2 * v_ref[...] + (1.0 - ADAM_B2) * jnp.square(g)
        m_hat = m / (1.0 - ADAM_B1 ** ADAM_STEP)
        v_hat = v / (1.0 - ADAM_B2 ** ADAM_STEP)
        d_ref[...] = -ADAM_LR * (m_hat / (jnp.sqrt(v_hat) + ADAM_EPS) + ADAM_WD * w_ref[...])
        nm_ref[...] = m
        nv_ref[...] = v

    spec = pl.BlockSpec((tile, cols), lambda i: (i, 0))
    two_d = jax.ShapeDtypeStruct((rows, cols), F32)
    outs = pl.pallas_call(body, grid=(rows // tile,), in_specs=[spec] * 4, out_specs=[spec] * 3, out_shape=[two_d] * 3,
                          compiler_params=_params(("arbitrary",)), name=name)(
        *[a.reshape(rows, cols) for a in (w, g, m, v)])
    return [o.reshape(shape) for o in outs]


HBM_SPEC = pl.BlockSpec(memory_space=pl.ANY)


def _position():
    return lax.axis_index("x"), lax.axis_index("y"), lax.axis_index("c")


def all_gather(name, shard):
    def body(x_ref, out_ref, send_sems, recv_sems, local_sem):
        x, y, c = _position()
        me, sibling = (x, y, c), (x, y, 1 - c)
        chips = [(1 - x, y), (x, 1 - y), (1 - x, 1 - y)]

        def block(px, py, pc):
            return out_ref.at[4 * px + 2 * py + pc]

        def copy(k, blk, to, src=None):
            return pltpu.make_async_remote_copy(
                src_ref=block(*blk) if src is None else src, dst_ref=block(*blk), send_sem=send_sems.at[k],
                recv_sem=recv_sems.at[k], device_id=to, device_id_type=MESH_ID)

        mine = pltpu.make_async_copy(x_ref, block(*me), local_sem)
        mine.start()
        first = [copy(0, me, sibling, src=x_ref)]
        first += [copy(1 + j, me, (*chip, c), src=x_ref) for j, chip in enumerate(chips)]
        for cp in first:
            cp.start()
        passed = [copy(4 + j, (*chip, c), sibling) for j, chip in enumerate(chips)]
        for j, chip in enumerate(chips):
            copy(1 + j, (*chip, c), me).wait_recv()
            passed[j].start()
        copy(0, sibling, me).wait_recv()
        for j, chip in enumerate(chips):
            copy(4 + j, (*chip, 1 - c), me).wait_recv()
        for cp in first + passed:
            cp.wait_send()
        mine.wait()

    return pl.pallas_call(
        body, in_specs=[HBM_SPEC], out_specs=HBM_SPEC,
        out_shape=jax.ShapeDtypeStruct((N_DEV,) + shard.shape, shard.dtype),
        scratch_shapes=[pltpu.SemaphoreType.DMA((7,)), pltpu.SemaphoreType.DMA((7,)), pltpu.SemaphoreType.DMA],
        name=name)(shard)


def pair_exchange(name, g):
    def body(g_ref, out_ref, send_sem, recv_sem):
        x, y, c = _position()
        cp = pltpu.make_async_remote_copy(src_ref=g_ref.at[1 - c], dst_ref=out_ref, send_sem=send_sem,
                                          recv_sem=recv_sem, device_id=(x, y, 1 - c), device_id_type=MESH_ID)
        cp.start()
        cp.wait()

    return pl.pallas_call(
        body, in_specs=[HBM_SPEC], out_specs=HBM_SPEC, out_shape=jax.ShapeDtypeStruct(g.shape[1:], g.dtype),
        scratch_shapes=[pltpu.SemaphoreType.DMA, pltpu.SemaphoreType.DMA], name=name)(g)


def chip_exchange(name, p):
    def body(p_ref, out_ref, send_sems, recv_sems):
        x, y, c = _position()
        copies = []
        for j, (px, py) in enumerate([(1 - x, y), (x, 1 - y), (1 - x, 1 - y)]):
            cp = pltpu.make_async_remote_copy(
                src_ref=p_ref.at[2 * px + py], dst_ref=out_ref.at[j], send_sem=send_sems.at[j],
                recv_sem=recv_sems.at[j], device_id=(px, py, c), device_id_type=MESH_ID)
            cp.start()
            copies.append(cp)
        for cp in copies:
            cp.wait()

    return pl.pallas_call(
        body, in_specs=[HBM_SPEC], out_specs=HBM_SPEC, out_shape=jax.ShapeDtypeStruct((3,) + p.shape[1:], p.dtype),
        scratch_shapes=[pltpu.SemaphoreType.DMA((3,)), pltpu.SemaphoreType.DMA((3,))], name=name)(p)


def _sum_tile(r):
    return _divisor(r, 512, 8) if r % 8 == 0 else r


def pair_reduce(name, g, got, my_c, my_chip, wire):
    _, nchip, r, c_ = g.shape
    tile = _sum_tile(r)

    def body(ids, g_ref, got_ref, p_ref, mine_ref):
        s = g_ref[...] + got_ref[...]
        p_ref[...] = s.astype(wire)

        @pl.when(pl.program_id(1) == ids[1])
        def _():
            mine_ref[...] = s

    return pl.pallas_call(
        body,
        grid_spec=pltpu.PrefetchScalarGridSpec(
            num_scalar_prefetch=1, grid=(r // tile, nchip),
            in_specs=[pl.BlockSpec((None, None, tile, c_), lambda i, k, ids: (ids[0], k, i, 0)),
                      pl.BlockSpec((None, tile, c_), lambda i, k, ids: (k, i, 0))],
            out_specs=[pl.BlockSpec((None, tile, c_), lambda i, k, ids: (k, i, 0)),
                       pl.BlockSpec((tile, c_), lambda i, k, ids: (i, 0))]),
        out_shape=[jax.ShapeDtypeStruct((nchip, r, c_), wire), jax.ShapeDtypeStruct((r, c_), F32)],
        compiler_params=_params(("arbitrary", "arbitrary")), name=name)(
        jnp.stack([my_c, my_chip]).astype(jnp.int32), g, got)


def chip_reduce(name, mine, got):
    r, c_ = mine.shape
    tile = _sum_tile(r)

    def body(m_ref, got_ref, o_ref):
        o_ref[...] = ((m_ref[...] + got_ref[0].astype(F32)) + got_ref[1].astype(F32)) + got_ref[2].astype(F32)

    return pl.pallas_call(
        body, grid=(r // tile,),
        in_specs=[pl.BlockSpec((tile, c_), lambda i: (i, 0)), pl.BlockSpec((3, tile, c_), lambda i: (0, i, 0))],
        out_specs=pl.BlockSpec((tile, c_), lambda i: (i, 0)), out_shape=jax.ShapeDtypeStruct((r, c_), F32),
        compiler_params=_params(("arbitrary",)), name=name)(mine, got)


def sum_blocks(name, a):
    n, r, c_ = a.shape
    tile = _sum_tile(r)

    def body(a_ref, o_ref):
        acc = a_ref[0]
        for k in range(1, n):
            acc = acc + a_ref[k]
        o_ref[...] = acc

    return pl.pallas_call(
        body, grid=(r // tile,), in_specs=[pl.BlockSpec((n, tile, c_), lambda i: (0, i, 0))],
        out_specs=pl.BlockSpec((tile, c_), lambda i: (i, 0)), out_shape=jax.ShapeDtypeStruct((r, c_), F32),
        compiler_params=_params(("arbitrary",)), name=name)(a)


GROUPS = [
    ("rows1024", ["ssd_w_out", "conv_w_out", "mla_w_o", "w_out", "xattn_w_q", "xattn_w_o", "ffn_w_out"], BF16, 1024),
    ("w_in", ["w_in"], BF16, 1114),
    ("ffn_w_in", ["ffn_w_in"], BF16, 704),
    ("cols256", ["xattn_w_kv", "mla_w_kv_b"], BF16, 256),
    ("w_q_b", ["mla_w_q_b"], BF16, 192),
    ("small", ["ssd_conv_w", "conv_dw_w", "gate_b"], F32, 128),
]
PACK_COLS = 1024


def _pack(arrays):
    flat = jnp.concatenate([a.reshape(-1) for a in arrays])
    rows = -(-flat.shape[0] // PACK_COLS)
    rows += -rows % 8
    return jnp.pad(flat, (0, rows * PACK_COLS - flat.shape[0])).reshape(rows, PACK_COLS)


def _unpack(buf, shapes):
    flat = buf.reshape(-1)
    out, off = [], 0
    for s in shapes:
        n = 1
        for d in s:
            n *= d
        out.append(flat[off:off + n].reshape(tuple(s)))
        off += n
    return out


def _stack_rows(arrays, width, lead):
    return jnp.concatenate([a.reshape(a.shape[:lead] + (-1, width)) for a in arrays], axis=lead)


def _unstack_rows(buf, shapes, width, lead):
    out, off = [], 0
    for s in shapes:
        n = 1
        for d in s:
            n *= d
        rows = n // width
        idx = (slice(None),) * lead + (slice(off, off + rows),)
        out.append(buf[idx].reshape(buf.shape[:lead] + tuple(s)))
        off += rows
    return out


def _join_shards(blocks, axis):
    ax = axis + 1
    moved = jnp.moveaxis(blocks, 0, ax)
    s = moved.shape
    return moved.reshape(s[:ax] + (s[ax] * s[ax + 1],) + s[ax + 2:])


def _split_by_owner(full, axis):
    ax = axis + 1
    s = full.shape
    cut = full.reshape(s[:ax] + (2, 2, 2, s[ax] // N_DEV) + s[ax + 1:])
    cut = jnp.moveaxis(cut, (ax + 2, ax, ax + 1), (0, 1, 2))
    return cut.reshape((2, 4) + cut.shape[3:])


def _pad_cols(w, n):
    return jnp.pad(w, ((0, 0), (0, n - w.shape[1])))


def _prep_layer(w):
    o = [0]
    for n in IN_SIZES:
        o.append(o[-1] + n)
    w_in = w["w_in"]
    kv_lat = o[5]
    q = w["mla_w_q_b"].reshape(MLA_Q_RANK, MLA_HEADS, MLA_NOPE + MLA_ROPE)
    kv = w["mla_w_kv_b"].reshape(MLA_KV_RANK, MLA_HEADS, MLA_NOPE + MLA_V)

    def row(v):
        return v.reshape(1, -1)

    def norm_pair(g):
        return _pad_cols(row(g), 2 * LANES)

    return {
        "mix_norm_g": row(w["mix_norm_g"]),
        "w_z": w_in[:, o[0]:o[1]], "w_xbc": w_in[:, o[1]:o[2]], "w_dt": _pad_cols(w_in[:, o[2]:o[3]], LANES),
        "w_glu": w_in[:, o[3]:o[4]], "w_q": w_in[:, o[4]:o[5]], "w_ckv": w_in[:, kv_lat:kv_lat + MLA_KV_RANK],
        "w_kr": _pad_cols(w_in[:, kv_lat + MLA_KV_RANK:o[6]], LANES), "w_gate": w_in[:, o[6]:o[7]],
        "ssd_conv_w": w["ssd_conv_w"], "ssd_conv_b": row(w["ssd_conv_b"]),
        "ssd_dt_bias": _pad_cols(row(w["ssd_dt_bias"]), LANES), "ssd_a_log": _pad_cols(row(w["ssd_a_log"]), LANES),
        "ssd_d": _pad_cols(row(w["ssd_d"]), LANES), "ssd_norm_g": row(w["ssd_norm_g"]), "ssd_w_out": w["ssd_w_out"],
        "conv_dw_w": w["conv_dw_w"], "conv_dw_b": row(w["conv_dw_b"]), "conv_ln_g": row(w["conv_ln_g"]),
        "conv_ln_b": row(w["conv_ln_b"]), "conv_w_out": w["conv_w_out"],
        "mla_q_a_g": row(w["mla_q_a_g"]), "mla_kv_a_g": row(w["mla_kv_a_g"]),
        "w_qn": q[:, :, :MLA_NOPE].reshape(MLA_Q_RANK, -1),
        "w_qr": jnp.pad(q[:, :, MLA_NOPE:], ((0, 0), (0, 0), (0, LANES - MLA_ROPE))).reshape(MLA_Q_RANK, -1),
        "w_kn": kv[:, :, :MLA_NOPE].reshape(MLA_KV_RANK, -1), "w_v": kv[:, :, MLA_NOPE:].reshape(MLA_KV_RANK, -1),
        "mla_q_norm_g": norm_pair(w["mla_q_norm_g"]), "mla_k_norm_g": norm_pair(w["mla_k_norm_g"]),
        "mla_w_o": w["mla_w_o"], "gate_b": row(w["gate_b"]), "w_out": w["w_out"],
        "xattn_norm_g": row(w["xattn_norm_g"]), "mem_norm_g": row(w["mem_norm_g"]), "xattn_w_q": w["xattn_w_q"],
        "w_xk": w["xattn_w_kv"][:, :D_MODEL], "w_xv": w["xattn_w_kv"][:, D_MODEL:],
        "xattn_q_norm_g": row(w["xattn_q_norm_g"]), "xattn_k_norm_g": row(w["xattn_k_norm_g"]),
        "xattn_w_o": w["xattn_w_o"], "ffn_norm_g": row(w["ffn_norm_g"]),
        "w_ffn_gate": w["ffn_w_in"][:, :FFN_HIDDEN], "w_ffn_up": w["ffn_w_in"][:, FFN_HIDDEN:],
        "ffn_w_out": w["ffn_w_out"],
    }


def _unprep_grads(g):
    n_dt, n_kr = IN_SIZES[2], MLA_ROPE
    qn = g["w_qn"].reshape(MLA_Q_RANK, MLA_HEADS, MLA_NOPE)
    qr = g["w_qr"].reshape(MLA_Q_RANK, MLA_HEADS, LANES)[:, :, :MLA_ROPE]
    kn = g["w_kn"].reshape(MLA_KV_RANK, MLA_HEADS, MLA_NOPE)
    vv = g["w_v"].reshape(MLA_KV_RANK, MLA_HEADS, MLA_V)
    flat = lambda v: v.reshape(-1)
    return {
        "mix_norm_g": flat(g["mix_norm_g"]),
        "w_in": jnp.concatenate([g["w_z"], g["w_xbc"], g["w_dt"][:, :n_dt], g["w_glu"], g["w_q"], g["w_ckv"],
                                 g["w_kr"][:, :n_kr], g["w_gate"]], axis=1),
        "ssd_conv_w": g["ssd_conv_w"], "ssd_conv_b": flat(g["ssd_conv_b"]),
        "ssd_dt_bias": flat(g["ssd_dt_bias"])[:SSD_HEADS], "ssd_a_log": flat(g["ssd_a_log"])[:SSD_HEADS],
        "ssd_d": flat(g["ssd_d"])[:SSD_HEADS], "ssd_norm_g": flat(g["ssd_norm_g"]), "ssd_w_out": g["ssd_w_out"],
        "conv_dw_w": g["conv_dw_w"], "conv_dw_b": flat(g["conv_dw_b"]), "conv_ln_g": flat(g["conv_ln_g"]),
        "conv_ln_b": flat(g["conv_ln_b"]), "conv_w_out": g["conv_w_out"],
        "mla_q_a_g": flat(g["mla_q_a_g"]), "mla_w_q_b": jnp.concatenate([qn, qr], axis=2).reshape(MLA_Q_RANK, -1),
        "mla_kv_a_g": flat(g["mla_kv_a_g"]), "mla_w_kv_b": jnp.concatenate([kn, vv], axis=2).reshape(MLA_KV_RANK, -1),
        "mla_q_norm_g": flat(g["mla_q_norm_g"])[:MLA_NOPE + MLA_ROPE],
        "mla_k_norm_g": flat(g["mla_k_norm_g"])[:MLA_NOPE + MLA_ROPE],
        "mla_w_o": g["mla_w_o"], "gate_b": g["gate_b"].reshape(3, D_MODEL), "w_out": g["w_out"],
        "xattn_norm_g": flat(g["xattn_norm_g"]), "mem_norm_g": flat(g["mem_norm_g"]), "xattn_w_q": g["xattn_w_q"],
        "xattn_w_kv": jnp.concatenate([g["w_xk"], g["w_xv"]], axis=1),
        "xattn_q_norm_g": flat(g["xattn_q_norm_g"]), "xattn_k_norm_g": flat(g["xattn_k_norm_g"]),
        "xattn_w_o": g["xattn_w_o"], "ffn_norm_g": flat(g["ffn_norm_g"]),
        "ffn_w_in": jnp.concatenate([g["w_ffn_gate"], g["w_ffn_up"]], axis=1), "ffn_w_out": g["ffn_w_out"],
    }


def _layer(l, x, mem, cosf, sinf, w):
    t = x.shape[0]
    n = lambda s: f"l{l}_{s}"
    tile = min(ROW_TILE, t)
    grid = (1, t // tile)

    def rowwise(name, f, ins, width):
        return tmap(n(name), f, grid, ins, [_row_out(t, width, tile)])[0]

    u = rms_norm(n("mix_norm"), x, w["mix_norm_g"])
    z = matmul(n("in_z"), u, w["w_z"])
    xbc = matmul(n("in_xbc"), u, w["w_xbc"])
    dtr = matmul(n("in_dt"), u, w["w_dt"])
    glu = matmul(n("in_glu"), u, w["w_glu"])
    q_lat = matmul(n("in_q"), u, w["w_q"])
    c_kv = matmul(n("in_ckv"), u, w["w_ckv"])
    kr_raw = matmul(n("in_kr"), u, w["w_kr"])
    gate_logits = matmul(n("in_gate"), u, w["w_gate"])

    xc = dwconv(n("ssd_conv"), xbc, w["ssd_conv_w"], w["ssd_conv_b"])
    y_scan = ssd_scan(n("ssd_scan"), xc, dtr, w["ssd_dt_bias"], w["ssd_a_log"], w["ssd_d"])
    y_norm = rowwise("ssd_gate_norm", _ssd_gate_norm_f, [_rows(y_scan, tile), _rows(z, tile), _whole(w["ssd_norm_g"])],
                     SSD_INNER)
    y_ssd = matmul(n("ssd_out"), y_norm, w["ssd_w_out"])

    v = rowwise("glu", _glu_f, [_rows(glu, tile)], D_MODEL)
    v = dwconv(n("conv_dw"), v, w["conv_dw_w"], w["conv_dw_b"])
    v = rowwise("conv_ln_silu", _ln_silu_f, [_rows(v, tile), _whole(w["conv_ln_g"]), _whole(w["conv_ln_b"])], D_MODEL)
    y_conv = matmul(n("conv_out"), v, w["conv_w_out"])

    q_n = rms_norm(n("q_a_norm"), q_lat, w["mla_q_a_g"])
    qn_raw = matmul(n("q_nope"), q_n, w["w_qn"])
    qr_raw = matmul(n("q_rope"), q_n, w["w_qr"])
    c_n = rms_norm(n("kv_a_norm"), c_kv, w["mla_kv_a_g"])
    kn_raw = matmul(n("k_nope"), c_n, w["w_kn"])
    val = matmul(n("mla_v"), c_n, w["w_v"])
    tables = [_rows(cosf, tile, "n"), _rows(sinf, tile, "n")]
    kn = rowwise("k_nope_norm", _k_nope_f, [_rows(kn_raw, tile), _whole(w["mla_k_norm_g"])], MLA_HEADS * MLA_NOPE)
    kr = rowwise("k_rope", _k_rope_f, [_rows(kr_raw, tile)] + tables + [_whole(w["mla_k_norm_g"])], LANES)
    wide = _row_out(t, MLA_HEADS * LANES, tile)
    qn, qr = tmap(n("q_prep"), _q_prep_f, grid,
                  [_rows(qn_raw, tile), _rows(qr_raw, tile)] + tables + [_whole(w["mla_q_norm_g"])], [wide, wide])
    att = mla_attention(n("mla_attn"), qn, qr, kn, kr, val)
    y_mla = matmul(n("mla_out"), att, w["mla_w_o"])

    merged = rowwise("merge", _merge_f, [_rows(gate_logits, tile), _whole(w["gate_b"]), _rows(y_ssd, tile),
                                         _rows(y_conv, tile), _rows(y_mla, tile)], D_MODEL)
    x = matmul(n("mix_out"), merged, w["w_out"], res=x)

    h = rms_norm(n("xattn_norm"), x, w["xattn_norm_g"])
    mem_n = rms_norm_nograd_x(n("mem_norm"), mem, w["mem_norm_g"])
    xq = matmul(n("xattn_q"), h, w["xattn_w_q"])
    xk = matmul(n("xattn_k"), mem_n, w["w_xk"])
    xv = matmul(n("xattn_v"), mem_n, w["w_xv"])
    m = mem.shape[0]
    txq = min(XATT_Q_TILE, t)
    kv_head = lambda arr: (arr, (m, X_HEAD_DIM), lambda o, i: (0, o), "ai")
    xo = tmap(n("xattn"), _xattn_f, (X_HEADS, t // txq),
              [(xq, (txq, X_HEAD_DIM), lambda o, i: (i, o), "t"), kv_head(xk), kv_head(xv),
               _whole(w["xattn_q_norm_g"]), _whole(w["xattn_k_norm_g"])],
              [((t, D_MODEL), (txq, X_HEAD_DIM), lambda o, i: (i, o))])[0]
    x = matmul(n("xattn_out"), xo, w["xattn_w_o"], res=x)

    h = rms_norm(n("ffn_norm"), x, w["ffn_norm_g"])
    gate = matmul(n("ffn_gate"), h, w["w_ffn_gate"])
    up = matmul(n("ffn_up"), h, w["w_ffn_up"])
    act = rowwise("swiglu", _swiglu_f, [_rows(gate, tile), _rows(up, tile)], FFN_HIDDEN)
    return matmul(n("ffn_out"), act, w["ffn_w_out"], res=x)


def _rope_tables(positions):
    inv = ROPE_THETA ** (-jnp.arange(0, MLA_ROPE, 2, dtype=F32) / MLA_ROPE)
    ang = positions.astype(F32)[:, None] * inv
    pad = jnp.zeros((positions.shape[0], LANES - MLA_ROPE), F32)
    cos, sin = jnp.cos(ang), jnp.sin(ang)
    return jnp.concatenate([cos, cos, pad], axis=1), jnp.concatenate([sin, sin, pad], axis=1)


def local_step(x, mem, positions, target, weights):
    cosf, sinf = _rope_tables(positions)
    prepped = [_prep_layer({k: v[l] for k, v in weights.items()}) for l in range(DEPTH)]

    def forward(x, prepped):
        for l in range(DEPTH):
            x = _layer(l, x, mem, cosf, sinf, prepped[l])
        return x

    y, pull = jax.vjp(forward, x, prepped)
    sq, dy = loss_head(y, target)
    gx, gp = pull(dy)
    per_layer = [_unprep_grads(g) for g in gp]
    grads = {k: jnp.stack([pl_[k] for pl_ in per_layer]) for k in WEIGHTS}
    return sq, gx, grads


def _step(x, mem, positions, loss_target, w, m, v):
    xi, yi, ci = _position()

    full = {n: w[n] for n in REPLICATED}
    for gname, names, wire, width in GROUPS:
        shapes = [w[n].shape for n in names]
        stacked = _stack_rows([w[n] for n in names], width, 0).astype(wire)
        gathered = all_gather("gather_" + gname, stacked)
        for n, b in zip(names, _unstack_rows(gathered, shapes, width, 1)):
            full[n] = _join_shards(b, SHARDED[n]).astype(F32)

    sq, gx, grads = local_step(x[0], mem[0], positions[0], loss_target[0], full)
    loss = lax.psum(0.5 * jnp.sum(sq) / D_MODEL, ("x", "y", "c"))

    g_shard = {}
    for gname, names, wire, width in GROUPS:
        shapes = [w[n].shape for n in names]
        by_owner = _stack_rows([_split_by_owner(grads[n], SHARDED[n]) for n in names], width, 2)
        from_sibling = pair_exchange("pair_exchange_" + gname, by_owner)
        chip_partial, mine = pair_reduce("pair_reduce_" + gname, by_owner, from_sibling, ci, 2 * xi + yi, wire)
        from_chips = chip_exchange("chip_exchange_" + gname, chip_partial)
        reduced = chip_reduce("chip_reduce_" + gname, mine, from_chips)
        g_shard.update(zip(names, _unstack_rows(reduced, shapes, width, 0)))

    rep_shapes = [w[n].shape for n in REPLICATED]
    rep_all = all_gather("small_grads_all_gather", _pack([grads[n] for n in REPLICATED]))
    g_rep = dict(zip(REPLICATED, _unpack(sum_blocks("small_grads_sum", rep_all), rep_shapes)))

    out_g, out_d, out_m, out_v = [], [], [], []
    for n in WEIGHTS:
        g = g_shard[n] if n in SHARDED else g_rep[n]
        d, nm, nv = adamw("adamw_" + n, w[n], g, m[n], v[n])
        out_g.append(g)
        out_d.append(d)
        out_m.append(nm)
        out_v.append(nv)
    return (loss, gx[None], *out_g, *out_d, *out_m, *out_v)


def kernel(x, mem, positions, mix_norm_g, w_in, ssd_conv_w, ssd_conv_b, ssd_dt_bias, ssd_a_log, ssd_d, ssd_norm_g, ssd_w_out, conv_dw_w, conv_dw_b, conv_ln_g, conv_ln_b, conv_w_out, mla_q_a_g, mla_w_q_b, mla_kv_a_g, mla_w_kv_b, mla_q_norm_g, mla_k_norm_g, mla_w_o, gate_b, w_out, xattn_norm_g, mem_norm_g, xattn_w_q, xattn_w_kv, xattn_q_norm_g, xattn_k_norm_g, xattn_w_o, ffn_norm_g, ffn_w_in, ffn_w_out, loss_target, m_mix_norm_g, m_w_in, m_ssd_conv_w, m_ssd_conv_b, m_ssd_dt_bias, m_ssd_a_log, m_ssd_d, m_ssd_norm_g, m_ssd_w_out, m_conv_dw_w, m_conv_dw_b, m_conv_ln_g, m_conv_ln_b, m_conv_w_out, m_mla_q_a_g, m_mla_w_q_b, m_mla_kv_a_g, m_mla_w_kv_b, m_mla_q_norm_g, m_mla_k_norm_g, m_mla_w_o, m_gate_b, m_w_out, m_xattn_norm_g, m_mem_norm_g, m_xattn_w_q, m_xattn_w_kv, m_xattn_q_norm_g, m_xattn_k_norm_g, m_xattn_w_o, m_ffn_norm_g, m_ffn_w_in, m_ffn_w_out, v_mix_norm_g, v_w_in, v_ssd_conv_w, v_ssd_conv_b, v_ssd_dt_bias, v_ssd_a_log, v_ssd_d, v_ssd_norm_g, v_ssd_w_out, v_conv_dw_w, v_conv_dw_b, v_conv_ln_g, v_conv_ln_b, v_conv_w_out, v_mla_q_a_g, v_mla_w_q_b, v_mla_kv_a_g, v_mla_w_kv_b, v_mla_q_norm_g, v_mla_k_norm_g, v_mla_w_o, v_gate_b, v_w_out, v_xattn_norm_g, v_mem_norm_g, v_xattn_w_q, v_xattn_w_kv, v_xattn_q_norm_g, v_xattn_k_norm_g, v_xattn_w_o, v_ffn_norm_g, v_ffn_w_in, v_ffn_w_out):
    args = locals()
    w = {n: args[n] for n in WEIGHTS}
    m = {n: args["m_" + n] for n in WEIGHTS}
    v = {n: args["v_" + n] for n in WEIGHTS}
    return _step(x, mem, positions, loss_target, w, m, v)
```

```python
from typing import NamedTuple

import jax
import jax.numpy as jnp
from jax import lax
from jax.experimental import pallas as pl
from jax.experimental.pallas import tpu as pltpu

F32 = jnp.float32
BF16 = jnp.bfloat16
HIGHEST = lax.Precision.HIGHEST
MESH_ID = pl.DeviceIdType.MESH

VMEM_LIMIT_BYTES = 56 * 1024 * 1024
LANES = 128

EPS = 1e-6
DEPTH = 4
D_MODEL = 1024
N_DEV = 8
SSD_HEADS = 16
SSD_HEAD_DIM = 64
SSD_STATE = 128
SSD_GROUPS = 4
SSD_INNER = 1024
SSD_TILE = 256
CONV_K = 31
SSD_CONV_K = 4
MLA_HEADS = 8
MLA_NOPE = 128
MLA_ROPE = 64
MLA_V = 128
MLA_Q_RANK = 384
MLA_KV_RANK = 256
ATT_CHUNK = 64
ROPE_THETA = 10000.0
X_HEADS = 4
X_HEAD_DIM = 256
FFN_HIDDEN = 2816
IN_SIZES = (1024, 2048, 16, 2048, 384, 320, 3072)

ADAM_LR = 0.001
ADAM_B1 = 0.9
ADAM_B2 = 0.999
ADAM_EPS = 1e-08
ADAM_WD = 0.01
ADAM_STEP = 10

MM_FULL_K = 3072
ROW_TILE = 256
ATT_BLOCK = 512
XATT_Q_TILE = 512

SHARDED = {
    "w_in": 1, "ssd_conv_w": 1, "ssd_w_out": 0, "conv_dw_w": 1, "conv_w_out": 0, "mla_w_q_b": 1, "mla_w_kv_b": 1,
    "mla_w_o": 0, "gate_b": 1, "w_out": 0, "xattn_w_q": 0, "xattn_w_kv": 1, "xattn_w_o": 0, "ffn_w_in": 1,
    "ffn_w_out": 0,
}
WEIGHTS = ["mix_norm_g", "w_in", "ssd_conv_w", "ssd_conv_b", "ssd_dt_bias", "ssd_a_log", "ssd_d", "ssd_norm_g",
           "ssd_w_out", "conv_dw_w", "conv_dw_b", "conv_ln_g", "conv_ln_b", "conv_w_out", "mla_q_a_g", "mla_w_q_b",
           "mla_kv_a_g", "mla_w_kv_b", "mla_q_norm_g", "mla_k_norm_g", "mla_w_o", "gate_b", "w_out", "xattn_norm_g",
           "mem_norm_g", "xattn_w_q", "xattn_w_kv", "xattn_q_norm_g", "xattn_k_norm_g", "xattn_w_o", "ffn_norm_g",
           "ffn_w_in", "ffn_w_out"]
REPLICATED = [n for n in WEIGHTS if n not in SHARDED]


def _params(sem=None):
    return pltpu.CompilerParams(dimension_semantics=sem, vmem_limit_bytes=VMEM_LIMIT_BYTES)


def _divisor(n, cap, mult):
    if n <= cap:
        return n
    for d in range(cap - cap % mult, 0, -mult):
        if n % d == 0:
            return d
    raise ValueError(f"no tile for {n}")


def _dg(a, b, ca, cb):
    return lax.dot_general(a.astype(BF16), b.astype(BF16), (((ca,), (cb,)), ((), ())), preferred_element_type=F32)


@jax.custom_vjp
def bdot(a, b):
    return _dg(a, b, 1, 0)


bdot.defvjp(lambda a, b: (_dg(a, b, 1, 0), (a, b)), lambda r, g: (_dg(g, r[1], 1, 1), _dg(r[0], g, 0, 0)))


@jax.custom_vjp
def bdot_nt(a, b):
    return _dg(a, b, 1, 1)


bdot_nt.defvjp(lambda a, b: (_dg(a, b, 1, 1), (a, b)), lambda r, g: (_dg(g, r[1], 1, 0), _dg(g, r[0], 0, 0)))


@jax.custom_vjp
def bdot_tn(a, b):
    return _dg(a, b, 0, 0)


bdot_tn.defvjp(lambda a, b: (_dg(a, b, 0, 0), (a, b)), lambda r, g: (_dg(r[1], g, 1, 1), _dg(r[0], g, 1, 0)))


def hdot(a, b):
    return jnp.dot(a, b, precision=HIGHEST, preferred_element_type=F32)


def _iota(shape, dim):
    return lax.broadcasted_iota(jnp.int32, shape, dim)


def _mm(name, a, b, ta=False, tb=False, res=None):
    m, k = (a.shape[1], a.shape[0]) if ta else a.shape
    n = b.shape[0] if tb else b.shape[1]
    tm = _divisor(m, 1024, LANES if ta else 8)
    tn = _divisor(n, 1536, LANES)
    tk = k if k <= MM_FULL_K else _divisor(k, 1024, LANES)
    if tk == k and k > 1024:
        tm = _divisor(m, 512, LANES if ta else 8)
    nk = k // tk
    dims = (((0 if ta else 1,), (1 if tb else 0,)), ((), ()))

    def body(*refs):
        if res is None:
            a_ref, b_ref, o_ref = refs
        else:
            a_ref, b_ref, r_ref, o_ref = refs
        part = lax.dot_general(a_ref[...].astype(BF16), b_ref[...].astype(BF16), dims, preferred_element_type=F32)
        if nk == 1:
            o_ref[...] = part if res is None else part + r_ref[...]
        else:
            kk = pl.program_id(2)

            @pl.when(kk == 0)
            def _():
                o_ref[...] = part if res is None else part + r_ref[...]

            @pl.when(kk != 0)
            def _():
                o_ref[...] += part

    a_spec = pl.BlockSpec((tk, tm), lambda i, j, kk: (kk, i)) if ta else pl.BlockSpec((tm, tk), lambda i, j, kk: (i, kk))
    b_spec = pl.BlockSpec((tn, tk), lambda i, j, kk: (j, kk)) if tb else pl.BlockSpec((tk, tn), lambda i, j, kk: (kk, j))
    o_spec = pl.BlockSpec((tm, tn), lambda i, j, kk: (i, j))
    in_specs = [a_spec, b_spec] + ([] if res is None else [o_spec])
    args = (a, b) + (() if res is None else (res,))
    return pl.pallas_call(
        body, grid=(m // tm, n // tn, nk), in_specs=in_specs, out_specs=o_spec,
        out_shape=jax.ShapeDtypeStruct((m, n), F32),
        compiler_params=_params(("parallel", "parallel", "arbitrary")), name=name)(*args)


class Mat(NamedTuple):
    value: jax.Array
    slot: jax.Array


MATRICES = frozenset([
    "w_z", "w_xbc", "w_dt", "w_glu", "w_q", "w_ckv", "w_kr", "w_gate", "ssd_w_out", "conv_w_out", "w_qn", "w_qr",
    "w_kn", "w_v", "mla_w_o", "w_out", "xattn_w_q", "w_xk", "w_xv", "xattn_w_o", "w_ffn_gate", "w_ffn_up",
    "ffn_w_out"])


def matmul(name, a, mat, res=None):
    w, slot = mat
    if res is None:
        @jax.custom_vjp
        def run(a, w, slot):
            return _mm(name, a, w)

        def fwd(a, w, slot):
            return run(a, w, slot), (a, w)

        def bwd(r, g):
            return _mm(name + "_da", g, r[1], tb=True), None, _mm(name + "_dw", r[0], g, ta=True)

        run.defvjp(fwd, bwd)
        return run(a, w, slot)

    @jax.custom_vjp
    def run_res(a, w, slot, res):
        return _mm(name, a, w, res=res)

    def fwd_res(a, w, slot, res):
        return run_res(a, w, slot, res), (a, w)

    def bwd_res(r, g):
        return _mm(name + "_da", g, r[1], tb=True), None, _mm(name + "_dw", r[0], g, ta=True), g

    run_res.defvjp(fwd_res, bwd_res)
    return run_res(a, w, slot, res)


def tmap(name, f, grid, ins, outs):
    arrays = [x[0] for x in ins]
    kinds = [x[3] for x in ins]
    in_specs = [pl.BlockSpec(x[1], x[2]) for x in ins]
    out_specs = [pl.BlockSpec(x[1], x[2]) for x in outs]
    out_shape = [jax.ShapeDtypeStruct(x[0], F32) for x in outs]
    n_in, n_out = len(ins), len(outs)
    didx = [k for k, kd in enumerate(kinds) if kd != "n"]

    def fwd_call(*arrs):
        def body(*refs):
            pids = (pl.program_id(0), pl.program_id(1))
            vals = f(pids, *[r[...] for r in refs[:n_in]])
            for r, v in zip(refs[n_in:], vals):
                r[...] = v

        return pl.pallas_call(body, grid=grid, in_specs=in_specs, out_specs=out_specs, out_shape=out_shape,
                              compiler_params=_params(("arbitrary", "arbitrary")), name=name)(*arrs)

    def bwd_call(arrs, cts):
        def body(*refs):
            o, i = pl.program_id(0), pl.program_id(1)
            vals = [r[...] for r in refs[:n_in]]

            def g(*dv):
                full = list(vals)
                for k, v in zip(didx, dv):
                    full[k] = v
                return tuple(f((o, i), *full))

            _, vjp = jax.vjp(g, *[vals[k] for k in didx])
            grads = vjp(tuple(r[...] for r in refs[n_in:n_in + n_out]))
            for k, gr, r in zip(didx, grads, refs[n_in + n_out:]):
                if kinds[k] == "t":
                    r[...] = gr
                else:
                    first = (i == 0) if kinds[k] == "ai" else jnp.logical_and(o == 0, i == 0)

                    @pl.when(first)
                    def _(r=r, gr=gr):
                        r[...] = gr

                    @pl.when(jnp.logical_not(first))
                    def _(r=r, gr=gr):
                        r[...] += gr

        g_specs = [in_specs[k] for k in didx]
        g_shape = [jax.ShapeDtypeStruct(arrs[k].shape, F32) for k in didx]
        return pl.pallas_call(body, grid=grid, in_specs=in_specs + out_specs, out_specs=g_specs, out_shape=g_shape,
                              compiler_params=_params(("arbitrary", "arbitrary")), name=name + "_bwd")(*arrs, *cts)

    @jax.custom_vjp
    def run(*arrs):
        return tuple(fwd_call(*arrs))

    def run_fwd(*arrs):
        return run(*arrs), arrs

    def run_bwd(arrs, cts):
        gs = bwd_call(arrs, cts)
        full = [None] * n_in
        for k, g in zip(didx, gs):
            full[k] = g
        return tuple(full)

    run.defvjp(run_fwd, run_bwd)
    return run(*arrays)


def _rows(arr, tile, kind="t"):
    return (arr, (tile, arr.shape[1]), lambda o, i: (i, 0), kind)


def _whole(arr, kind="ag"):
    return (arr, arr.shape, lambda o, i: (0, 0), kind)


def _row_out(t, n, tile):
    return ((t, n), (tile, n), lambda o, i: (i, 0))


def _rms(x, g, n=None):
    ms = jnp.sum(x * x, axis=-1, keepdims=True) / (x.shape[-1] if n is None else n)
    return x * lax.rsqrt(ms + EPS) * g


def rms_norm(name, x, g):
    t, n = x.shape
    tile = min(ROW_TILE, t)
    return tmap(name, lambda p, x, g: (_rms(x, g),), (1, t // tile), [_rows(x, tile), _whole(g)],
                [_row_out(t, n, tile)])[0]


def rms_norm_nograd_x(name, x, g):
    t, n = x.shape
    tile = min(ROW_TILE, t)
    return tmap(name, lambda p, x, g: (_rms(x, g),), (1, t // tile), [_rows(x, tile, "n"), _whole(g)],
                [_row_out(t, n, tile)])[0]


def _glu_f(p, glu):
    h = glu.shape[1] // 2
    return (glu[:, :h] * jax.nn.sigmoid(glu[:, h:]),)


def _ln_silu_f(p, v, g, b):
    mu = jnp.mean(v, axis=-1, keepdims=True)
    xc = v - mu
    var = jnp.mean(xc * xc, axis=-1, keepdims=True)
    return (jax.nn.silu(xc * lax.rsqrt(var + EPS) * g + b),)


def _ssd_gate_norm_f(p, y, z, g):
    v = y * jax.nn.silu(z)
    w = SSD_INNER // SSD_GROUPS
    parts = []
    for k in range(SSD_GROUPS):
        vg = v[:, k * w:(k + 1) * w]
        parts.append(vg * lax.rsqrt(jnp.mean(vg * vg, axis=-1, keepdims=True) + EPS))
    return (jnp.concatenate(parts, axis=1) * g,)


def _merge_f(p, gl, gb, y0, y1, y2):
    g = jax.nn.sigmoid(gl + gb)
    d = D_MODEL
    return (g[:, :d] * y0 + g[:, d:2 * d] * y1 + g[:, 2 * d:] * y2,)


def _swiglu_f(p, gate, up):
    return (jax.nn.silu(gate) * up,)


def _rot_matrix():
    r, c = _iota((LANES, LANES), 0), _iota((LANES, LANES), 1)
    h = MLA_ROPE // 2
    plus = jnp.logical_and(c >= h, jnp.logical_and(c < 2 * h, r == c - h))
    minus = jnp.logical_and(c < h, r == c + h)
    return plus.astype(F32) - minus.astype(F32)


def _rope(x, cosf, sinf):
    return x * cosf + hdot(x, _rot_matrix()) * sinf


def _per_head(f, x):
    return jnp.concatenate([f(x[:, h * LANES:(h + 1) * LANES]) for h in range(x.shape[1] // LANES)], axis=1)


def _k_nope_f(p, kn_raw, kg):
    return (_per_head(lambda x: _rms(x, kg[:, :MLA_NOPE]), kn_raw),)


def _k_rope_f(p, kr_raw, cosf, sinf, kg):
    return (_rope(_rms(kr_raw, kg[:, MLA_NOPE:], n=MLA_ROPE), cosf, sinf),)


def _q_prep_f(p, qn_raw, qr_raw, cosf, sinf, qg):
    qn = _per_head(lambda x: _rms(x, qg[:, :MLA_NOPE]), qn_raw)
    qr = _per_head(lambda x: _rope(_rms(x, qg[:, MLA_NOPE:], n=MLA_ROPE), cosf, sinf), qr_raw)
    return qn, qr


def _softmax(s):
    m = jnp.max(s, axis=-1, keepdims=True)
    e = jnp.exp(s - m)
    return e / jnp.sum(e, axis=-1, keepdims=True)


def _xattn_f(p, q, k, v, qg, kg):
    s = bdot_nt(_rms(q, qg), _rms(k, kg)) * (X_HEAD_DIM ** -0.5)
    return (bdot(_softmax(s), v),)


ATT_SCALE = (MLA_NOPE + MLA_ROPE) ** -0.5
NT_DIMS = (((1,), (1,)), ((), ()))
NN_DIMS = (((1,), (0,)), ((), ()))
TN_DIMS = (((0,), (0,)), ((), ()))


def _att_specs(t, blk):
    q_spec = pl.BlockSpec((blk, LANES), lambda h, i: (i, h))
    k_spec = pl.BlockSpec((t, LANES), lambda h, i: (0, h))
    shared = pl.BlockSpec((t, LANES), lambda h, i: (0, 0))
    lse_spec = pl.BlockSpec((None, blk, 1), lambda h, i: (h, i, 0))
    return q_spec, k_spec, shared, lse_spec


def _diagonal_mask(blk):
    return (_iota((blk, blk), 1) // ATT_CHUNK) <= (_iota((blk, blk), 0) // ATT_CHUNK)


def _att_keys(kn_ref, kr_ref, j, blk):
    ks = pl.ds(pl.multiple_of(j * blk, blk), blk)
    return ks, jnp.concatenate([kn_ref[ks, :], kr_ref[ks, :]], axis=1).astype(BF16)


def _att_fwd(name, qn, qr, kn, kr, v):
    t, width = qn.shape
    heads = width // LANES
    blk = min(ATT_BLOCK, t)

    def body(qn_ref, qr_ref, kn_ref, kr_ref, v_ref, o_ref, lse_ref):
        i = pl.program_id(1)
        q = jnp.concatenate([qn_ref[...], qr_ref[...]], axis=1).astype(BF16)

        def scores(j):
            _, k = _att_keys(kn_ref, kr_ref, j, blk)
            return lax.dot_general(q, k, NT_DIMS, preferred_element_type=F32)

        def weighted_values(p, j):
            ks = pl.ds(pl.multiple_of(j * blk, blk), blk)
            return lax.dot_general(p, v_ref[ks, :].astype(BF16), NN_DIMS, preferred_element_type=F32)

        def softmax_step(s, m, l):
            m_new = jnp.maximum(m, jnp.max(s, axis=1, keepdims=True))
            alpha = jnp.exp(m - m_new)
            p = jnp.exp(s - m_new)
            return m_new, alpha, alpha * l + jnp.sum(p, axis=1, keepdims=True), p.astype(BF16)

        def step(j, carry):
            s, p_prev, m, l, acc = carry
            s_next = scores(j + 1)
            pv_prev = weighted_values(p_prev, jnp.maximum(j - 1, 0))
            m, alpha, l, p = softmax_step(s * ATT_SCALE, m, l)
            return s_next, p, m, l, alpha * (acc + pv_prev)

        init = (scores(0), jnp.zeros((blk, blk), BF16), jnp.full((blk, 1), -1e30, F32), jnp.zeros((blk, 1), F32),
                jnp.zeros((blk, LANES), F32))
        s, p_prev, m, l, acc = lax.fori_loop(0, i, step, init)
        pv_prev = weighted_values(p_prev, jnp.maximum(i - 1, 0))
        s = jnp.where(_diagonal_mask(blk), s * ATT_SCALE, -1e30)
        m, alpha, l, p = softmax_step(s, m, l)
        acc = alpha * (acc + pv_prev) + weighted_values(p, i)
        o_ref[...] = acc / l
        lse_ref[...] = m + jnp.log(l)

    q_spec, k_spec, shared, lse_spec = _att_specs(t, blk)
    return pl.pallas_call(
        body, grid=(heads, t // blk), in_specs=[q_spec, q_spec, k_spec, shared, k_spec], out_specs=[q_spec, lse_spec],
        out_shape=[jax.ShapeDtypeStruct((t, width), F32), jax.ShapeDtypeStruct((heads, t, 1), F32)],
        compiler_params=_params(("arbitrary", "arbitrary")), name=name)(qn, qr, kn, kr, v)


def _att_bwd(name, qn, qr, kn, kr, v, o, lse, do):
    t, width = qn.shape
    heads = width // LANES
    blk = min(ATT_BLOCK, t)

    def body(qn_ref, qr_ref, kn_ref, kr_ref, v_ref, o_ref, lse_ref, do_ref, dqn_ref, dqr_ref, dkn_ref, dkr_ref,
             dv_ref):
        h, i = pl.program_id(0), pl.program_id(1)

        @pl.when(i == 0)
        def _():
            dkn_ref[...] = jnp.zeros_like(dkn_ref)
            dv_ref[...] = jnp.zeros_like(dv_ref)

        @pl.when(jnp.logical_and(h == 0, i == 0))
        def _():
            dkr_ref[...] = jnp.zeros_like(dkr_ref)

        q = jnp.concatenate([qn_ref[...], qr_ref[...]], axis=1).astype(BF16)
        do = do_ref[...]
        do16 = do.astype(BF16)
        delta = jnp.sum(do * o_ref[...], axis=1, keepdims=True)
        lse = lse_ref[...]

        def issue(j):
            ks, k = _att_keys(kn_ref, kr_ref, j, blk)
            s = lax.dot_general(q, k, NT_DIMS, preferred_element_type=F32)
            dp = lax.dot_general(do16, v_ref[ks, :].astype(BF16), NT_DIMS, preferred_element_type=F32)
            return s, dp

        def retire(p, ds, j, dq):
            ks, k = _att_keys(kn_ref, kr_ref, j, blk)
            dv_ref[ks, :] += lax.dot_general(p, do16, TN_DIMS, preferred_element_type=F32)
            dk = lax.dot_general(ds, q, TN_DIMS, preferred_element_type=F32)
            dkn_ref[ks, :] += dk[:, :LANES]
            dkr_ref[ks, :] += dk[:, LANES:]
            return dq + lax.dot_general(ds, k, NN_DIMS, preferred_element_type=F32)

        def probs(s, dp, masked):
            s = s * ATT_SCALE
            if masked:
                s = jnp.where(_diagonal_mask(blk), s, -1e30)
            p = jnp.exp(s - lse)
            return p.astype(BF16), (p * (dp - delta) * ATT_SCALE).astype(BF16)

        def step(j, carry):
            s, dp, p_prev, ds_prev, dq = carry
            s_next, dp_next = issue(j + 1)
            dq = retire(p_prev, ds_prev, jnp.maximum(j - 1, 0), dq)
            p, ds = probs(s, dp, False)
            return s_next, dp_next, p, ds, dq

        none = jnp.zeros((blk, blk), BF16)
        s, dp, p_prev, ds_prev, dq = lax.fori_loop(0, i, step,
                                                   issue(0) + (none, none, jnp.zeros((blk, 2 * LANES), F32)))
        dq = retire(p_prev, ds_prev, jnp.maximum(i - 1, 0), dq)
        p, ds = probs(s, dp, True)
        dq = retire(p, ds, i, dq)
        dqn_ref[...] = dq[:, :LANES]
        dqr_ref[...] = dq[:, LANES:]

    q_spec, k_spec, shared, lse_spec = _att_specs(t, blk)
    big, one = jax.ShapeDtypeStruct((t, width), F32), jax.ShapeDtypeStruct((t, LANES), F32)
    return pl.pallas_call(
        body, grid=(heads, t // blk),
        in_specs=[q_spec, q_spec, k_spec, shared, k_spec, q_spec, lse_spec, q_spec],
        out_specs=[q_spec, q_spec, k_spec, shared, k_spec], out_shape=[big, big, big, one, big],
        compiler_params=_params(("arbitrary", "arbitrary")), name=name)(qn, qr, kn, kr, v, o, lse, do)


def mla_attention(name, qn, qr, kn, kr, v):
    @jax.custom_vjp
    def run(qn, qr, kn, kr, v):
        return _att_fwd(name, qn, qr, kn, kr, v)[0]

    def fwd(qn, qr, kn, kr, v):
        o, lse = _att_fwd(name, qn, qr, kn, kr, v)
        return o, (qn, qr, kn, kr, v, o, lse)

    def bwd(r, g):
        return tuple(_att_bwd(name + "_bwd", *r, g))

    run.defvjp(fwd, bwd)
    return run(qn, qr, kn, kr, v)


def _shift_down(v, s, rows):
    return v if s == 0 else jnp.where(rows >= s, pltpu.roll(v, s, 0), 0.0)


def _shift_up(v, s, rows):
    t = v.shape[0]
    return v if s == 0 else jnp.where(rows < t - s, pltpu.roll(v, t - s, 0), 0.0)


def _dwconv_fwd(name, x, w, b):
    t, c = x.shape
    kw = w.shape[0]

    def body(x_ref, w_ref, b_ref, y_ref):
        x = x_ref[...]
        rows = _iota(x.shape, 0)
        acc = jnp.zeros_like(x) + b_ref[...]
        for k in range(kw):
            acc = acc + w_ref[k:k + 1, :] * _shift_down(x, kw - 1 - k, rows)
        y_ref[...] = acc

    col = lambda i: (0, i)
    return pl.pallas_call(
        body, grid=(c // LANES,),
        in_specs=[pl.BlockSpec((t, LANES), col), pl.BlockSpec((kw, LANES), col), pl.BlockSpec((1, LANES), col)],
        out_specs=pl.BlockSpec((t, LANES), col), out_shape=jax.ShapeDtypeStruct((t, c), F32),
        compiler_params=_params(("arbitrary",)), name=name)(x, w, b)


def _dwconv_bwd(name, x, w, dy):
    t, c = x.shape
    kw = w.shape[0]

    def body(x_ref, w_ref, dy_ref, dx_ref, dw_ref, db_ref):
        x, dy = x_ref[...], dy_ref[...]
        rows = _iota(x.shape, 0)
        dx = jnp.zeros_like(x)
        for k in range(kw):
            s = kw - 1 - k
            dx = dx + w_ref[k:k + 1, :] * _shift_up(dy, s, rows)
            dw_ref[k:k + 1, :] = jnp.sum(dy * _shift_down(x, s, rows), axis=0, keepdims=True)
        dx_ref[...] = dx
        db_ref[...] = jnp.sum(dy, axis=0, keepdims=True)

    col = lambda i: (0, i)
    big, wsp, bsp = pl.BlockSpec((t, LANES), col), pl.BlockSpec((kw, LANES), col), pl.BlockSpec((1, LANES), col)
    return pl.pallas_call(
        body, grid=(c // LANES,), in_specs=[big, wsp, big], out_specs=[big, wsp, bsp],
        out_shape=[jax.ShapeDtypeStruct((t, c), F32), jax.ShapeDtypeStruct((kw, c), F32),
                   jax.ShapeDtypeStruct((1, c), F32)],
        compiler_params=_params(("arbitrary",)), name=name)(x, w, dy)


def dwconv(name, x, w, b):
    @jax.custom_vjp
    def run(x, w, b):
        return _dwconv_fwd(name, x, w, b)

    def fwd(x, w, b):
        return run(x, w, b), (x, w)

    def bwd(r, g):
        return tuple(_dwconv_bwd(name + "_bwd", r[0], r[1], g))

    run.defvjp(fwd, bwd)
    return run(x, w, b)


def _ssd_tile(xc, dtr, dtb, alog, dsk, prev):
    ln = xc.shape[0]
    gw = SSD_INNER // SSD_GROUPS
    ns = SSD_STATE
    xs = jax.nn.silu(xc[:, :SSD_INNER])
    bm = jax.nn.silu(xc[:, SSD_INNER:SSD_INNER + SSD_GROUPS * ns])
    cm = jax.nn.silu(xc[:, SSD_INNER + SSD_GROUPS * ns:])
    dt = jax.nn.softplus(dtr + dtb)
    a = dt * (-jnp.exp(alog))
    expand = (_iota((LANES, SSD_INNER), 0) == _iota((LANES, SSD_INNER), 1) // SSD_HEAD_DIM).astype(F32)
    causal = _iota((ln, ln), 0) >= _iota((ln, ln), 1)
    acs_h = hdot(causal.astype(F32), a)
    acs_c = hdot(acs_h, expand)
    dt_c = hdot(dt, expand)

    def row_per_column(v):
        return jnp.mean(hdot(jnp.broadcast_to(v, (8, LANES)), expand), axis=0, keepdims=True)

    aend_c = row_per_column(jnp.sum(a, axis=0, keepdims=True))
    xdt = xs * dt_c
    to_end = xdt * jnp.exp(aend_c - acs_c)
    from_start = jnp.exp(acs_c)
    acs_ht = acs_h.T
    lane_h, sub_h = _iota((1, LANES), 1), _iota((LANES, 1), 0)
    head_of_col = _iota((1, gw), 1) // SSD_HEAD_DIM
    ys, states = [], []
    for g in range(SSD_GROUPS):
        cg = cm[:, g * ns:(g + 1) * ns]
        bg = bm[:, g * ns:(g + 1) * ns]
        cols = slice(g * gw, (g + 1) * gw)
        y = bdot(cg, prev[:, cols]) * from_start[:, cols]
        states.append(bdot_tn(bg, to_end[:, cols]))
        cb = bdot_nt(cg, bg)
        for r in range(gw // SSD_HEAD_DIM):
            h = g * (gw // SSD_HEAD_DIM) + r
            col = jnp.sum(jnp.where(lane_h == h, acs_h, 0.0), axis=1, keepdims=True)
            row = jnp.sum(jnp.where(sub_h == h, acs_ht, 0.0), axis=0, keepdims=True)
            decay = jnp.exp(jnp.where(causal, col - row, -1e30))
            y = y + jnp.where(head_of_col == r, bdot(cb * decay, xdt[:, cols]), 0.0)
        ys.append(y)
    y = jnp.concatenate(ys, axis=1) + row_per_column(dsk) * xs
    new = prev * jnp.exp(aend_c) + jnp.concatenate(states, axis=1)
    return y, new


def _ssd_fwd(name, xc, dtr, dtb, alog, dsk):
    t = xc.shape[0]
    ln = min(SSD_TILE, t)
    nt = t // ln

    def body(xc_ref, dtr_ref, dtb_ref, alog_ref, dsk_ref, y_ref, prev_ref, carry):
        @pl.when(pl.program_id(0) == 0)
        def _():
            carry[...] = jnp.zeros_like(carry)

        prev = carry[...]
        prev_ref[...] = prev
        y, new = _ssd_tile(xc_ref[...], dtr_ref[...], dtb_ref[...], alog_ref[...], dsk_ref[...], prev)
        y_ref[...] = y
        carry[...] = new

    row = lambda i: (i, 0)
    par = pl.BlockSpec((1, LANES), lambda i: (0, 0))
    return pl.pallas_call(
        body, grid=(nt,),
        in_specs=[pl.BlockSpec((ln, xc.shape[1]), row), pl.BlockSpec((ln, LANES), row), par, par, par],
        out_specs=[pl.BlockSpec((ln, SSD_INNER), row), pl.BlockSpec((None, SSD_STATE, SSD_INNER), lambda i: (i, 0, 0))],
        out_shape=[jax.ShapeDtypeStruct((t, SSD_INNER), F32), jax.ShapeDtypeStruct((nt, SSD_STATE, SSD_INNER), F32)],
        scratch_shapes=[pltpu.VMEM((SSD_STATE, SSD_INNER), F32)],
        compiler_params=_params(("arbitrary",)), name=name)(xc, dtr, dtb, alog, dsk)


def _ssd_bwd(name, xc, dtr, dtb, alog, dsk, prevs, dy):
    t = xc.shape[0]
    ln = min(SSD_TILE, t)
    nt = t // ln

    def body(xc_ref, dtr_ref, dtb_ref, alog_ref, dsk_ref, prev_ref, dy_ref, dxc_ref, ddtr_ref, ddtb_ref, dalog_ref,
             ddsk_ref, dcarry):
        i = pl.program_id(0)

        @pl.when(i == 0)
        def _():
            dcarry[...] = jnp.zeros_like(dcarry)

        _, vjp = jax.vjp(_ssd_tile, xc_ref[...], dtr_ref[...], dtb_ref[...], alog_ref[...], dsk_ref[...], prev_ref[...])
        dxc, ddtr, ddtb, dalog, ddsk, dprev = vjp((dy_ref[...], dcarry[...]))
        dxc_ref[...] = dxc
        ddtr_ref[...] = ddtr
        dcarry[...] = dprev
        for r, gr in ((ddtb_ref, ddtb), (dalog_ref, dalog), (ddsk_ref, ddsk)):
            @pl.when(i == 0)
            def _(r=r, gr=gr):
                r[...] = gr

            @pl.when(i != 0)
            def _(r=r, gr=gr):
                r[...] += gr

    row = lambda i: (nt - 1 - i, 0)
    par = pl.BlockSpec((1, LANES), lambda i: (0, 0))
    big, dts = pl.BlockSpec((ln, xc.shape[1]), row), pl.BlockSpec((ln, LANES), row)
    par_shape = jax.ShapeDtypeStruct((1, LANES), F32)
    return pl.pallas_call(
        body, grid=(nt,),
        in_specs=[big, dts, par, par, par, pl.BlockSpec((None, SSD_STATE, SSD_INNER), lambda i: (nt - 1 - i, 0, 0)),
                  pl.BlockSpec((ln, SSD_INNER), row)],
        out_specs=[big, dts, par, par, par],
        out_shape=[jax.ShapeDtypeStruct(xc.shape, F32), jax.ShapeDtypeStruct(dtr.shape, F32), par_shape, par_shape,
                   par_shape],
        scratch_shapes=[pltpu.VMEM((SSD_STATE, SSD_INNER), F32)],
        compiler_params=_params(("arbitrary",)), name=name)(xc, dtr, dtb, alog, dsk, prevs, dy)


def ssd_scan(name, xc, dtr, dtb, alog, dsk):
    @jax.custom_vjp
    def run(xc, dtr, dtb, alog, dsk):
        return _ssd_fwd(name, xc, dtr, dtb, alog, dsk)[0]

    def fwd(xc, dtr, dtb, alog, dsk):
        y, prevs = _ssd_fwd(name, xc, dtr, dtb, alog, dsk)
        return y, (xc, dtr, dtb, alog, dsk, prevs)

    def bwd(r, g):
        return tuple(_ssd_bwd(name + "_bwd", *r, g))

    run.defvjp(fwd, bwd)
    return run(xc, dtr, dtb, alog, dsk)


def loss_head(y, target):
    t, n = y.shape
    tile = min(ROW_TILE, t)

    def body(y_ref, t_ref, dy_ref, acc_ref):
        d = y_ref[...] - t_ref[...]
        dy_ref[...] = d * (1.0 / n)

        @pl.when(pl.program_id(0) == 0)
        def _():
            acc_ref[...] = jnp.zeros_like(acc_ref)

        acc_ref[...] += jnp.sum(d * d, axis=0, keepdims=True)

    row = pl.BlockSpec((tile, n), lambda i: (i, 0))
    dy, acc = pl.pallas_call(
        body, grid=(t // tile,), in_specs=[row, row], out_specs=[row, pl.BlockSpec((1, n), lambda i: (0, 0))],
        out_shape=[jax.ShapeDtypeStruct((t, n), F32), jax.ShapeDtypeStruct((1, n), F32)],
        compiler_params=_params(("arbitrary",)), name="loss_head")(y, target)
    return acc, dy


def adamw(name, w, g, m, v):
    shape = w.shape
    cols = shape[-1]
    rows = w.size // cols
    tile = _divisor(rows, 512, 8) if rows % 8 == 0 else rows

    def body(w_ref, g_ref, m_ref, v_ref, d_ref, nm_ref, nv_ref):
        g = g_ref[...]
        m = ADAM_B1 * m_ref[...] + (1.0 - ADAM_B1) * g
        v = ADAM_B2 * v_ref[...] + (1.0 - ADAM_B2) * jnp.square(g)
        m_hat = m / (1.0 - ADAM_B1 ** ADAM_STEP)
        v_hat = v / (1.0 - ADAM_B2 ** ADAM_STEP)
        d_ref[...] = -ADAM_LR * (m_hat / (jnp.sqrt(v_hat) + ADAM_EPS) + ADAM_WD * w_ref[...])
        nm_ref[...] = m
        nv_ref[...] = v

    spec = pl.BlockSpec((tile, cols), lambda i: (i, 0))
    two_d = jax.ShapeDtypeStruct((rows, cols), F32)
    outs = pl.pallas_call(body, grid=(rows // tile,), in_specs=[spec] * 4, out_specs=[spec] * 3, out_shape=[two_d] * 3,
                          compiler_params=_params(("arbitrary",)), name=name)(
        *[a.reshape(rows, cols) for a in (w, g, m, v)])
    return [o.reshape(shape) for o in outs]


HBM_SPEC = pl.BlockSpec(memory_space=pl.ANY)


def _position():
    return lax.axis_index("x"), lax.axis_index("y"), lax.axis_index("c")


def all_gather(name, shard):
    def body(x_ref, out_ref, send_sems, recv_sems, local_sem):
        x, y, c = _position()
        me, sibling = (x, y, c), (x, y, 1 - c)
        chips = [(1 - x, y), (x, 1 - y), (1 - x, 1 - y)]

        def block(px, py, pc):
            return out_ref.at[4 * px + 2 * py + pc]

        def copy(k, blk, to, src=None):
            return pltpu.make_async_remote_copy(
                src_ref=block(*blk) if src is None else src, dst_ref=block(*blk), send_sem=send_sems.at[k],
                recv_sem=recv_sems.at[k], device_id=to, device_id_type=MESH_ID)

        mine = pltpu.make_async_copy(x_ref, block(*me), local_sem)
        mine.start()
        first = [copy(0, me, sibling, src=x_ref)]
        first += [copy(1 + j, me, (*chip, c), src=x_ref) for j, chip in enumerate(chips)]
        for cp in first:
            cp.start()
        passed = [copy(4 + j, (*chip, c), sibling) for j, chip in enumerate(chips)]
        for j, chip in enumerate(chips):
            copy(1 + j, (*chip, c), me).wait_recv()
            passed[j].start()
        copy(0, sibling, me).wait_recv()
        for j, chip in enumerate(chips):
            copy(4 + j, (*chip, 1 - c), me).wait_recv()
        for cp in first + passed:
            cp.wait_send()
        mine.wait()

    return pl.pallas_call(
        body, in_specs=[HBM_SPEC], out_specs=HBM_SPEC,
        out_shape=jax.ShapeDtypeStruct((N_DEV,) + shard.shape, shard.dtype),
        scratch_shapes=[pltpu.SemaphoreType.DMA((7,)), pltpu.SemaphoreType.DMA((7,)), pltpu.SemaphoreType.DMA],
        name=name)(shard)


def pair_exchange(name, g):
    def body(g_ref, out_ref, send_sem, recv_sem):
        x, y, c = _position()
        cp = pltpu.make_async_remote_copy(src_ref=g_ref.at[1 - c], dst_ref=out_ref, send_sem=send_sem,
                                          recv_sem=recv_sem, device_id=(x, y, 1 - c), device_id_type=MESH_ID)
        cp.start()
        cp.wait()

    return pl.pallas_call(
        body, in_specs=[HBM_SPEC], out_specs=HBM_SPEC, out_shape=jax.ShapeDtypeStruct(g.shape[1:], g.dtype),
        scratch_shapes=[pltpu.SemaphoreType.DMA, pltpu.SemaphoreType.DMA], name=name)(g)


def chip_exchange(name, p):
    def body(p_ref, out_ref, send_sems, recv_sems):
        x, y, c = _position()
        copies = []
        for j, (px, py) in enumerate([(1 - x, y), (x, 1 - y), (1 - x, 1 - y)]):
            cp = pltpu.make_async_remote_copy(
                src_ref=p_ref.at[2 * px + py], dst_ref=out_ref.at[j], send_sem=send_sems.at[j],
                recv_sem=recv_sems.at[j], device_id=(px, py, c), device_id_type=MESH_ID)
            cp.start()
            copies.append(cp)
        for cp in copies:
            cp.wait()

    return pl.pallas_call(
        body, in_specs=[HBM_SPEC], out_specs=HBM_SPEC, out_shape=jax.ShapeDtypeStruct((3,) + p.shape[1:], p.dtype),
        scratch_shapes=[pltpu.SemaphoreType.DMA((3,)), pltpu.SemaphoreType.DMA((3,))], name=name)(p)


def _sum_tile(r):
    return _divisor(r, 512, 8) if r % 8 == 0 else r


def pair_reduce(name, g, got, my_c, my_chip, wire):
    _, nchip, r, c_ = g.shape
    tile = _sum_tile(r)

    def body(ids, g_ref, got_ref, p_ref, mine_ref):
        s = g_ref[...] + got_ref[...]
        p_ref[...] = s.astype(wire)

        @pl.when(pl.program_id(1) == ids[1])
        def _():
            mine_ref[...] = s

    return pl.pallas_call(
        body,
        grid_spec=pltpu.PrefetchScalarGridSpec(
            num_scalar_prefetch=1, grid=(r // tile, nchip),
            in_specs=[pl.BlockSpec((None, None, tile, c_), lambda i, k, ids: (ids[0], k, i, 0)),
                      pl.BlockSpec((None, tile, c_), lambda i, k, ids: (k, i, 0))],
            out_specs=[pl.BlockSpec((None, tile, c_), lambda i, k, ids: (k, i, 0)),
                       pl.BlockSpec((tile, c_), lambda i, k, ids: (i, 0))]),
        out_shape=[jax.ShapeDtypeStruct((nchip, r, c_), wire), jax.ShapeDtypeStruct((r, c_), F32)],
        compiler_params=_params(("arbitrary", "arbitrary")), name=name)(
        jnp.stack([my_c, my_chip]).astype(jnp.int32), g, got)


def chip_reduce(name, mine, got):
    r, c_ = mine.shape
    tile = _sum_tile(r)

    def body(m_ref, got_ref, o_ref):
        o_ref[...] = ((m_ref[...] + got_ref[0].astype(F32)) + got_ref[1].astype(F32)) + got_ref[2].astype(F32)

    return pl.pallas_call(
        body, grid=(r // tile,),
        in_specs=[pl.BlockSpec((tile, c_), lambda i: (i, 0)), pl.BlockSpec((3, tile, c_), lambda i: (0, i, 0))],
        out_specs=pl.BlockSpec((tile, c_), lambda i: (i, 0)), out_shape=jax.ShapeDtypeStruct((r, c_), F32),
        compiler_params=_params(("arbitrary",)), name=name)(mine, got)


def sum_blocks(name, a):
    n, r, c_ = a.shape
    tile = _sum_tile(r)

    def body(a_ref, o_ref):
        acc = a_ref[0]
        for k in range(1, n):
            acc = acc + a_ref[k]
        o_ref[...] = acc

    return pl.pallas_call(
        body, grid=(r // tile,), in_specs=[pl.BlockSpec((n, tile, c_), lambda i: (0, i, 0))],
        out_specs=pl.BlockSpec((tile, c_), lambda i: (i, 0)), out_shape=jax.ShapeDtypeStruct((r, c_), F32),
        compiler_params=_params(("arbitrary",)), name=name)(a)


GROUPS = [
    ("rows1024", ["ssd_w_out", "conv_w_out", "mla_w_o", "w_out", "xattn_w_q", "xattn_w_o", "ffn_w_out"], BF16, 1024),
    ("w_in", ["w_in"], BF16, 1114),
    ("ffn_w_in", ["ffn_w_in"], BF16, 704),
    ("cols256", ["xattn_w_kv", "mla_w_kv_b"], BF16, 256),
    ("w_q_b", ["mla_w_q_b"], BF16, 192),
    ("small", ["ssd_conv_w", "conv_dw_w", "gate_b"], F32, 128),
]
PACK_COLS = 1024


def _pack(arrays):
    flat = jnp.concatenate([a.reshape(-1) for a in arrays])
    rows = -(-flat.shape[0] // PACK_COLS)
    rows += -rows % 8
    return jnp.pad(flat, (0, rows * PACK_COLS - flat.shape[0])).reshape(rows, PACK_COLS)


def _unpack(buf, shapes):
    flat = buf.reshape(-1)
    out, off = [], 0
    for s in shapes:
        n = 1
        for d in s:
            n *= d
        out.append(flat[off:off + n].reshape(tuple(s)))
        off += n
    return out


def _stack_rows(arrays, width, lead):
    return jnp.concatenate([a.reshape(a.shape[:lead] + (-1, width)) for a in arrays], axis=lead)


def _unstack_rows(buf, shapes, width, lead):
    out, off = [], 0
    for s in shapes:
        n = 1
        for d in s:
            n *= d
        rows = n // width
        idx = (slice(None),) * lead + (slice(off, off + rows),)
        out.append(buf[idx].reshape(buf.shape[:lead] + tuple(s)))
        off += rows
    return out


def _join_shards(blocks, axis):
    ax = axis + 1
    moved = jnp.moveaxis(blocks, 0, ax)
    s = moved.shape
    return moved.reshape(s[:ax] + (s[ax] * s[ax + 1],) + s[ax + 2:])


def _split_by_owner(full, axis):
    ax = axis + 1
    s = full.shape
    cut = full.reshape(s[:ax] + (2, 2, 2, s[ax] // N_DEV) + s[ax + 1:])
    cut = jnp.moveaxis(cut, (ax + 2, ax, ax + 1), (0, 1, 2))
    return cut.reshape((2, 4) + cut.shape[3:])


def _pad_cols(w, n):
    return jnp.pad(w, ((0, 0), (0, n - w.shape[1])))


def _prep_layer(w):
    o = [0]
    for n in IN_SIZES:
        o.append(o[-1] + n)
    w_in = w["w_in"]
    kv_lat = o[5]
    q = w["mla_w_q_b"].reshape(MLA_Q_RANK, MLA_HEADS, MLA_NOPE + MLA_ROPE)
    kv = w["mla_w_kv_b"].reshape(MLA_KV_RANK, MLA_HEADS, MLA_NOPE + MLA_V)

    def row(v):
        return v.reshape(1, -1)

    def norm_pair(g):
        return _pad_cols(row(g), 2 * LANES)

    return {
        "mix_norm_g": row(w["mix_norm_g"]),
        "w_z": w_in[:, o[0]:o[1]], "w_xbc": w_in[:, o[1]:o[2]], "w_dt": _pad_cols(w_in[:, o[2]:o[3]], LANES),
        "w_glu": w_in[:, o[3]:o[4]], "w_q": w_in[:, o[4]:o[5]], "w_ckv": w_in[:, kv_lat:kv_lat + MLA_KV_RANK],
        "w_kr": _pad_cols(w_in[:, kv_lat + MLA_KV_RANK:o[6]], LANES), "w_gate": w_in[:, o[6]:o[7]],
        "ssd_conv_w": w["ssd_conv_w"], "ssd_conv_b": row(w["ssd_conv_b"]),
        "ssd_dt_bias": _pad_cols(row(w["ssd_dt_bias"]), LANES), "ssd_a_log": _pad_cols(row(w["ssd_a_log"]), LANES),
        "ssd_d": _pad_cols(row(w["ssd_d"]), LANES), "ssd_norm_g": row(w["ssd_norm_g"]), "ssd_w_out": w["ssd_w_out"],
        "conv_dw_w": w["conv_dw_w"], "conv_dw_b": row(w["conv_dw_b"]), "conv_ln_g": row(w["conv_ln_g"]),
        "conv_ln_b": row(w["conv_ln_b"]), "conv_w_out": w["conv_w_out"],
        "mla_q_a_g": row(w["mla_q_a_g"]), "mla_kv_a_g": row(w["mla_kv_a_g"]),
        "w_qn": q[:, :, :MLA_NOPE].reshape(MLA_Q_RANK, -1),
        "w_qr": jnp.pad(q[:, :, MLA_NOPE:], ((0, 0), (0, 0), (0, LANES - MLA_ROPE))).reshape(MLA_Q_RANK, -1),
        "w_kn": kv[:, :, :MLA_NOPE].reshape(MLA_KV_RANK, -1), "w_v": kv[:, :, MLA_NOPE:].reshape(MLA_KV_RANK, -1),
        "mla_q_norm_g": norm_pair(w["mla_q_norm_g"]), "mla_k_norm_g": norm_pair(w["mla_k_norm_g"]),
        "mla_w_o": w["mla_w_o"], "gate_b": row(w["gate_b"]), "w_out": w["w_out"],
        "xattn_norm_g": row(w["xattn_norm_g"]), "mem_norm_g": row(w["mem_norm_g"]), "xattn_w_q": w["xattn_w_q"],
        "w_xk": w["xattn_w_kv"][:, :D_MODEL], "w_xv": w["xattn_w_kv"][:, D_MODEL:],
        "xattn_q_norm_g": row(w["xattn_q_norm_g"]), "xattn_k_norm_g": row(w["xattn_k_norm_g"]),
        "xattn_w_o": w["xattn_w_o"], "ffn_norm_g": row(w["ffn_norm_g"]),
        "w_ffn_gate": w["ffn_w_in"][:, :FFN_HIDDEN], "w_ffn_up": w["ffn_w_in"][:, FFN_HIDDEN:],
        "ffn_w_out": w["ffn_w_out"],
    }


def _unprep_grads(g):
    n_dt, n_kr = IN_SIZES[2], MLA_ROPE
    qn = g["w_qn"].reshape(MLA_Q_RANK, MLA_HEADS, MLA_NOPE)
    qr = g["w_qr"].reshape(MLA_Q_RANK, MLA_HEADS, LANES)[:, :, :MLA_ROPE]
    kn = g["w_kn"].reshape(MLA_KV_RANK, MLA_HEADS, MLA_NOPE)
    vv = g["w_v"].reshape(MLA_KV_RANK, MLA_HEADS, MLA_V)
    flat = lambda v: v.reshape(-1)
    return {
        "mix_norm_g": flat(g["mix_norm_g"]),
        "w_in": jnp.concatenate([g["w_z"], g["w_xbc"], g["w_dt"][:, :n_dt], g["w_glu"], g["w_q"], g["w_ckv"],
                                 g["w_kr"][:, :n_kr], g["w_gate"]], axis=1),
        "ssd_conv_w": g["ssd_conv_w"], "ssd_conv_b": flat(g["ssd_conv_b"]),
        "ssd_dt_bias": flat(g["ssd_dt_bias"])[:SSD_HEADS], "ssd_a_log": flat(g["ssd_a_log"])[:SSD_HEADS],
        "ssd_d": flat(g["ssd_d"])[:SSD_HEADS], "ssd_norm_g": flat(g["ssd_norm_g"]), "ssd_w_out": g["ssd_w_out"],
        "conv_dw_w": g["conv_dw_w"], "conv_dw_b": flat(g["conv_dw_b"]), "conv_ln_g": flat(g["conv_ln_g"]),
        "conv_ln_b": flat(g["conv_ln_b"]), "conv_w_out": g["conv_w_out"],
        "mla_q_a_g": flat(g["mla_q_a_g"]), "mla_w_q_b": jnp.concatenate([qn, qr], axis=2).reshape(MLA_Q_RANK, -1),
        "mla_kv_a_g": flat(g["mla_kv_a_g"]), "mla_w_kv_b": jnp.concatenate([kn, vv], axis=2).reshape(MLA_KV_RANK, -1),
        "mla_q_norm_g": flat(g["mla_q_norm_g"])[:MLA_NOPE + MLA_ROPE],
        "mla_k_norm_g": flat(g["mla_k_norm_g"])[:MLA_NOPE + MLA_ROPE],
        "mla_w_o": g["mla_w_o"], "gate_b": g["gate_b"].reshape(3, D_MODEL), "w_out": g["w_out"],
        "xattn_norm_g": flat(g["xattn_norm_g"]), "mem_norm_g": flat(g["mem_norm_g"]), "xattn_w_q": g["xattn_w_q"],
        "xattn_w_kv": jnp.concatenate([g["w_xk"], g["w_xv"]], axis=1),
        "xattn_q_norm_g": flat(g["xattn_q_norm_g"]), "xattn_k_norm_g": flat(g["xattn_k_norm_g"]),
        "xattn_w_o": g["xattn_w_o"], "ffn_norm_g": flat(g["ffn_norm_g"]),
        "ffn_w_in": jnp.concatenate([g["w_ffn_gate"], g["w_ffn_up"]], axis=1), "ffn_w_out": g["ffn_w_out"],
    }


def _layer(l, x, mem, cosf, sinf, w):
    t = x.shape[0]
    n = lambda s: f"l{l}_{s}"
    tile = min(ROW_TILE, t)
    grid = (1, t // tile)

    def rowwise(name, f, ins, width):
        return tmap(n(name), f, grid, ins, [_row_out(t, width, tile)])[0]

    u = rms_norm(n("mix_norm"), x, w["mix_norm_g"])
    z = matmul(n("in_z"), u, w["w_z"])
    xbc = matmul(n("in_xbc"), u, w["w_xbc"])
    dtr = matmul(n("in_dt"), u, w["w_dt"])
    glu = matmul(n("in_glu"), u, w["w_glu"])
    q_lat = matmul(n("in_q"), u, w["w_q"])
    c_kv = matmul(n("in_ckv"), u, w["w_ckv"])
    kr_raw = matmul(n("in_kr"), u, w["w_kr"])
    gate_logits = matmul(n("in_gate"), u, w["w_gate"])

    xc = dwconv(n("ssd_conv"), xbc, w["ssd_conv_w"], w["ssd_conv_b"])
    y_scan = ssd_scan(n("ssd_scan"), xc, dtr, w["ssd_dt_bias"], w["ssd_a_log"], w["ssd_d"])
    y_norm = rowwise("ssd_gate_norm", _ssd_gate_norm_f, [_rows(y_scan, tile), _rows(z, tile), _whole(w["ssd_norm_g"])],
                     SSD_INNER)
    y_ssd = matmul(n("ssd_out"), y_norm, w["ssd_w_out"])

    v = rowwise("glu", _glu_f, [_rows(glu, tile)], D_MODEL)
    v = dwconv(n("conv_dw"), v, w["conv_dw_w"], w["conv_dw_b"])
    v = rowwise("conv_ln_silu", _ln_silu_f, [_rows(v, tile), _whole(w["conv_ln_g"]), _whole(w["conv_ln_b"])], D_MODEL)
    y_conv = matmul(n("conv_out"), v, w["conv_w_out"])

    q_n = rms_norm(n("q_a_norm"), q_lat, w["mla_q_a_g"])
    qn_raw = matmul(n("q_nope"), q_n, w["w_qn"])
    qr_raw = matmul(n("q_rope"), q_n, w["w_qr"])
    c_n = rms_norm(n("kv_a_norm"), c_kv, w["mla_kv_a_g"])
    kn_raw = matmul(n("k_nope"), c_n, w["w_kn"])
    val = matmul(n("mla_v"), c_n, w["w_v"])
    tables = [_rows(cosf, tile, "n"), _rows(sinf, tile, "n")]
    kn = rowwise("k_nope_norm", _k_nope_f, [_rows(kn_raw, tile), _whole(w["mla_k_norm_g"])], MLA_HEADS * MLA_NOPE)
    kr = rowwise("k_rope", _k_rope_f, [_rows(kr_raw, tile)] + tables + [_whole(w["mla_k_norm_g"])], LANES)
    wide = _row_out(t, MLA_HEADS * LANES, tile)
    qn, qr = tmap(n("q_prep"), _q_prep_f, grid,
                  [_rows(qn_raw, tile), _rows(qr_raw, tile)] + tables + [_whole(w["mla_q_norm_g"])], [wide, wide])
    att = mla_attention(n("mla_attn"), qn, qr, kn, kr, val)
    y_mla = matmul(n("mla_out"), att, w["mla_w_o"])

    merged = rowwise("merge", _merge_f, [_rows(gate_logits, tile), _whole(w["gate_b"]), _rows(y_ssd, tile),
                                         _rows(y_conv, tile), _rows(y_mla, tile)], D_MODEL)
    x = matmul(n("mix_out"), merged, w["w_out"], res=x)

    h = rms_norm(n("xattn_norm"), x, w["xattn_norm_g"])
    mem_n = rms_norm_nograd_x(n("mem_norm"), mem, w["mem_norm_g"])
    xq = matmul(n("xattn_q"), h, w["xattn_w_q"])
    xk = matmul(n("xattn_k"), mem_n, w["w_xk"])
    xv = matmul(n("xattn_v"), mem_n, w["w_xv"])
    m = mem.shape[0]
    txq = min(XATT_Q_TILE, t)
    kv_head = lambda arr: (arr, (m, X_HEAD_DIM), lambda o, i: (0, o), "ai")
    xo = tmap(n("xattn"), _xattn_f, (X_HEADS, t // txq),
              [(xq, (txq, X_HEAD_DIM), lambda o, i: (i, o), "t"), kv_head(xk), kv_head(xv),
               _whole(w["xattn_q_norm_g"]), _whole(w["xattn_k_norm_g"])],
              [((t, D_MODEL), (txq, X_HEAD_DIM), lambda o, i: (i, o))])[0]
    x = matmul(n("xattn_out"), xo, w["xattn_w_o"], res=x)

    h = rms_norm(n("ffn_norm"), x, w["ffn_norm_g"])
    gate = matmul(n("ffn_gate"), h, w["w_ffn_gate"])
    up = matmul(n("ffn_up"), h, w["w_ffn_up"])
    act = rowwise("swiglu", _swiglu_f, [_rows(gate, tile), _rows(up, tile)], FFN_HIDDEN)
    return matmul(n("ffn_out"), act, w["ffn_w_out"], res=x)


def _rope_tables(positions):
    inv = ROPE_THETA ** (-jnp.arange(0, MLA_ROPE, 2, dtype=F32) / MLA_ROPE)
    ang = positions.astype(F32)[:, None] * inv
    pad = jnp.zeros((positions.shape[0], LANES - MLA_ROPE), F32)
    cos, sin = jnp.cos(ang), jnp.sin(ang)
    return jnp.concatenate([cos, cos, pad], axis=1), jnp.concatenate([sin, sin, pad], axis=1)


def local_step(x, mem, positions, target, weights):
    cosf, sinf = _rope_tables(positions)
    prepped = [_prep_layer({k: v[l] for k, v in weights.items()}) for l in range(DEPTH)]
    diff = [{k: jnp.zeros(v.shape, F32) if k in MATRICES else v for k, v in p.items()} for p in prepped]

    def forward(x, diff):
        for l in range(DEPTH):
            w = {k: Mat(prepped[l][k], s) if k in MATRICES else s for k, s in diff[l].items()}
            x = _layer(l, x, mem, cosf, sinf, w)
        return x

    y, pull = jax.vjp(forward, x, diff)
    sq, dy = loss_head(y, target)
    gx, gp = pull(dy)
    per_layer = [_unprep_grads(g) for g in gp]
    grads = {k: jnp.stack([pl_[k] for pl_ in per_layer]) for k in WEIGHTS}
    return sq, gx, grads


def _step(x, mem, positions, loss_target, w, m, v):
    xi, yi, ci = _position()

    full = {n: w[n] for n in REPLICATED}
    for gname, names, wire, width in GROUPS:
        shapes = [w[n].shape for n in names]
        stacked = _stack_rows([w[n] for n in names], width, 0).astype(wire)
        gathered = all_gather("gather_" + gname, stacked)
        for n, b in zip(names, _unstack_rows(gathered, shapes, width, 1)):
            full[n] = _join_shards(b, SHARDED[n])

    sq, gx, grads = local_step(x[0], mem[0], positions[0], loss_target[0], full)
    loss = lax.psum(0.5 * jnp.sum(sq) / D_MODEL, ("x", "y", "c"))

    g_shard = {}
    for gname, names, wire, width in GROUPS:
        shapes = [w[n].shape for n in names]
        by_owner = _stack_rows([_split_by_owner(grads[n], SHARDED[n]) for n in names], width, 2)
        from_sibling = pair_exchange("pair_exchange_" + gname, by_owner)
        chip_partial, mine = pair_reduce("pair_reduce_" + gname, by_owner, from_sibling, ci, 2 * xi + yi, wire)
        from_chips = chip_exchange("chip_exchange_" + gname, chip_partial)
        reduced = chip_reduce("chip_reduce_" + gname, mine, from_chips)
        g_shard.update(zip(names, _unstack_rows(reduced, shapes, width, 0)))

    rep_shapes = [w[n].shape for n in REPLICATED]
    rep_all = all_gather("small_grads_all_gather", _pack([grads[n] for n in REPLICATED]))
    g_rep = dict(zip(REPLICATED, _unpack(sum_blocks("small_grads_sum", rep_all), rep_shapes)))

    out_g, out_d, out_m, out_v = [], [], [], []
    for n in WEIGHTS:
        g = g_shard[n] if n in SHARDED else g_rep[n]
        d, nm, nv = adamw("adamw_" + n, w[n], g, m[n], v[n])
        out_g.append(g)
        out_d.append(d)
        out_m.append(nm)
        out_v.append(nv)
    return (loss, gx[None], *out_g, *out_d, *out_m, *out_v)


def kernel(x, mem, positions, mix_norm_g, w_in, ssd_conv_w, ssd_conv_b, ssd_dt_bias, ssd_a_log, ssd_d, ssd_norm_g, ssd_w_out, conv_dw_w, conv_dw_b, conv_ln_g, conv_ln_b, conv_w_out, mla_q_a_g, mla_w_q_b, mla_kv_a_g, mla_w_kv_b, mla_q_norm_g, mla_k_norm_g, mla_w_o, gate_b, w_out, xattn_norm_g, mem_norm_g, xattn_w_q, xattn_w_kv, xattn_q_norm_g, xattn_k_norm_g, xattn_w_o, ffn_norm_g, ffn_w_in, ffn_w_out, loss_target, m_mix_norm_g, m_w_in, m_ssd_conv_w, m_ssd_conv_b, m_ssd_dt_bias, m_ssd_a_log, m_ssd_d, m_ssd_norm_g, m_ssd_w_out, m_conv_dw_w, m_conv_dw_b, m_conv_ln_g, m_conv_ln_b, m_conv_w_out, m_mla_q_a_g, m_mla_w_q_b, m_mla_kv_a_g, m_mla_w_kv_b, m_mla_q_norm_g, m_mla_k_norm_g, m_mla_w_o, m_gate_b, m_w_out, m_xattn_norm_g, m_mem_norm_g, m_xattn_w_q, m_xattn_w_kv, m_xattn_q_norm_g, m_xattn_k_norm_g, m_xattn_w_o, m_ffn_norm_g, m_ffn_w_in, m_ffn_w_out, v_mix_norm_g, v_w_in, v_ssd_conv_w, v_ssd_conv_b, v_ssd_dt_bias, v_ssd_a_log, v_ssd_d, v_ssd_norm_g, v_ssd_w_out, v_conv_dw_w, v_conv_dw_b, v_conv_ln_g, v_conv_ln_b, v_conv_w_out, v_mla_q_a_g, v_mla_w_q_b, v_mla_kv_a_g, v_mla_w_kv_b, v_mla_q_norm_g, v_mla_k_norm_g, v_mla_w_o, v_gate_b, v_w_out, v_xattn_norm_g, v_mem_norm_g, v_xattn_w_q, v_xattn_w_kv, v_xattn_q_norm_g, v_xattn_k_norm_g, v_xattn_w_o, v_ffn_norm_g, v_ffn_w_in, v_ffn_w_out):
    args = locals()
    w = {n: args[n] for n in WEIGHTS}
    m = {n: args["m_" + n] for n in WEIGHTS}
    v = {n: args["v_" + n] for n in WEIGHTS}
    return _step(x, mem, positions, loss_target, w, m, v)
```

```python
from typing import NamedTuple

import jax
import jax.numpy as jnp
from jax import lax
from jax.experimental import pallas as pl
from jax.experimental.pallas import tpu as pltpu

F32 = jnp.float32
BF16 = jnp.bfloat16
HIGHEST = lax.Precision.HIGHEST
MESH_ID = pl.DeviceIdType.MESH

VMEM_LIMIT_BYTES = 56 * 1024 * 1024
LANES = 128

EPS = 1e-6
DEPTH = 4
D_MODEL = 1024
N_DEV = 8
SSD_HEADS = 16
SSD_HEAD_DIM = 64
SSD_STATE = 128
SSD_GROUPS = 4
SSD_INNER = 1024
SSD_TILE = 256
CONV_K = 31
SSD_CONV_K = 4
MLA_HEADS = 8
MLA_NOPE = 128
MLA_ROPE = 64
MLA_V = 128
MLA_Q_RANK = 384
MLA_KV_RANK = 256
ATT_CHUNK = 64
ROPE_THETA = 10000.0
X_HEADS = 4
X_HEAD_DIM = 256
FFN_HIDDEN = 2816
IN_SIZES = (1024, 2048, 16, 2048, 384, 320, 3072)

ADAM_LR = 0.001
ADAM_B1 = 0.9
ADAM_B2 = 0.999
ADAM_EPS = 1e-08
ADAM_WD = 0.01
ADAM_STEP = 10

MM_FULL_K = 3072
ROW_TILE = 256
ATT_BLOCK = 512
XATT_Q_TILE = 512

SHARDED = {
    "w_in": 1, "ssd_conv_w": 1, "ssd_w_out": 0, "conv_dw_w": 1, "conv_w_out": 0, "mla_w_q_b": 1, "mla_w_kv_b": 1,
    "mla_w_o": 0, "gate_b": 1, "w_out": 0, "xattn_w_q": 0, "xattn_w_kv": 1, "xattn_w_o": 0, "ffn_w_in": 1,
    "ffn_w_out": 0,
}
WEIGHTS = ["mix_norm_g", "w_in", "ssd_conv_w", "ssd_conv_b", "ssd_dt_bias", "ssd_a_log", "ssd_d", "ssd_norm_g",
           "ssd_w_out", "conv_dw_w", "conv_dw_b", "conv_ln_g", "conv_ln_b", "conv_w_out", "mla_q_a_g", "mla_w_q_b",
           "mla_kv_a_g", "mla_w_kv_b", "mla_q_norm_g", "mla_k_norm_g", "mla_w_o", "gate_b", "w_out", "xattn_norm_g",
           "mem_norm_g", "xattn_w_q", "xattn_w_kv", "xattn_q_norm_g", "xattn_k_norm_g", "xattn_w_o", "ffn_norm_g",
           "ffn_w_in", "ffn_w_out"]
REPLICATED = [n for n in WEIGHTS if n not in SHARDED]


def _params(sem=None):
    return pltpu.CompilerParams(dimension_semantics=sem, vmem_limit_bytes=VMEM_LIMIT_BYTES)


def _divisor(n, cap, mult):
    if n <= cap:
        return n
    for d in range(cap - cap % mult, 0, -mult):
        if n % d == 0:
            return d
    raise ValueError(f"no tile for {n}")


def _dg(a, b, ca, cb):
    return lax.dot_general(a.astype(BF16), b.astype(BF16), (((ca,), (cb,)), ((), ())), preferred_element_type=F32)


@jax.custom_vjp
def bdot(a, b):
    return _dg(a, b, 1, 0)


bdot.defvjp(lambda a, b: (_dg(a, b, 1, 0), (a, b)), lambda r, g: (_dg(g, r[1], 1, 1), _dg(r[0], g, 0, 0)))


@jax.custom_vjp
def bdot_nt(a, b):
    return _dg(a, b, 1, 1)


bdot_nt.defvjp(lambda a, b: (_dg(a, b, 1, 1), (a, b)), lambda r, g: (_dg(g, r[1], 1, 0), _dg(g, r[0], 0, 0)))


@jax.custom_vjp
def bdot_tn(a, b):
    return _dg(a, b, 0, 0)


bdot_tn.defvjp(lambda a, b: (_dg(a, b, 0, 0), (a, b)), lambda r, g: (_dg(r[1], g, 1, 1), _dg(r[0], g, 1, 0)))


def hdot(a, b):
    return jnp.dot(a, b, precision=HIGHEST, preferred_element_type=F32)


def _iota(shape, dim):
    return lax.broadcasted_iota(jnp.int32, shape, dim)


def _mm(name, a, b, ta=False, tb=False, res=None):
    m, k = (a.shape[1], a.shape[0]) if ta else a.shape
    n = b.shape[0] if tb else b.shape[1]
    tm = _divisor(m, 1024, LANES if ta else 8)
    tn = _divisor(n, 1536, LANES)
    tk = k if k <= MM_FULL_K else _divisor(k, 1024, LANES)
    if tk == k and k > 1024:
        tm = _divisor(m, 512, LANES if ta else 8)
    nk = k // tk
    dims = (((0 if ta else 1,), (1 if tb else 0,)), ((), ()))

    def body(*refs):
        if res is None:
            a_ref, b_ref, o_ref = refs
        else:
            a_ref, b_ref, r_ref, o_ref = refs
        part = lax.dot_general(a_ref[...].astype(BF16), b_ref[...].astype(BF16), dims, preferred_element_type=F32)
        if nk == 1:
            o_ref[...] = part if res is None else part + r_ref[...]
        else:
            kk = pl.program_id(2)

            @pl.when(kk == 0)
            def _():
                o_ref[...] = part if res is None else part + r_ref[...]

            @pl.when(kk != 0)
            def _():
                o_ref[...] += part

    a_spec = pl.BlockSpec((tk, tm), lambda i, j, kk: (kk, i)) if ta else pl.BlockSpec((tm, tk), lambda i, j, kk: (i, kk))
    b_spec = pl.BlockSpec((tn, tk), lambda i, j, kk: (j, kk)) if tb else pl.BlockSpec((tk, tn), lambda i, j, kk: (kk, j))
    o_spec = pl.BlockSpec((tm, tn), lambda i, j, kk: (i, j))
    in_specs = [a_spec, b_spec] + ([] if res is None else [o_spec])
    args = (a, b) + (() if res is None else (res,))
    return pl.pallas_call(
        body, grid=(m // tm, n // tn, nk), in_specs=in_specs, out_specs=o_spec,
        out_shape=jax.ShapeDtypeStruct((m, n), F32),
        compiler_params=_params(("parallel", "parallel", "arbitrary")), name=name)(*args)


class Mat(NamedTuple):
    value: jax.Array
    slot: jax.Array


MATRICES = frozenset([
    "w_z", "w_xbc", "w_dt", "w_glu", "w_q", "w_ckv", "w_kr", "w_gate", "ssd_w_out", "conv_w_out", "w_qn", "w_qr",
    "w_kn", "w_v", "mla_w_o", "w_out", "xattn_w_q", "w_xk", "w_xv", "xattn_w_o", "w_ffn_gate", "w_ffn_up",
    "ffn_w_out"])


def matmul(name, a, mat, res=None):
    w, slot = mat
    if res is None:
        @jax.custom_vjp
        def run(a, w, slot):
            return _mm(name, a, w)

        def fwd(a, w, slot):
            return run(a, w, slot), (a, w)

        def bwd(r, g):
            return _mm(name + "_da", g, r[1], tb=True), None, _mm(name + "_dw", r[0], g, ta=True)

        run.defvjp(fwd, bwd)
        return run(a, w, slot)

    @jax.custom_vjp
    def run_res(a, w, slot, res):
        return _mm(name, a, w, res=res)

    def fwd_res(a, w, slot, res):
        return run_res(a, w, slot, res), (a, w)

    def bwd_res(r, g):
        return _mm(name + "_da", g, r[1], tb=True), None, _mm(name + "_dw", r[0], g, ta=True), g

    run_res.defvjp(fwd_res, bwd_res)
    return run_res(a, w, slot, res)


def multi_matmul(name, a, mats, keys):
    ws, slots = tuple(m.value for m in mats), tuple(m.slot for m in mats)

    @jax.custom_vjp
    def run(a, ws, slots):
        return tuple(_mm(f"{name}_{k}", a, w) for k, w in zip(keys, ws))

    def fwd(a, ws, slots):
        return run(a, ws, slots), (a, ws)

    def bwd(r, gs):
        a, ws = r
        da = None
        for k, w, g in zip(keys, ws, gs):
            da = _mm(f"{name}_{k}_da", g, w, tb=True, res=da)
        dws = tuple(_mm(f"{name}_{k}_dw", a, g, ta=True) for k, g in zip(keys, gs))
        return da, tuple(None for _ in ws), dws

    run.defvjp(fwd, bwd)
    return run(a, ws, slots)


def tmap(name, f, grid, ins, outs, through=None):
    arrays = [x[0] for x in ins]
    kinds = [x[3] for x in ins]
    in_specs = [pl.BlockSpec(x[1], x[2]) for x in ins]
    out_specs = [pl.BlockSpec(x[1], x[2]) for x in outs]
    out_shape = [jax.ShapeDtypeStruct(x[0], F32) for x in outs]
    n_in, n_out = len(ins), len(outs)
    n_through = 0 if through is None else 1
    assert through is None or kinds[through] == "t"
    didx = [k for k, kd in enumerate(kinds) if kd != "n"]

    def fwd_call(*arrs):
        def body(*refs):
            pids = (pl.program_id(0), pl.program_id(1))
            vals = f(pids, *[r[...] for r in refs[:n_in]])
            for r, v in zip(refs[n_in:], vals):
                r[...] = v

        return pl.pallas_call(body, grid=grid, in_specs=in_specs, out_specs=out_specs, out_shape=out_shape,
                              compiler_params=_params(("arbitrary", "arbitrary")), name=name)(*arrs)

    def bwd_call(arrs, cts):
        def body(*refs):
            o, i = pl.program_id(0), pl.program_id(1)
            vals = [r[...] for r in refs[:n_in]]

            def g(*dv):
                full = list(vals)
                for k, v in zip(didx, dv):
                    full[k] = v
                return tuple(f((o, i), *full))

            _, vjp = jax.vjp(g, *[vals[k] for k in didx])
            grads = vjp(tuple(r[...] for r in refs[n_in:n_in + n_out]))
            for k, gr, r in zip(didx, grads, refs[n_in + n_out + n_through:]):
                if k == through:
                    r[...] = gr + refs[n_in + n_out][...]
                elif kinds[k] == "t":
                    r[...] = gr
                else:
                    first = (i == 0) if kinds[k] == "ai" else jnp.logical_and(o == 0, i == 0)

                    @pl.when(first)
                    def _(r=r, gr=gr):
                        r[...] = gr

                    @pl.when(jnp.logical_not(first))
                    def _(r=r, gr=gr):
                        r[...] += gr

        g_specs = [in_specs[k] for k in didx]
        g_shape = [jax.ShapeDtypeStruct(arrs[k].shape, F32) for k in didx]
        ct_specs = out_specs + ([in_specs[through]] if n_through else [])
        return pl.pallas_call(body, grid=grid, in_specs=in_specs + ct_specs, out_specs=g_specs, out_shape=g_shape,
                              compiler_params=_params(("arbitrary", "arbitrary")), name=name + "_bwd")(*arrs, *cts)

    @jax.custom_vjp
    def run(*arrs):
        return tuple(fwd_call(*arrs)) + ((arrs[through],) if n_through else ())

    def run_fwd(*arrs):
        return run(*arrs), arrs

    def run_bwd(arrs, cts):
        gs = bwd_call(arrs, cts)
        full = [None] * n_in
        for k, g in zip(didx, gs):
            full[k] = g
        return tuple(full)

    run.defvjp(run_fwd, run_bwd)
    return run(*arrays)


def _rows(arr, tile, kind="t"):
    return (arr, (tile, arr.shape[1]), lambda o, i: (i, 0), kind)


def _whole(arr, kind="ag"):
    return (arr, arr.shape, lambda o, i: (0, 0), kind)


def _row_out(t, n, tile):
    return ((t, n), (tile, n), lambda o, i: (i, 0))


def _rms(x, g, n=None):
    ms = jnp.sum(x * x, axis=-1, keepdims=True) / (x.shape[-1] if n is None else n)
    return x * lax.rsqrt(ms + EPS) * g


def rms_norm(name, x, g):
    t, n = x.shape
    tile = min(ROW_TILE, t)
    return tmap(name, lambda p, x, g: (_rms(x, g),), (1, t // tile), [_rows(x, tile), _whole(g)],
                [_row_out(t, n, tile)])[0]


def rms_norm_through(name, x, g):
    t, n = x.shape
    tile = min(ROW_TILE, t)
    return tmap(name, lambda p, x, g: (_rms(x, g),), (1, t // tile), [_rows(x, tile), _whole(g)],
                [_row_out(t, n, tile)], through=0)


def rms_norm_nograd_x(name, x, g):
    t, n = x.shape
    tile = min(ROW_TILE, t)
    return tmap(name, lambda p, x, g: (_rms(x, g),), (1, t // tile), [_rows(x, tile, "n"), _whole(g)],
                [_row_out(t, n, tile)])[0]


def _glu_f(p, glu):
    h = glu.shape[1] // 2
    return (glu[:, :h] * jax.nn.sigmoid(glu[:, h:]),)


def _ln_silu_f(p, v, g, b):
    mu = jnp.mean(v, axis=-1, keepdims=True)
    xc = v - mu
    var = jnp.mean(xc * xc, axis=-1, keepdims=True)
    return (jax.nn.silu(xc * lax.rsqrt(var + EPS) * g + b),)


def _ssd_gate_norm_f(p, y, z, g):
    v = y * jax.nn.silu(z)
    w = SSD_INNER // SSD_GROUPS
    parts = []
    for k in range(SSD_GROUPS):
        vg = v[:, k * w:(k + 1) * w]
        parts.append(vg * lax.rsqrt(jnp.mean(vg * vg, axis=-1, keepdims=True) + EPS))
    return (jnp.concatenate(parts, axis=1) * g,)


def _merge_f(p, gl, gb, y0, y1, y2):
    g = jax.nn.sigmoid(gl + gb)
    d = D_MODEL
    return (g[:, :d] * y0 + g[:, d:2 * d] * y1 + g[:, 2 * d:] * y2,)


def _swiglu_f(p, gate, up):
    return (jax.nn.silu(gate) * up,)


def _rot_matrix():
    r, c = _iota((LANES, LANES), 0), _iota((LANES, LANES), 1)
    h = MLA_ROPE // 2
    plus = jnp.logical_and(c >= h, jnp.logical_and(c < 2 * h, r == c - h))
    minus = jnp.logical_and(c < h, r == c + h)
    return plus.astype(F32) - minus.astype(F32)


def _rope(x, cosf, sinf):
    return x * cosf + hdot(x, _rot_matrix()) * sinf


def _per_head(f, x):
    return jnp.concatenate([f(x[:, h * LANES:(h + 1) * LANES]) for h in range(x.shape[1] // LANES)], axis=1)


def _k_nope_f(p, kn_raw, kg):
    return (_per_head(lambda x: _rms(x, kg[:, :MLA_NOPE]), kn_raw),)


def _k_rope_f(p, kr_raw, cosf, sinf, kg):
    return (_rope(_rms(kr_raw, kg[:, MLA_NOPE:], n=MLA_ROPE), cosf, sinf),)


def _q_prep_f(p, qn_raw, qr_raw, cosf, sinf, qg):
    qn = _per_head(lambda x: _rms(x, qg[:, :MLA_NOPE]), qn_raw)
    qr = _per_head(lambda x: _rope(_rms(x, qg[:, MLA_NOPE:], n=MLA_ROPE), cosf, sinf), qr_raw)
    return qn, qr


def _softmax(s):
    m = jnp.max(s, axis=-1, keepdims=True)
    e = jnp.exp(s - m)
    return e / jnp.sum(e, axis=-1, keepdims=True)


def _xattn_f(p, q, k, v, qg, kg):
    s = bdot_nt(_rms(q, qg), _rms(k, kg)) * (X_HEAD_DIM ** -0.5)
    return (bdot(_softmax(s), v),)


ATT_SCALE = (MLA_NOPE + MLA_ROPE) ** -0.5
NT_DIMS = (((1,), (1,)), ((), ()))
NN_DIMS = (((1,), (0,)), ((), ()))
TN_DIMS = (((0,), (0,)), ((), ()))


def _att_specs(t, blk):
    q_spec = pl.BlockSpec((blk, LANES), lambda h, i: (i, h))
    k_spec = pl.BlockSpec((t, LANES), lambda h, i: (0, h))
    shared = pl.BlockSpec((t, LANES), lambda h, i: (0, 0))
    lse_spec = pl.BlockSpec((None, blk, 1), lambda h, i: (h, i, 0))
    return q_spec, k_spec, shared, lse_spec


def _diagonal_mask(blk):
    return (_iota((blk, blk), 1) // ATT_CHUNK) <= (_iota((blk, blk), 0) // ATT_CHUNK)


def _att_keys(kn_ref, kr_ref, j, blk):
    ks = pl.ds(pl.multiple_of(j * blk, blk), blk)
    return ks, jnp.concatenate([kn_ref[ks, :], kr_ref[ks, :]], axis=1).astype(BF16)


def _att_fwd(name, qn, qr, kn, kr, v):
    t, width = qn.shape
    heads = width // LANES
    blk = min(ATT_BLOCK, t)

    def body(qn_ref, qr_ref, kn_ref, kr_ref, v_ref, o_ref, lse_ref):
        i = pl.program_id(1)
        q = jnp.concatenate([qn_ref[...], qr_ref[...]], axis=1).astype(BF16)

        def scores(j):
            _, k = _att_keys(kn_ref, kr_ref, j, blk)
            return lax.dot_general(q, k, NT_DIMS, preferred_element_type=F32)

        def weighted_values(p, j):
            ks = pl.ds(pl.multiple_of(j * blk, blk), blk)
            return lax.dot_general(p, v_ref[ks, :].astype(BF16), NN_DIMS, preferred_element_type=F32)

        def softmax_step(s, m, l):
            m_new = jnp.maximum(m, jnp.max(s, axis=1, keepdims=True))
            alpha = jnp.exp(m - m_new)
            p = jnp.exp(s - m_new)
            return m_new, alpha, alpha * l + jnp.sum(p, axis=1, keepdims=True), p.astype(BF16)

        def step(j, carry):
            s, p_prev, m, l, acc = carry
            s_next = scores(j + 1)
            pv_prev = weighted_values(p_prev, jnp.maximum(j - 1, 0))
            m, alpha, l, p = softmax_step(s * ATT_SCALE, m, l)
            return s_next, p, m, l, alpha * (acc + pv_prev)

        init = (scores(0), jnp.zeros((blk, blk), BF16), jnp.full((blk, 1), -1e30, F32), jnp.zeros((blk, 1), F32),
                jnp.zeros((blk, LANES), F32))
        s, p_prev, m, l, acc = lax.fori_loop(0, i, step, init)
        pv_prev = weighted_values(p_prev, jnp.maximum(i - 1, 0))
        s = jnp.where(_diagonal_mask(blk), s * ATT_SCALE, -1e30)
        m, alpha, l, p = softmax_step(s, m, l)
        acc = alpha * (acc + pv_prev) + weighted_values(p, i)
        o_ref[...] = acc / l
        lse_ref[...] = m + jnp.log(l)

    q_spec, k_spec, shared, lse_spec = _att_specs(t, blk)
    return pl.pallas_call(
        body, grid=(heads, t // blk), in_specs=[q_spec, q_spec, k_spec, shared, k_spec], out_specs=[q_spec, lse_spec],
        out_shape=[jax.ShapeDtypeStruct((t, width), F32), jax.ShapeDtypeStruct((heads, t, 1), F32)],
        compiler_params=_params(("arbitrary", "arbitrary")), name=name)(qn, qr, kn, kr, v)


def _att_bwd(name, qn, qr, kn, kr, v, o, lse, do):
    t, width = qn.shape
    heads = width // LANES
    blk = min(ATT_BLOCK, t)

    def body(qn_ref, qr_ref, kn_ref, kr_ref, v_ref, o_ref, lse_ref, do_ref, dqn_ref, dqr_ref, dkn_ref, dkr_ref,
             dv_ref):
        h, i = pl.program_id(0), pl.program_id(1)

        @pl.when(i == 0)
        def _():
            dkn_ref[...] = jnp.zeros_like(dkn_ref)
            dv_ref[...] = jnp.zeros_like(dv_ref)

        @pl.when(jnp.logical_and(h == 0, i == 0))
        def _():
            dkr_ref[...] = jnp.zeros_like(dkr_ref)

        q = jnp.concatenate([qn_ref[...], qr_ref[...]], axis=1).astype(BF16)
        do = do_ref[...]
        do16 = do.astype(BF16)
        delta = jnp.sum(do * o_ref[...], axis=1, keepdims=True)
        lse = lse_ref[...]

        def issue(j):
            ks, k = _att_keys(kn_ref, kr_ref, j, blk)
            s = lax.dot_general(q, k, NT_DIMS, preferred_element_type=F32)
            dp = lax.dot_general(do16, v_ref[ks, :].astype(BF16), NT_DIMS, preferred_element_type=F32)
            return s, dp

        def retire(p, ds, j, dq):
            ks, k = _att_keys(kn_ref, kr_ref, j, blk)
            dv_ref[ks, :] += lax.dot_general(p, do16, TN_DIMS, preferred_element_type=F32)
            dk = lax.dot_general(ds, q, TN_DIMS, preferred_element_type=F32)
            dkn_ref[ks, :] += dk[:, :LANES]
            dkr_ref[ks, :] += dk[:, LANES:]
            return dq + lax.dot_general(ds, k, NN_DIMS, preferred_element_type=F32)

        def probs(s, dp, masked):
            s = s * ATT_SCALE
            if masked:
                s = jnp.where(_diagonal_mask(blk), s, -1e30)
            p = jnp.exp(s - lse)
            return p.astype(BF16), (p * (dp - delta) * ATT_SCALE).astype(BF16)

        def step(j, carry):
            s, dp, p_prev, ds_prev, dq = carry
            s_next, dp_next = issue(j + 1)
            dq = retire(p_prev, ds_prev, jnp.maximum(j - 1, 0), dq)
            p, ds = probs(s, dp, False)
            return s_next, dp_next, p, ds, dq

        none = jnp.zeros((blk, blk), BF16)
        s, dp, p_prev, ds_prev, dq = lax.fori_loop(0, i, step,
                                                   issue(0) + (none, none, jnp.zeros((blk, 2 * LANES), F32)))
        dq = retire(p_prev, ds_prev, jnp.maximum(i - 1, 0), dq)
        p, ds = probs(s, dp, True)
        dq = retire(p, ds, i, dq)
        dqn_ref[...] = dq[:, :LANES]
        dqr_ref[...] = dq[:, LANES:]

    q_spec, k_spec, shared, lse_spec = _att_specs(t, blk)
    big, one = jax.ShapeDtypeStruct((t, width), F32), jax.ShapeDtypeStruct((t, LANES), F32)
    return pl.pallas_call(
        body, grid=(heads, t // blk),
        in_specs=[q_spec, q_spec, k_spec, shared, k_spec, q_spec, lse_spec, q_spec],
        out_specs=[q_spec, q_spec, k_spec, shared, k_spec], out_shape=[big, big, big, one, big],
        compiler_params=_params(("arbitrary", "arbitrary")), name=name)(qn, qr, kn, kr, v, o, lse, do)


def mla_attention(name, qn, qr, kn, kr, v):
    @jax.custom_vjp
    def run(qn, qr, kn, kr, v):
        return _att_fwd(name, qn, qr, kn, kr, v)[0]

    def fwd(qn, qr, kn, kr, v):
        o, lse = _att_fwd(name, qn, qr, kn, kr, v)
        return o, (qn, qr, kn, kr, v, o, lse)

    def bwd(r, g):
        return tuple(_att_bwd(name + "_bwd", *r, g))

    run.defvjp(fwd, bwd)
    return run(qn, qr, kn, kr, v)


def _shift_down(v, s, rows):
    return v if s == 0 else jnp.where(rows >= s, pltpu.roll(v, s, 0), 0.0)


def _shift_up(v, s, rows):
    t = v.shape[0]
    return v if s == 0 else jnp.where(rows < t - s, pltpu.roll(v, t - s, 0), 0.0)


def _dwconv_fwd(name, x, w, b):
    t, c = x.shape
    kw = w.shape[0]

    def body(x_ref, w_ref, b_ref, y_ref):
        x = x_ref[...]
        rows = _iota(x.shape, 0)
        acc = jnp.zeros_like(x) + b_ref[...]
        for k in range(kw):
            acc = acc + w_ref[k:k + 1, :] * _shift_down(x, kw - 1 - k, rows)
        y_ref[...] = acc

    col = lambda i: (0, i)
    return pl.pallas_call(
        body, grid=(c // LANES,),
        in_specs=[pl.BlockSpec((t, LANES), col), pl.BlockSpec((kw, LANES), col), pl.BlockSpec((1, LANES), col)],
        out_specs=pl.BlockSpec((t, LANES), col), out_shape=jax.ShapeDtypeStruct((t, c), F32),
        compiler_params=_params(("arbitrary",)), name=name)(x, w, b)


def _dwconv_bwd(name, x, w, dy):
    t, c = x.shape
    kw = w.shape[0]

    def body(x_ref, w_ref, dy_ref, dx_ref, dw_ref, db_ref):
        x, dy = x_ref[...], dy_ref[...]
        rows = _iota(x.shape, 0)
        dx = jnp.zeros_like(x)
        for k in range(kw):
            s = kw - 1 - k
            dx = dx + w_ref[k:k + 1, :] * _shift_up(dy, s, rows)
            dw_ref[k:k + 1, :] = jnp.sum(dy * _shift_down(x, s, rows), axis=0, keepdims=True)
        dx_ref[...] = dx
        db_ref[...] = jnp.sum(dy, axis=0, keepdims=True)

    col = lambda i: (0, i)
    big, wsp, bsp = pl.BlockSpec((t, LANES), col), pl.BlockSpec((kw, LANES), col), pl.BlockSpec((1, LANES), col)
    return pl.pallas_call(
        body, grid=(c // LANES,), in_specs=[big, wsp, big], out_specs=[big, wsp, bsp],
        out_shape=[jax.ShapeDtypeStruct((t, c), F32), jax.ShapeDtypeStruct((kw, c), F32),
                   jax.ShapeDtypeStruct((1, c), F32)],
        compiler_params=_params(("arbitrary",)), name=name)(x, w, dy)


def dwconv(name, x, w, b):
    @jax.custom_vjp
    def run(x, w, b):
        return _dwconv_fwd(name, x, w, b)

    def fwd(x, w, b):
        return run(x, w, b), (x, w)

    def bwd(r, g):
        return tuple(_dwconv_bwd(name + "_bwd", r[0], r[1], g))

    run.defvjp(fwd, bwd)
    return run(x, w, b)


def _ssd_tile(xc, dtr, dtb, alog, dsk, prev):
    ln = xc.shape[0]
    gw = SSD_INNER // SSD_GROUPS
    ns = SSD_STATE
    xs = jax.nn.silu(xc[:, :SSD_INNER])
    bm = jax.nn.silu(xc[:, SSD_INNER:SSD_INNER + SSD_GROUPS * ns])
    cm = jax.nn.silu(xc[:, SSD_INNER + SSD_GROUPS * ns:])
    dt = jax.nn.softplus(dtr + dtb)
    a = dt * (-jnp.exp(alog))
    expand = (_iota((LANES, SSD_INNER), 0) == _iota((LANES, SSD_INNER), 1) // SSD_HEAD_DIM).astype(F32)
    causal = _iota((ln, ln), 0) >= _iota((ln, ln), 1)
    acs_h = hdot(causal.astype(F32), a)
    acs_c = hdot(acs_h, expand)
    dt_c = hdot(dt, expand)

    def row_per_column(v):
        return jnp.mean(hdot(jnp.broadcast_to(v, (8, LANES)), expand), axis=0, keepdims=True)

    aend_c = row_per_column(jnp.sum(a, axis=0, keepdims=True))
    xdt = xs * dt_c
    to_end = xdt * jnp.exp(aend_c - acs_c)
    from_start = jnp.exp(acs_c)
    acs_ht = acs_h.T
    lane_h, sub_h = _iota((1, LANES), 1), _iota((LANES, 1), 0)
    head_of_col = _iota((1, gw), 1) // SSD_HEAD_DIM
    ys, states = [], []
    for g in range(SSD_GROUPS):
        cg = cm[:, g * ns:(g + 1) * ns]
        bg = bm[:, g * ns:(g + 1) * ns]
        cols = slice(g * gw, (g + 1) * gw)
        y = bdot(cg, prev[:, cols]) * from_start[:, cols]
        states.append(bdot_tn(bg, to_end[:, cols]))
        cb = bdot_nt(cg, bg)
        for r in range(gw // SSD_HEAD_DIM):
            h = g * (gw // SSD_HEAD_DIM) + r
            col = jnp.sum(jnp.where(lane_h == h, acs_h, 0.0), axis=1, keepdims=True)
            row = jnp.sum(jnp.where(sub_h == h, acs_ht, 0.0), axis=0, keepdims=True)
            decay = jnp.exp(jnp.where(causal, col - row, -1e30))
            y = y + jnp.where(head_of_col == r, bdot(cb * decay, xdt[:, cols]), 0.0)
        ys.append(y)
    y = jnp.concatenate(ys, axis=1) + row_per_column(dsk) * xs
    new = prev * jnp.exp(aend_c) + jnp.concatenate(states, axis=1)
    return y, new


def _ssd_fwd(name, xc, dtr, dtb, alog, dsk):
    t = xc.shape[0]
    ln = min(SSD_TILE, t)
    nt = t // ln

    def body(xc_ref, dtr_ref, dtb_ref, alog_ref, dsk_ref, y_ref, prev_ref, carry):
        @pl.when(pl.program_id(0) == 0)
        def _():
            carry[...] = jnp.zeros_like(carry)

        prev = carry[...]
        prev_ref[...] = prev
        y, new = _ssd_tile(xc_ref[...], dtr_ref[...], dtb_ref[...], alog_ref[...], dsk_ref[...], prev)
        y_ref[...] = y
        carry[...] = new

    row = lambda i: (i, 0)
    par = pl.BlockSpec((1, LANES), lambda i: (0, 0))
    return pl.pallas_call(
        body, grid=(nt,),
        in_specs=[pl.BlockSpec((ln, xc.shape[1]), row), pl.BlockSpec((ln, LANES), row), par, par, par],
        out_specs=[pl.BlockSpec((ln, SSD_INNER), row), pl.BlockSpec((None, SSD_STATE, SSD_INNER), lambda i: (i, 0, 0))],
        out_shape=[jax.ShapeDtypeStruct((t, SSD_INNER), F32), jax.ShapeDtypeStruct((nt, SSD_STATE, SSD_INNER), F32)],
        scratch_shapes=[pltpu.VMEM((SSD_STATE, SSD_INNER), F32)],
        compiler_params=_params(("arbitrary",)), name=name)(xc, dtr, dtb, alog, dsk)


def _ssd_bwd(name, xc, dtr, dtb, alog, dsk, prevs, dy):
    t = xc.shape[0]
    ln = min(SSD_TILE, t)
    nt = t // ln

    def body(xc_ref, dtr_ref, dtb_ref, alog_ref, dsk_ref, prev_ref, dy_ref, dxc_ref, ddtr_ref, ddtb_ref, dalog_ref,
             ddsk_ref, dcarry):
        i = pl.program_id(0)

        @pl.when(i == 0)
        def _():
            dcarry[...] = jnp.zeros_like(dcarry)

        _, vjp = jax.vjp(_ssd_tile, xc_ref[...], dtr_ref[...], dtb_ref[...], alog_ref[...], dsk_ref[...], prev_ref[...])
        dxc, ddtr, ddtb, dalog, ddsk, dprev = vjp((dy_ref[...], dcarry[...]))
        dxc_ref[...] = dxc
        ddtr_ref[...] = ddtr
        dcarry[...] = dprev
        for r, gr in ((ddtb_ref, ddtb), (dalog_ref, dalog), (ddsk_ref, ddsk)):
            @pl.when(i == 0)
            def _(r=r, gr=gr):
                r[...] = gr

            @pl.when(i != 0)
            def _(r=r, gr=gr):
                r[...] += gr

    row = lambda i: (nt - 1 - i, 0)
    par = pl.BlockSpec((1, LANES), lambda i: (0, 0))
    big, dts = pl.BlockSpec((ln, xc.shape[1]), row), pl.BlockSpec((ln, LANES), row)
    par_shape = jax.ShapeDtypeStruct((1, LANES), F32)
    return pl.pallas_call(
        body, grid=(nt,),
        in_specs=[big, dts, par, par, par, pl.BlockSpec((None, SSD_STATE, SSD_INNER), lambda i: (nt - 1 - i, 0, 0)),
                  pl.BlockSpec((ln, SSD_INNER), row)],
        out_specs=[big, dts, par, par, par],
        out_shape=[jax.ShapeDtypeStruct(xc.shape, F32), jax.ShapeDtypeStruct(dtr.shape, F32), par_shape, par_shape,
                   par_shape],
        scratch_shapes=[pltpu.VMEM((SSD_STATE, SSD_INNER), F32)],
        compiler_params=_params(("arbitrary",)), name=name)(xc, dtr, dtb, alog, dsk, prevs, dy)


def ssd_scan(name, xc, dtr, dtb, alog, dsk):
    @jax.custom_vjp
    def run(xc, dtr, dtb, alog, dsk):
        return _ssd_fwd(name, xc, dtr, dtb, alog, dsk)[0]

    def fwd(xc, dtr, dtb, alog, dsk):
        y, prevs = _ssd_fwd(name, xc, dtr, dtb, alog, dsk)
        return y, (xc, dtr, dtb, alog, dsk, prevs)

    def bwd(r, g):
        return tuple(_ssd_bwd(name + "_bwd", *r, g))

    run.defvjp(fwd, bwd)
    return run(xc, dtr, dtb, alog, dsk)


def loss_head(y, target):
    t, n = y.shape
    tile = min(ROW_TILE, t)

    def body(y_ref, t_ref, dy_ref, acc_ref):
        d = y_ref[...] - t_ref[...]
        dy_ref[...] = d * (1.0 / n)

        @pl.when(pl.program_id(0) == 0)
        def _():
            acc_ref[...] = jnp.zeros_like(acc_ref)

        acc_ref[...] += jnp.sum(d * d, axis=0, keepdims=True)

    row = pl.BlockSpec((tile, n), lambda i: (i, 0))
    dy, acc = pl.pallas_call(
        body, grid=(t // tile,), in_specs=[row, row], out_specs=[row, pl.BlockSpec((1, n), lambda i: (0, 0))],
        out_shape=[jax.ShapeDtypeStruct((t, n), F32), jax.ShapeDtypeStruct((1, n), F32)],
        compiler_params=_params(("arbitrary",)), name="loss_head")(y, target)
    return acc, dy


def adamw(name, w, g, m, v):
    shape = w.shape
    cols = shape[-1]
    rows = w.size // cols
    tile = _divisor(rows, 512, 8) if rows % 8 == 0 else rows

    def body(w_ref, g_ref, m_ref, v_ref, d_ref, nm_ref, nv_ref):
        g = g_ref[...]
        m = ADAM_B1 * m_ref[...] + (1.0 - ADAM_B1) * g
        v = ADAM_B2 * v_ref[...] + (1.0 - ADAM_B2) * jnp.square(g)
        m_hat = m / (1.0 - ADAM_B1 ** ADAM_STEP)
        v_hat = v / (1.0 - ADAM_B2 ** ADAM_STEP)
        d_ref[...] = -ADAM_LR * (m_hat / (jnp.sqrt(v_hat) + ADAM_EPS) + ADAM_WD * w_ref[...])
        nm_ref[...] = m
        nv_ref[...] = v

    spec = pl.BlockSpec((tile, cols), lambda i: (i, 0))
    two_d = jax.ShapeDtypeStruct((rows, cols), F32)
    outs = pl.pallas_call(body, grid=(rows // tile,), in_specs=[spec] * 4, out_specs=[spec] * 3, out_shape=[two_d] * 3,
                          compiler_params=_params(("arbitrary",)), name=name)(
        *[a.reshape(rows, cols) for a in (w, g, m, v)])
    return [o.reshape(shape) for o in outs]


HBM_SPEC = pl.BlockSpec(memory_space=pl.ANY)


def _position():
    return lax.axis_index("x"), lax.axis_index("y"), lax.axis_index("c")


def all_gather(name, shard):
    def body(x_ref, out_ref, send_sems, recv_sems, local_sem):
        x, y, c = _position()
        me, sibling = (x, y, c), (x, y, 1 - c)
        chips = [(1 - x, y), (x, 1 - y), (1 - x, 1 - y)]

        def block(px, py, pc):
            return out_ref.at[4 * px + 2 * py + pc]

        def copy(k, blk, to, src=None):
            return pltpu.make_async_remote_copy(
                src_ref=block(*blk) if src is None else src, dst_ref=block(*blk), send_sem=send_sems.at[k],
                recv_sem=recv_sems.at[k], device_id=to, device_id_type=MESH_ID)

        mine = pltpu.make_async_copy(x_ref, block(*me), local_sem)
        mine.start()
        first = [copy(0, me, sibling, src=x_ref)]
        first += [copy(1 + j, me, (*chip, c), src=x_ref) for j, chip in enumerate(chips)]
        for cp in first:
            cp.start()
        passed = [copy(4 + j, (*chip, c), sibling) for j, chip in enumerate(chips)]
        for j, chip in enumerate(chips):
            copy(1 + j, (*chip, c), me).wait_recv()
            passed[j].start()
        copy(0, sibling, me).wait_recv()
        for j, chip in enumerate(chips):
            copy(4 + j, (*chip, 1 - c), me).wait_recv()
        for cp in first + passed:
            cp.wait_send()
        mine.wait()

    return pl.pallas_call(
        body, in_specs=[HBM_SPEC], out_specs=HBM_SPEC,
        out_shape=jax.ShapeDtypeStruct((N_DEV,) + shard.shape, shard.dtype),
        scratch_shapes=[pltpu.SemaphoreType.DMA((7,)), pltpu.SemaphoreType.DMA((7,)), pltpu.SemaphoreType.DMA],
        name=name)(shard)


def pair_exchange(name, g):
    def body(g_ref, out_ref, send_sem, recv_sem):
        x, y, c = _position()
        cp = pltpu.make_async_remote_copy(src_ref=g_ref.at[1 - c], dst_ref=out_ref, send_sem=send_sem,
                                          recv_sem=recv_sem, device_id=(x, y, 1 - c), device_id_type=MESH_ID)
        cp.start()
        cp.wait()

    return pl.pallas_call(
        body, in_specs=[HBM_SPEC], out_specs=HBM_SPEC, out_shape=jax.ShapeDtypeStruct(g.shape[1:], g.dtype),
        scratch_shapes=[pltpu.SemaphoreType.DMA, pltpu.SemaphoreType.DMA], name=name)(g)


def chip_exchange(name, p):
    def body(p_ref, out_ref, send_sems, recv_sems):
        x, y, c = _position()
        copies = []
        for j, (px, py) in enumerate([(1 - x, y), (x, 1 - y), (1 - x, 1 - y)]):
            cp = pltpu.make_async_remote_copy(
                src_ref=p_ref.at[2 * px + py], dst_ref=out_ref.at[j], send_sem=send_sems.at[j],
                recv_sem=recv_sems.at[j], device_id=(px, py, c), device_id_type=MESH_ID)
            cp.start()
            copies.append(cp)
        for cp in copies:
            cp.wait()

    return pl.pallas_call(
        body, in_specs=[HBM_SPEC], out_specs=HBM_SPEC, out_shape=jax.ShapeDtypeStruct((3,) + p.shape[1:], p.dtype),
        scratch_shapes=[pltpu.SemaphoreType.DMA((3,)), pltpu.SemaphoreType.DMA((3,))], name=name)(p)


def _sum_tile(r):
    return _divisor(r, 512, 8) if r % 8 == 0 else r


def pair_reduce(name, g, got, my_c, my_chip, wire):
    _, nchip, r, c_ = g.shape
    tile = _sum_tile(r)

    def body(ids, g_ref, got_ref, p_ref, mine_ref):
        s = g_ref[...] + got_ref[...]
        p_ref[...] = s.astype(wire)

        @pl.when(pl.program_id(1) == ids[1])
        def _():
            mine_ref[...] = s

    return pl.pallas_call(
        body,
        grid_spec=pltpu.PrefetchScalarGridSpec(
            num_scalar_prefetch=1, grid=(r // tile, nchip),
            in_specs=[pl.BlockSpec((None, None, tile, c_), lambda i, k, ids: (ids[0], k, i, 0)),
                      pl.BlockSpec((None, tile, c_), lambda i, k, ids: (k, i, 0))],
            out_specs=[pl.BlockSpec((None, tile, c_), lambda i, k, ids: (k, i, 0)),
                       pl.BlockSpec((tile, c_), lambda i, k, ids: (i, 0))]),
        out_shape=[jax.ShapeDtypeStruct((nchip, r, c_), wire), jax.ShapeDtypeStruct((r, c_), F32)],
        compiler_params=_params(("arbitrary", "arbitrary")), name=name)(
        jnp.stack([my_c, my_chip]).astype(jnp.int32), g, got)


def chip_reduce(name, mine, got):
    r, c_ = mine.shape
    tile = _sum_tile(r)

    def body(m_ref, got_ref, o_ref):
        o_ref[...] = ((m_ref[...] + got_ref[0].astype(F32)) + got_ref[1].astype(F32)) + got_ref[2].astype(F32)

    return pl.pallas_call(
        body, grid=(r // tile,),
        in_specs=[pl.BlockSpec((tile, c_), lambda i: (i, 0)), pl.BlockSpec((3, tile, c_), lambda i: (0, i, 0))],
        out_specs=pl.BlockSpec((tile, c_), lambda i: (i, 0)), out_shape=jax.ShapeDtypeStruct((r, c_), F32),
        compiler_params=_params(("arbitrary",)), name=name)(mine, got)


def sum_blocks(name, a):
    n, r, c_ = a.shape
    tile = _sum_tile(r)

    def body(a_ref, o_ref):
        acc = a_ref[0]
        for k in range(1, n):
            acc = acc + a_ref[k]
        o_ref[...] = acc

    return pl.pallas_call(
        body, grid=(r // tile,), in_specs=[pl.BlockSpec((n, tile, c_), lambda i: (0, i, 0))],
        out_specs=pl.BlockSpec((tile, c_), lambda i: (i, 0)), out_shape=jax.ShapeDtypeStruct((r, c_), F32),
        compiler_params=_params(("arbitrary",)), name=name)(a)


GROUPS = [
    ("rows1024", ["ssd_w_out", "conv_w_out", "mla_w_o", "w_out", "xattn_w_q", "xattn_w_o", "ffn_w_out"], BF16, 1024),
    ("w_in", ["w_in"], BF16, 1114),
    ("ffn_w_in", ["ffn_w_in"], BF16, 704),
    ("cols256", ["xattn_w_kv", "mla_w_kv_b"], BF16, 256),
    ("w_q_b", ["mla_w_q_b"], BF16, 192),
    ("small", ["ssd_conv_w", "conv_dw_w", "gate_b"], F32, 128),
]
PACK_COLS = 1024


def _pack(arrays):
    flat = jnp.concatenate([a.reshape(-1) for a in arrays])
    rows = -(-flat.shape[0] // PACK_COLS)
    rows += -rows % 8
    return jnp.pad(flat, (0, rows * PACK_COLS - flat.shape[0])).reshape(rows, PACK_COLS)


def _unpack(buf, shapes):
    flat = buf.reshape(-1)
    out, off = [], 0
    for s in shapes:
        n = 1
        for d in s:
            n *= d
        out.append(flat[off:off + n].reshape(tuple(s)))
        off += n
    return out


def _stack_rows(arrays, width, lead):
    return jnp.concatenate([a.reshape(a.shape[:lead] + (-1, width)) for a in arrays], axis=lead)


def _unstack_rows(buf, shapes, width, lead):
    out, off = [], 0
    for s in shapes:
        n = 1
        for d in s:
            n *= d
        rows = n // width
        idx = (slice(None),) * lead + (slice(off, off + rows),)
        out.append(buf[idx].reshape(buf.shape[:lead] + tuple(s)))
        off += rows
    return out


def _join_shards(blocks, axis):
    ax = axis + 1
    moved = jnp.moveaxis(blocks, 0, ax)
    s = moved.shape
    return moved.reshape(s[:ax] + (s[ax] * s[ax + 1],) + s[ax + 2:])


def _split_by_owner(full, axis):
    ax = axis + 1
    s = full.shape
    cut = full.reshape(s[:ax] + (2, 2, 2, s[ax] // N_DEV) + s[ax + 1:])
    cut = jnp.moveaxis(cut, (ax + 2, ax, ax + 1), (0, 1, 2))
    return cut.reshape((2, 4) + cut.shape[3:])


def _blocks_by_owner(blocks):
    cut = blocks.reshape((2, 2, 2) + blocks.shape[1:])
    return jnp.moveaxis(cut, 2, 0).reshape((2, 4) + blocks.shape[1:])


def _pad_cols(w, n):
    return jnp.pad(w, ((0, 0), (0, n - w.shape[1])))


def _regroup_cols(srcs, widths):
    starts = [0]
    for s in srcs:
        starts.append(starts[-1] + s.shape[1])
    assert starts[-1] == sum(widths)
    out, lo = [], 0
    for wd in widths:
        hi = lo + wd
        parts = []
        for s, a, b in zip(srcs, starts[:-1], starts[1:]):
            u, v = max(lo, a), min(hi, b)
            if u < v:
                parts.append(s[:, u - a:v - a])
        out.append(parts[0] if len(parts) == 1 else jnp.concatenate(parts, axis=1))
        lo = hi
    return out


COL_BLOCKED = ("w_in", "ffn_w_in", "xattn_w_kv", "mla_w_kv_b", "mla_w_q_b")
W_IN_PIECES = (1024, 2048, 16, 2048, 384, MLA_KV_RANK, MLA_ROPE, 3072)


def _prep_layer(w):
    w_z, w_xbc, w_dt, w_glu, w_q, w_ckv, w_kr, w_gate = _regroup_cols(list(w["w_in"]), W_IN_PIECES)
    w_ffn_gate, w_ffn_up = _regroup_cols(list(w["ffn_w_in"]), (FFN_HIDDEN, FFN_HIDDEN))
    w_xk, w_xv = _regroup_cols(list(w["xattn_w_kv"]), (D_MODEL, D_MODEL))
    q, kv = w["mla_w_q_b"], w["mla_w_kv_b"]

    def row(v):
        return v.reshape(1, -1)

    def norm_pair(g):
        return _pad_cols(row(g), 2 * LANES)

    return {
        "mix_norm_g": row(w["mix_norm_g"]),
        "w_z": w_z, "w_xbc": w_xbc, "w_dt": _pad_cols(w_dt, LANES), "w_glu": w_glu, "w_q": w_q, "w_ckv": w_ckv,
        "w_kr": _pad_cols(w_kr, LANES), "w_gate": w_gate,
        "ssd_conv_w": w["ssd_conv_w"], "ssd_conv_b": row(w["ssd_conv_b"]),
        "ssd_dt_bias": _pad_cols(row(w["ssd_dt_bias"]), LANES), "ssd_a_log": _pad_cols(row(w["ssd_a_log"]), LANES),
        "ssd_d": _pad_cols(row(w["ssd_d"]), LANES), "ssd_norm_g": row(w["ssd_norm_g"]), "ssd_w_out": w["ssd_w_out"],
        "conv_dw_w": w["conv_dw_w"], "conv_dw_b": row(w["conv_dw_b"]), "conv_ln_g": row(w["conv_ln_g"]),
        "conv_ln_b": row(w["conv_ln_b"]), "conv_w_out": w["conv_w_out"],
        "mla_q_a_g": row(w["mla_q_a_g"]), "mla_kv_a_g": row(w["mla_kv_a_g"]),
        "w_qn": jnp.concatenate([q[h, :, :MLA_NOPE] for h in range(MLA_HEADS)], axis=1),
        "w_qr": jnp.concatenate([_pad_cols(q[h, :, MLA_NOPE:], LANES) for h in range(MLA_HEADS)], axis=1),
        "w_kn": jnp.concatenate([kv[h, :, :MLA_NOPE] for h in range(MLA_HEADS)], axis=1),
        "w_v": jnp.concatenate([kv[h, :, MLA_NOPE:] for h in range(MLA_HEADS)], axis=1),
        "mla_q_norm_g": norm_pair(w["mla_q_norm_g"]), "mla_k_norm_g": norm_pair(w["mla_k_norm_g"]),
        "mla_w_o": w["mla_w_o"], "gate_b": row(w["gate_b"]), "w_out": w["w_out"],
        "xattn_norm_g": row(w["xattn_norm_g"]), "mem_norm_g": row(w["mem_norm_g"]), "xattn_w_q": w["xattn_w_q"],
        "w_xk": w_xk, "w_xv": w_xv,
        "xattn_q_norm_g": row(w["xattn_q_norm_g"]), "xattn_k_norm_g": row(w["xattn_k_norm_g"]),
        "xattn_w_o": w["xattn_w_o"], "ffn_norm_g": row(w["ffn_norm_g"]),
        "w_ffn_gate": w_ffn_gate, "w_ffn_up": w_ffn_up, "ffn_w_out": w["ffn_w_out"],
    }


def _unprep_grads(g):
    n_dt, n_kr = IN_SIZES[2], MLA_ROPE
    flat = lambda v: v.reshape(-1)

    def blocks(srcs):
        total = sum(s.shape[1] for s in srcs)
        return jnp.stack(_regroup_cols(srcs, (total // N_DEV,) * N_DEV))

    def head(a, h, n=LANES):
        return a[:, h * LANES:h * LANES + n]

    return {
        "mix_norm_g": flat(g["mix_norm_g"]),
        "w_in": blocks([g["w_z"], g["w_xbc"], g["w_dt"][:, :n_dt], g["w_glu"], g["w_q"], g["w_ckv"],
                        g["w_kr"][:, :n_kr], g["w_gate"]]),
        "ssd_conv_w": g["ssd_conv_w"], "ssd_conv_b": flat(g["ssd_conv_b"]),
        "ssd_dt_bias": flat(g["ssd_dt_bias"])[:SSD_HEADS], "ssd_a_log": flat(g["ssd_a_log"])[:SSD_HEADS],
        "ssd_d": flat(g["ssd_d"])[:SSD_HEADS], "ssd_norm_g": flat(g["ssd_norm_g"]), "ssd_w_out": g["ssd_w_out"],
        "conv_dw_w": g["conv_dw_w"], "conv_dw_b": flat(g["conv_dw_b"]), "conv_ln_g": flat(g["conv_ln_g"]),
        "conv_ln_b": flat(g["conv_ln_b"]), "conv_w_out": g["conv_w_out"],
        "mla_q_a_g": flat(g["mla_q_a_g"]),
        "mla_w_q_b": jnp.stack([jnp.concatenate([head(g["w_qn"], h), head(g["w_qr"], h, MLA_ROPE)], axis=1)
                                for h in range(MLA_HEADS)]),
        "mla_kv_a_g": flat(g["mla_kv_a_g"]),
        "mla_w_kv_b": jnp.stack([jnp.concatenate([head(g["w_kn"], h), head(g["w_v"], h)], axis=1)
                                 for h in range(MLA_HEADS)]),
        "mla_q_norm_g": flat(g["mla_q_norm_g"])[:MLA_NOPE + MLA_ROPE],
        "mla_k_norm_g": flat(g["mla_k_norm_g"])[:MLA_NOPE + MLA_ROPE],
        "mla_w_o": g["mla_w_o"], "gate_b": g["gate_b"].reshape(3, D_MODEL), "w_out": g["w_out"],
        "xattn_norm_g": flat(g["xattn_norm_g"]), "mem_norm_g": flat(g["mem_norm_g"]), "xattn_w_q": g["xattn_w_q"],
        "xattn_w_kv": blocks([g["w_xk"], g["w_xv"]]),
        "xattn_q_norm_g": flat(g["xattn_q_norm_g"]), "xattn_k_norm_g": flat(g["xattn_k_norm_g"]),
        "xattn_w_o": g["xattn_w_o"], "ffn_norm_g": flat(g["ffn_norm_g"]),
        "ffn_w_in": blocks([g["w_ffn_gate"], g["w_ffn_up"]]), "ffn_w_out": g["ffn_w_out"],
    }


def _layer(l, x, mem, cosf, sinf, w):
    t = x.shape[0]
    n = lambda s: f"l{l}_{s}"
    tile = min(ROW_TILE, t)
    grid = (1, t // tile)

    def rowwise(name, f, ins, width):
        return tmap(n(name), f, grid, ins, [_row_out(t, width, tile)])[0]

    u, x = rms_norm_through(n("mix_norm"), x, w["mix_norm_g"])
    in_keys = ["z", "xbc", "dt", "glu", "q", "ckv", "kr", "gate"]
    z, xbc, dtr, glu, q_lat, c_kv, kr_raw, gate_logits = multi_matmul(n("in"), u, [w["w_" + k] for k in in_keys],
                                                                      in_keys)

    xc = dwconv(n("ssd_conv"), xbc, w["ssd_conv_w"], w["ssd_conv_b"])
    y_scan = ssd_scan(n("ssd_scan"), xc, dtr, w["ssd_dt_bias"], w["ssd_a_log"], w["ssd_d"])
    y_norm = rowwise("ssd_gate_norm", _ssd_gate_norm_f, [_rows(y_scan, tile), _rows(z, tile), _whole(w["ssd_norm_g"])],
                     SSD_INNER)
    y_ssd = matmul(n("ssd_out"), y_norm, w["ssd_w_out"])

    v = rowwise("glu", _glu_f, [_rows(glu, tile)], D_MODEL)
    v = dwconv(n("conv_dw"), v, w["conv_dw_w"], w["conv_dw_b"])
    v = rowwise("conv_ln_silu", _ln_silu_f, [_rows(v, tile), _whole(w["conv_ln_g"]), _whole(w["conv_ln_b"])], D_MODEL)
    y_conv = matmul(n("conv_out"), v, w["conv_w_out"])

    q_n = rms_norm(n("q_a_norm"), q_lat, w["mla_q_a_g"])
    qn_raw, qr_raw = multi_matmul(n("q"), q_n, [w["w_qn"], w["w_qr"]], ["nope", "rope"])
    c_n = rms_norm(n("kv_a_norm"), c_kv, w["mla_kv_a_g"])
    kn_raw, val = multi_matmul(n("kv"), c_n, [w["w_kn"], w["w_v"]], ["nope", "v"])
    tables = [_rows(cosf, tile, "n"), _rows(sinf, tile, "n")]
    kn = rowwise("k_nope_norm", _k_nope_f, [_rows(kn_raw, tile), _whole(w["mla_k_norm_g"])], MLA_HEADS * MLA_NOPE)
    kr = rowwise("k_rope", _k_rope_f, [_rows(kr_raw, tile)] + tables + [_whole(w["mla_k_norm_g"])], LANES)
    wide = _row_out(t, MLA_HEADS * LANES, tile)
    qn, qr = tmap(n("q_prep"), _q_prep_f, grid,
                  [_rows(qn_raw, tile), _rows(qr_raw, tile)] + tables + [_whole(w["mla_q_norm_g"])], [wide, wide])
    att = mla_attention(n("mla_attn"), qn, qr, kn, kr, val)
    y_mla = matmul(n("mla_out"), att, w["mla_w_o"])

    merged = rowwise("merge", _merge_f, [_rows(gate_logits, tile), _whole(w["gate_b"]), _rows(y_ssd, tile),
                                         _rows(y_conv, tile), _rows(y_mla, tile)], D_MODEL)
    x = matmul(n("mix_out"), merged, w["w_out"], res=x)

    h, x = rms_norm_through(n("xattn_norm"), x, w["xattn_norm_g"])
    mem_n = rms_norm_nograd_x(n("mem_norm"), mem, w["mem_norm_g"])
    xq = matmul(n("xattn_q"), h, w["xattn_w_q"])
    xk, xv = multi_matmul(n("xattn_kv"), mem_n, [w["w_xk"], w["w_xv"]], ["k", "v"])
    m = mem.shape[0]
    txq = min(XATT_Q_TILE, t)
    kv_head = lambda arr: (arr, (m, X_HEAD_DIM), lambda o, i: (0, o), "ai")
    xo = tmap(n("xattn"), _xattn_f, (X_HEADS, t // txq),
              [(xq, (txq, X_HEAD_DIM), lambda o, i: (i, o), "t"), kv_head(xk), kv_head(xv),
               _whole(w["xattn_q_norm_g"]), _whole(w["xattn_k_norm_g"])],
              [((t, D_MODEL), (txq, X_HEAD_DIM), lambda o, i: (i, o))])[0]
    x = matmul(n("xattn_out"), xo, w["xattn_w_o"], res=x)

    h, x = rms_norm_through(n("ffn_norm"), x, w["ffn_norm_g"])
    gate, up = multi_matmul(n("ffn_in"), h, [w["w_ffn_gate"], w["w_ffn_up"]], ["gate", "up"])
    act = rowwise("swiglu", _swiglu_f, [_rows(gate, tile), _rows(up, tile)], FFN_HIDDEN)
    return matmul(n("ffn_out"), act, w["ffn_w_out"], res=x)


def _rope_tables(positions):
    inv = ROPE_THETA ** (-jnp.arange(0, MLA_ROPE, 2, dtype=F32) / MLA_ROPE)
    ang = positions.astype(F32)[:, None] * inv
    pad = jnp.zeros((positions.shape[0], LANES - MLA_ROPE), F32)
    cos, sin = jnp.cos(ang), jnp.sin(ang)
    return jnp.concatenate([cos, cos, pad], axis=1), jnp.concatenate([sin, sin, pad], axis=1)


def local_step(x, mem, positions, target, weights):
    cosf, sinf = _rope_tables(positions)
    prepped = [_prep_layer({k: v[:, l] if k in COL_BLOCKED else v[l] for k, v in weights.items()})
               for l in range(DEPTH)]
    diff = [{k: jnp.zeros(v.shape, F32) if k in MATRICES else v for k, v in p.items()} for p in prepped]

    def forward(x, diff):
        for l in range(DEPTH):
            w = {k: Mat(prepped[l][k], s) if k in MATRICES else s for k, s in diff[l].items()}
            x = _layer(l, x, mem, cosf, sinf, w)
        return x

    y, pull = jax.vjp(forward, x, diff)
    sq, dy = loss_head(y, target)
    gx, gp = pull(dy)
    per_layer = [_unprep_grads(g) for g in gp]
    grads = {k: jnp.stack([pl_[k] for pl_ in per_layer], axis=1 if k in COL_BLOCKED else 0) for k in WEIGHTS}
    return sq, gx, grads


def _step(x, mem, positions, loss_target, w, m, v):
    xi, yi, ci = _position()

    full = {n: w[n] for n in REPLICATED}
    for gname, names, wire, width in GROUPS:
        shapes = [w[n].shape for n in names]
        stacked = _stack_rows([w[n] for n in names], width, 0).astype(wire)
        gathered = all_gather("gather_" + gname, stacked)
        for n, b in zip(names, _unstack_rows(gathered, shapes, width, 1)):
            full[n] = b if n in COL_BLOCKED else _join_shards(b, SHARDED[n])

    sq, gx, grads = local_step(x[0], mem[0], positions[0], loss_target[0], full)
    loss = lax.psum(0.5 * jnp.sum(sq) / D_MODEL, ("x", "y", "c"))

    g_shard = {}
    for gname, names, wire, width in GROUPS:
        shapes = [w[n].shape for n in names]
        by_owner = _stack_rows([_blocks_by_owner(grads[n]) if n in COL_BLOCKED else
                                _split_by_owner(grads[n], SHARDED[n]) for n in names], width, 2)
        from_sibling = pair_exchange("pair_exchange_" + gname, by_owner)
        chip_partial, mine = pair_reduce("pair_reduce_" + gname, by_owner, from_sibling, ci, 2 * xi + yi, wire)
        from_chips = chip_exchange("chip_exchange_" + gname, chip_partial)
        reduced = chip_reduce("chip_reduce_" + gname, mine, from_chips)
        g_shard.update(zip(names, _unstack_rows(reduced, shapes, width, 0)))

    rep_shapes = [w[n].shape for n in REPLICATED]
    rep_all = all_gather("small_grads_all_gather", _pack([grads[n] for n in REPLICATED]))
    g_rep = dict(zip(REPLICATED, _unpack(sum_blocks("small_grads_sum", rep_all), rep_shapes)))

    out_g, out_d, out_m, out_v = [], [], [], []
    for n in WEIGHTS:
        g = g_shard[n] if n in SHARDED else g_rep[n]
        d, nm, nv = adamw("adamw_" + n, w[n], g, m[n], v[n])
        out_g.append(g)
        out_d.append(d)
        out_m.append(nm)
        out_v.append(nv)
    return (loss, gx[None], *out_g, *out_d, *out_m, *out_v)


def kernel(x, mem, positions, mix_norm_g, w_in, ssd_conv_w, ssd_conv_b, ssd_dt_bias, ssd_a_log, ssd_d, ssd_norm_g, ssd_w_out, conv_dw_w, conv_dw_b, conv_ln_g, conv_ln_b, conv_w_out, mla_q_a_g, mla_w_q_b, mla_kv_a_g, mla_w_kv_b, mla_q_norm_g, mla_k_norm_g, mla_w_o, gate_b, w_out, xattn_norm_g, mem_norm_g, xattn_w_q, xattn_w_kv, xattn_q_norm_g, xattn_k_norm_g, xattn_w_o, ffn_norm_g, ffn_w_in, ffn_w_out, loss_target, m_mix_norm_g, m_w_in, m_ssd_conv_w, m_ssd_conv_b, m_ssd_dt_bias, m_ssd_a_log, m_ssd_d, m_ssd_norm_g, m_ssd_w_out, m_conv_dw_w, m_conv_dw_b, m_conv_ln_g, m_conv_ln_b, m_conv_w_out, m_mla_q_a_g, m_mla_w_q_b, m_mla_kv_a_g, m_mla_w_kv_b, m_mla_q_norm_g, m_mla_k_norm_g, m_mla_w_o, m_gate_b, m_w_out, m_xattn_norm_g, m_mem_norm_g, m_xattn_w_q, m_xattn_w_kv, m_xattn_q_norm_g, m_xattn_k_norm_g, m_xattn_w_o, m_ffn_norm_g, m_ffn_w_in, m_ffn_w_out, v_mix_norm_g, v_w_in, v_ssd_conv_w, v_ssd_conv_b, v_ssd_dt_bias, v_ssd_a_log, v_ssd_d, v_ssd_norm_g, v_ssd_w_out, v_conv_dw_w, v_conv_dw_b, v_conv_ln_g, v_conv_ln_b, v_conv_w_out, v_mla_q_a_g, v_mla_w_q_b, v_mla_kv_a_g, v_mla_w_kv_b, v_mla_q_norm_g, v_mla_k_norm_g, v_mla_w_o, v_gate_b, v_w_out, v_xattn_norm_g, v_mem_norm_g, v_xattn_w_q, v_xattn_w_kv, v_xattn_q_norm_g, v_xattn_k_norm_g, v_xattn_w_o, v_ffn_norm_g, v_ffn_w_in, v_ffn_w_out):
    args = locals()
    w = {n: args[n] for n in WEIGHTS}
    m = {n: args["m_" + n] for n in WEIGHTS}
    v = {n: args["v_" + n] for n in WEIGHTS}
    return _step(x, mem, positions, loss_target, w, m, v)
```

```python
from typing import NamedTuple

import jax
import jax.numpy as jnp
from jax import lax
from jax.experimental import pallas as pl
from jax.experimental.pallas import tpu as pltpu

F32 = jnp.float32
BF16 = jnp.bfloat16
HIGHEST = lax.Precision.HIGHEST
MESH_ID = pl.DeviceIdType.MESH

VMEM_LIMIT_BYTES = 56 * 1024 * 1024
LANES = 128

EPS = 1e-6
DEPTH = 4
D_MODEL = 1024
N_DEV = 8
SSD_HEADS = 16
SSD_HEAD_DIM = 64
SSD_STATE = 128
SSD_GROUPS = 4
SSD_INNER = 1024
SSD_TILE = 256
CONV_K = 31
SSD_CONV_K = 4
MLA_HEADS = 8
MLA_NOPE = 128
MLA_ROPE = 64
MLA_V = 128
MLA_Q_RANK = 384
MLA_KV_RANK = 256
ATT_CHUNK = 64
ROPE_THETA = 10000.0
X_HEADS = 4
X_HEAD_DIM = 256
FFN_HIDDEN = 2816
IN_SIZES = (1024, 2048, 16, 2048, 384, 320, 3072)

ADAM_LR = 0.001
ADAM_B1 = 0.9
ADAM_B2 = 0.999
ADAM_EPS = 1e-08
ADAM_WD = 0.01
ADAM_STEP = 10

MM_FULL_K = 3072
ROW_TILE = 256
ATT_BLOCK = 512
XATT_Q_TILE = 512

SHARDED = {
    "w_in": 1, "ssd_conv_w": 1, "ssd_w_out": 0, "conv_dw_w": 1, "conv_w_out": 0, "mla_w_q_b": 1, "mla_w_kv_b": 1,
    "mla_w_o": 0, "gate_b": 1, "w_out": 0, "xattn_w_q": 0, "xattn_w_kv": 1, "xattn_w_o": 0, "ffn_w_in": 1,
    "ffn_w_out": 0,
}
WEIGHTS = ["mix_norm_g", "w_in", "ssd_conv_w", "ssd_conv_b", "ssd_dt_bias", "ssd_a_log", "ssd_d", "ssd_norm_g",
           "ssd_w_out", "conv_dw_w", "conv_dw_b", "conv_ln_g", "conv_ln_b", "conv_w_out", "mla_q_a_g", "mla_w_q_b",
           "mla_kv_a_g", "mla_w_kv_b", "mla_q_norm_g", "mla_k_norm_g", "mla_w_o", "gate_b", "w_out", "xattn_norm_g",
           "mem_norm_g", "xattn_w_q", "xattn_w_kv", "xattn_q_norm_g", "xattn_k_norm_g", "xattn_w_o", "ffn_norm_g",
           "ffn_w_in", "ffn_w_out"]
REPLICATED = [n for n in WEIGHTS if n not in SHARDED]


def _params(sem=None):
    return pltpu.CompilerParams(dimension_semantics=sem, vmem_limit_bytes=VMEM_LIMIT_BYTES)


def _divisor(n, cap, mult):
    if n <= cap:
        return n
    for d in range(cap - cap % mult, 0, -mult):
        if n % d == 0:
            return d
    raise ValueError(f"no tile for {n}")


def _dg(a, b, ca, cb):
    return lax.dot_general(a.astype(BF16), b.astype(BF16), (((ca,), (cb,)), ((), ())), preferred_element_type=F32)


@jax.custom_vjp
def bdot(a, b):
    return _dg(a, b, 1, 0)


bdot.defvjp(lambda a, b: (_dg(a, b, 1, 0), (a, b)), lambda r, g: (_dg(g, r[1], 1, 1), _dg(r[0], g, 0, 0)))


@jax.custom_vjp
def bdot_nt(a, b):
    return _dg(a, b, 1, 1)


bdot_nt.defvjp(lambda a, b: (_dg(a, b, 1, 1), (a, b)), lambda r, g: (_dg(g, r[1], 1, 0), _dg(g, r[0], 0, 0)))


@jax.custom_vjp
def bdot_tn(a, b):
    return _dg(a, b, 0, 0)


bdot_tn.defvjp(lambda a, b: (_dg(a, b, 0, 0), (a, b)), lambda r, g: (_dg(r[1], g, 1, 1), _dg(r[0], g, 1, 0)))


def hdot(a, b):
    return jnp.dot(a, b, precision=HIGHEST, preferred_element_type=F32)


def _iota(shape, dim):
    return lax.broadcasted_iota(jnp.int32, shape, dim)


def _mm(name, a, b, ta=False, tb=False, res=None):
    m, k = (a.shape[1], a.shape[0]) if ta else a.shape
    n = b.shape[0] if tb else b.shape[1]
    tm = _divisor(m, 1024, LANES if ta else 8)
    tn = _divisor(n, 1536, LANES)
    tk = k if k <= MM_FULL_K else _divisor(k, 1024, LANES)
    if tk == k and k > 1024:
        tm = _divisor(m, 512, LANES if ta else 8)
    nk = k // tk
    dims = (((0 if ta else 1,), (1 if tb else 0,)), ((), ()))

    def body(*refs):
        if res is None:
            a_ref, b_ref, o_ref = refs
        else:
            a_ref, b_ref, r_ref, o_ref = refs
        part = lax.dot_general(a_ref[...].astype(BF16), b_ref[...].astype(BF16), dims, preferred_element_type=F32)
        if nk == 1:
            o_ref[...] = part if res is None else part + r_ref[...]
        else:
            kk = pl.program_id(2)

            @pl.when(kk == 0)
            def _():
                o_ref[...] = part if res is None else part + r_ref[...]

            @pl.when(kk != 0)
            def _():
                o_ref[...] += part

    a_spec = pl.BlockSpec((tk, tm), lambda i, j, kk: (kk, i)) if ta else pl.BlockSpec((tm, tk), lambda i, j, kk: (i, kk))
    b_spec = pl.BlockSpec((tn, tk), lambda i, j, kk: (j, kk)) if tb else pl.BlockSpec((tk, tn), lambda i, j, kk: (kk, j))
    o_spec = pl.BlockSpec((tm, tn), lambda i, j, kk: (i, j))
    in_specs = [a_spec, b_spec] + ([] if res is None else [o_spec])
    args = (a, b) + (() if res is None else (res,))
    return pl.pallas_call(
        body, grid=(m // tm, n // tn, nk), in_specs=in_specs, out_specs=o_spec,
        out_shape=jax.ShapeDtypeStruct((m, n), F32),
        compiler_params=_params(("parallel", "parallel", "arbitrary")), name=name)(*args)


class Mat(NamedTuple):
    value: jax.Array
    slot: jax.Array


MATRICES = frozenset([
    "w_z", "w_xbc", "w_dt", "w_glu", "w_q", "w_ckv", "w_kr", "w_gate", "ssd_w_out", "conv_w_out", "w_qn", "w_qr",
    "w_kn", "w_v", "mla_w_o", "w_out", "xattn_w_q", "w_xk", "w_xv", "xattn_w_o", "w_ffn_gate", "w_ffn_up",
    "ffn_w_out"])


class Act(NamedTuple):
    value: jax.Array
    slot: jax.Array


def _operand(a):
    return (a.value, a.slot) if isinstance(a, Act) else (a, a)


def matmul(name, a, mat, res=None):
    w, slot = mat
    a, a_slot = _operand(a)
    if res is None:
        @jax.custom_vjp
        def run(a, a_slot, w, slot):
            return _mm(name, a, w)

        def fwd(a, a_slot, w, slot):
            return run(a, a_slot, w, slot), (a, w)

        def bwd(r, g):
            return None, _mm(name + "_da", g, r[1], tb=True), None, _mm(name + "_dw", r[0], g, ta=True)

        run.defvjp(fwd, bwd)
        return run(a, a_slot, w, slot)

    @jax.custom_vjp
    def run_res(a, a_slot, w, slot, res):
        return _mm(name, a, w, res=res)

    def fwd_res(a, a_slot, w, slot, res):
        return run_res(a, a_slot, w, slot, res), (a, w)

    def bwd_res(r, g):
        return None, _mm(name + "_da", g, r[1], tb=True), None, _mm(name + "_dw", r[0], g, ta=True), g

    run_res.defvjp(fwd_res, bwd_res)
    return run_res(a, a_slot, w, slot, res)


def multi_matmul(name, a, mats, keys):
    ws, slots = tuple(m.value for m in mats), tuple(m.slot for m in mats)
    a, a_slot = _operand(a)

    @jax.custom_vjp
    def run(a, a_slot, ws, slots):
        return tuple(_mm(f"{name}_{k}", a, w) for k, w in zip(keys, ws))

    def fwd(a, a_slot, ws, slots):
        return run(a, a_slot, ws, slots), (a, ws)

    def bwd(r, gs):
        a, ws = r
        da = None
        for k, w, g in zip(keys, ws, gs):
            da = _mm(f"{name}_{k}_da", g, w, tb=True, res=da)
        dws = tuple(_mm(f"{name}_{k}_dw", a, g, ta=True) for k, g in zip(keys, gs))
        return None, da, tuple(None for _ in ws), dws

    run.defvjp(fwd, bwd)
    return run(a, a_slot, ws, slots)


def tmap(name, f, grid, ins, outs, through=None, narrow=()):
    arrays = [x[0] for x in ins]
    kinds = [x[3] for x in ins]
    in_specs = [pl.BlockSpec(x[1], x[2]) for x in ins]
    out_specs = [pl.BlockSpec(x[1], x[2]) for x in outs]
    out_shape = [jax.ShapeDtypeStruct(x[0], BF16 if k in narrow else F32) for k, x in enumerate(outs)]
    n_in, n_out = len(ins), len(outs)
    n_through = 0 if through is None else 1
    assert through is None or kinds[through] == "t"
    didx = [k for k, kd in enumerate(kinds) if kd != "n"]

    def fwd_call(*arrs):
        def body(*refs):
            pids = (pl.program_id(0), pl.program_id(1))
            vals = f(pids, *[r[...] for r in refs[:n_in]])
            for r, v in zip(refs[n_in:], vals):
                r[...] = v.astype(r.dtype)

        return pl.pallas_call(body, grid=grid, in_specs=in_specs, out_specs=out_specs, out_shape=out_shape,
                              compiler_params=_params(("arbitrary", "arbitrary")), name=name)(*arrs)

    def bwd_call(arrs, cts):
        def body(*refs):
            o, i = pl.program_id(0), pl.program_id(1)
            vals = [r[...] for r in refs[:n_in]]

            def g(*dv):
                full = list(vals)
                for k, v in zip(didx, dv):
                    full[k] = v
                return tuple(f((o, i), *full))

            _, vjp = jax.vjp(g, *[vals[k] for k in didx])
            grads = vjp(tuple(r[...] for r in refs[n_in:n_in + n_out]))
            for k, gr, r in zip(didx, grads, refs[n_in + n_out + n_through:]):
                if k == through:
                    r[...] = gr + refs[n_in + n_out][...]
                elif kinds[k] == "t":
                    r[...] = gr
                else:
                    first = (i == 0) if kinds[k] == "ai" else jnp.logical_and(o == 0, i == 0)

                    @pl.when(first)
                    def _(r=r, gr=gr):
                        r[...] = gr

                    @pl.when(jnp.logical_not(first))
                    def _(r=r, gr=gr):
                        r[...] += gr

        g_specs = [in_specs[k] for k in didx]
        g_shape = [jax.ShapeDtypeStruct(arrs[k].shape, F32) for k in didx]
        ct_specs = out_specs + ([in_specs[through]] if n_through else [])
        return pl.pallas_call(body, grid=grid, in_specs=in_specs + ct_specs, out_specs=g_specs, out_shape=g_shape,
                              compiler_params=_params(("arbitrary", "arbitrary")), name=name + "_bwd")(*arrs, *cts)

    @jax.custom_vjp
    def run(*arrs):
        res = [Act(o, jnp.zeros(o.shape, F32)) if k in narrow else o for k, o in enumerate(fwd_call(*arrs))]
        return tuple(res) + ((arrs[through],) if n_through else ())

    def run_fwd(*arrs):
        return run(*arrs), arrs

    def run_bwd(arrs, cts):
        cts = [c.slot if isinstance(c, Act) else c for c in cts]
        gs = bwd_call(arrs, cts)
        full = [None] * n_in
        for k, g in zip(didx, gs):
            full[k] = g
        return tuple(full)

    run.defvjp(run_fwd, run_bwd)
    return run(*arrays)


def _rows(arr, tile, kind="t"):
    return (arr, (tile, arr.shape[1]), lambda o, i: (i, 0), kind)


def _whole(arr, kind="ag"):
    return (arr, arr.shape, lambda o, i: (0, 0), kind)


def _row_out(t, n, tile):
    return ((t, n), (tile, n), lambda o, i: (i, 0))


def _rms(x, g, n=None):
    ms = jnp.sum(x * x, axis=-1, keepdims=True) / (x.shape[-1] if n is None else n)
    return x * lax.rsqrt(ms + EPS) * g


def rms_norm(name, x, g):
    t, n = x.shape
    tile = min(ROW_TILE, t)
    return tmap(name, lambda p, x, g: (_rms(x, g),), (1, t // tile), [_rows(x, tile), _whole(g)],
                [_row_out(t, n, tile)], narrow=(0,))[0]


def rms_norm_through(name, x, g):
    t, n = x.shape
    tile = min(ROW_TILE, t)
    return tmap(name, lambda p, x, g: (_rms(x, g),), (1, t // tile), [_rows(x, tile), _whole(g)],
                [_row_out(t, n, tile)], through=0, narrow=(0,))


def rms_norm_nograd_x(name, x, g):
    t, n = x.shape
    tile = min(ROW_TILE, t)
    return tmap(name, lambda p, x, g: (_rms(x, g),), (1, t // tile), [_rows(x, tile, "n"), _whole(g)],
                [_row_out(t, n, tile)], narrow=(0,))[0]


def _glu_f(p, glu):
    h = glu.shape[1] // 2
    return (glu[:, :h] * jax.nn.sigmoid(glu[:, h:]),)


def _ln_silu_f(p, v, g, b):
    mu = jnp.mean(v, axis=-1, keepdims=True)
    xc = v - mu
    var = jnp.mean(xc * xc, axis=-1, keepdims=True)
    return (jax.nn.silu(xc * lax.rsqrt(var + EPS) * g + b),)


def _ssd_gate_norm_f(p, y, z, g):
    v = y * jax.nn.silu(z)
    w = SSD_INNER // SSD_GROUPS
    parts = []
    for k in range(SSD_GROUPS):
        vg = v[:, k * w:(k + 1) * w]
        parts.append(vg * lax.rsqrt(jnp.mean(vg * vg, axis=-1, keepdims=True) + EPS))
    return (jnp.concatenate(parts, axis=1) * g,)


def _merge_f(p, gl, gb, y0, y1, y2):
    g = jax.nn.sigmoid(gl + gb)
    d = D_MODEL
    return (g[:, :d] * y0 + g[:, d:2 * d] * y1 + g[:, 2 * d:] * y2,)


def _swiglu_f(p, gate, up):
    return (jax.nn.silu(gate) * up,)


def _rot_matrix():
    r, c = _iota((LANES, LANES), 0), _iota((LANES, LANES), 1)
    h = MLA_ROPE // 2
    plus = jnp.logical_and(c >= h, jnp.logical_and(c < 2 * h, r == c - h))
    minus = jnp.logical_and(c < h, r == c + h)
    return plus.astype(F32) - minus.astype(F32)


def _rope(x, cosf, sinf):
    return x * cosf + hdot(x, _rot_matrix()) * sinf


def _per_head(f, x):
    return jnp.concatenate([f(x[:, h * LANES:(h + 1) * LANES]) for h in range(x.shape[1] // LANES)], axis=1)


def _k_nope_f(p, kn_raw, kg):
    return (_per_head(lambda x: _rms(x, kg[:, :MLA_NOPE]), kn_raw),)


def _k_rope_f(p, kr_raw, cosf, sinf, kg):
    return (_rope(_rms(kr_raw, kg[:, MLA_NOPE:], n=MLA_ROPE), cosf, sinf),)


def _q_prep_f(p, qn_raw, qr_raw, cosf, sinf, qg):
    qn = _per_head(lambda x: _rms(x, qg[:, :MLA_NOPE]), qn_raw)
    qr = _per_head(lambda x: _rope(_rms(x, qg[:, MLA_NOPE:], n=MLA_ROPE), cosf, sinf), qr_raw)
    return qn, qr


def _softmax(s):
    m = jnp.max(s, axis=-1, keepdims=True)
    e = jnp.exp(s - m)
    return e / jnp.sum(e, axis=-1, keepdims=True)


def _xattn_f(p, q, k, v, qg, kg):
    s = bdot_nt(_rms(q, qg), _rms(k, kg)) * (X_HEAD_DIM ** -0.5)
    return (bdot(_softmax(s), v),)


ATT_SCALE = (MLA_NOPE + MLA_ROPE) ** -0.5
NT_DIMS = (((1,), (1,)), ((), ()))
NN_DIMS = (((1,), (0,)), ((), ()))
TN_DIMS = (((0,), (0,)), ((), ()))


def _att_specs(t, blk):
    q_spec = pl.BlockSpec((blk, LANES), lambda h, i: (i, h))
    k_spec = pl.BlockSpec((t, LANES), lambda h, i: (0, h))
    shared = pl.BlockSpec((t, LANES), lambda h, i: (0, 0))
    lse_spec = pl.BlockSpec((None, blk, 1), lambda h, i: (h, i, 0))
    return q_spec, k_spec, shared, lse_spec


def _diagonal_mask(blk):
    return (_iota((blk, blk), 1) // ATT_CHUNK) <= (_iota((blk, blk), 0) // ATT_CHUNK)


def _att_keys(kn_ref, kr_ref, j, blk):
    ks = pl.ds(pl.multiple_of(j * blk, blk), blk)
    return ks, jnp.concatenate([kn_ref[ks, :], kr_ref[ks, :]], axis=1).astype(BF16)


def _att_fwd(name, qn, qr, kn, kr, v):
    t, width = qn.shape
    heads = width // LANES
    blk = min(ATT_BLOCK, t)

    def body(qn_ref, qr_ref, kn_ref, kr_ref, v_ref, o_ref, lse_ref):
        i = pl.program_id(1)
        q = jnp.concatenate([qn_ref[...], qr_ref[...]], axis=1).astype(BF16)

        def scores(j):
            _, k = _att_keys(kn_ref, kr_ref, j, blk)
            return lax.dot_general(q, k, NT_DIMS, preferred_element_type=F32)

        def weighted_values(p, j):
            ks = pl.ds(pl.multiple_of(j * blk, blk), blk)
            return lax.dot_general(p, v_ref[ks, :].astype(BF16), NN_DIMS, preferred_element_type=F32)

        def softmax_step(s, m, l):
            m_new = jnp.maximum(m, jnp.max(s, axis=1, keepdims=True))
            alpha = jnp.exp(m - m_new)
            p = jnp.exp(s - m_new)
            return m_new, alpha, alpha * l + jnp.sum(p, axis=1, keepdims=True), p.astype(BF16)

        def step(j, carry):
            s, p_prev, m, l, acc = carry
            s_next = scores(j + 1)
            pv_prev = weighted_values(p_prev, jnp.maximum(j - 1, 0))
            m, alpha, l, p = softmax_step(s * ATT_SCALE, m, l)
            return s_next, p, m, l, alpha * (acc + pv_prev)

        init = (scores(0), jnp.zeros((blk, blk), BF16), jnp.full((blk, 1), -1e30, F32), jnp.zeros((blk, 1), F32),
                jnp.zeros((blk, LANES), F32))
        s, p_prev, m, l, acc = lax.fori_loop(0, i, step, init)
        pv_prev = weighted_values(p_prev, jnp.maximum(i - 1, 0))
        s = jnp.where(_diagonal_mask(blk), s * ATT_SCALE, -1e30)
        m, alpha, l, p = softmax_step(s, m, l)
        acc = alpha * (acc + pv_prev) + weighted_values(p, i)
        o_ref[...] = acc / l
        lse_ref[...] = m + jnp.log(l)

    q_spec, k_spec, shared, lse_spec = _att_specs(t, blk)
    return pl.pallas_call(
        body, grid=(heads, t // blk), in_specs=[q_spec, q_spec, k_spec, shared, k_spec], out_specs=[q_spec, lse_spec],
        out_shape=[jax.ShapeDtypeStruct((t, width), F32), jax.ShapeDtypeStruct((heads, t, 1), F32)],
        compiler_params=_params(("arbitrary", "arbitrary")), name=name)(qn, qr, kn, kr, v)


def _att_bwd(name, qn, qr, kn, kr, v, o, lse, do):
    t, width = qn.shape
    heads = width // LANES
    blk = min(ATT_BLOCK, t)

    def body(qn_ref, qr_ref, kn_ref, kr_ref, v_ref, o_ref, lse_ref, do_ref, dqn_ref, dqr_ref, dkn_ref, dkr_ref,
             dv_ref):
        h, i = pl.program_id(0), pl.program_id(1)

        @pl.when(i == 0)
        def _():
            dkn_ref[...] = jnp.zeros_like(dkn_ref)
            dv_ref[...] = jnp.zeros_like(dv_ref)

        @pl.when(jnp.logical_and(h == 0, i == 0))
        def _():
            dkr_ref[...] = jnp.zeros_like(dkr_ref)

        q = jnp.concatenate([qn_ref[...], qr_ref[...]], axis=1).astype(BF16)
        do = do_ref[...]
        do16 = do.astype(BF16)
        delta = jnp.sum(do * o_ref[...], axis=1, keepdims=True)
        lse = lse_ref[...]

        def issue(j):
            ks, k = _att_keys(kn_ref, kr_ref, j, blk)
            s = lax.dot_general(q, k, NT_DIMS, preferred_element_type=F32)
            dp = lax.dot_general(do16, v_ref[ks, :].astype(BF16), NT_DIMS, preferred_element_type=F32)
            return s, dp

        def retire(p, ds, j, dq):
            ks, k = _att_keys(kn_ref, kr_ref, j, blk)
            dv_ref[ks, :] += lax.dot_general(p, do16, TN_DIMS, preferred_element_type=F32)
            dk = lax.dot_general(ds, q, TN_DIMS, preferred_element_type=F32)
            dkn_ref[ks, :] += dk[:, :LANES]
            dkr_ref[ks, :] += dk[:, LANES:]
            return dq + lax.dot_general(ds, k, NN_DIMS, preferred_element_type=F32)

        def probs(s, dp, masked):
            s = s * ATT_SCALE
            if masked:
                s = jnp.where(_diagonal_mask(blk), s, -1e30)
            p = jnp.exp(s - lse)
            return p.astype(BF16), (p * (dp - delta) * ATT_SCALE).astype(BF16)

        def step(j, carry):
            s, dp, p_prev, ds_prev, dq = carry
            s_next, dp_next = issue(j + 1)
            dq = retire(p_prev, ds_prev, jnp.maximum(j - 1, 0), dq)
            p, ds = probs(s, dp, False)
            return s_next, dp_next, p, ds, dq

        none = jnp.zeros((blk, blk), BF16)
        s, dp, p_prev, ds_prev, dq = lax.fori_loop(0, i, step,
                                                   issue(0) + (none, none, jnp.zeros((blk, 2 * LANES), F32)))
        dq = retire(p_prev, ds_prev, jnp.maximum(i - 1, 0), dq)
        p, ds = probs(s, dp, True)
        dq = retire(p, ds, i, dq)
        dqn_ref[...] = dq[:, :LANES]
        dqr_ref[...] = dq[:, LANES:]

    q_spec, k_spec, shared, lse_spec = _att_specs(t, blk)
    big, one = jax.ShapeDtypeStruct((t, width), F32), jax.ShapeDtypeStruct((t, LANES), F32)
    return pl.pallas_call(
        body, grid=(heads, t // blk),
        in_specs=[q_spec, q_spec, k_spec, shared, k_spec, q_spec, lse_spec, q_spec],
        out_specs=[q_spec, q_spec, k_spec, shared, k_spec], out_shape=[big, big, big, one, big],
        compiler_params=_params(("arbitrary", "arbitrary")), name=name)(qn, qr, kn, kr, v, o, lse, do)


def mla_attention(name, qn, qr, kn, kr, v):
    @jax.custom_vjp
    def run(qn, qr, kn, kr, v):
        return _att_fwd(name, qn, qr, kn, kr, v)[0]

    def fwd(qn, qr, kn, kr, v):
        o, lse = _att_fwd(name, qn, qr, kn, kr, v)
        return o, (qn, qr, kn, kr, v, o, lse)

    def bwd(r, g):
        return tuple(_att_bwd(name + "_bwd", *r, g))

    run.defvjp(fwd, bwd)
    return run(qn, qr, kn, kr, v)


def _shift_down(v, s, rows):
    return v if s == 0 else jnp.where(rows >= s, pltpu.roll(v, s, 0), 0.0)


def _shift_up(v, s, rows):
    t = v.shape[0]
    return v if s == 0 else jnp.where(rows < t - s, pltpu.roll(v, t - s, 0), 0.0)


def _dwconv_fwd(name, x, w, b):
    t, c = x.shape
    kw = w.shape[0]

    def body(x_ref, w_ref, b_ref, y_ref):
        x = x_ref[...]
        rows = _iota(x.shape, 0)
        acc = jnp.zeros_like(x) + b_ref[...]
        for k in range(kw):
            acc = acc + w_ref[k:k + 1, :] * _shift_down(x, kw - 1 - k, rows)
        y_ref[...] = acc

    col = lambda i: (0, i)
    return pl.pallas_call(
        body, grid=(c // LANES,),
        in_specs=[pl.BlockSpec((t, LANES), col), pl.BlockSpec((kw, LANES), col), pl.BlockSpec((1, LANES), col)],
        out_specs=pl.BlockSpec((t, LANES), col), out_shape=jax.ShapeDtypeStruct((t, c), F32),
        compiler_params=_params(("arbitrary",)), name=name)(x, w, b)


def _dwconv_bwd(name, x, w, dy):
    t, c = x.shape
    kw = w.shape[0]

    def body(x_ref, w_ref, dy_ref, dx_ref, dw_ref, db_ref):
        x, dy = x_ref[...], dy_ref[...]
        rows = _iota(x.shape, 0)
        dx = jnp.zeros_like(x)
        for k in range(kw):
            s = kw - 1 - k
            dx = dx + w_ref[k:k + 1, :] * _shift_up(dy, s, rows)
            dw_ref[k:k + 1, :] = jnp.sum(dy * _shift_down(x, s, rows), axis=0, keepdims=True)
        dx_ref[...] = dx
        db_ref[...] = jnp.sum(dy, axis=0, keepdims=True)

    col = lambda i: (0, i)
    big, wsp, bsp = pl.BlockSpec((t, LANES), col), pl.BlockSpec((kw, LANES), col), pl.BlockSpec((1, LANES), col)
    return pl.pallas_call(
        body, grid=(c // LANES,), in_specs=[big, wsp, big], out_specs=[big, wsp, bsp],
        out_shape=[jax.ShapeDtypeStruct((t, c), F32), jax.ShapeDtypeStruct((kw, c), F32),
                   jax.ShapeDtypeStruct((1, c), F32)],
        compiler_params=_params(("arbitrary",)), name=name)(x, w, dy)


def dwconv(name, x, w, b):
    @jax.custom_vjp
    def run(x, w, b):
        return _dwconv_fwd(name, x, w, b)

    def fwd(x, w, b):
        return run(x, w, b), (x, w)

    def bwd(r, g):
        return tuple(_dwconv_bwd(name + "_bwd", r[0], r[1], g))

    run.defvjp(fwd, bwd)
    return run(x, w, b)


def _ssd_tile(xc, dtr, dtb, alog, dsk, prev):
    ln = xc.shape[0]
    gw = SSD_INNER // SSD_GROUPS
    ns = SSD_STATE
    xs = jax.nn.silu(xc[:, :SSD_INNER])
    bm = jax.nn.silu(xc[:, SSD_INNER:SSD_INNER + SSD_GROUPS * ns])
    cm = jax.nn.silu(xc[:, SSD_INNER + SSD_GROUPS * ns:])
    dt = jax.nn.softplus(dtr + dtb)
    a = dt * (-jnp.exp(alog))
    expand = (_iota((LANES, SSD_INNER), 0) == _iota((LANES, SSD_INNER), 1) // SSD_HEAD_DIM).astype(F32)
    causal = _iota((ln, ln), 0) >= _iota((ln, ln), 1)
    acs_h = hdot(causal.astype(F32), a)
    acs_c = hdot(acs_h, expand)
    dt_c = hdot(dt, expand)

    def row_per_column(v):
        return jnp.mean(hdot(jnp.broadcast_to(v, (8, LANES)), expand), axis=0, keepdims=True)

    aend_c = row_per_column(jnp.sum(a, axis=0, keepdims=True))
    xdt = xs * dt_c
    to_end = xdt * jnp.exp(aend_c - acs_c)
    from_start = jnp.exp(acs_c)
    acs_ht = acs_h.T
    lane_h, sub_h = _iota((1, LANES), 1), _iota((LANES, 1), 0)
    head_of_col = _iota((1, gw), 1) // SSD_HEAD_DIM
    ys, states = [], []
    for g in range(SSD_GROUPS):
        cg = cm[:, g * ns:(g + 1) * ns]
        bg = bm[:, g * ns:(g + 1) * ns]
        cols = slice(g * gw, (g + 1) * gw)
        y = bdot(cg, prev[:, cols]) * from_start[:, cols]
        states.append(bdot_tn(bg, to_end[:, cols]))
        cb = bdot_nt(cg, bg)
        for r in range(gw // SSD_HEAD_DIM):
            h = g * (gw // SSD_HEAD_DIM) + r
            col = jnp.sum(jnp.where(lane_h == h, acs_h, 0.0), axis=1, keepdims=True)
            row = jnp.sum(jnp.where(sub_h == h, acs_ht, 0.0), axis=0, keepdims=True)
            decay = jnp.exp(jnp.where(causal, col - row, -1e30))
            y = y + jnp.where(head_of_col == r, bdot(cb * decay, xdt[:, cols]), 0.0)
        ys.append(y)
    y = jnp.concatenate(ys, axis=1) + row_per_column(dsk) * xs
    new = prev * jnp.exp(aend_c) + jnp.concatenate(states, axis=1)
    return y, new


def _ssd_fwd(name, xc, dtr, dtb, alog, dsk):
    t = xc.shape[0]
    ln = min(SSD_TILE, t)
    nt = t // ln

    def body(xc_ref, dtr_ref, dtb_ref, alog_ref, dsk_ref, y_ref, prev_ref, carry):
        @pl.when(pl.program_id(0) == 0)
        def _():
            carry[...] = jnp.zeros_like(carry)

        prev = carry[...]
        prev_ref[...] = prev
        y, new = _ssd_tile(xc_ref[...], dtr_ref[...], dtb_ref[...], alog_ref[...], dsk_ref[...], prev)
        y_ref[...] = y
        carry[...] = new

    row = lambda i: (i, 0)
    par = pl.BlockSpec((1, LANES), lambda i: (0, 0))
    return pl.pallas_call(
        body, grid=(nt,),
        in_specs=[pl.BlockSpec((ln, xc.shape[1]), row), pl.BlockSpec((ln, LANES), row), par, par, par],
        out_specs=[pl.BlockSpec((ln, SSD_INNER), row), pl.BlockSpec((None, SSD_STATE, SSD_INNER), lambda i: (i, 0, 0))],
        out_shape=[jax.ShapeDtypeStruct((t, SSD_INNER), F32), jax.ShapeDtypeStruct((nt, SSD_STATE, SSD_INNER), F32)],
        scratch_shapes=[pltpu.VMEM((SSD_STATE, SSD_INNER), F32)],
        compiler_params=_params(("arbitrary",)), name=name)(xc, dtr, dtb, alog, dsk)


def _ssd_bwd(name, xc, dtr, dtb, alog, dsk, prevs, dy):
    t = xc.shape[0]
    ln = min(SSD_TILE, t)
    nt = t // ln

    def body(xc_ref, dtr_ref, dtb_ref, alog_ref, dsk_ref, prev_ref, dy_ref, dxc_ref, ddtr_ref, ddtb_ref, dalog_ref,
             ddsk_ref, dcarry):
        i = pl.program_id(0)

        @pl.when(i == 0)
        def _():
            dcarry[...] = jnp.zeros_like(dcarry)

        _, vjp = jax.vjp(_ssd_tile, xc_ref[...], dtr_ref[...], dtb_ref[...], alog_ref[...], dsk_ref[...], prev_ref[...])
        dxc, ddtr, ddtb, dalog, ddsk, dprev = vjp((dy_ref[...], dcarry[...]))
        dxc_ref[...] = dxc
        ddtr_ref[...] = ddtr
        dcarry[...] = dprev
        for r, gr in ((ddtb_ref, ddtb), (dalog_ref, dalog), (ddsk_ref, ddsk)):
            @pl.when(i == 0)
            def _(r=r, gr=gr):
                r[...] = gr

            @pl.when(i != 0)
            def _(r=r, gr=gr):
                r[...] += gr

    row = lambda i: (nt - 1 - i, 0)
    par = pl.BlockSpec((1, LANES), lambda i: (0, 0))
    big, dts = pl.BlockSpec((ln, xc.shape[1]), row), pl.BlockSpec((ln, LANES), row)
    par_shape = jax.ShapeDtypeStruct((1, LANES), F32)
    return pl.pallas_call(
        body, grid=(nt,),
        in_specs=[big, dts, par, par, par, pl.BlockSpec((None, SSD_STATE, SSD_INNER), lambda i: (nt - 1 - i, 0, 0)),
                  pl.BlockSpec((ln, SSD_INNER), row)],
        out_specs=[big, dts, par, par, par],
        out_shape=[jax.ShapeDtypeStruct(xc.shape, F32), jax.ShapeDtypeStruct(dtr.shape, F32), par_shape, par_shape,
                   par_shape],
        scratch_shapes=[pltpu.VMEM((SSD_STATE, SSD_INNER), F32)],
        compiler_params=_params(("arbitrary",)), name=name)(xc, dtr, dtb, alog, dsk, prevs, dy)


def ssd_scan(name, xc, dtr, dtb, alog, dsk):
    @jax.custom_vjp
    def run(xc, dtr, dtb, alog, dsk):
        return _ssd_fwd(name, xc, dtr, dtb, alog, dsk)[0]

    def fwd(xc, dtr, dtb, alog, dsk):
        y, prevs = _ssd_fwd(name, xc, dtr, dtb, alog, dsk)
        return y, (xc, dtr, dtb, alog, dsk, prevs)

    def bwd(r, g):
        return tuple(_ssd_bwd(name + "_bwd", *r, g))

    run.defvjp(fwd, bwd)
    return run(xc, dtr, dtb, alog, dsk)


def loss_head(y, target):
    t, n = y.shape
    tile = min(ROW_TILE, t)

    def body(y_ref, t_ref, dy_ref, acc_ref):
        d = y_ref[...] - t_ref[...]
        dy_ref[...] = d * (1.0 / n)

        @pl.when(pl.program_id(0) == 0)
        def _():
            acc_ref[...] = jnp.zeros_like(acc_ref)

        acc_ref[...] += jnp.sum(d * d, axis=0, keepdims=True)

    row = pl.BlockSpec((tile, n), lambda i: (i, 0))
    dy, acc = pl.pallas_call(
        body, grid=(t // tile,), in_specs=[row, row], out_specs=[row, pl.BlockSpec((1, n), lambda i: (0, 0))],
        out_shape=[jax.ShapeDtypeStruct((t, n), F32), jax.ShapeDtypeStruct((1, n), F32)],
        compiler_params=_params(("arbitrary",)), name="loss_head")(y, target)
    return acc, dy


def adamw(name, w, g, m, v):
    shape = w.shape
    cols = shape[-1]
    rows = w.size // cols
    tile = _divisor(rows, 512, 8) if rows % 8 == 0 else rows

    def body(w_ref, g_ref, m_ref, v_ref, d_ref, nm_ref, nv_ref):
        g = g_ref[...]
        m = ADAM_B1 * m_ref[...] + (1.0 - ADAM_B1) * g
        v = ADAM_B2 * v_ref[...] + (1.0 - ADAM_B2) * jnp.square(g)
        m_hat = m / (1.0 - ADAM_B1 ** ADAM_STEP)
        v_hat = v / (1.0 - ADAM_B2 ** ADAM_STEP)
        d_ref[...] = -ADAM_LR * (m_hat / (jnp.sqrt(v_hat) + ADAM_EPS) + ADAM_WD * w_ref[...])
        nm_ref[...] = m
        nv_ref[...] = v

    spec = pl.BlockSpec((tile, cols), lambda i: (i, 0))
    two_d = jax.ShapeDtypeStruct((rows, cols), F32)
    outs = pl.pallas_call(body, grid=(rows // tile,), in_specs=[spec] * 4, out_specs=[spec] * 3, out_shape=[two_d] * 3,
                          compiler_params=_params(("arbitrary",)), name=name)(
        *[a.reshape(rows, cols) for a in (w, g, m, v)])
    return [o.reshape(shape) for o in outs]


HBM_SPEC = pl.BlockSpec(memory_space=pl.ANY)


def _position():
    return lax.axis_index("x"), lax.axis_index("y"), lax.axis_index("c")


def all_gather(name, shard):
    def body(x_ref, out_ref, send_sems, recv_sems, local_sem):
        x, y, c = _position()
        me, sibling = (x, y, c), (x, y, 1 - c)
        chips = [(1 - x, y), (x, 1 - y), (1 - x, 1 - y)]

        def block(px, py, pc):
            return out_ref.at[4 * px + 2 * py + pc]

        def copy(k, blk, to, src=None):
            return pltpu.make_async_remote_copy(
                src_ref=block(*blk) if src is None else src, dst_ref=block(*blk), send_sem=send_sems.at[k],
                recv_sem=recv_sems.at[k], device_id=to, device_id_type=MESH_ID)

        mine = pltpu.make_async_copy(x_ref, block(*me), local_sem)
        mine.start()
        first = [copy(0, me, sibling, src=x_ref)]
        first += [copy(1 + j, me, (*chip, c), src=x_ref) for j, chip in enumerate(chips)]
        for cp in first:
            cp.start()
        passed = [copy(4 + j, (*chip, c), sibling) for j, chip in enumerate(chips)]
        for j, chip in enumerate(chips):
            copy(1 + j, (*chip, c), me).wait_recv()
            passed[j].start()
        copy(0, sibling, me).wait_recv()
        for j, chip in enumerate(chips):
            copy(4 + j, (*chip, 1 - c), me).wait_recv()
        for cp in first + passed:
            cp.wait_send()
        mine.wait()

    return pl.pallas_call(
        body, in_specs=[HBM_SPEC], out_specs=HBM_SPEC,
        out_shape=jax.ShapeDtypeStruct((N_DEV,) + shard.shape, shard.dtype),
        scratch_shapes=[pltpu.SemaphoreType.DMA((7,)), pltpu.SemaphoreType.DMA((7,)), pltpu.SemaphoreType.DMA],
        name=name)(shard)


def pair_exchange(name, g):
    def body(g_ref, out_ref, send_sem, recv_sem):
        x, y, c = _position()
        cp = pltpu.make_async_remote_copy(src_ref=g_ref.at[1 - c], dst_ref=out_ref, send_sem=send_sem,
                                          recv_sem=recv_sem, device_id=(x, y, 1 - c), device_id_type=MESH_ID)
        cp.start()
        cp.wait()

    return pl.pallas_call(
        body, in_specs=[HBM_SPEC], out_specs=HBM_SPEC, out_shape=jax.ShapeDtypeStruct(g.shape[1:], g.dtype),
        scratch_shapes=[pltpu.SemaphoreType.DMA, pltpu.SemaphoreType.DMA], name=name)(g)


def chip_exchange(name, p):
    def body(p_ref, out_ref, send_sems, recv_sems):
        x, y, c = _position()
        copies = []
        for j, (px, py) in enumerate([(1 - x, y), (x, 1 - y), (1 - x, 1 - y)]):
            cp = pltpu.make_async_remote_copy(
                src_ref=p_ref.at[2 * px + py], dst_ref=out_ref.at[j], send_sem=send_sems.at[j],
                recv_sem=recv_sems.at[j], device_id=(px, py, c), device_id_type=MESH_ID)
            cp.start()
            copies.append(cp)
        for cp in copies:
            cp.wait()

    return pl.pallas_call(
        body, in_specs=[HBM_SPEC], out_specs=HBM_SPEC, out_shape=jax.ShapeDtypeStruct((3,) + p.shape[1:], p.dtype),
        scratch_shapes=[pltpu.SemaphoreType.DMA((3,)), pltpu.SemaphoreType.DMA((3,))], name=name)(p)


def _sum_tile(r):
    return _divisor(r, 512, 8) if r % 8 == 0 else r


def pair_reduce(name, g, got, my_c, my_chip, wire):
    _, nchip, r, c_ = g.shape
    tile = _sum_tile(r)

    def body(ids, g_ref, got_ref, p_ref, mine_ref):
        s = g_ref[...].astype(F32) + got_ref[...].astype(F32)
        p_ref[...] = s.astype(wire)

        @pl.when(pl.program_id(1) == ids[1])
        def _():
            mine_ref[...] = s

    return pl.pallas_call(
        body,
        grid_spec=pltpu.PrefetchScalarGridSpec(
            num_scalar_prefetch=1, grid=(r // tile, nchip),
            in_specs=[pl.BlockSpec((None, None, tile, c_), lambda i, k, ids: (ids[0], k, i, 0)),
                      pl.BlockSpec((None, tile, c_), lambda i, k, ids: (k, i, 0))],
            out_specs=[pl.BlockSpec((None, tile, c_), lambda i, k, ids: (k, i, 0)),
                       pl.BlockSpec((tile, c_), lambda i, k, ids: (i, 0))]),
        out_shape=[jax.ShapeDtypeStruct((nchip, r, c_), wire), jax.ShapeDtypeStruct((r, c_), F32)],
        compiler_params=_params(("arbitrary", "arbitrary")), name=name)(
        jnp.stack([my_c, my_chip]).astype(jnp.int32), g, got)


def chip_reduce(name, mine, got):
    r, c_ = mine.shape
    tile = _sum_tile(r)

    def body(m_ref, got_ref, o_ref):
        o_ref[...] = ((m_ref[...] + got_ref[0].astype(F32)) + got_ref[1].astype(F32)) + got_ref[2].astype(F32)

    return pl.pallas_call(
        body, grid=(r // tile,),
        in_specs=[pl.BlockSpec((tile, c_), lambda i: (i, 0)), pl.BlockSpec((3, tile, c_), lambda i: (0, i, 0))],
        out_specs=pl.BlockSpec((tile, c_), lambda i: (i, 0)), out_shape=jax.ShapeDtypeStruct((r, c_), F32),
        compiler_params=_params(("arbitrary",)), name=name)(mine, got)


def sum_blocks(name, a):
    n, r, c_ = a.shape
    tile = _sum_tile(r)

    def body(a_ref, o_ref):
        acc = a_ref[0]
        for k in range(1, n):
            acc = acc + a_ref[k]
        o_ref[...] = acc

    return pl.pallas_call(
        body, grid=(r // tile,), in_specs=[pl.BlockSpec((n, tile, c_), lambda i: (0, i, 0))],
        out_specs=pl.BlockSpec((tile, c_), lambda i: (i, 0)), out_shape=jax.ShapeDtypeStruct((r, c_), F32),
        compiler_params=_params(("arbitrary",)), name=name)(a)


GROUPS = [
    ("rows1024", ["ssd_w_out", "conv_w_out", "mla_w_o", "w_out", "xattn_w_q", "xattn_w_o", "ffn_w_out"], BF16, 1024),
    ("w_in", ["w_in"], BF16, 1114),
    ("ffn_w_in", ["ffn_w_in"], BF16, 704),
    ("cols256", ["xattn_w_kv", "mla_w_kv_b"], BF16, 256),
    ("w_q_b", ["mla_w_q_b"], BF16, 192),
    ("small", ["ssd_conv_w", "conv_dw_w", "gate_b"], F32, 128),
]
PACK_COLS = 1024


def _pack(arrays):
    flat = jnp.concatenate([a.reshape(-1) for a in arrays])
    rows = -(-flat.shape[0] // PACK_COLS)
    rows += -rows % 8
    return jnp.pad(flat, (0, rows * PACK_COLS - flat.shape[0])).reshape(rows, PACK_COLS)


def _unpack(buf, shapes):
    flat = buf.reshape(-1)
    out, off = [], 0
    for s in shapes:
        n = 1
        for d in s:
            n *= d
        out.append(flat[off:off + n].reshape(tuple(s)))
        off += n
    return out


def _stack_rows(arrays, width, lead):
    return jnp.concatenate([a.reshape(a.shape[:lead] + (-1, width)) for a in arrays], axis=lead)


def _unstack_rows(buf, shapes, width, lead):
    out, off = [], 0
    for s in shapes:
        n = 1
        for d in s:
            n *= d
        rows = n // width
        idx = (slice(None),) * lead + (slice(off, off + rows),)
        out.append(buf[idx].reshape(buf.shape[:lead] + tuple(s)))
        off += rows
    return out


def _join_shards(blocks, axis):
    ax = axis + 1
    moved = jnp.moveaxis(blocks, 0, ax)
    s = moved.shape
    return moved.reshape(s[:ax] + (s[ax] * s[ax + 1],) + s[ax + 2:])


def _split_by_owner(full, axis):
    ax = axis + 1
    s = full.shape
    cut = full.reshape(s[:ax] + (2, 2, 2, s[ax] // N_DEV) + s[ax + 1:])
    cut = jnp.moveaxis(cut, (ax + 2, ax, ax + 1), (0, 1, 2))
    return cut.reshape((2, 4) + cut.shape[3:])


def _blocks_by_owner(blocks):
    cut = blocks.reshape((2, 2, 2) + blocks.shape[1:])
    return jnp.moveaxis(cut, 2, 0).reshape((2, 4) + blocks.shape[1:])


def _pad_cols(w, n):
    return jnp.pad(w, ((0, 0), (0, n - w.shape[1])))


def _regroup_cols(srcs, widths):
    starts = [0]
    for s in srcs:
        starts.append(starts[-1] + s.shape[1])
    assert starts[-1] == sum(widths)
    out, lo = [], 0
    for wd in widths:
        hi = lo + wd
        parts = []
        for s, a, b in zip(srcs, starts[:-1], starts[1:]):
            u, v = max(lo, a), min(hi, b)
            if u < v:
                parts.append(s[:, u - a:v - a])
        out.append(parts[0] if len(parts) == 1 else jnp.concatenate(parts, axis=1))
        lo = hi
    return out


COL_BLOCKED = ("w_in", "ffn_w_in", "xattn_w_kv", "mla_w_kv_b", "mla_w_q_b")
W_IN_PIECES = (1024, 2048, 16, 2048, 384, MLA_KV_RANK, MLA_ROPE, 3072)


def _prep_layer(w):
    w_z, w_xbc, w_dt, w_glu, w_q, w_ckv, w_kr, w_gate = _regroup_cols(list(w["w_in"]), W_IN_PIECES)
    w_ffn_gate, w_ffn_up = _regroup_cols(list(w["ffn_w_in"]), (FFN_HIDDEN, FFN_HIDDEN))
    w_xk, w_xv = _regroup_cols(list(w["xattn_w_kv"]), (D_MODEL, D_MODEL))
    q, kv = w["mla_w_q_b"], w["mla_w_kv_b"]

    def row(v):
        return v.reshape(1, -1)

    def norm_pair(g):
        return _pad_cols(row(g), 2 * LANES)

    return {
        "mix_norm_g": row(w["mix_norm_g"]),
        "w_z": w_z, "w_xbc": w_xbc, "w_dt": _pad_cols(w_dt, LANES), "w_glu": w_glu, "w_q": w_q, "w_ckv": w_ckv,
        "w_kr": _pad_cols(w_kr, LANES), "w_gate": w_gate,
        "ssd_conv_w": w["ssd_conv_w"], "ssd_conv_b": row(w["ssd_conv_b"]),
        "ssd_dt_bias": _pad_cols(row(w["ssd_dt_bias"]), LANES), "ssd_a_log": _pad_cols(row(w["ssd_a_log"]), LANES),
        "ssd_d": _pad_cols(row(w["ssd_d"]), LANES), "ssd_norm_g": row(w["ssd_norm_g"]), "ssd_w_out": w["ssd_w_out"],
        "conv_dw_w": w["conv_dw_w"], "conv_dw_b": row(w["conv_dw_b"]), "conv_ln_g": row(w["conv_ln_g"]),
        "conv_ln_b": row(w["conv_ln_b"]), "conv_w_out": w["conv_w_out"],
        "mla_q_a_g": row(w["mla_q_a_g"]), "mla_kv_a_g": row(w["mla_kv_a_g"]),
        "w_qn": jnp.concatenate([q[h, :, :MLA_NOPE] for h in range(MLA_HEADS)], axis=1),
        "w_qr": jnp.concatenate([_pad_cols(q[h, :, MLA_NOPE:], LANES) for h in range(MLA_HEADS)], axis=1),
        "w_kn": jnp.concatenate([kv[h, :, :MLA_NOPE] for h in range(MLA_HEADS)], axis=1),
        "w_v": jnp.concatenate([kv[h, :, MLA_NOPE:] for h in range(MLA_HEADS)], axis=1),
        "mla_q_norm_g": norm_pair(w["mla_q_norm_g"]), "mla_k_norm_g": norm_pair(w["mla_k_norm_g"]),
        "mla_w_o": w["mla_w_o"], "gate_b": row(w["gate_b"]), "w_out": w["w_out"],
        "xattn_norm_g": row(w["xattn_norm_g"]), "mem_norm_g": row(w["mem_norm_g"]), "xattn_w_q": w["xattn_w_q"],
        "w_xk": w_xk, "w_xv": w_xv,
        "xattn_q_norm_g": row(w["xattn_q_norm_g"]), "xattn_k_norm_g": row(w["xattn_k_norm_g"]),
        "xattn_w_o": w["xattn_w_o"], "ffn_norm_g": row(w["ffn_norm_g"]),
        "w_ffn_gate": w_ffn_gate, "w_ffn_up": w_ffn_up, "ffn_w_out": w["ffn_w_out"],
    }


def _unprep_grads(g):
    n_dt, n_kr = IN_SIZES[2], MLA_ROPE
    flat = lambda v: v.reshape(-1)

    def blocks(srcs):
        total = sum(s.shape[1] for s in srcs)
        return jnp.stack(_regroup_cols(srcs, (total // N_DEV,) * N_DEV))

    def head(a, h, n=LANES):
        return a[:, h * LANES:h * LANES + n]

    return {
        "mix_norm_g": flat(g["mix_norm_g"]),
        "w_in": blocks([g["w_z"], g["w_xbc"], g["w_dt"][:, :n_dt], g["w_glu"], g["w_q"], g["w_ckv"],
                        g["w_kr"][:, :n_kr], g["w_gate"]]),
        "ssd_conv_w": g["ssd_conv_w"], "ssd_conv_b": flat(g["ssd_conv_b"]),
        "ssd_dt_bias": flat(g["ssd_dt_bias"])[:SSD_HEADS], "ssd_a_log": flat(g["ssd_a_log"])[:SSD_HEADS],
        "ssd_d": flat(g["ssd_d"])[:SSD_HEADS], "ssd_norm_g": flat(g["ssd_norm_g"]), "ssd_w_out": g["ssd_w_out"],
        "conv_dw_w": g["conv_dw_w"], "conv_dw_b": flat(g["conv_dw_b"]), "conv_ln_g": flat(g["conv_ln_g"]),
        "conv_ln_b": flat(g["conv_ln_b"]), "conv_w_out": g["conv_w_out"],
        "mla_q_a_g": flat(g["mla_q_a_g"]),
        "mla_w_q_b": jnp.stack([jnp.concatenate([head(g["w_qn"], h), head(g["w_qr"], h, MLA_ROPE)], axis=1)
                                for h in range(MLA_HEADS)]),
        "mla_kv_a_g": flat(g["mla_kv_a_g"]),
        "mla_w_kv_b": jnp.stack([jnp.concatenate([head(g["w_kn"], h), head(g["w_v"], h)], axis=1)
                                 for h in range(MLA_HEADS)]),
        "mla_q_norm_g": flat(g["mla_q_norm_g"])[:MLA_NOPE + MLA_ROPE],
        "mla_k_norm_g": flat(g["mla_k_norm_g"])[:MLA_NOPE + MLA_ROPE],
        "mla_w_o": g["mla_w_o"], "gate_b": g["gate_b"].reshape(3, D_MODEL), "w_out": g["w_out"],
        "xattn_norm_g": flat(g["xattn_norm_g"]), "mem_norm_g": flat(g["mem_norm_g"]), "xattn_w_q": g["xattn_w_q"],
        "xattn_w_kv": blocks([g["w_xk"], g["w_xv"]]),
        "xattn_q_norm_g": flat(g["xattn_q_norm_g"]), "xattn_k_norm_g": flat(g["xattn_k_norm_g"]),
        "xattn_w_o": g["xattn_w_o"], "ffn_norm_g": flat(g["ffn_norm_g"]),
        "ffn_w_in": blocks([g["w_ffn_gate"], g["w_ffn_up"]]), "ffn_w_out": g["ffn_w_out"],
    }


def _layer(l, x, mem, cosf, sinf, w):
    t = x.shape[0]
    n = lambda s: f"l{l}_{s}"
    tile = min(ROW_TILE, t)
    grid = (1, t // tile)

    def rowwise(name, f, ins, width, to_matmul=False):
        return tmap(n(name), f, grid, ins, [_row_out(t, width, tile)], narrow=(0,) if to_matmul else ())[0]

    u, x = rms_norm_through(n("mix_norm"), x, w["mix_norm_g"])
    in_keys = ["z", "xbc", "dt", "glu", "q", "ckv", "kr", "gate"]
    z, xbc, dtr, glu, q_lat, c_kv, kr_raw, gate_logits = multi_matmul(n("in"), u, [w["w_" + k] for k in in_keys],
                                                                      in_keys)

    xc = dwconv(n("ssd_conv"), xbc, w["ssd_conv_w"], w["ssd_conv_b"])
    y_scan = ssd_scan(n("ssd_scan"), xc, dtr, w["ssd_dt_bias"], w["ssd_a_log"], w["ssd_d"])
    y_norm = rowwise("ssd_gate_norm", _ssd_gate_norm_f, [_rows(y_scan, tile), _rows(z, tile), _whole(w["ssd_norm_g"])],
                     SSD_INNER, to_matmul=True)
    y_ssd = matmul(n("ssd_out"), y_norm, w["ssd_w_out"])

    v = rowwise("glu", _glu_f, [_rows(glu, tile)], D_MODEL)
    v = dwconv(n("conv_dw"), v, w["conv_dw_w"], w["conv_dw_b"])
    v = rowwise("conv_ln_silu", _ln_silu_f, [_rows(v, tile), _whole(w["conv_ln_g"]), _whole(w["conv_ln_b"])], D_MODEL,
                to_matmul=True)
    y_conv = matmul(n("conv_out"), v, w["conv_w_out"])

    q_n = rms_norm(n("q_a_norm"), q_lat, w["mla_q_a_g"])
    qn_raw, qr_raw = multi_matmul(n("q"), q_n, [w["w_qn"], w["w_qr"]], ["nope", "rope"])
    c_n = rms_norm(n("kv_a_norm"), c_kv, w["mla_kv_a_g"])
    kn_raw, val = multi_matmul(n("kv"), c_n, [w["w_kn"], w["w_v"]], ["nope", "v"])
    tables = [_rows(cosf, tile, "n"), _rows(sinf, tile, "n")]
    kn = rowwise("k_nope_norm", _k_nope_f, [_rows(kn_raw, tile), _whole(w["mla_k_norm_g"])], MLA_HEADS * MLA_NOPE)
    kr = rowwise("k_rope", _k_rope_f, [_rows(kr_raw, tile)] + tables + [_whole(w["mla_k_norm_g"])], LANES)
    wide = _row_out(t, MLA_HEADS * LANES, tile)
    qn, qr = tmap(n("q_prep"), _q_prep_f, grid,
                  [_rows(qn_raw, tile), _rows(qr_raw, tile)] + tables + [_whole(w["mla_q_norm_g"])], [wide, wide])
    att = mla_attention(n("mla_attn"), qn, qr, kn, kr, val)
    y_mla = matmul(n("mla_out"), att, w["mla_w_o"])

    merged = rowwise("merge", _merge_f, [_rows(gate_logits, tile), _whole(w["gate_b"]), _rows(y_ssd, tile),
                                         _rows(y_conv, tile), _rows(y_mla, tile)], D_MODEL, to_matmul=True)
    x = matmul(n("mix_out"), merged, w["w_out"], res=x)

    h, x = rms_norm_through(n("xattn_norm"), x, w["xattn_norm_g"])
    mem_n = rms_norm_nograd_x(n("mem_norm"), mem, w["mem_norm_g"])
    xq = matmul(n("xattn_q"), h, w["xattn_w_q"])
    xk, xv = multi_matmul(n("xattn_kv"), mem_n, [w["w_xk"], w["w_xv"]], ["k", "v"])
    m = mem.shape[0]
    txq = min(XATT_Q_TILE, t)
    kv_head = lambda arr: (arr, (m, X_HEAD_DIM), lambda o, i: (0, o), "ai")
    xo = tmap(n("xattn"), _xattn_f, (X_HEADS, t // txq),
              [(xq, (txq, X_HEAD_DIM), lambda o, i: (i, o), "t"), kv_head(xk), kv_head(xv),
               _whole(w["xattn_q_norm_g"]), _whole(w["xattn_k_norm_g"])],
              [((t, D_MODEL), (txq, X_HEAD_DIM), lambda o, i: (i, o))], narrow=(0,))[0]
    x = matmul(n("xattn_out"), xo, w["xattn_w_o"], res=x)

    h, x = rms_norm_through(n("ffn_norm"), x, w["ffn_norm_g"])
    gate, up = multi_matmul(n("ffn_in"), h, [w["w_ffn_gate"], w["w_ffn_up"]], ["gate", "up"])
    act = rowwise("swiglu", _swiglu_f, [_rows(gate, tile), _rows(up, tile)], FFN_HIDDEN, to_matmul=True)
    return matmul(n("ffn_out"), act, w["ffn_w_out"], res=x)


def _rope_tables(positions):
    inv = ROPE_THETA ** (-jnp.arange(0, MLA_ROPE, 2, dtype=F32) / MLA_ROPE)
    ang = positions.astype(F32)[:, None] * inv
    pad = jnp.zeros((positions.shape[0], LANES - MLA_ROPE), F32)
    cos, sin = jnp.cos(ang), jnp.sin(ang)
    return jnp.concatenate([cos, cos, pad], axis=1), jnp.concatenate([sin, sin, pad], axis=1)


def local_step(x, mem, positions, target, weights):
    cosf, sinf = _rope_tables(positions)
    prepped = [_prep_layer({k: v[:, l] if k in COL_BLOCKED else v[l] for k, v in weights.items()})
               for l in range(DEPTH)]
    diff = [{k: jnp.zeros(v.shape, F32) if k in MATRICES else v for k, v in p.items()} for p in prepped]

    def forward(x, diff):
        for l in range(DEPTH):
            w = {k: Mat(prepped[l][k], s) if k in MATRICES else s for k, s in diff[l].items()}
            x = _layer(l, x, mem, cosf, sinf, w)
        return x

    y, pull = jax.vjp(forward, x, diff)
    sq, dy = loss_head(y, target)
    gx, gp = pull(dy)
    per_layer = [_unprep_grads(g) for g in gp]
    grads = {k: jnp.stack([pl_[k] for pl_ in per_layer], axis=1 if k in COL_BLOCKED else 0) for k in WEIGHTS}
    return sq, gx, grads


def _step(x, mem, positions, loss_target, w, m, v):
    xi, yi, ci = _position()

    full = {n: w[n] for n in REPLICATED}
    for gname, names, wire, width in GROUPS:
        shapes = [w[n].shape for n in names]
        stacked = _stack_rows([w[n] for n in names], width, 0).astype(wire)
        gathered = all_gather("gather_" + gname, stacked)
        for n, b in zip(names, _unstack_rows(gathered, shapes, width, 1)):
            full[n] = b if n in COL_BLOCKED else _join_shards(b, SHARDED[n])

    sq, gx, grads = local_step(x[0], mem[0], positions[0], loss_target[0], full)
    loss = lax.psum(0.5 * jnp.sum(sq) / D_MODEL, ("x", "y", "c"))

    g_shard = {}
    for gname, names, wire, width in GROUPS:
        shapes = [w[n].shape for n in names]
        by_owner = _stack_rows([_blocks_by_owner(grads[n]) if n in COL_BLOCKED else
                                _split_by_owner(grads[n], SHARDED[n]) for n in names], width, 2)
        by_owner = by_owner.astype(wire)
        from_sibling = pair_exchange("pair_exchange_" + gname, by_owner)
        chip_partial, mine = pair_reduce("pair_reduce_" + gname, by_owner, from_sibling, ci, 2 * xi + yi, wire)
        from_chips = chip_exchange("chip_exchange_" + gname, chip_partial)
        reduced = chip_reduce("chip_reduce_" + gname, mine, from_chips)
        g_shard.update(zip(names, _unstack_rows(reduced, shapes, width, 0)))

    rep_shapes = [w[n].shape for n in REPLICATED]
    rep_all = all_gather("small_grads_all_gather", _pack([grads[n] for n in REPLICATED]))
    g_rep = dict(zip(REPLICATED, _unpack(sum_blocks("small_grads_sum", rep_all), rep_shapes)))

    out_g, out_d, out_m, out_v = [], [], [], []
    for n in WEIGHTS:
        g = g_shard[n] if n in SHARDED else g_rep[n]
        d, nm, nv = adamw("adamw_" + n, w[n], g, m[n], v[n])
        out_g.append(g)
        out_d.append(d)
        out_m.append(nm)
        out_v.append(nv)
    return (loss, gx[None], *out_g, *out_d, *out_m, *out_v)


def kernel(x, mem, positions, mix_norm_g, w_in, ssd_conv_w, ssd_conv_b, ssd_dt_bias, ssd_a_log, ssd_d, ssd_norm_g, ssd_w_out, conv_dw_w, conv_dw_b, conv_ln_g, conv_ln_b, conv_w_out, mla_q_a_g, mla_w_q_b, mla_kv_a_g, mla_w_kv_b, mla_q_norm_g, mla_k_norm_g, mla_w_o, gate_b, w_out, xattn_norm_g, mem_norm_g, xattn_w_q, xattn_w_kv, xattn_q_norm_g, xattn_k_norm_g, xattn_w_o, ffn_norm_g, ffn_w_in, ffn_w_out, loss_target, m_mix_norm_g, m_w_in, m_ssd_conv_w, m_ssd_conv_b, m_ssd_dt_bias, m_ssd_a_log, m_ssd_d, m_ssd_norm_g, m_ssd_w_out, m_conv_dw_w, m_conv_dw_b, m_conv_ln_g, m_conv_ln_b, m_conv_w_out, m_mla_q_a_g, m_mla_w_q_b, m_mla_kv_a_g, m_mla_w_kv_b, m_mla_q_norm_g, m_mla_k_norm_g, m_mla_w_o, m_gate_b, m_w_out, m_xattn_norm_g, m_mem_norm_g, m_xattn_w_q, m_xattn_w_kv, m_xattn_q_norm_g, m_xattn_k_norm_g, m_xattn_w_o, m_ffn_norm_g, m_ffn_w_in, m_ffn_w_out, v_mix_norm_g, v_w_in, v_ssd_conv_w, v_ssd_conv_b, v_ssd_dt_bias, v_ssd_a_log, v_ssd_d, v_ssd_norm_g, v_ssd_w_out, v_conv_dw_w, v_conv_dw_b, v_conv_ln_g, v_conv_ln_b, v_conv_w_out, v_mla_q_a_g, v_mla_w_q_b, v_mla_kv_a_g, v_mla_w_kv_b, v_mla_q_norm_g, v_mla_k_norm_g, v_mla_w_o, v_gate_b, v_w_out, v_xattn_norm_g, v_mem_norm_g, v_xattn_w_q, v_xattn_w_kv, v_xattn_q_norm_g, v_xattn_k_norm_g, v_xattn_w_o, v_ffn_norm_g, v_ffn_w_in, v_ffn_w_out):
    args = locals()
    w = {n: args[n] for n in WEIGHTS}
    m = {n: args["m_" + n] for n in WEIGHTS}
    v = {n: args["v_" + n] for n in WEIGHTS}
    return _step(x, mem, positions, loss_target, w, m, v)
```

```python
from typing import NamedTuple

import jax
import jax.numpy as jnp
from jax import lax
from jax.experimental import pallas as pl
from jax.experimental.pallas import tpu as pltpu

F32 = jnp.float32
BF16 = jnp.bfloat16
HIGHEST = lax.Precision.HIGHEST
MESH_ID = pl.DeviceIdType.MESH

VMEM_LIMIT_BYTES = 56 * 1024 * 1024
LANES = 128

EPS = 1e-6
DEPTH = 4
D_MODEL = 1024
N_DEV = 8
SSD_HEADS = 16
SSD_HEAD_DIM = 64
SSD_STATE = 128
SSD_GROUPS = 4
SSD_INNER = 1024
SSD_TILE = 256
CONV_K = 31
SSD_CONV_K = 4
MLA_HEADS = 8
MLA_NOPE = 128
MLA_ROPE = 64
MLA_V = 128
MLA_Q_RANK = 384
MLA_KV_RANK = 256
ATT_CHUNK = 64
ROPE_THETA = 10000.0
X_HEADS = 4
X_HEAD_DIM = 256
FFN_HIDDEN = 2816
IN_SIZES = (1024, 2048, 16, 2048, 384, 320, 3072)

ADAM_LR = 0.001
ADAM_B1 = 0.9
ADAM_B2 = 0.999
ADAM_EPS = 1e-08
ADAM_WD = 0.01
ADAM_STEP = 10

MM_FULL_K = 3072
ROW_TILE = 256
ATT_BLOCK = 512
XATT_Q_TILE = 512

SHARDED = {
    "w_in": 1, "ssd_conv_w": 1, "ssd_w_out": 0, "conv_dw_w": 1, "conv_w_out": 0, "mla_w_q_b": 1, "mla_w_kv_b": 1,
    "mla_w_o": 0, "gate_b": 1, "w_out": 0, "xattn_w_q": 0, "xattn_w_kv": 1, "xattn_w_o": 0, "ffn_w_in": 1,
    "ffn_w_out": 0,
}
WEIGHTS = ["mix_norm_g", "w_in", "ssd_conv_w", "ssd_conv_b", "ssd_dt_bias", "ssd_a_log", "ssd_d", "ssd_norm_g",
           "ssd_w_out", "conv_dw_w", "conv_dw_b", "conv_ln_g", "conv_ln_b", "conv_w_out", "mla_q_a_g", "mla_w_q_b",
           "mla_kv_a_g", "mla_w_kv_b", "mla_q_norm_g", "mla_k_norm_g", "mla_w_o", "gate_b", "w_out", "xattn_norm_g",
           "mem_norm_g", "xattn_w_q", "xattn_w_kv", "xattn_q_norm_g", "xattn_k_norm_g", "xattn_w_o", "ffn_norm_g",
           "ffn_w_in", "ffn_w_out"]
REPLICATED = [n for n in WEIGHTS if n not in SHARDED]


def _params(sem=None):
    return pltpu.CompilerParams(dimension_semantics=sem, vmem_limit_bytes=VMEM_LIMIT_BYTES)


def _divisor(n, cap, mult):
    if n <= cap:
        return n
    for d in range(cap - cap % mult, 0, -mult):
        if n % d == 0:
            return d
    raise ValueError(f"no tile for {n}")


def _dg(a, b, ca, cb):
    return lax.dot_general(a.astype(BF16), b.astype(BF16), (((ca,), (cb,)), ((), ())), preferred_element_type=F32)


@jax.custom_vjp
def bdot(a, b):
    return _dg(a, b, 1, 0)


bdot.defvjp(lambda a, b: (_dg(a, b, 1, 0), (a, b)), lambda r, g: (_dg(g, r[1], 1, 1), _dg(r[0], g, 0, 0)))


@jax.custom_vjp
def bdot_nt(a, b):
    return _dg(a, b, 1, 1)


bdot_nt.defvjp(lambda a, b: (_dg(a, b, 1, 1), (a, b)), lambda r, g: (_dg(g, r[1], 1, 0), _dg(g, r[0], 0, 0)))


@jax.custom_vjp
def bdot_tn(a, b):
    return _dg(a, b, 0, 0)


bdot_tn.defvjp(lambda a, b: (_dg(a, b, 0, 0), (a, b)), lambda r, g: (_dg(r[1], g, 1, 1), _dg(r[0], g, 1, 0)))


def hdot(a, b):
    return jnp.dot(a, b, precision=HIGHEST, preferred_element_type=F32)


def _iota(shape, dim):
    return lax.broadcasted_iota(jnp.int32, shape, dim)


def _mm(name, a, b, ta=False, tb=False, res=None):
    m, k = (a.shape[1], a.shape[0]) if ta else a.shape
    n = b.shape[0] if tb else b.shape[1]
    tm = _divisor(m, 1536, LANES) if ta else _divisor(m, 1024, 8)
    tn = _divisor(n, 1536, LANES)
    tk = k if k <= MM_FULL_K else _divisor(k, 1024, LANES)
    if tk == k and k > 1024:
        tm = _divisor(m, 512, LANES if ta else 8)
    nk = k // tk
    dims = (((0 if ta else 1,), (1 if tb else 0,)), ((), ()))

    def body(*refs):
        if res is None:
            a_ref, b_ref, o_ref = refs
        else:
            a_ref, b_ref, r_ref, o_ref = refs
        part = lax.dot_general(a_ref[...].astype(BF16), b_ref[...].astype(BF16), dims, preferred_element_type=F32)
        if nk == 1:
            o_ref[...] = part if res is None else part + r_ref[...]
        else:
            kk = pl.program_id(2)

            @pl.when(kk == 0)
            def _():
                o_ref[...] = part if res is None else part + r_ref[...]

            @pl.when(kk != 0)
            def _():
                o_ref[...] += part

    a_spec = pl.BlockSpec((tk, tm), lambda i, j, kk: (kk, i)) if ta else pl.BlockSpec((tm, tk), lambda i, j, kk: (i, kk))
    b_spec = pl.BlockSpec((tn, tk), lambda i, j, kk: (j, kk)) if tb else pl.BlockSpec((tk, tn), lambda i, j, kk: (kk, j))
    o_spec = pl.BlockSpec((tm, tn), lambda i, j, kk: (i, j))
    in_specs = [a_spec, b_spec] + ([] if res is None else [o_spec])
    args = (a, b) + (() if res is None else (res,))
    return pl.pallas_call(
        body, grid=(m // tm, n // tn, nk), in_specs=in_specs, out_specs=o_spec,
        out_shape=jax.ShapeDtypeStruct((m, n), F32),
        compiler_params=_params(("parallel", "parallel", "arbitrary")), name=name)(*args)


class Mat(NamedTuple):
    value: jax.Array
    slot: jax.Array


MATRICES = frozenset([
    "w_z", "w_xbc", "w_dt", "w_glu", "w_q", "w_ckv", "w_kr", "w_gate", "ssd_w_out", "conv_w_out", "w_qn", "w_qr",
    "w_kn", "w_v", "mla_w_o", "w_out", "xattn_w_q", "w_xk", "w_xv", "xattn_w_o", "w_ffn_gate", "w_ffn_up",
    "ffn_w_out"])


class Act(NamedTuple):
    value: jax.Array
    slot: jax.Array


def _operand(a):
    return (a.value, a.slot) if isinstance(a, Act) else (a, a)


def matmul(name, a, mat, res=None):
    w, slot = mat
    a, a_slot = _operand(a)
    if res is None:
        @jax.custom_vjp
        def run(a, a_slot, w, slot):
            return _mm(name, a, w)

        def fwd(a, a_slot, w, slot):
            return run(a, a_slot, w, slot), (a, w)

        def bwd(r, g):
            return None, _mm(name + "_da", g, r[1], tb=True), None, _mm(name + "_dw", r[0], g, ta=True)

        run.defvjp(fwd, bwd)
        return run(a, a_slot, w, slot)

    @jax.custom_vjp
    def run_res(a, a_slot, w, slot, res):
        return _mm(name, a, w, res=res)

    def fwd_res(a, a_slot, w, slot, res):
        return run_res(a, a_slot, w, slot, res), (a, w)

    def bwd_res(r, g):
        return None, _mm(name + "_da", g, r[1], tb=True), None, _mm(name + "_dw", r[0], g, ta=True), g

    run_res.defvjp(fwd_res, bwd_res)
    return run_res(a, a_slot, w, slot, res)


def multi_matmul(name, a, mats, keys):
    ws, slots = tuple(m.value for m in mats), tuple(m.slot for m in mats)
    a, a_slot = _operand(a)

    @jax.custom_vjp
    def run(a, a_slot, ws, slots):
        return tuple(_mm(f"{name}_{k}", a, w) for k, w in zip(keys, ws))

    def fwd(a, a_slot, ws, slots):
        return run(a, a_slot, ws, slots), (a, ws)

    def bwd(r, gs):
        a, ws = r
        da = None
        for k, w, g in zip(keys, ws, gs):
            da = _mm(f"{name}_{k}_da", g, w, tb=True, res=da)
        dws = tuple(_mm(f"{name}_{k}_dw", a, g, ta=True) for k, g in zip(keys, gs))
        return None, da, tuple(None for _ in ws), dws

    run.defvjp(fwd, bwd)
    return run(a, a_slot, ws, slots)


def tmap(name, f, grid, ins, outs, through=None, narrow=()):
    arrays = [x[0] for x in ins]
    kinds = [x[3] for x in ins]
    in_specs = [pl.BlockSpec(x[1], x[2]) for x in ins]
    out_specs = [pl.BlockSpec(x[1], x[2]) for x in outs]
    out_shape = [jax.ShapeDtypeStruct(x[0], BF16 if k in narrow else F32) for k, x in enumerate(outs)]
    n_in, n_out = len(ins), len(outs)
    n_through = 0 if through is None else 1
    assert through is None or kinds[through] == "t"
    didx = [k for k, kd in enumerate(kinds) if kd != "n"]

    def fwd_call(*arrs):
        def body(*refs):
            pids = (pl.program_id(0), pl.program_id(1))
            vals = f(pids, *[r[...] for r in refs[:n_in]])
            for r, v in zip(refs[n_in:], vals):
                r[...] = v.astype(r.dtype)

        return pl.pallas_call(body, grid=grid, in_specs=in_specs, out_specs=out_specs, out_shape=out_shape,
                              compiler_params=_params(("arbitrary", "arbitrary")), name=name)(*arrs)

    def bwd_call(arrs, cts):
        def body(*refs):
            o, i = pl.program_id(0), pl.program_id(1)
            vals = [r[...] for r in refs[:n_in]]

            def g(*dv):
                full = list(vals)
                for k, v in zip(didx, dv):
                    full[k] = v
                return tuple(f((o, i), *full))

            _, vjp = jax.vjp(g, *[vals[k] for k in didx])
            grads = vjp(tuple(r[...] for r in refs[n_in:n_in + n_out]))
            for k, gr, r in zip(didx, grads, refs[n_in + n_out + n_through:]):
                if k == through:
                    r[...] = gr + refs[n_in + n_out][...]
                elif kinds[k] == "t":
                    r[...] = gr
                else:
                    first = (i == 0) if kinds[k] == "ai" else jnp.logical_and(o == 0, i == 0)

                    @pl.when(first)
                    def _(r=r, gr=gr):
                        r[...] = gr

                    @pl.when(jnp.logical_not(first))
                    def _(r=r, gr=gr):
                        r[...] += gr

        g_specs = [in_specs[k] for k in didx]
        g_shape = [jax.ShapeDtypeStruct(arrs[k].shape, F32) for k in didx]
        ct_specs = out_specs + ([in_specs[through]] if n_through else [])
        return pl.pallas_call(body, grid=grid, in_specs=in_specs + ct_specs, out_specs=g_specs, out_shape=g_shape,
                              compiler_params=_params(("arbitrary", "arbitrary")), name=name + "_bwd")(*arrs, *cts)

    @jax.custom_vjp
    def run(*arrs):
        res = [Act(o, jnp.zeros(o.shape, F32)) if k in narrow else o for k, o in enumerate(fwd_call(*arrs))]
        return tuple(res) + ((arrs[through],) if n_through else ())

    def run_fwd(*arrs):
        return run(*arrs), arrs

    def run_bwd(arrs, cts):
        cts = [c.slot if isinstance(c, Act) else c for c in cts]
        gs = bwd_call(arrs, cts)
        full = [None] * n_in
        for k, g in zip(didx, gs):
            full[k] = g
        return tuple(full)

    run.defvjp(run_fwd, run_bwd)
    return run(*arrays)


def _rows(arr, tile, kind="t"):
    return (arr, (tile, arr.shape[1]), lambda o, i: (i, 0), kind)


def _whole(arr, kind="ag"):
    return (arr, arr.shape, lambda o, i: (0, 0), kind)


def _row_out(t, n, tile):
    return ((t, n), (tile, n), lambda o, i: (i, 0))


def _rms(x, g, n=None):
    ms = jnp.sum(x * x, axis=-1, keepdims=True) / (x.shape[-1] if n is None else n)
    return x * lax.rsqrt(ms + EPS) * g


def rms_norm(name, x, g):
    t, n = x.shape
    tile = min(ROW_TILE, t)
    return tmap(name, lambda p, x, g: (_rms(x, g),), (1, t // tile), [_rows(x, tile), _whole(g)],
                [_row_out(t, n, tile)], narrow=(0,))[0]


def rms_norm_through(name, x, g):
    t, n = x.shape
    tile = min(ROW_TILE, t)
    return tmap(name, lambda p, x, g: (_rms(x, g),), (1, t // tile), [_rows(x, tile), _whole(g)],
                [_row_out(t, n, tile)], through=0, narrow=(0,))


def rms_norm_nograd_x(name, x, g):
    t, n = x.shape
    tile = min(ROW_TILE, t)
    return tmap(name, lambda p, x, g: (_rms(x, g),), (1, t // tile), [_rows(x, tile, "n"), _whole(g)],
                [_row_out(t, n, tile)], narrow=(0,))[0]


def _glu_f(p, glu):
    h = glu.shape[1] // 2
    return (glu[:, :h] * jax.nn.sigmoid(glu[:, h:]),)


def _ln_silu_f(p, v, g, b):
    mu = jnp.mean(v, axis=-1, keepdims=True)
    xc = v - mu
    var = jnp.mean(xc * xc, axis=-1, keepdims=True)
    return (jax.nn.silu(xc * lax.rsqrt(var + EPS) * g + b),)


def _ssd_gate_norm_f(p, y, z, g):
    v = y * jax.nn.silu(z)
    w = SSD_INNER // SSD_GROUPS
    parts = []
    for k in range(SSD_GROUPS):
        vg = v[:, k * w:(k + 1) * w]
        parts.append(vg * lax.rsqrt(jnp.mean(vg * vg, axis=-1, keepdims=True) + EPS))
    return (jnp.concatenate(parts, axis=1) * g,)


def _merge_f(p, gl, gb, y0, y1, y2):
    g = jax.nn.sigmoid(gl + gb)
    d = D_MODEL
    return (g[:, :d] * y0 + g[:, d:2 * d] * y1 + g[:, 2 * d:] * y2,)


def _swiglu_f(p, gate, up):
    return (jax.nn.silu(gate) * up,)


def _rot_matrix():
    r, c = _iota((LANES, LANES), 0), _iota((LANES, LANES), 1)
    h = MLA_ROPE // 2
    plus = jnp.logical_and(c >= h, jnp.logical_and(c < 2 * h, r == c - h))
    minus = jnp.logical_and(c < h, r == c + h)
    return plus.astype(F32) - minus.astype(F32)


def _rope(x, cosf, sinf):
    return x * cosf + hdot(x, _rot_matrix()) * sinf


def _per_head(f, x):
    return jnp.concatenate([f(x[:, h * LANES:(h + 1) * LANES]) for h in range(x.shape[1] // LANES)], axis=1)


def _k_nope_f(p, kn_raw, kg):
    return (_per_head(lambda x: _rms(x, kg[:, :MLA_NOPE]), kn_raw),)


def _k_rope_f(p, kr_raw, cosf, sinf, kg):
    return (_rope(_rms(kr_raw, kg[:, MLA_NOPE:], n=MLA_ROPE), cosf, sinf),)


def _q_prep_f(p, qn_raw, qr_raw, cosf, sinf, qg):
    qn = _per_head(lambda x: _rms(x, qg[:, :MLA_NOPE]), qn_raw)
    qr = _per_head(lambda x: _rope(_rms(x, qg[:, MLA_NOPE:], n=MLA_ROPE), cosf, sinf), qr_raw)
    return qn, qr


def _softmax(s):
    m = jnp.max(s, axis=-1, keepdims=True)
    e = jnp.exp(s - m)
    return e / jnp.sum(e, axis=-1, keepdims=True)


def _xattn_f(p, q, k, v, qg, kg):
    s = bdot_nt(_rms(q, qg), _rms(k, kg)) * (X_HEAD_DIM ** -0.5)
    return (bdot(_softmax(s), v),)


ATT_SCALE = (MLA_NOPE + MLA_ROPE) ** -0.5
NT_DIMS = (((1,), (1,)), ((), ()))
NN_DIMS = (((1,), (0,)), ((), ()))
TN_DIMS = (((0,), (0,)), ((), ()))


def _att_specs(t, blk):
    q_spec = pl.BlockSpec((blk, LANES), lambda h, i: (i, h))
    k_spec = pl.BlockSpec((t, LANES), lambda h, i: (0, h))
    shared = pl.BlockSpec((t, LANES), lambda h, i: (0, 0))
    lse_spec = pl.BlockSpec((None, blk, 1), lambda h, i: (h, i, 0))
    return q_spec, k_spec, shared, lse_spec


def _diagonal_mask(blk):
    return (_iota((blk, blk), 1) // ATT_CHUNK) <= (_iota((blk, blk), 0) // ATT_CHUNK)


def _att_keys(kn_ref, kr_ref, j, blk):
    ks = pl.ds(pl.multiple_of(j * blk, blk), blk)
    return ks, jnp.concatenate([kn_ref[ks, :], kr_ref[ks, :]], axis=1).astype(BF16)


def _att_fwd(name, qn, qr, kn, kr, v, hosted=()):
    t, width = qn.shape
    heads = width // LANES
    blk = min(ATT_BLOCK, t)
    nh = len(hosted)

    def body(qn_ref, qr_ref, kn_ref, kr_ref, v_ref, *rest):
        shard_refs, (o_ref, lse_ref), rest = rest[:nh], rest[nh:nh + 2], rest[nh + 2:]
        full_refs, sems = rest[:nh], rest[nh:]
        i = pl.program_id(1)
        gathers = [_gather_copies(shard_refs[k], full_refs[k], *sems[3 * k:3 * k + 3]) for k in range(nh)]
        if nh:
            @pl.when(jnp.logical_and(pl.program_id(0) == 0, i == 0))
            def _():
                for start, _ in gathers:
                    start()
        q = jnp.concatenate([qn_ref[...], qr_ref[...]], axis=1).astype(BF16)

        def scores(j):
            _, k = _att_keys(kn_ref, kr_ref, j, blk)
            return lax.dot_general(q, k, NT_DIMS, preferred_element_type=F32)

        def weighted_values(p, j):
            ks = pl.ds(pl.multiple_of(j * blk, blk), blk)
            return lax.dot_general(p, v_ref[ks, :].astype(BF16), NN_DIMS, preferred_element_type=F32)

        def softmax_step(s, m, l):
            m_new = jnp.maximum(m, jnp.max(s, axis=1, keepdims=True))
            alpha = jnp.exp(m - m_new)
            p = jnp.exp(s - m_new)
            return m_new, alpha, alpha * l + jnp.sum(p, axis=1, keepdims=True), p.astype(BF16)

        def step(j, carry):
            s, p_prev, m, l, acc = carry
            s_next = scores(j + 1)
            pv_prev = weighted_values(p_prev, jnp.maximum(j - 1, 0))
            m, alpha, l, p = softmax_step(s * ATT_SCALE, m, l)
            return s_next, p, m, l, alpha * (acc + pv_prev)

        init = (scores(0), jnp.zeros((blk, blk), BF16), jnp.full((blk, 1), -1e30, F32), jnp.zeros((blk, 1), F32),
                jnp.zeros((blk, LANES), F32))
        s, p_prev, m, l, acc = lax.fori_loop(0, i, step, init)
        pv_prev = weighted_values(p_prev, jnp.maximum(i - 1, 0))
        s = jnp.where(_diagonal_mask(blk), s * ATT_SCALE, -1e30)
        m, alpha, l, p = softmax_step(s, m, l)
        acc = alpha * (acc + pv_prev) + weighted_values(p, i)
        o_ref[...] = acc / l
        lse_ref[...] = m + jnp.log(l)
        if nh:
            @pl.when(jnp.logical_and(pl.program_id(0) == heads - 1, i == t // blk - 1))
            def _():
                for _, finish in gathers:
                    finish()

    q_spec, k_spec, shared, lse_spec = _att_specs(t, blk)
    sem_shapes = [pltpu.SemaphoreType.DMA((7,)), pltpu.SemaphoreType.DMA((7,)), pltpu.SemaphoreType.DMA] * nh
    return pl.pallas_call(
        body, grid=(heads, t // blk), in_specs=[q_spec, q_spec, k_spec, shared, k_spec] + [HBM_SPEC] * nh,
        out_specs=[q_spec, lse_spec] + [HBM_SPEC] * nh,
        out_shape=[jax.ShapeDtypeStruct((t, width), F32), jax.ShapeDtypeStruct((heads, t, 1), F32)] +
                  [jax.ShapeDtypeStruct((N_DEV,) + s.shape, s.dtype) for s in hosted],
        scratch_shapes=sem_shapes, compiler_params=_params(("arbitrary", "arbitrary")), name=name)(
        qn, qr, kn, kr, v, *hosted)


def _att_bwd(name, qn, qr, kn, kr, v, o, lse, do):
    t, width = qn.shape
    heads = width // LANES
    blk = min(ATT_BLOCK, t)

    def body(qn_ref, qr_ref, kn_ref, kr_ref, v_ref, o_ref, lse_ref, do_ref, dqn_ref, dqr_ref, dkn_ref, dkr_ref,
             dv_ref):
        h, i = pl.program_id(0), pl.program_id(1)

        @pl.when(i == 0)
        def _():
            dkn_ref[...] = jnp.zeros_like(dkn_ref)
            dv_ref[...] = jnp.zeros_like(dv_ref)

        @pl.when(jnp.logical_and(h == 0, i == 0))
        def _():
            dkr_ref[...] = jnp.zeros_like(dkr_ref)

        q = jnp.concatenate([qn_ref[...], qr_ref[...]], axis=1).astype(BF16)
        do = do_ref[...]
        do16 = do.astype(BF16)
        delta = jnp.sum(do * o_ref[...], axis=1, keepdims=True)
        lse = lse_ref[...]

        def issue(j):
            ks, k = _att_keys(kn_ref, kr_ref, j, blk)
            s = lax.dot_general(q, k, NT_DIMS, preferred_element_type=F32)
            dp = lax.dot_general(do16, v_ref[ks, :].astype(BF16), NT_DIMS, preferred_element_type=F32)
            return s, dp

        def retire(p, ds, j, dq):
            ks, k = _att_keys(kn_ref, kr_ref, j, blk)
            dv_ref[ks, :] += lax.dot_general(p, do16, TN_DIMS, preferred_element_type=F32)
            dk = lax.dot_general(ds, q, TN_DIMS, preferred_element_type=F32)
            dkn_ref[ks, :] += dk[:, :LANES]
            dkr_ref[ks, :] += dk[:, LANES:]
            return dq + lax.dot_general(ds, k, NN_DIMS, preferred_element_type=F32)

        def probs(s, dp, masked):
            s = s * ATT_SCALE
            if masked:
                s = jnp.where(_diagonal_mask(blk), s, -1e30)
            p = jnp.exp(s - lse)
            return p.astype(BF16), (p * (dp - delta) * ATT_SCALE).astype(BF16)

        def step(j, carry):
            s, dp, p_prev, ds_prev, dq = carry
            s_next, dp_next = issue(j + 1)
            dq = retire(p_prev, ds_prev, jnp.maximum(j - 1, 0), dq)
            p, ds = probs(s, dp, False)
            return s_next, dp_next, p, ds, dq

        none = jnp.zeros((blk, blk), BF16)
        s, dp, p_prev, ds_prev, dq = lax.fori_loop(0, i, step,
                                                   issue(0) + (none, none, jnp.zeros((blk, 2 * LANES), F32)))
        dq = retire(p_prev, ds_prev, jnp.maximum(i - 1, 0), dq)
        p, ds = probs(s, dp, True)
        dq = retire(p, ds, i, dq)
        dqn_ref[...] = dq[:, :LANES]
        dqr_ref[...] = dq[:, LANES:]

    q_spec, k_spec, shared, lse_spec = _att_specs(t, blk)
    big, one = jax.ShapeDtypeStruct((t, width), F32), jax.ShapeDtypeStruct((t, LANES), F32)
    return pl.pallas_call(
        body, grid=(heads, t // blk),
        in_specs=[q_spec, q_spec, k_spec, shared, k_spec, q_spec, lse_spec, q_spec],
        out_specs=[q_spec, q_spec, k_spec, shared, k_spec], out_shape=[big, big, big, one, big],
        compiler_params=_params(("arbitrary", "arbitrary")), name=name)(qn, qr, kn, kr, v, o, lse, do)


def mla_attention(name, qn, qr, kn, kr, v, hosted=()):
    @jax.custom_vjp
    def run(qn, qr, kn, kr, v, hosted):
        o, _, *gathered = _att_fwd(name, qn, qr, kn, kr, v, hosted)
        return o, tuple(gathered)

    def fwd(qn, qr, kn, kr, v, hosted):
        o, lse, *gathered = _att_fwd(name, qn, qr, kn, kr, v, hosted)
        return (o, tuple(gathered)), (qn, qr, kn, kr, v, o, lse)

    def bwd(r, g):
        return tuple(_att_bwd(name + "_bwd", *r, g[0])) + ((None,) * len(hosted),)

    run.defvjp(fwd, bwd)
    return run(qn, qr, kn, kr, v, tuple(hosted))


def _shift_down(v, s, rows):
    return v if s == 0 else jnp.where(rows >= s, pltpu.roll(v, s, 0), 0.0)


def _shift_up(v, s, rows):
    t = v.shape[0]
    return v if s == 0 else jnp.where(rows < t - s, pltpu.roll(v, t - s, 0), 0.0)


def _dwconv_fwd(name, x, w, b):
    t, c = x.shape
    kw = w.shape[0]

    def body(x_ref, w_ref, b_ref, y_ref):
        x = x_ref[...]
        rows = _iota(x.shape, 0)
        acc = jnp.zeros_like(x) + b_ref[...]
        for k in range(kw):
            acc = acc + w_ref[k:k + 1, :] * _shift_down(x, kw - 1 - k, rows)
        y_ref[...] = acc

    col = lambda i: (0, i)
    return pl.pallas_call(
        body, grid=(c // LANES,),
        in_specs=[pl.BlockSpec((t, LANES), col), pl.BlockSpec((kw, LANES), col), pl.BlockSpec((1, LANES), col)],
        out_specs=pl.BlockSpec((t, LANES), col), out_shape=jax.ShapeDtypeStruct((t, c), F32),
        compiler_params=_params(("arbitrary",)), name=name)(x, w, b)


def _dwconv_bwd(name, x, w, dy):
    t, c = x.shape
    kw = w.shape[0]

    def body(x_ref, w_ref, dy_ref, dx_ref, dw_ref, db_ref):
        x, dy = x_ref[...], dy_ref[...]
        rows = _iota(x.shape, 0)
        dx = jnp.zeros_like(x)
        for k in range(kw):
            s = kw - 1 - k
            dx = dx + w_ref[k:k + 1, :] * _shift_up(dy, s, rows)
            dw_ref[k:k + 1, :] = jnp.sum(dy * _shift_down(x, s, rows), axis=0, keepdims=True)
        dx_ref[...] = dx
        db_ref[...] = jnp.sum(dy, axis=0, keepdims=True)

    col = lambda i: (0, i)
    big, wsp, bsp = pl.BlockSpec((t, LANES), col), pl.BlockSpec((kw, LANES), col), pl.BlockSpec((1, LANES), col)
    return pl.pallas_call(
        body, grid=(c // LANES,), in_specs=[big, wsp, big], out_specs=[big, wsp, bsp],
        out_shape=[jax.ShapeDtypeStruct((t, c), F32), jax.ShapeDtypeStruct((kw, c), F32),
                   jax.ShapeDtypeStruct((1, c), F32)],
        compiler_params=_params(("arbitrary",)), name=name)(x, w, dy)


def dwconv(name, x, w, b):
    @jax.custom_vjp
    def run(x, w, b):
        return _dwconv_fwd(name, x, w, b)

    def fwd(x, w, b):
        return run(x, w, b), (x, w)

    def bwd(r, g):
        return tuple(_dwconv_bwd(name + "_bwd", r[0], r[1], g))

    run.defvjp(fwd, bwd)
    return run(x, w, b)


def _ssd_tile(xc, dtr, dtb, alog, dsk, prev):
    ln = xc.shape[0]
    gw = SSD_INNER // SSD_GROUPS
    ns = SSD_STATE
    xs = jax.nn.silu(xc[:, :SSD_INNER])
    bm = jax.nn.silu(xc[:, SSD_INNER:SSD_INNER + SSD_GROUPS * ns])
    cm = jax.nn.silu(xc[:, SSD_INNER + SSD_GROUPS * ns:])
    dt = jax.nn.softplus(dtr + dtb)
    a = dt * (-jnp.exp(alog))
    expand = (_iota((LANES, SSD_INNER), 0) == _iota((LANES, SSD_INNER), 1) // SSD_HEAD_DIM).astype(F32)
    causal = _iota((ln, ln), 0) >= _iota((ln, ln), 1)
    acs_h = hdot(causal.astype(F32), a)
    acs_c = hdot(acs_h, expand)
    dt_c = hdot(dt, expand)

    def row_per_column(v):
        return jnp.mean(hdot(jnp.broadcast_to(v, (8, LANES)), expand), axis=0, keepdims=True)

    aend_c = row_per_column(jnp.sum(a, axis=0, keepdims=True))
    xdt = xs * dt_c
    to_end = xdt * jnp.exp(aend_c - acs_c)
    from_start = jnp.exp(acs_c)
    acs_ht = acs_h.T
    lane_h, sub_h = _iota((1, LANES), 1), _iota((LANES, 1), 0)
    head_of_col = _iota((1, gw), 1) // SSD_HEAD_DIM
    ys, states = [], []
    for g in range(SSD_GROUPS):
        cg = cm[:, g * ns:(g + 1) * ns]
        bg = bm[:, g * ns:(g + 1) * ns]
        cols = slice(g * gw, (g + 1) * gw)
        y = bdot(cg, prev[:, cols]) * from_start[:, cols]
        states.append(bdot_tn(bg, to_end[:, cols]))
        cb = bdot_nt(cg, bg)
        for r in range(gw // SSD_HEAD_DIM):
            h = g * (gw // SSD_HEAD_DIM) + r
            col = jnp.sum(jnp.where(lane_h == h, acs_h, 0.0), axis=1, keepdims=True)
            row = jnp.sum(jnp.where(sub_h == h, acs_ht, 0.0), axis=0, keepdims=True)
            decay = jnp.exp(jnp.where(causal, col - row, -1e30))
            y = y + jnp.where(head_of_col == r, bdot(cb * decay, xdt[:, cols]), 0.0)
        ys.append(y)
    y = jnp.concatenate(ys, axis=1) + row_per_column(dsk) * xs
    new = prev * jnp.exp(aend_c) + jnp.concatenate(states, axis=1)
    return y, new


def _ssd_fwd(name, xc, dtr, dtb, alog, dsk):
    t = xc.shape[0]
    ln = min(SSD_TILE, t)
    nt = t // ln

    def body(xc_ref, dtr_ref, dtb_ref, alog_ref, dsk_ref, y_ref, prev_ref, carry):
        @pl.when(pl.program_id(0) == 0)
        def _():
            carry[...] = jnp.zeros_like(carry)

        prev = carry[...]
        prev_ref[...] = prev
        y, new = _ssd_tile(xc_ref[...], dtr_ref[...], dtb_ref[...], alog_ref[...], dsk_ref[...], prev)
        y_ref[...] = y
        carry[...] = new

    row = lambda i: (i, 0)
    par = pl.BlockSpec((1, LANES), lambda i: (0, 0))
    return pl.pallas_call(
        body, grid=(nt,),
        in_specs=[pl.BlockSpec((ln, xc.shape[1]), row), pl.BlockSpec((ln, LANES), row), par, par, par],
        out_specs=[pl.BlockSpec((ln, SSD_INNER), row), pl.BlockSpec((None, SSD_STATE, SSD_INNER), lambda i: (i, 0, 0))],
        out_shape=[jax.ShapeDtypeStruct((t, SSD_INNER), F32), jax.ShapeDtypeStruct((nt, SSD_STATE, SSD_INNER), F32)],
        scratch_shapes=[pltpu.VMEM((SSD_STATE, SSD_INNER), F32)],
        compiler_params=_params(("arbitrary",)), name=name)(xc, dtr, dtb, alog, dsk)


def _ssd_bwd(name, xc, dtr, dtb, alog, dsk, prevs, dy):
    t = xc.shape[0]
    ln = min(SSD_TILE, t)
    nt = t // ln

    def body(xc_ref, dtr_ref, dtb_ref, alog_ref, dsk_ref, prev_ref, dy_ref, dxc_ref, ddtr_ref, ddtb_ref, dalog_ref,
             ddsk_ref, dcarry):
        i = pl.program_id(0)

        @pl.when(i == 0)
        def _():
            dcarry[...] = jnp.zeros_like(dcarry)

        _, vjp = jax.vjp(_ssd_tile, xc_ref[...], dtr_ref[...], dtb_ref[...], alog_ref[...], dsk_ref[...], prev_ref[...])
        dxc, ddtr, ddtb, dalog, ddsk, dprev = vjp((dy_ref[...], dcarry[...]))
        dxc_ref[...] = dxc
        ddtr_ref[...] = ddtr
        dcarry[...] = dprev
        for r, gr in ((ddtb_ref, ddtb), (dalog_ref, dalog), (ddsk_ref, ddsk)):
            @pl.when(i == 0)
            def _(r=r, gr=gr):
                r[...] = gr

            @pl.when(i != 0)
            def _(r=r, gr=gr):
                r[...] += gr

    row = lambda i: (nt - 1 - i, 0)
    par = pl.BlockSpec((1, LANES), lambda i: (0, 0))
    big, dts = pl.BlockSpec((ln, xc.shape[1]), row), pl.BlockSpec((ln, LANES), row)
    par_shape = jax.ShapeDtypeStruct((1, LANES), F32)
    return pl.pallas_call(
        body, grid=(nt,),
        in_specs=[big, dts, par, par, par, pl.BlockSpec((None, SSD_STATE, SSD_INNER), lambda i: (nt - 1 - i, 0, 0)),
                  pl.BlockSpec((ln, SSD_INNER), row)],
        out_specs=[big, dts, par, par, par],
        out_shape=[jax.ShapeDtypeStruct(xc.shape, F32), jax.ShapeDtypeStruct(dtr.shape, F32), par_shape, par_shape,
                   par_shape],
        scratch_shapes=[pltpu.VMEM((SSD_STATE, SSD_INNER), F32)],
        compiler_params=_params(("arbitrary",)), name=name)(xc, dtr, dtb, alog, dsk, prevs, dy)


def ssd_scan(name, xc, dtr, dtb, alog, dsk):
    @jax.custom_vjp
    def run(xc, dtr, dtb, alog, dsk):
        return _ssd_fwd(name, xc, dtr, dtb, alog, dsk)[0]

    def fwd(xc, dtr, dtb, alog, dsk):
        y, prevs = _ssd_fwd(name, xc, dtr, dtb, alog, dsk)
        return y, (xc, dtr, dtb, alog, dsk, prevs)

    def bwd(r, g):
        return tuple(_ssd_bwd(name + "_bwd", *r, g))

    run.defvjp(fwd, bwd)
    return run(xc, dtr, dtb, alog, dsk)


def loss_head(y, target):
    t, n = y.shape
    tile = min(ROW_TILE, t)

    def body(y_ref, t_ref, dy_ref, acc_ref):
        d = y_ref[...] - t_ref[...]
        dy_ref[...] = d * (1.0 / n)

        @pl.when(pl.program_id(0) == 0)
        def _():
            acc_ref[...] = jnp.zeros_like(acc_ref)

        acc_ref[...] += jnp.sum(d * d, axis=0, keepdims=True)

    row = pl.BlockSpec((tile, n), lambda i: (i, 0))
    dy, acc = pl.pallas_call(
        body, grid=(t // tile,), in_specs=[row, row], out_specs=[row, pl.BlockSpec((1, n), lambda i: (0, 0))],
        out_shape=[jax.ShapeDtypeStruct((t, n), F32), jax.ShapeDtypeStruct((1, n), F32)],
        compiler_params=_params(("arbitrary",)), name="loss_head")(y, target)
    return acc, dy


def adamw(name, w, g, m, v):
    shape = w.shape
    cols = shape[-1]
    rows = w.size // cols
    tile = _divisor(rows, 512, 8) if rows % 8 == 0 else rows

    def body(w_ref, g_ref, m_ref, v_ref, d_ref, nm_ref, nv_ref):
        g = g_ref[...]
        m = ADAM_B1 * m_ref[...] + (1.0 - ADAM_B1) * g
        v = ADAM_B2 * v_ref[...] + (1.0 - ADAM_B2) * jnp.square(g)
        m_hat = m / (1.0 - ADAM_B1 ** ADAM_STEP)
        v_hat = v / (1.0 - ADAM_B2 ** ADAM_STEP)
        d_ref[...] = -ADAM_LR * (m_hat / (jnp.sqrt(v_hat) + ADAM_EPS) + ADAM_WD * w_ref[...])
        nm_ref[...] = m
        nv_ref[...] = v

    spec = pl.BlockSpec((tile, cols), lambda i: (i, 0))
    two_d = jax.ShapeDtypeStruct((rows, cols), F32)
    outs = pl.pallas_call(body, grid=(rows // tile,), in_specs=[spec] * 4, out_specs=[spec] * 3, out_shape=[two_d] * 3,
                          compiler_params=_params(("arbitrary",)), name=name)(
        *[a.reshape(rows, cols) for a in (w, g, m, v)])
    return [o.reshape(shape) for o in outs]


HBM_SPEC = pl.BlockSpec(memory_space=pl.ANY)


def _position():
    return lax.axis_index("x"), lax.axis_index("y"), lax.axis_index("c")


def _gather_copies(x_ref, out_ref, send_sems, recv_sems, local_sem):
    x, y, c = _position()
    me, sibling = (x, y, c), (x, y, 1 - c)
    chips = [(1 - x, y), (x, 1 - y), (1 - x, 1 - y)]

    def block(px, py, pc):
        return out_ref.at[4 * px + 2 * py + pc]

    def copy(k, blk, to, src=None):
        return pltpu.make_async_remote_copy(
            src_ref=block(*blk) if src is None else src, dst_ref=block(*blk), send_sem=send_sems.at[k],
            recv_sem=recv_sems.at[k], device_id=to, device_id_type=MESH_ID)

    mine = pltpu.make_async_copy(x_ref, block(*me), local_sem)
    first = [copy(0, me, sibling, src=x_ref)]
    first += [copy(1 + j, me, (*chip, c), src=x_ref) for j, chip in enumerate(chips)]
    passed = [copy(4 + j, (*chip, c), sibling) for j, chip in enumerate(chips)]

    def start():
        mine.start()
        for cp in first:
            cp.start()

    def finish():
        for j, chip in enumerate(chips):
            copy(1 + j, (*chip, c), me).wait_recv()
            passed[j].start()
        copy(0, sibling, me).wait_recv()
        for j, chip in enumerate(chips):
            copy(4 + j, (*chip, 1 - c), me).wait_recv()
        for cp in first + passed:
            cp.wait_send()
        mine.wait()

    return start, finish


def all_gather(name, shard):
    def body(x_ref, out_ref, send_sems, recv_sems, local_sem):
        start, finish = _gather_copies(x_ref, out_ref, send_sems, recv_sems, local_sem)
        start()
        finish()

    return pl.pallas_call(
        body, in_specs=[HBM_SPEC], out_specs=HBM_SPEC,
        out_shape=jax.ShapeDtypeStruct((N_DEV,) + shard.shape, shard.dtype),
        scratch_shapes=[pltpu.SemaphoreType.DMA((7,)), pltpu.SemaphoreType.DMA((7,)), pltpu.SemaphoreType.DMA],
        name=name)(shard)


def pair_exchange(name, g):
    def body(g_ref, out_ref, send_sem, recv_sem):
        x, y, c = _position()
        cp = pltpu.make_async_remote_copy(src_ref=g_ref.at[1 - c], dst_ref=out_ref, send_sem=send_sem,
                                          recv_sem=recv_sem, device_id=(x, y, 1 - c), device_id_type=MESH_ID)
        cp.start()
        cp.wait()

    return pl.pallas_call(
        body, in_specs=[HBM_SPEC], out_specs=HBM_SPEC, out_shape=jax.ShapeDtypeStruct(g.shape[1:], g.dtype),
        scratch_shapes=[pltpu.SemaphoreType.DMA, pltpu.SemaphoreType.DMA], name=name)(g)


def chip_exchange(name, p):
    def body(p_ref, out_ref, send_sems, recv_sems):
        x, y, c = _position()
        copies = []
        for j, (px, py) in enumerate([(1 - x, y), (x, 1 - y), (1 - x, 1 - y)]):
            cp = pltpu.make_async_remote_copy(
                src_ref=p_ref.at[2 * px + py], dst_ref=out_ref.at[j], send_sem=send_sems.at[j],
                recv_sem=recv_sems.at[j], device_id=(px, py, c), device_id_type=MESH_ID)
            cp.start()
            copies.append(cp)
        for cp in copies:
            cp.wait()

    return pl.pallas_call(
        body, in_specs=[HBM_SPEC], out_specs=HBM_SPEC, out_shape=jax.ShapeDtypeStruct((3,) + p.shape[1:], p.dtype),
        scratch_shapes=[pltpu.SemaphoreType.DMA((3,)), pltpu.SemaphoreType.DMA((3,))], name=name)(p)


def _sum_tile(r):
    return _divisor(r, 512, 8) if r % 8 == 0 else r


def pair_reduce(name, g, got, my_c, my_chip, wire):
    _, nchip, r, c_ = g.shape
    tile = _sum_tile(r)

    def body(ids, g_ref, got_ref, p_ref, mine_ref):
        s = g_ref[...].astype(F32) + got_ref[...].astype(F32)
        p_ref[...] = s.astype(wire)

        @pl.when(pl.program_id(1) == ids[1])
        def _():
            mine_ref[...] = s

    return pl.pallas_call(
        body,
        grid_spec=pltpu.PrefetchScalarGridSpec(
            num_scalar_prefetch=1, grid=(r // tile, nchip),
            in_specs=[pl.BlockSpec((None, None, tile, c_), lambda i, k, ids: (ids[0], k, i, 0)),
                      pl.BlockSpec((None, tile, c_), lambda i, k, ids: (k, i, 0))],
            out_specs=[pl.BlockSpec((None, tile, c_), lambda i, k, ids: (k, i, 0)),
                       pl.BlockSpec((tile, c_), lambda i, k, ids: (i, 0))]),
        out_shape=[jax.ShapeDtypeStruct((nchip, r, c_), wire), jax.ShapeDtypeStruct((r, c_), F32)],
        compiler_params=_params(("arbitrary", "arbitrary")), name=name)(
        jnp.stack([my_c, my_chip]).astype(jnp.int32), g, got)


def chip_reduce(name, mine, got):
    r, c_ = mine.shape
    tile = _sum_tile(r)

    def body(m_ref, got_ref, o_ref):
        o_ref[...] = ((m_ref[...] + got_ref[0].astype(F32)) + got_ref[1].astype(F32)) + got_ref[2].astype(F32)

    return pl.pallas_call(
        body, grid=(r // tile,),
        in_specs=[pl.BlockSpec((tile, c_), lambda i: (i, 0)), pl.BlockSpec((3, tile, c_), lambda i: (0, i, 0))],
        out_specs=pl.BlockSpec((tile, c_), lambda i: (i, 0)), out_shape=jax.ShapeDtypeStruct((r, c_), F32),
        compiler_params=_params(("arbitrary",)), name=name)(mine, got)


def sum_blocks(name, a):
    n, r, c_ = a.shape
    tile = _sum_tile(r)

    def body(a_ref, o_ref):
        acc = a_ref[0]
        for k in range(1, n):
            acc = acc + a_ref[k]
        o_ref[...] = acc

    return pl.pallas_call(
        body, grid=(r // tile,), in_specs=[pl.BlockSpec((n, tile, c_), lambda i: (0, i, 0))],
        out_specs=pl.BlockSpec((tile, c_), lambda i: (i, 0)), out_shape=jax.ShapeDtypeStruct((r, c_), F32),
        compiler_params=_params(("arbitrary",)), name=name)(a)


GROUPS = [
    ("rows1024", ["ssd_w_out", "conv_w_out", "mla_w_o", "w_out", "xattn_w_q", "xattn_w_o", "ffn_w_out"], BF16, 1024),
    ("w_in", ["w_in"], BF16, 1114),
    ("ffn_w_in", ["ffn_w_in"], BF16, 704),
    ("cols256", ["xattn_w_kv", "mla_w_kv_b"], BF16, 256),
    ("w_q_b", ["mla_w_q_b"], BF16, 192),
    ("small", ["ssd_conv_w", "conv_dw_w", "gate_b"], F32, 128),
]
HOSTED, UPFRONT = GROUPS[:3], GROUPS[3:]
PACK_COLS = 1024


def _pack(arrays):
    flat = jnp.concatenate([a.reshape(-1) for a in arrays])
    rows = -(-flat.shape[0] // PACK_COLS)
    rows += -rows % 8
    return jnp.pad(flat, (0, rows * PACK_COLS - flat.shape[0])).reshape(rows, PACK_COLS)


def _unpack(buf, shapes):
    flat = buf.reshape(-1)
    out, off = [], 0
    for s in shapes:
        n = 1
        for d in s:
            n *= d
        out.append(flat[off:off + n].reshape(tuple(s)))
        off += n
    return out


def _stack_rows(arrays, width, lead):
    return jnp.concatenate([a.reshape(a.shape[:lead] + (-1, width)) for a in arrays], axis=lead)


def _unstack_rows(buf, shapes, width, lead):
    out, off = [], 0
    for s in shapes:
        n = 1
        for d in s:
            n *= d
        rows = n // width
        idx = (slice(None),) * lead + (slice(off, off + rows),)
        out.append(buf[idx].reshape(buf.shape[:lead] + tuple(s)))
        off += rows
    return out


def _join_shards(blocks, axis):
    ax = axis + 1
    moved = jnp.moveaxis(blocks, 0, ax)
    s = moved.shape
    return moved.reshape(s[:ax] + (s[ax] * s[ax + 1],) + s[ax + 2:])


def _split_by_owner(full, axis):
    ax = axis + 1
    s = full.shape
    cut = full.reshape(s[:ax] + (2, 2, 2, s[ax] // N_DEV) + s[ax + 1:])
    cut = jnp.moveaxis(cut, (ax + 2, ax, ax + 1), (0, 1, 2))
    return cut.reshape((2, 4) + cut.shape[3:])


def _blocks_by_owner(blocks):
    cut = blocks.reshape((2, 2, 2) + blocks.shape[1:])
    return jnp.moveaxis(cut, 2, 0).reshape((2, 4) + blocks.shape[1:])


def _pad_cols(w, n):
    return jnp.pad(w, ((0, 0), (0, n - w.shape[1])))


def _regroup_cols(srcs, widths):
    starts = [0]
    for s in srcs:
        starts.append(starts[-1] + s.shape[1])
    assert starts[-1] == sum(widths)
    out, lo = [], 0
    for wd in widths:
        hi = lo + wd
        parts = []
        for s, a, b in zip(srcs, starts[:-1], starts[1:]):
            u, v = max(lo, a), min(hi, b)
            if u < v:
                parts.append(s[:, u - a:v - a])
        out.append(parts[0] if len(parts) == 1 else jnp.concatenate(parts, axis=1))
        lo = hi
    return out


COL_BLOCKED = ("w_in", "ffn_w_in", "xattn_w_kv", "mla_w_kv_b", "mla_w_q_b")
W_IN_PIECES = (1024, 2048, 16, 2048, 384, MLA_KV_RANK, MLA_ROPE, 3072)


def _prep_layer(w):
    w_z, w_xbc, w_dt, w_glu, w_q, w_ckv, w_kr, w_gate = _regroup_cols(list(w["w_in"]), W_IN_PIECES)
    w_ffn_gate, w_ffn_up = _regroup_cols(list(w["ffn_w_in"]), (FFN_HIDDEN, FFN_HIDDEN))
    w_xk, w_xv = _regroup_cols(list(w["xattn_w_kv"]), (D_MODEL, D_MODEL))
    q, kv = w["mla_w_q_b"], w["mla_w_kv_b"]

    def row(v):
        return v.reshape(1, -1)

    def norm_pair(g):
        return _pad_cols(row(g), 2 * LANES)

    return {
        "mix_norm_g": row(w["mix_norm_g"]),
        "w_z": w_z, "w_xbc": w_xbc, "w_dt": _pad_cols(w_dt, LANES), "w_glu": w_glu, "w_q": w_q, "w_ckv": w_ckv,
        "w_kr": _pad_cols(w_kr, LANES), "w_gate": w_gate,
        "ssd_conv_w": w["ssd_conv_w"], "ssd_conv_b": row(w["ssd_conv_b"]),
        "ssd_dt_bias": _pad_cols(row(w["ssd_dt_bias"]), LANES), "ssd_a_log": _pad_cols(row(w["ssd_a_log"]), LANES),
        "ssd_d": _pad_cols(row(w["ssd_d"]), LANES), "ssd_norm_g": row(w["ssd_norm_g"]), "ssd_w_out": w["ssd_w_out"],
        "conv_dw_w": w["conv_dw_w"], "conv_dw_b": row(w["conv_dw_b"]), "conv_ln_g": row(w["conv_ln_g"]),
        "conv_ln_b": row(w["conv_ln_b"]), "conv_w_out": w["conv_w_out"],
        "mla_q_a_g": row(w["mla_q_a_g"]), "mla_kv_a_g": row(w["mla_kv_a_g"]),
        "w_qn": jnp.concatenate([q[h, :, :MLA_NOPE] for h in range(MLA_HEADS)], axis=1),
        "w_qr": jnp.concatenate([_pad_cols(q[h, :, MLA_NOPE:], LANES) for h in range(MLA_HEADS)], axis=1),
        "w_kn": jnp.concatenate([kv[h, :, :MLA_NOPE] for h in range(MLA_HEADS)], axis=1),
        "w_v": jnp.concatenate([kv[h, :, MLA_NOPE:] for h in range(MLA_HEADS)], axis=1),
        "mla_q_norm_g": norm_pair(w["mla_q_norm_g"]), "mla_k_norm_g": norm_pair(w["mla_k_norm_g"]),
        "mla_w_o": w["mla_w_o"], "gate_b": row(w["gate_b"]), "w_out": w["w_out"],
        "xattn_norm_g": row(w["xattn_norm_g"]), "mem_norm_g": row(w["mem_norm_g"]), "xattn_w_q": w["xattn_w_q"],
        "w_xk": w_xk, "w_xv": w_xv,
        "xattn_q_norm_g": row(w["xattn_q_norm_g"]), "xattn_k_norm_g": row(w["xattn_k_norm_g"]),
        "xattn_w_o": w["xattn_w_o"], "ffn_norm_g": row(w["ffn_norm_g"]),
        "w_ffn_gate": w_ffn_gate, "w_ffn_up": w_ffn_up, "ffn_w_out": w["ffn_w_out"],
    }


def _unprep_grads(g):
    n_dt, n_kr = IN_SIZES[2], MLA_ROPE
    flat = lambda v: v.reshape(-1)

    def blocks(srcs):
        total = sum(s.shape[1] for s in srcs)
        return jnp.stack(_regroup_cols(srcs, (total // N_DEV,) * N_DEV))

    def head(a, h, n=LANES):
        return a[:, h * LANES:h * LANES + n]

    return {
        "mix_norm_g": flat(g["mix_norm_g"]),
        "w_in": blocks([g["w_z"], g["w_xbc"], g["w_dt"][:, :n_dt], g["w_glu"], g["w_q"], g["w_ckv"],
                        g["w_kr"][:, :n_kr], g["w_gate"]]),
        "ssd_conv_w": g["ssd_conv_w"], "ssd_conv_b": flat(g["ssd_conv_b"]),
        "ssd_dt_bias": flat(g["ssd_dt_bias"])[:SSD_HEADS], "ssd_a_log": flat(g["ssd_a_log"])[:SSD_HEADS],
        "ssd_d": flat(g["ssd_d"])[:SSD_HEADS], "ssd_norm_g": flat(g["ssd_norm_g"]), "ssd_w_out": g["ssd_w_out"],
        "conv_dw_w": g["conv_dw_w"], "conv_dw_b": flat(g["conv_dw_b"]), "conv_ln_g": flat(g["conv_ln_g"]),
        "conv_ln_b": flat(g["conv_ln_b"]), "conv_w_out": g["conv_w_out"],
        "mla_q_a_g": flat(g["mla_q_a_g"]),
        "mla_w_q_b": jnp.stack([jnp.concatenate([head(g["w_qn"], h), head(g["w_qr"], h, MLA_ROPE)], axis=1)
                                for h in range(MLA_HEADS)]),
        "mla_kv_a_g": flat(g["mla_kv_a_g"]),
        "mla_w_kv_b": jnp.stack([jnp.concatenate([head(g["w_kn"], h), head(g["w_v"], h)], axis=1)
                                 for h in range(MLA_HEADS)]),
        "mla_q_norm_g": flat(g["mla_q_norm_g"])[:MLA_NOPE + MLA_ROPE],
        "mla_k_norm_g": flat(g["mla_k_norm_g"])[:MLA_NOPE + MLA_ROPE],
        "mla_w_o": g["mla_w_o"], "gate_b": g["gate_b"].reshape(3, D_MODEL), "w_out": g["w_out"],
        "xattn_norm_g": flat(g["xattn_norm_g"]), "mem_norm_g": flat(g["mem_norm_g"]), "xattn_w_q": g["xattn_w_q"],
        "xattn_w_kv": blocks([g["w_xk"], g["w_xv"]]),
        "xattn_q_norm_g": flat(g["xattn_q_norm_g"]), "xattn_k_norm_g": flat(g["xattn_k_norm_g"]),
        "xattn_w_o": g["xattn_w_o"], "ffn_norm_g": flat(g["ffn_norm_g"]),
        "ffn_w_in": blocks([g["w_ffn_gate"], g["w_ffn_up"]]), "ffn_w_out": g["ffn_w_out"],
    }


def _layer(l, x, mem, cosf, sinf, w, hosted=()):
    t = x.shape[0]
    n = lambda s: f"l{l}_{s}"
    tile = min(ROW_TILE, t)
    grid = (1, t // tile)

    def rowwise(name, f, ins, width, to_matmul=False):
        return tmap(n(name), f, grid, ins, [_row_out(t, width, tile)], narrow=(0,) if to_matmul else ())[0]

    u, x = rms_norm_through(n("mix_norm"), x, w["mix_norm_g"])
    in_keys = ["z", "xbc", "dt", "glu", "q", "ckv", "kr", "gate"]
    z, xbc, dtr, glu, q_lat, c_kv, kr_raw, gate_logits = multi_matmul(n("in"), u, [w["w_" + k] for k in in_keys],
                                                                      in_keys)

    xc = dwconv(n("ssd_conv"), xbc, w["ssd_conv_w"], w["ssd_conv_b"])
    y_scan = ssd_scan(n("ssd_scan"), xc, dtr, w["ssd_dt_bias"], w["ssd_a_log"], w["ssd_d"])
    y_norm = rowwise("ssd_gate_norm", _ssd_gate_norm_f, [_rows(y_scan, tile), _rows(z, tile), _whole(w["ssd_norm_g"])],
                     SSD_INNER, to_matmul=True)
    y_ssd = matmul(n("ssd_out"), y_norm, w["ssd_w_out"])

    v = rowwise("glu", _glu_f, [_rows(glu, tile)], D_MODEL)
    v = dwconv(n("conv_dw"), v, w["conv_dw_w"], w["conv_dw_b"])
    v = rowwise("conv_ln_silu", _ln_silu_f, [_rows(v, tile), _whole(w["conv_ln_g"]), _whole(w["conv_ln_b"])], D_MODEL,
                to_matmul=True)
    y_conv = matmul(n("conv_out"), v, w["conv_w_out"])

    q_n = rms_norm(n("q_a_norm"), q_lat, w["mla_q_a_g"])
    qn_raw, qr_raw = multi_matmul(n("q"), q_n, [w["w_qn"], w["w_qr"]], ["nope", "rope"])
    c_n = rms_norm(n("kv_a_norm"), c_kv, w["mla_kv_a_g"])
    kn_raw, val = multi_matmul(n("kv"), c_n, [w["w_kn"], w["w_v"]], ["nope", "v"])
    tables = [_rows(cosf, tile, "n"), _rows(sinf, tile, "n")]
    kn = rowwise("k_nope_norm", _k_nope_f, [_rows(kn_raw, tile), _whole(w["mla_k_norm_g"])], MLA_HEADS * MLA_NOPE)
    kr = rowwise("k_rope", _k_rope_f, [_rows(kr_raw, tile)] + tables + [_whole(w["mla_k_norm_g"])], LANES)
    wide = _row_out(t, MLA_HEADS * LANES, tile)
    qn, qr = tmap(n("q_prep"), _q_prep_f, grid,
                  [_rows(qn_raw, tile), _rows(qr_raw, tile)] + tables + [_whole(w["mla_q_norm_g"])], [wide, wide])
    att, gathered = mla_attention(n("mla_attn"), qn, qr, kn, kr, val, hosted)
    y_mla = matmul(n("mla_out"), att, w["mla_w_o"])

    merged = rowwise("merge", _merge_f, [_rows(gate_logits, tile), _whole(w["gate_b"]), _rows(y_ssd, tile),
                                         _rows(y_conv, tile), _rows(y_mla, tile)], D_MODEL, to_matmul=True)
    x = matmul(n("mix_out"), merged, w["w_out"], res=x)

    h, x = rms_norm_through(n("xattn_norm"), x, w["xattn_norm_g"])
    mem_n = rms_norm_nograd_x(n("mem_norm"), mem, w["mem_norm_g"])
    xq = matmul(n("xattn_q"), h, w["xattn_w_q"])
    xk, xv = multi_matmul(n("xattn_kv"), mem_n, [w["w_xk"], w["w_xv"]], ["k", "v"])
    m = mem.shape[0]
    txq = min(XATT_Q_TILE, t)
    kv_head = lambda arr: (arr, (m, X_HEAD_DIM), lambda o, i: (0, o), "ai")
    xo = tmap(n("xattn"), _xattn_f, (X_HEADS, t // txq),
              [(xq, (txq, X_HEAD_DIM), lambda o, i: (i, o), "t"), kv_head(xk), kv_head(xv),
               _whole(w["xattn_q_norm_g"]), _whole(w["xattn_k_norm_g"])],
              [((t, D_MODEL), (txq, X_HEAD_DIM), lambda o, i: (i, o))], narrow=(0,))[0]
    x = matmul(n("xattn_out"), xo, w["xattn_w_o"], res=x)

    h, x = rms_norm_through(n("ffn_norm"), x, w["ffn_norm_g"])
    gate, up = multi_matmul(n("ffn_in"), h, [w["w_ffn_gate"], w["w_ffn_up"]], ["gate", "up"])
    act = rowwise("swiglu", _swiglu_f, [_rows(gate, tile), _rows(up, tile)], FFN_HIDDEN, to_matmul=True)
    return matmul(n("ffn_out"), act, w["ffn_w_out"], res=x), gathered


def _rope_tables(positions):
    inv = ROPE_THETA ** (-jnp.arange(0, MLA_ROPE, 2, dtype=F32) / MLA_ROPE)
    ang = positions.astype(F32)[:, None] * inv
    pad = jnp.zeros((positions.shape[0], LANES - MLA_ROPE), F32)
    cos, sin = jnp.cos(ang), jnp.sin(ang)
    return jnp.concatenate([cos, cos, pad], axis=1), jnp.concatenate([sin, sin, pad], axis=1)


def local_step(x, mem, positions, target, weights, gathered0, later, shard_shapes):
    cosf, sinf = _rope_tables(positions)

    def layer_weights(l, big):
        w = {k: v[:, l] if k in COL_BLOCKED else v[l] for k, v in weights.items()}
        for (_, names, _, width), stack in zip(HOSTED, big):
            for n, b in zip(names, _unstack_rows(stack, [shard_shapes[n] for n in names], width, 1)):
                w[n] = b if n in COL_BLOCKED else b.reshape((-1,) + b.shape[2:])
        return w

    diff = [{k: jnp.zeros(v.shape, F32) if k in MATRICES else v
             for k, v in _prep_layer(layer_weights(l, gathered0)).items()} for l in range(DEPTH)]

    def forward(x, diff):
        big = gathered0
        for l in range(DEPTH):
            mats = _prep_layer(layer_weights(l, big))
            w = {k: Mat(mats[k], s) if k in MATRICES else s for k, s in diff[l].items()}
            x, big = _layer(l, x, mem, cosf, sinf, w, later[l] if l + 1 < DEPTH else ())
        return x

    y, pull = jax.vjp(forward, x, diff)
    sq, dy = loss_head(y, target)
    gx, gp = pull(dy)
    per_layer = [_unprep_grads(g) for g in gp]
    grads = {k: jnp.stack([pl_[k] for pl_ in per_layer], axis=1 if k in COL_BLOCKED else 0) for k in WEIGHTS}
    return sq, gx, grads


def _step(x, mem, positions, loss_target, w, m, v):
    xi, yi, ci = _position()

    full = {n: w[n] for n in REPLICATED}
    for gname, names, wire, width in UPFRONT:
        shapes = [w[n].shape for n in names]
        stacked = _stack_rows([w[n] for n in names], width, 0).astype(wire)
        gathered = all_gather("gather_" + gname, stacked)
        for n, b in zip(names, _unstack_rows(gathered, shapes, width, 1)):
            full[n] = b if n in COL_BLOCKED else _join_shards(b, SHARDED[n])
    shards = [[_stack_rows([w[n][l] for n in names], width, 0).astype(wire) for _, names, wire, width in HOSTED]
              for l in range(DEPTH)]
    gathered0 = [all_gather("gather0_" + g[0], s) for g, s in zip(HOSTED, shards[0])]
    shard_shapes = {n: w[n].shape[1:] for g in HOSTED for n in g[1]}

    sq, gx, grads = local_step(x[0], mem[0], positions[0], loss_target[0], full, gathered0, shards[1:], shard_shapes)
    loss = lax.psum(0.5 * jnp.sum(sq) / D_MODEL, ("x", "y", "c"))

    g_shard = {}
    for gname, names, wire, width in GROUPS:
        shapes = [w[n].shape for n in names]
        by_owner = _stack_rows([_blocks_by_owner(grads[n]) if n in COL_BLOCKED else
                                _split_by_owner(grads[n], SHARDED[n]) for n in names], width, 2)
        by_owner = by_owner.astype(wire)
        from_sibling = pair_exchange("pair_exchange_" + gname, by_owner)
        chip_partial, mine = pair_reduce("pair_reduce_" + gname, by_owner, from_sibling, ci, 2 * xi + yi, wire)
        from_chips = chip_exchange("chip_exchange_" + gname, chip_partial)
        reduced = chip_reduce("chip_reduce_" + gname, mine, from_chips)
        g_shard.update(zip(names, _unstack_rows(reduced, shapes, width, 0)))

    rep_shapes = [w[n].shape for n in REPLICATED]
    rep_all = all_gather("small_grads_all_gather", _pack([grads[n] for n in REPLICATED]))
    g_rep = dict(zip(REPLICATED, _unpack(sum_blocks("small_grads_sum", rep_all), rep_shapes)))

    out_g, out_d, out_m, out_v = [], [], [], []
    for n in WEIGHTS:
        g = g_shard[n] if n in SHARDED else g_rep[n]
        d, nm, nv = adamw("adamw_" + n, w[n], g, m[n], v[n])
        out_g.append(g)
        out_d.append(d)
        out_m.append(nm)
        out_v.append(nv)
    return (loss, gx[None], *out_g, *out_d, *out_m, *out_v)


def kernel(x, mem, positions, mix_norm_g, w_in, ssd_conv_w, ssd_conv_b, ssd_dt_bias, ssd_a_log, ssd_d, ssd_norm_g, ssd_w_out, conv_dw_w, conv_dw_b, conv_ln_g, conv_ln_b, conv_w_out, mla_q_a_g, mla_w_q_b, mla_kv_a_g, mla_w_kv_b, mla_q_norm_g, mla_k_norm_g, mla_w_o, gate_b, w_out, xattn_norm_g, mem_norm_g, xattn_w_q, xattn_w_kv, xattn_q_norm_g, xattn_k_norm_g, xattn_w_o, ffn_norm_g, ffn_w_in, ffn_w_out, loss_target, m_mix_norm_g, m_w_in, m_ssd_conv_w, m_ssd_conv_b, m_ssd_dt_bias, m_ssd_a_log, m_ssd_d, m_ssd_norm_g, m_ssd_w_out, m_conv_dw_w, m_conv_dw_b, m_conv_ln_g, m_conv_ln_b, m_conv_w_out, m_mla_q_a_g, m_mla_w_q_b, m_mla_kv_a_g, m_mla_w_kv_b, m_mla_q_norm_g, m_mla_k_norm_g, m_mla_w_o, m_gate_b, m_w_out, m_xattn_norm_g, m_mem_norm_g, m_xattn_w_q, m_xattn_w_kv, m_xattn_q_norm_g, m_xattn_k_norm_g, m_xattn_w_o, m_ffn_norm_g, m_ffn_w_in, m_ffn_w_out, v_mix_norm_g, v_w_in, v_ssd_conv_w, v_ssd_conv_b, v_ssd_dt_bias, v_ssd_a_log, v_ssd_d, v_ssd_norm_g, v_ssd_w_out, v_conv_dw_w, v_conv_dw_b, v_conv_ln_g, v_conv_ln_b, v_conv_w_out, v_mla_q_a_g, v_mla_w_q_b, v_mla_kv_a_g, v_mla_w_kv_b, v_mla_q_norm_g, v_mla_k_norm_g, v_mla_w_o, v_gate_b, v_w_out, v_xattn_norm_g, v_mem_norm_g, v_xattn_w_q, v_xattn_w_kv, v_xattn_q_norm_g, v_xattn_k_norm_g, v_xattn_w_o, v_ffn_norm_g, v_ffn_w_in, v_ffn_w_out):
    args = locals()
    w = {n: args[n] for n in WEIGHTS}
    m = {n: args["m_" + n] for n in WEIGHTS}
    v = {n: args["v_" + n] for n in WEIGHTS}
    return _step(x, mem, positions, loss_target, w, m, v)
```

```python
from typing import NamedTuple

import jax
import jax.numpy as jnp
from jax import lax
from jax.experimental import pallas as pl
from jax.experimental.pallas import tpu as pltpu

F32 = jnp.float32
BF16 = jnp.bfloat16
HIGHEST = lax.Precision.HIGHEST
MESH_ID = pl.DeviceIdType.MESH

VMEM_LIMIT_BYTES = 56 * 1024 * 1024
LANES = 128

EPS = 1e-6
DEPTH = 4
D_MODEL = 1024
N_DEV = 8
SSD_HEADS = 16
SSD_HEAD_DIM = 64
SSD_STATE = 128
SSD_GROUPS = 4
SSD_INNER = 1024
SSD_TILE = 256
CONV_K = 31
SSD_CONV_K = 4
MLA_HEADS = 8
MLA_NOPE = 128
MLA_ROPE = 64
MLA_V = 128
MLA_Q_RANK = 384
MLA_KV_RANK = 256
ATT_CHUNK = 64
ROPE_THETA = 10000.0
X_HEADS = 4
X_HEAD_DIM = 256
FFN_HIDDEN = 2816
IN_SIZES = (1024, 2048, 16, 2048, 384, 320, 3072)

ADAM_LR = 0.001
ADAM_B1 = 0.9
ADAM_B2 = 0.999
ADAM_EPS = 1e-08
ADAM_WD = 0.01
ADAM_STEP = 10

MM_FULL_K = 3072
ROW_TILE = 256
ATT_BLOCK = 512
XATT_Q_TILE = 512

SHARDED = {
    "w_in": 1, "ssd_conv_w": 1, "ssd_w_out": 0, "conv_dw_w": 1, "conv_w_out": 0, "mla_w_q_b": 1, "mla_w_kv_b": 1,
    "mla_w_o": 0, "gate_b": 1, "w_out": 0, "xattn_w_q": 0, "xattn_w_kv": 1, "xattn_w_o": 0, "ffn_w_in": 1,
    "ffn_w_out": 0,
}
WEIGHTS = ["mix_norm_g", "w_in", "ssd_conv_w", "ssd_conv_b", "ssd_dt_bias", "ssd_a_log", "ssd_d", "ssd_norm_g",
           "ssd_w_out", "conv_dw_w", "conv_dw_b", "conv_ln_g", "conv_ln_b", "conv_w_out", "mla_q_a_g", "mla_w_q_b",
           "mla_kv_a_g", "mla_w_kv_b", "mla_q_norm_g", "mla_k_norm_g", "mla_w_o", "gate_b", "w_out", "xattn_norm_g",
           "mem_norm_g", "xattn_w_q", "xattn_w_kv", "xattn_q_norm_g", "xattn_k_norm_g", "xattn_w_o", "ffn_norm_g",
           "ffn_w_in", "ffn_w_out"]
REPLICATED = [n for n in WEIGHTS if n not in SHARDED]


def _params(sem=None):
    return pltpu.CompilerParams(dimension_semantics=sem, vmem_limit_bytes=VMEM_LIMIT_BYTES)


def _divisor(n, cap, mult):
    if n <= cap:
        return n
    for d in range(cap - cap % mult, 0, -mult):
        if n % d == 0:
            return d
    raise ValueError(f"no tile for {n}")


def _dg(a, b, ca, cb):
    return lax.dot_general(a.astype(BF16), b.astype(BF16), (((ca,), (cb,)), ((), ())), preferred_element_type=F32)


@jax.custom_vjp
def bdot(a, b):
    return _dg(a, b, 1, 0)


bdot.defvjp(lambda a, b: (_dg(a, b, 1, 0), (a, b)), lambda r, g: (_dg(g, r[1], 1, 1), _dg(r[0], g, 0, 0)))


@jax.custom_vjp
def bdot_nt(a, b):
    return _dg(a, b, 1, 1)


bdot_nt.defvjp(lambda a, b: (_dg(a, b, 1, 1), (a, b)), lambda r, g: (_dg(g, r[1], 1, 0), _dg(g, r[0], 0, 0)))


@jax.custom_vjp
def bdot_tn(a, b):
    return _dg(a, b, 0, 0)


bdot_tn.defvjp(lambda a, b: (_dg(a, b, 0, 0), (a, b)), lambda r, g: (_dg(r[1], g, 1, 1), _dg(r[0], g, 1, 0)))


def hdot(a, b):
    return jnp.dot(a, b, precision=HIGHEST, preferred_element_type=F32)


def _iota(shape, dim):
    return lax.broadcasted_iota(jnp.int32, shape, dim)


def _mm(name, a, b, ta=False, tb=False, res=None):
    m, k = (a.shape[1], a.shape[0]) if ta else a.shape
    n = b.shape[0] if tb else b.shape[1]
    tm = _divisor(m, 1536, LANES) if ta else _divisor(m, 1024, 8)
    tn = _divisor(n, 1536, LANES)
    tk = k if k <= MM_FULL_K else _divisor(k, 1024, LANES)
    if tk == k and k > 1024:
        tm = _divisor(m, 512, LANES if ta else 8)
    nk = k // tk
    dims = (((0 if ta else 1,), (1 if tb else 0,)), ((), ()))

    def body(*refs):
        if res is None:
            a_ref, b_ref, o_ref = refs
        else:
            a_ref, b_ref, r_ref, o_ref = refs
        part = lax.dot_general(a_ref[...].astype(BF16), b_ref[...].astype(BF16), dims, preferred_element_type=F32)
        if nk == 1:
            o_ref[...] = part if res is None else part + r_ref[...]
        else:
            kk = pl.program_id(2)

            @pl.when(kk == 0)
            def _():
                o_ref[...] = part if res is None else part + r_ref[...]

            @pl.when(kk != 0)
            def _():
                o_ref[...] += part

    a_spec = pl.BlockSpec((tk, tm), lambda i, j, kk: (kk, i)) if ta else pl.BlockSpec((tm, tk), lambda i, j, kk: (i, kk))
    b_spec = pl.BlockSpec((tn, tk), lambda i, j, kk: (j, kk)) if tb else pl.BlockSpec((tk, tn), lambda i, j, kk: (kk, j))
    o_spec = pl.BlockSpec((tm, tn), lambda i, j, kk: (i, j))
    in_specs = [a_spec, b_spec] + ([] if res is None else [o_spec])
    args = (a, b) + (() if res is None else (res,))
    return pl.pallas_call(
        body, grid=(m // tm, n // tn, nk), in_specs=in_specs, out_specs=o_spec,
        out_shape=jax.ShapeDtypeStruct((m, n), F32),
        compiler_params=_params(("parallel", "parallel", "arbitrary")), name=name)(*args)


class Mat(NamedTuple):
    value: jax.Array
    slot: jax.Array


MATRICES = frozenset([
    "w_z", "w_xbc", "w_dt", "w_glu", "w_q", "w_ckv", "w_kr", "w_gate", "ssd_w_out", "conv_w_out", "w_qn", "w_qr",
    "w_kn", "w_v", "mla_w_o", "w_out", "xattn_w_q", "w_xk", "w_xv", "xattn_w_o", "w_ffn_gate", "w_ffn_up",
    "ffn_w_out"])


class Act(NamedTuple):
    value: jax.Array
    slot: jax.Array


def _operand(a):
    return (a.value, a.slot) if isinstance(a, Act) else (a, a)


def matmul(name, a, mat, res=None):
    w, slot = mat
    a, a_slot = _operand(a)
    if res is None:
        @jax.custom_vjp
        def run(a, a_slot, w, slot):
            return _mm(name, a, w)

        def fwd(a, a_slot, w, slot):
            return run(a, a_slot, w, slot), (a, w)

        def bwd(r, g):
            return None, _mm(name + "_da", g, r[1], tb=True), None, _mm(name + "_dw", r[0], g, ta=True)

        run.defvjp(fwd, bwd)
        return run(a, a_slot, w, slot)

    @jax.custom_vjp
    def run_res(a, a_slot, w, slot, res):
        return _mm(name, a, w, res=res)

    def fwd_res(a, a_slot, w, slot, res):
        return run_res(a, a_slot, w, slot, res), (a, w)

    def bwd_res(r, g):
        return None, _mm(name + "_da", g, r[1], tb=True), None, _mm(name + "_dw", r[0], g, ta=True), g

    run_res.defvjp(fwd_res, bwd_res)
    return run_res(a, a_slot, w, slot, res)


def multi_matmul(name, a, mats, keys):
    ws, slots = tuple(m.value for m in mats), tuple(m.slot for m in mats)
    a, a_slot = _operand(a)

    @jax.custom_vjp
    def run(a, a_slot, ws, slots):
        return tuple(_mm(f"{name}_{k}", a, w) for k, w in zip(keys, ws))

    def fwd(a, a_slot, ws, slots):
        return run(a, a_slot, ws, slots), (a, ws)

    def bwd(r, gs):
        a, ws = r
        da = None
        for k, w, g in zip(keys, ws, gs):
            da = _mm(f"{name}_{k}_da", g, w, tb=True, res=da)
        dws = tuple(_mm(f"{name}_{k}_dw", a, g, ta=True) for k, g in zip(keys, gs))
        return None, da, tuple(None for _ in ws), dws

    run.defvjp(fwd, bwd)
    return run(a, a_slot, ws, slots)


def tmap(name, f, grid, ins, outs, through=None, narrow=()):
    arrays = [x[0] for x in ins]
    kinds = [x[3] for x in ins]
    in_specs = [pl.BlockSpec(x[1], x[2]) for x in ins]
    out_specs = [pl.BlockSpec(x[1], x[2]) for x in outs]
    out_shape = [jax.ShapeDtypeStruct(x[0], BF16 if k in narrow else F32) for k, x in enumerate(outs)]
    n_in, n_out = len(ins), len(outs)
    n_through = 0 if through is None else 1
    assert through is None or kinds[through] == "t"
    didx = [k for k, kd in enumerate(kinds) if kd != "n"]

    def fwd_call(*arrs):
        def body(*refs):
            pids = (pl.program_id(0), pl.program_id(1))
            vals = f(pids, *[r[...] for r in refs[:n_in]])
            for r, v in zip(refs[n_in:], vals):
                r[...] = v.astype(r.dtype)

        return pl.pallas_call(body, grid=grid, in_specs=in_specs, out_specs=out_specs, out_shape=out_shape,
                              compiler_params=_params(("arbitrary", "arbitrary")), name=name)(*arrs)

    def bwd_call(arrs, cts):
        def body(*refs):
            o, i = pl.program_id(0), pl.program_id(1)
            vals = [r[...] for r in refs[:n_in]]

            def g(*dv):
                full = list(vals)
                for k, v in zip(didx, dv):
                    full[k] = v
                return tuple(f((o, i), *full))

            _, vjp = jax.vjp(g, *[vals[k] for k in didx])
            grads = vjp(tuple(r[...] for r in refs[n_in:n_in + n_out]))
            for k, gr, r in zip(didx, grads, refs[n_in + n_out + n_through:]):
                if k == through:
                    r[...] = gr + refs[n_in + n_out][...]
                elif kinds[k] == "t":
                    r[...] = gr
                else:
                    first = (i == 0) if kinds[k] == "ai" else jnp.logical_and(o == 0, i == 0)

                    @pl.when(first)
                    def _(r=r, gr=gr):
                        r[...] = gr

                    @pl.when(jnp.logical_not(first))
                    def _(r=r, gr=gr):
                        r[...] += gr

        g_specs = [in_specs[k] for k in didx]
        g_shape = [jax.ShapeDtypeStruct(arrs[k].shape, F32) for k in didx]
        ct_specs = out_specs + ([in_specs[through]] if n_through else [])
        return pl.pallas_call(body, grid=grid, in_specs=in_specs + ct_specs, out_specs=g_specs, out_shape=g_shape,
                              compiler_params=_params(("arbitrary", "arbitrary")), name=name + "_bwd")(*arrs, *cts)

    @jax.custom_vjp
    def run(*arrs):
        res = [Act(o, jnp.zeros(o.shape, F32)) if k in narrow else o for k, o in enumerate(fwd_call(*arrs))]
        return tuple(res) + ((arrs[through],) if n_through else ())

    def run_fwd(*arrs):
        return run(*arrs), arrs

    def run_bwd(arrs, cts):
        cts = [c.slot if isinstance(c, Act) else c for c in cts]
        gs = bwd_call(arrs, cts)
        full = [None] * n_in
        for k, g in zip(didx, gs):
            full[k] = g
        return tuple(full)

    run.defvjp(run_fwd, run_bwd)
    return run(*arrays)


def _rows(arr, tile, kind="t"):
    return (arr, (tile, arr.shape[1]), lambda o, i: (i, 0), kind)


def _whole(arr, kind="ag"):
    return (arr, arr.shape, lambda o, i: (0, 0), kind)


def _row_out(t, n, tile):
    return ((t, n), (tile, n), lambda o, i: (i, 0))


def _rms(x, g, n=None):
    ms = jnp.sum(x * x, axis=-1, keepdims=True) / (x.shape[-1] if n is None else n)
    return x * lax.rsqrt(ms + EPS) * g


def rms_norm(name, x, g):
    t, n = x.shape
    tile = min(ROW_TILE, t)
    return tmap(name, lambda p, x, g: (_rms(x, g),), (1, t // tile), [_rows(x, tile), _whole(g)],
                [_row_out(t, n, tile)], narrow=(0,))[0]


def rms_norm_through(name, x, g):
    t, n = x.shape
    tile = min(ROW_TILE, t)
    return tmap(name, lambda p, x, g: (_rms(x, g),), (1, t // tile), [_rows(x, tile), _whole(g)],
                [_row_out(t, n, tile)], through=0, narrow=(0,))


def rms_norm_nograd_x(name, x, g):
    t, n = x.shape
    tile = min(ROW_TILE, t)
    return tmap(name, lambda p, x, g: (_rms(x, g),), (1, t // tile), [_rows(x, tile, "n"), _whole(g)],
                [_row_out(t, n, tile)], narrow=(0,))[0]


def _glu_f(p, glu):
    h = glu.shape[1] // 2
    return (glu[:, :h] * jax.nn.sigmoid(glu[:, h:]),)


def _ln_silu_f(p, v, g, b):
    mu = jnp.mean(v, axis=-1, keepdims=True)
    xc = v - mu
    var = jnp.mean(xc * xc, axis=-1, keepdims=True)
    return (jax.nn.silu(xc * lax.rsqrt(var + EPS) * g + b),)


def _ssd_gate_norm_f(p, y, z, g):
    v = y * jax.nn.silu(z)
    w = SSD_INNER // SSD_GROUPS
    parts = []
    for k in range(SSD_GROUPS):
        vg = v[:, k * w:(k + 1) * w]
        parts.append(vg * lax.rsqrt(jnp.mean(vg * vg, axis=-1, keepdims=True) + EPS))
    return (jnp.concatenate(parts, axis=1) * g,)


def _merge_f(p, gl, gb, y0, y1, y2):
    g = jax.nn.sigmoid(gl + gb)
    d = D_MODEL
    return (g[:, :d] * y0 + g[:, d:2 * d] * y1 + g[:, 2 * d:] * y2,)


def _swiglu_f(p, gate, up):
    return (jax.nn.silu(gate) * up,)


def _rot_matrix():
    r, c = _iota((LANES, LANES), 0), _iota((LANES, LANES), 1)
    h = MLA_ROPE // 2
    plus = jnp.logical_and(c >= h, jnp.logical_and(c < 2 * h, r == c - h))
    minus = jnp.logical_and(c < h, r == c + h)
    return plus.astype(F32) - minus.astype(F32)


def _rope(x, cosf, sinf):
    return x * cosf + hdot(x, _rot_matrix()) * sinf


def _per_head(f, x):
    return jnp.concatenate([f(x[:, h * LANES:(h + 1) * LANES]) for h in range(x.shape[1] // LANES)], axis=1)


def _k_nope_f(p, kn_raw, kg):
    return (_per_head(lambda x: _rms(x, kg[:, :MLA_NOPE]), kn_raw),)


def _k_rope_f(p, kr_raw, cosf, sinf, kg):
    return (_rope(_rms(kr_raw, kg[:, MLA_NOPE:], n=MLA_ROPE), cosf, sinf),)


def _q_prep_f(p, qn_raw, qr_raw, cosf, sinf, qg):
    qn = _per_head(lambda x: _rms(x, qg[:, :MLA_NOPE]), qn_raw)
    qr = _per_head(lambda x: _rope(_rms(x, qg[:, MLA_NOPE:], n=MLA_ROPE), cosf, sinf), qr_raw)
    return qn, qr


def _softmax(s):
    m = jnp.max(s, axis=-1, keepdims=True)
    e = jnp.exp(s - m)
    return e / jnp.sum(e, axis=-1, keepdims=True)


def _xattn_f(p, q, k, v, qg, kg):
    s = bdot_nt(_rms(q, qg), _rms(k, kg)) * (X_HEAD_DIM ** -0.5)
    return (bdot(_softmax(s), v),)


ATT_SCALE = (MLA_NOPE + MLA_ROPE) ** -0.5
NT_DIMS = (((1,), (1,)), ((), ()))
NN_DIMS = (((1,), (0,)), ((), ()))
TN_DIMS = (((0,), (0,)), ((), ()))


def _att_specs(t, blk):
    q_spec = pl.BlockSpec((blk, LANES), lambda h, i: (i, h))
    k_spec = pl.BlockSpec((t, LANES), lambda h, i: (0, h))
    shared = pl.BlockSpec((t, LANES), lambda h, i: (0, 0))
    lse_spec = pl.BlockSpec((None, blk, 1), lambda h, i: (h, i, 0))
    return q_spec, k_spec, shared, lse_spec


def _diagonal_mask(blk):
    return (_iota((blk, blk), 1) // ATT_CHUNK) <= (_iota((blk, blk), 0) // ATT_CHUNK)


def _att_keys(kn_ref, kr_ref, j, blk):
    ks = pl.ds(pl.multiple_of(j * blk, blk), blk)
    return ks, jnp.concatenate([kn_ref[ks, :], kr_ref[ks, :]], axis=1).astype(BF16)


def _att_fwd(name, qn, qr, kn, kr, v, hosted=()):
    t, width = qn.shape
    heads = width // LANES
    blk = min(ATT_BLOCK, t)
    nh = len(hosted)

    def body(qn_ref, qr_ref, kn_ref, kr_ref, v_ref, *rest):
        shard_refs, (o_ref, lse_ref), rest = rest[:nh], rest[nh:nh + 2], rest[nh + 2:]
        full_refs, sems = rest[:nh], rest[nh:]
        i = pl.program_id(1)
        gathers = [_gather_copies(shard_refs[k], full_refs[k], *sems[3 * k:3 * k + 3]) for k in range(nh)]
        if nh:
            @pl.when(jnp.logical_and(pl.program_id(0) == 0, i == 0))
            def _():
                for start, _ in gathers:
                    start()
        q = jnp.concatenate([qn_ref[...], qr_ref[...]], axis=1).astype(BF16)

        def scores(j):
            _, k = _att_keys(kn_ref, kr_ref, j, blk)
            return lax.dot_general(q, k, NT_DIMS, preferred_element_type=F32)

        def weighted_values(p, j):
            ks = pl.ds(pl.multiple_of(j * blk, blk), blk)
            return lax.dot_general(p, v_ref[ks, :].astype(BF16), NN_DIMS, preferred_element_type=F32)

        def softmax_step(s, m, l):
            m_new = jnp.maximum(m, jnp.max(s, axis=1, keepdims=True))
            alpha = jnp.exp(m - m_new)
            p = jnp.exp(s - m_new)
            return m_new, alpha, alpha * l + jnp.sum(p, axis=1, keepdims=True), p.astype(BF16)

        def step(j, carry):
            s, p_prev, m, l, acc = carry
            s_next = scores(j + 1)
            pv_prev = weighted_values(p_prev, jnp.maximum(j - 1, 0))
            m, alpha, l, p = softmax_step(s * ATT_SCALE, m, l)
            return s_next, p, m, l, alpha * (acc + pv_prev)

        init = (scores(0), jnp.zeros((blk, blk), BF16), jnp.full((blk, 1), -1e30, F32), jnp.zeros((blk, 1), F32),
                jnp.zeros((blk, LANES), F32))
        s, p_prev, m, l, acc = lax.fori_loop(0, i, step, init)
        pv_prev = weighted_values(p_prev, jnp.maximum(i - 1, 0))
        s = jnp.where(_diagonal_mask(blk), s * ATT_SCALE, -1e30)
        m, alpha, l, p = softmax_step(s, m, l)
        acc = alpha * (acc + pv_prev) + weighted_values(p, i)
        o_ref[...] = acc / l
        lse_ref[...] = m + jnp.log(l)
        if nh:
            @pl.when(jnp.logical_and(pl.program_id(0) == heads - 1, i == t // blk - 1))
            def _():
                for _, finish in gathers:
                    finish()

    q_spec, k_spec, shared, lse_spec = _att_specs(t, blk)
    sem_shapes = [pltpu.SemaphoreType.DMA((7,)), pltpu.SemaphoreType.DMA((7,)), pltpu.SemaphoreType.DMA] * nh
    return pl.pallas_call(
        body, grid=(heads, t // blk), in_specs=[q_spec, q_spec, k_spec, shared, k_spec] + [HBM_SPEC] * nh,
        out_specs=[q_spec, lse_spec] + [HBM_SPEC] * nh,
        out_shape=[jax.ShapeDtypeStruct((t, width), F32), jax.ShapeDtypeStruct((heads, t, 1), F32)] +
                  [jax.ShapeDtypeStruct((N_DEV,) + s.shape, s.dtype) for s in hosted],
        scratch_shapes=sem_shapes, compiler_params=_params(("arbitrary", "arbitrary")), name=name)(
        qn, qr, kn, kr, v, *hosted)


def _att_bwd(name, qn, qr, kn, kr, v, o, lse, do, hosted=()):
    t, width = qn.shape
    heads = width // LANES
    blk = min(ATT_BLOCK, t)
    nh = len(hosted)

    def body(qn_ref, qr_ref, kn_ref, kr_ref, v_ref, o_ref, lse_ref, do_ref, *rest):
        sent_refs, (dqn_ref, dqr_ref, dkn_ref, dkr_ref, dv_ref), rest = rest[:nh], rest[nh:nh + 5], rest[nh + 5:]
        got_refs, sems = rest[:nh], rest[nh:]
        h, i = pl.program_id(0), pl.program_id(1)
        exchanges = [_chip_copies(sent_refs[k], got_refs[k], *sems[2 * k:2 * k + 2]) for k in range(nh)]
        if nh:
            @pl.when(jnp.logical_and(h == 0, i == 0))
            def _():
                for start, _ in exchanges:
                    start()

        @pl.when(i == 0)
        def _():
            dkn_ref[...] = jnp.zeros_like(dkn_ref)
            dv_ref[...] = jnp.zeros_like(dv_ref)

        @pl.when(jnp.logical_and(h == 0, i == 0))
        def _():
            dkr_ref[...] = jnp.zeros_like(dkr_ref)

        q = jnp.concatenate([qn_ref[...], qr_ref[...]], axis=1).astype(BF16)
        do = do_ref[...]
        do16 = do.astype(BF16)
        delta = jnp.sum(do * o_ref[...], axis=1, keepdims=True)
        lse = lse_ref[...]

        def issue(j):
            ks, k = _att_keys(kn_ref, kr_ref, j, blk)
            s = lax.dot_general(q, k, NT_DIMS, preferred_element_type=F32)
            dp = lax.dot_general(do16, v_ref[ks, :].astype(BF16), NT_DIMS, preferred_element_type=F32)
            return s, dp

        def retire(p, ds, j, dq):
            ks, k = _att_keys(kn_ref, kr_ref, j, blk)
            dv_ref[ks, :] += lax.dot_general(p, do16, TN_DIMS, preferred_element_type=F32)
            dk = lax.dot_general(ds, q, TN_DIMS, preferred_element_type=F32)
            dkn_ref[ks, :] += dk[:, :LANES]
            dkr_ref[ks, :] += dk[:, LANES:]
            return dq + lax.dot_general(ds, k, NN_DIMS, preferred_element_type=F32)

        def probs(s, dp, masked):
            s = s * ATT_SCALE
            if masked:
                s = jnp.where(_diagonal_mask(blk), s, -1e30)
            p = jnp.exp(s - lse)
            return p.astype(BF16), (p * (dp - delta) * ATT_SCALE).astype(BF16)

        def step(j, carry):
            s, dp, p_prev, ds_prev, dq = carry
            s_next, dp_next = issue(j + 1)
            dq = retire(p_prev, ds_prev, jnp.maximum(j - 1, 0), dq)
            p, ds = probs(s, dp, False)
            return s_next, dp_next, p, ds, dq

        none = jnp.zeros((blk, blk), BF16)
        s, dp, p_prev, ds_prev, dq = lax.fori_loop(0, i, step,
                                                   issue(0) + (none, none, jnp.zeros((blk, 2 * LANES), F32)))
        dq = retire(p_prev, ds_prev, jnp.maximum(i - 1, 0), dq)
        p, ds = probs(s, dp, True)
        dq = retire(p, ds, i, dq)
        dqn_ref[...] = dq[:, :LANES]
        dqr_ref[...] = dq[:, LANES:]
        if nh:
            @pl.when(jnp.logical_and(h == heads - 1, i == t // blk - 1))
            def _():
                for _, finish in exchanges:
                    finish()

    q_spec, k_spec, shared, lse_spec = _att_specs(t, blk)
    big, one = jax.ShapeDtypeStruct((t, width), F32), jax.ShapeDtypeStruct((t, LANES), F32)
    sem_shapes = [pltpu.SemaphoreType.DMA((3,)), pltpu.SemaphoreType.DMA((3,))] * nh
    return pl.pallas_call(
        body, grid=(heads, t // blk),
        in_specs=[q_spec, q_spec, k_spec, shared, k_spec, q_spec, lse_spec, q_spec] + [HBM_SPEC] * nh,
        out_specs=[q_spec, q_spec, k_spec, shared, k_spec] + [HBM_SPEC] * nh,
        out_shape=[big, big, big, one, big] + [jax.ShapeDtypeStruct((3,) + p.shape[1:], p.dtype) for p in hosted],
        scratch_shapes=sem_shapes, compiler_params=_params(("arbitrary", "arbitrary")), name=name)(
        qn, qr, kn, kr, v, o, lse, do, *hosted)


def mla_attention(name, qn, qr, kn, kr, v, hosted=(), carriers=()):
    @jax.custom_vjp
    def run(qn, qr, kn, kr, v, hosted, carriers):
        o, _, *gathered = _att_fwd(name, qn, qr, kn, kr, v, hosted)
        return o, tuple(gathered), tuple(jnp.zeros((4,) + c.shape[1:], c.dtype) for c in carriers)

    def fwd(qn, qr, kn, kr, v, hosted, carriers):
        o, lse, *gathered = _att_fwd(name, qn, qr, kn, kr, v, hosted)
        handles = tuple(jnp.zeros((4,) + c.shape[1:], c.dtype) for c in carriers)
        return (o, tuple(gathered), handles), (qn, qr, kn, kr, v, o, lse)

    def bwd(r, g):
        do, _, sent = g
        dqn, dqr, dkn, dkr, dv, *got = _att_bwd(name + "_bwd", *r, do, tuple(sent))
        return dqn, dqr, dkn, dkr, dv, (None,) * len(hosted), tuple(got)

    run.defvjp(fwd, bwd)
    return run(qn, qr, kn, kr, v, tuple(hosted), tuple(carriers))


def _shift_down(v, s, rows):
    return v if s == 0 else jnp.where(rows >= s, pltpu.roll(v, s, 0), 0.0)


def _shift_up(v, s, rows):
    t = v.shape[0]
    return v if s == 0 else jnp.where(rows < t - s, pltpu.roll(v, t - s, 0), 0.0)


def _dwconv_fwd(name, x, w, b):
    t, c = x.shape
    kw = w.shape[0]

    def body(x_ref, w_ref, b_ref, y_ref):
        x = x_ref[...]
        rows = _iota(x.shape, 0)
        acc = jnp.zeros_like(x) + b_ref[...]
        for k in range(kw):
            acc = acc + w_ref[k:k + 1, :] * _shift_down(x, kw - 1 - k, rows)
        y_ref[...] = acc

    col = lambda i: (0, i)
    return pl.pallas_call(
        body, grid=(c // LANES,),
        in_specs=[pl.BlockSpec((t, LANES), col), pl.BlockSpec((kw, LANES), col), pl.BlockSpec((1, LANES), col)],
        out_specs=pl.BlockSpec((t, LANES), col), out_shape=jax.ShapeDtypeStruct((t, c), F32),
        compiler_params=_params(("arbitrary",)), name=name)(x, w, b)


def _dwconv_bwd(name, x, w, dy):
    t, c = x.shape
    kw = w.shape[0]

    def body(x_ref, w_ref, dy_ref, dx_ref, dw_ref, db_ref):
        x, dy = x_ref[...], dy_ref[...]
        rows = _iota(x.shape, 0)
        dx = jnp.zeros_like(x)
        for k in range(kw):
            s = kw - 1 - k
            dx = dx + w_ref[k:k + 1, :] * _shift_up(dy, s, rows)
            dw_ref[k:k + 1, :] = jnp.sum(dy * _shift_down(x, s, rows), axis=0, keepdims=True)
        dx_ref[...] = dx
        db_ref[...] = jnp.sum(dy, axis=0, keepdims=True)

    col = lambda i: (0, i)
    big, wsp, bsp = pl.BlockSpec((t, LANES), col), pl.BlockSpec((kw, LANES), col), pl.BlockSpec((1, LANES), col)
    return pl.pallas_call(
        body, grid=(c // LANES,), in_specs=[big, wsp, big], out_specs=[big, wsp, bsp],
        out_shape=[jax.ShapeDtypeStruct((t, c), F32), jax.ShapeDtypeStruct((kw, c), F32),
                   jax.ShapeDtypeStruct((1, c), F32)],
        compiler_params=_params(("arbitrary",)), name=name)(x, w, dy)


def dwconv(name, x, w, b):
    @jax.custom_vjp
    def run(x, w, b):
        return _dwconv_fwd(name, x, w, b)

    def fwd(x, w, b):
        return run(x, w, b), (x, w)

    def bwd(r, g):
        return tuple(_dwconv_bwd(name + "_bwd", r[0], r[1], g))

    run.defvjp(fwd, bwd)
    return run(x, w, b)


def _ssd_tile(xc, dtr, dtb, alog, dsk, prev):
    ln = xc.shape[0]
    gw = SSD_INNER // SSD_GROUPS
    ns = SSD_STATE
    xs = jax.nn.silu(xc[:, :SSD_INNER])
    bm = jax.nn.silu(xc[:, SSD_INNER:SSD_INNER + SSD_GROUPS * ns])
    cm = jax.nn.silu(xc[:, SSD_INNER + SSD_GROUPS * ns:])
    dt = jax.nn.softplus(dtr + dtb)
    a = dt * (-jnp.exp(alog))
    expand = (_iota((LANES, SSD_INNER), 0) == _iota((LANES, SSD_INNER), 1) // SSD_HEAD_DIM).astype(F32)
    causal = _iota((ln, ln), 0) >= _iota((ln, ln), 1)
    acs_h = hdot(causal.astype(F32), a)
    acs_c = hdot(acs_h, expand)
    dt_c = hdot(dt, expand)

    def row_per_column(v):
        return jnp.mean(hdot(jnp.broadcast_to(v, (8, LANES)), expand), axis=0, keepdims=True)

    aend_c = row_per_column(jnp.sum(a, axis=0, keepdims=True))
    xdt = xs * dt_c
    to_end = xdt * jnp.exp(aend_c - acs_c)
    from_start = jnp.exp(acs_c)
    acs_ht = acs_h.T
    lane_h, sub_h = _iota((1, LANES), 1), _iota((LANES, 1), 0)
    head_of_col = _iota((1, gw), 1) // SSD_HEAD_DIM
    ys, states = [], []
    for g in range(SSD_GROUPS):
        cg = cm[:, g * ns:(g + 1) * ns]
        bg = bm[:, g * ns:(g + 1) * ns]
        cols = slice(g * gw, (g + 1) * gw)
        y = bdot(cg, prev[:, cols]) * from_start[:, cols]
        states.append(bdot_tn(bg, to_end[:, cols]))
        cb = bdot_nt(cg, bg)
        for r in range(gw // SSD_HEAD_DIM):
            h = g * (gw // SSD_HEAD_DIM) + r
            col = jnp.sum(jnp.where(lane_h == h, acs_h, 0.0), axis=1, keepdims=True)
            row = jnp.sum(jnp.where(sub_h == h, acs_ht, 0.0), axis=0, keepdims=True)
            decay = jnp.exp(jnp.where(causal, col - row, -1e30))
            y = y + jnp.where(head_of_col == r, bdot(cb * decay, xdt[:, cols]), 0.0)
        ys.append(y)
    y = jnp.concatenate(ys, axis=1) + row_per_column(dsk) * xs
    new = prev * jnp.exp(aend_c) + jnp.concatenate(states, axis=1)
    return y, new


def _ssd_fwd(name, xc, dtr, dtb, alog, dsk):
    t = xc.shape[0]
    ln = min(SSD_TILE, t)
    nt = t // ln

    def body(xc_ref, dtr_ref, dtb_ref, alog_ref, dsk_ref, y_ref, prev_ref, carry):
        @pl.when(pl.program_id(0) == 0)
        def _():
            carry[...] = jnp.zeros_like(carry)

        prev = carry[...]
        prev_ref[...] = prev
        y, new = _ssd_tile(xc_ref[...], dtr_ref[...], dtb_ref[...], alog_ref[...], dsk_ref[...], prev)
        y_ref[...] = y
        carry[...] = new

    row = lambda i: (i, 0)
    par = pl.BlockSpec((1, LANES), lambda i: (0, 0))
    return pl.pallas_call(
        body, grid=(nt,),
        in_specs=[pl.BlockSpec((ln, xc.shape[1]), row), pl.BlockSpec((ln, LANES), row), par, par, par],
        out_specs=[pl.BlockSpec((ln, SSD_INNER), row), pl.BlockSpec((None, SSD_STATE, SSD_INNER), lambda i: (i, 0, 0))],
        out_shape=[jax.ShapeDtypeStruct((t, SSD_INNER), F32), jax.ShapeDtypeStruct((nt, SSD_STATE, SSD_INNER), F32)],
        scratch_shapes=[pltpu.VMEM((SSD_STATE, SSD_INNER), F32)],
        compiler_params=_params(("arbitrary",)), name=name)(xc, dtr, dtb, alog, dsk)


def _ssd_bwd(name, xc, dtr, dtb, alog, dsk, prevs, dy):
    t = xc.shape[0]
    ln = min(SSD_TILE, t)
    nt = t // ln

    def body(xc_ref, dtr_ref, dtb_ref, alog_ref, dsk_ref, prev_ref, dy_ref, dxc_ref, ddtr_ref, ddtb_ref, dalog_ref,
             ddsk_ref, dcarry):
        i = pl.program_id(0)

        @pl.when(i == 0)
        def _():
            dcarry[...] = jnp.zeros_like(dcarry)

        _, vjp = jax.vjp(_ssd_tile, xc_ref[...], dtr_ref[...], dtb_ref[...], alog_ref[...], dsk_ref[...], prev_ref[...])
        dxc, ddtr, ddtb, dalog, ddsk, dprev = vjp((dy_ref[...], dcarry[...]))
        dxc_ref[...] = dxc
        ddtr_ref[...] = ddtr
        dcarry[...] = dprev
        for r, gr in ((ddtb_ref, ddtb), (dalog_ref, dalog), (ddsk_ref, ddsk)):
            @pl.when(i == 0)
            def _(r=r, gr=gr):
                r[...] = gr

            @pl.when(i != 0)
            def _(r=r, gr=gr):
                r[...] += gr

    row = lambda i: (nt - 1 - i, 0)
    par = pl.BlockSpec((1, LANES), lambda i: (0, 0))
    big, dts = pl.BlockSpec((ln, xc.shape[1]), row), pl.BlockSpec((ln, LANES), row)
    par_shape = jax.ShapeDtypeStruct((1, LANES), F32)
    return pl.pallas_call(
        body, grid=(nt,),
        in_specs=[big, dts, par, par, par, pl.BlockSpec((None, SSD_STATE, SSD_INNER), lambda i: (nt - 1 - i, 0, 0)),
                  pl.BlockSpec((ln, SSD_INNER), row)],
        out_specs=[big, dts, par, par, par],
        out_shape=[jax.ShapeDtypeStruct(xc.shape, F32), jax.ShapeDtypeStruct(dtr.shape, F32), par_shape, par_shape,
                   par_shape],
        scratch_shapes=[pltpu.VMEM((SSD_STATE, SSD_INNER), F32)],
        compiler_params=_params(("arbitrary",)), name=name)(xc, dtr, dtb, alog, dsk, prevs, dy)


def ssd_scan(name, xc, dtr, dtb, alog, dsk):
    @jax.custom_vjp
    def run(xc, dtr, dtb, alog, dsk):
        return _ssd_fwd(name, xc, dtr, dtb, alog, dsk)[0]

    def fwd(xc, dtr, dtb, alog, dsk):
        y, prevs = _ssd_fwd(name, xc, dtr, dtb, alog, dsk)
        return y, (xc, dtr, dtb, alog, dsk, prevs)

    def bwd(r, g):
        return tuple(_ssd_bwd(name + "_bwd", *r, g))

    run.defvjp(fwd, bwd)
    return run(xc, dtr, dtb, alog, dsk)


def loss_head(y, target):
    t, n = y.shape
    tile = min(ROW_TILE, t)

    def body(y_ref, t_ref, dy_ref, acc_ref):
        d = y_ref[...] - t_ref[...]
        dy_ref[...] = d * (1.0 / n)

        @pl.when(pl.program_id(0) == 0)
        def _():
            acc_ref[...] = jnp.zeros_like(acc_ref)

        acc_ref[...] += jnp.sum(d * d, axis=0, keepdims=True)

    row = pl.BlockSpec((tile, n), lambda i: (i, 0))
    dy, acc = pl.pallas_call(
        body, grid=(t // tile,), in_specs=[row, row], out_specs=[row, pl.BlockSpec((1, n), lambda i: (0, 0))],
        out_shape=[jax.ShapeDtypeStruct((t, n), F32), jax.ShapeDtypeStruct((1, n), F32)],
        compiler_params=_params(("arbitrary",)), name="loss_head")(y, target)
    return acc, dy


def adamw(name, w, g, m, v):
    shape = w.shape
    cols = shape[-1]
    rows = w.size // cols
    tile = _divisor(rows, 512, 8) if rows % 8 == 0 else rows

    def body(w_ref, g_ref, m_ref, v_ref, d_ref, nm_ref, nv_ref):
        g = g_ref[...]
        m = ADAM_B1 * m_ref[...] + (1.0 - ADAM_B1) * g
        v = ADAM_B2 * v_ref[...] + (1.0 - ADAM_B2) * jnp.square(g)
        m_hat = m / (1.0 - ADAM_B1 ** ADAM_STEP)
        v_hat = v / (1.0 - ADAM_B2 ** ADAM_STEP)
        d_ref[...] = -ADAM_LR * (m_hat / (jnp.sqrt(v_hat) + ADAM_EPS) + ADAM_WD * w_ref[...])
        nm_ref[...] = m
        nv_ref[...] = v

    spec = pl.BlockSpec((tile, cols), lambda i: (i, 0))
    two_d = jax.ShapeDtypeStruct((rows, cols), F32)
    outs = pl.pallas_call(body, grid=(rows // tile,), in_specs=[spec] * 4, out_specs=[spec] * 3, out_shape=[two_d] * 3,
                          compiler_params=_params(("arbitrary",)), name=name)(
        *[a.reshape(rows, cols) for a in (w, g, m, v)])
    return [o.reshape(shape) for o in outs]


HBM_SPEC = pl.BlockSpec(memory_space=pl.ANY)


def _position():
    return lax.axis_index("x"), lax.axis_index("y"), lax.axis_index("c")


def _gather_copies(x_ref, out_ref, send_sems, recv_sems, local_sem):
    x, y, c = _position()
    me, sibling = (x, y, c), (x, y, 1 - c)
    chips = [(1 - x, y), (x, 1 - y), (1 - x, 1 - y)]

    def block(px, py, pc):
        return out_ref.at[4 * px + 2 * py + pc]

    def copy(k, blk, to, src=None):
        return pltpu.make_async_remote_copy(
            src_ref=block(*blk) if src is None else src, dst_ref=block(*blk), send_sem=send_sems.at[k],
            recv_sem=recv_sems.at[k], device_id=to, device_id_type=MESH_ID)

    mine = pltpu.make_async_copy(x_ref, block(*me), local_sem)
    first = [copy(0, me, sibling, src=x_ref)]
    first += [copy(1 + j, me, (*chip, c), src=x_ref) for j, chip in enumerate(chips)]
    passed = [copy(4 + j, (*chip, c), sibling) for j, chip in enumerate(chips)]

    def start():
        mine.start()
        for cp in first:
            cp.start()

    def finish():
        for j, chip in enumerate(chips):
            copy(1 + j, (*chip, c), me).wait_recv()
            passed[j].start()
        copy(0, sibling, me).wait_recv()
        for j, chip in enumerate(chips):
            copy(4 + j, (*chip, 1 - c), me).wait_recv()
        for cp in first + passed:
            cp.wait_send()
        mine.wait()

    return start, finish


def all_gather(name, shard):
    def body(x_ref, out_ref, send_sems, recv_sems, local_sem):
        start, finish = _gather_copies(x_ref, out_ref, send_sems, recv_sems, local_sem)
        start()
        finish()

    return pl.pallas_call(
        body, in_specs=[HBM_SPEC], out_specs=HBM_SPEC,
        out_shape=jax.ShapeDtypeStruct((N_DEV,) + shard.shape, shard.dtype),
        scratch_shapes=[pltpu.SemaphoreType.DMA((7,)), pltpu.SemaphoreType.DMA((7,)), pltpu.SemaphoreType.DMA],
        name=name)(shard)


def pair_exchange(name, g):
    def body(g_ref, out_ref, send_sem, recv_sem):
        x, y, c = _position()
        cp = pltpu.make_async_remote_copy(src_ref=g_ref.at[1 - c], dst_ref=out_ref, send_sem=send_sem,
                                          recv_sem=recv_sem, device_id=(x, y, 1 - c), device_id_type=MESH_ID)
        cp.start()
        cp.wait()

    return pl.pallas_call(
        body, in_specs=[HBM_SPEC], out_specs=HBM_SPEC, out_shape=jax.ShapeDtypeStruct(g.shape[1:], g.dtype),
        scratch_shapes=[pltpu.SemaphoreType.DMA, pltpu.SemaphoreType.DMA], name=name)(g)


def _chip_copies(p_ref, out_ref, send_sems, recv_sems):
    x, y, c = _position()
    copies = [pltpu.make_async_remote_copy(
        src_ref=p_ref.at[2 * px + py], dst_ref=out_ref.at[j], send_sem=send_sems.at[j], recv_sem=recv_sems.at[j],
        device_id=(px, py, c), device_id_type=MESH_ID) for j, (px, py) in enumerate([(1 - x, y), (x, 1 - y), (1 - x, 1 - y)])]

    def start():
        for cp in copies:
            cp.start()

    def finish():
        for cp in copies:
            cp.wait()

    return start, finish


def chip_exchange(name, p):
    def body(p_ref, out_ref, send_sems, recv_sems):
        start, finish = _chip_copies(p_ref, out_ref, send_sems, recv_sems)
        start()
        finish()

    return pl.pallas_call(
        body, in_specs=[HBM_SPEC], out_specs=HBM_SPEC, out_shape=jax.ShapeDtypeStruct((3,) + p.shape[1:], p.dtype),
        scratch_shapes=[pltpu.SemaphoreType.DMA((3,)), pltpu.SemaphoreType.DMA((3,))], name=name)(p)


def _sum_tile(r):
    return _divisor(r, 512, 16) if r % 16 == 0 else r


def pair_reduce(name, g, got, my_c, my_chip, wire):
    _, nchip, r, c_ = g.shape
    tile = _sum_tile(r)

    def body(ids, g_ref, got_ref, p_ref, mine_ref):
        s = g_ref[...].astype(F32) + got_ref[...].astype(F32)
        p_ref[...] = s.astype(wire)

        @pl.when(pl.program_id(1) == ids[1])
        def _():
            mine_ref[...] = s

    return pl.pallas_call(
        body,
        grid_spec=pltpu.PrefetchScalarGridSpec(
            num_scalar_prefetch=1, grid=(r // tile, nchip),
            in_specs=[pl.BlockSpec((None, None, tile, c_), lambda i, k, ids: (ids[0], k, i, 0)),
                      pl.BlockSpec((None, tile, c_), lambda i, k, ids: (k, i, 0))],
            out_specs=[pl.BlockSpec((None, tile, c_), lambda i, k, ids: (k, i, 0)),
                       pl.BlockSpec((tile, c_), lambda i, k, ids: (i, 0))]),
        out_shape=[jax.ShapeDtypeStruct((nchip, r, c_), wire), jax.ShapeDtypeStruct((r, c_), F32)],
        compiler_params=_params(("arbitrary", "arbitrary")), name=name)(
        jnp.stack([my_c, my_chip]).astype(jnp.int32), g, got)


def chip_reduce(name, mine, got):
    r, c_ = mine.shape
    tile = _sum_tile(r)

    def body(m_ref, got_ref, o_ref):
        o_ref[...] = ((m_ref[...] + got_ref[0].astype(F32)) + got_ref[1].astype(F32)) + got_ref[2].astype(F32)

    return pl.pallas_call(
        body, grid=(r // tile,),
        in_specs=[pl.BlockSpec((tile, c_), lambda i: (i, 0)), pl.BlockSpec((3, tile, c_), lambda i: (0, i, 0))],
        out_specs=pl.BlockSpec((tile, c_), lambda i: (i, 0)), out_shape=jax.ShapeDtypeStruct((r, c_), F32),
        compiler_params=_params(("arbitrary",)), name=name)(mine, got)


def sum_blocks(name, a):
    n, r, c_ = a.shape
    tile = _sum_tile(r)

    def body(a_ref, o_ref):
        acc = a_ref[0]
        for k in range(1, n):
            acc = acc + a_ref[k]
        o_ref[...] = acc

    return pl.pallas_call(
        body, grid=(r // tile,), in_specs=[pl.BlockSpec((n, tile, c_), lambda i: (0, i, 0))],
        out_specs=pl.BlockSpec((tile, c_), lambda i: (i, 0)), out_shape=jax.ShapeDtypeStruct((r, c_), F32),
        compiler_params=_params(("arbitrary",)), name=name)(a)


GROUPS = [
    ("rows1024", ["ssd_w_out", "conv_w_out", "mla_w_o", "w_out", "xattn_w_q", "xattn_w_o", "ffn_w_out"], BF16, 1024),
    ("w_in", ["w_in"], BF16, 1114),
    ("ffn_w_in", ["ffn_w_in"], BF16, 704),
    ("cols256", ["xattn_w_kv", "mla_w_kv_b"], BF16, 256),
    ("w_q_b", ["mla_w_q_b"], BF16, 192),
    ("small", ["ssd_conv_w", "conv_dw_w", "gate_b"], F32, 128),
]
HOSTED, UPFRONT = GROUPS[:3], GROUPS[3:]
PACK_COLS = 1024


def _pack(arrays):
    flat = jnp.concatenate([a.reshape(-1) for a in arrays])
    rows = -(-flat.shape[0] // PACK_COLS)
    rows += -rows % 8
    return jnp.pad(flat, (0, rows * PACK_COLS - flat.shape[0])).reshape(rows, PACK_COLS)


def _unpack(buf, shapes):
    flat = buf.reshape(-1)
    out, off = [], 0
    for s in shapes:
        n = 1
        for d in s:
            n *= d
        out.append(flat[off:off + n].reshape(tuple(s)))
        off += n
    return out


def _stack_rows(arrays, width, lead):
    return jnp.concatenate([a.reshape(a.shape[:lead] + (-1, width)) for a in arrays], axis=lead)


def _unstack_rows(buf, shapes, width, lead):
    out, off = [], 0
    for s in shapes:
        n = 1
        for d in s:
            n *= d
        rows = n // width
        idx = (slice(None),) * lead + (slice(off, off + rows),)
        out.append(buf[idx].reshape(buf.shape[:lead] + tuple(s)))
        off += rows
    return out


def _join_shards(blocks, axis):
    ax = axis + 1
    moved = jnp.moveaxis(blocks, 0, ax)
    s = moved.shape
    return moved.reshape(s[:ax] + (s[ax] * s[ax + 1],) + s[ax + 2:])


def _split_by_owner(full, axis):
    ax = axis + 1
    s = full.shape
    cut = full.reshape(s[:ax] + (2, 2, 2, s[ax] // N_DEV) + s[ax + 1:])
    cut = jnp.moveaxis(cut, (ax + 2, ax, ax + 1), (0, 1, 2))
    return cut.reshape((2, 4) + cut.shape[3:])


def _blocks_by_owner(blocks):
    cut = blocks.reshape((2, 2, 2) + blocks.shape[1:])
    return jnp.moveaxis(cut, 2, 0).reshape((2, 4) + blocks.shape[1:])


def _pad_cols(w, n):
    return jnp.pad(w, ((0, 0), (0, n - w.shape[1])))


def _regroup_cols(srcs, widths):
    starts = [0]
    for s in srcs:
        starts.append(starts[-1] + s.shape[1])
    assert starts[-1] == sum(widths)
    out, lo = [], 0
    for wd in widths:
        hi = lo + wd
        parts = []
        for s, a, b in zip(srcs, starts[:-1], starts[1:]):
            u, v = max(lo, a), min(hi, b)
            if u < v:
                parts.append(s[:, u - a:v - a])
        out.append(parts[0] if len(parts) == 1 else jnp.concatenate(parts, axis=1))
        lo = hi
    return out


COL_BLOCKED = ("w_in", "ffn_w_in", "xattn_w_kv", "mla_w_kv_b", "mla_w_q_b")
W_IN_PIECES = (1024, 2048, 16, 2048, 384, MLA_KV_RANK, MLA_ROPE, 3072)


def _prep_layer(w):
    w_z, w_xbc, w_dt, w_glu, w_q, w_ckv, w_kr, w_gate = _regroup_cols(list(w["w_in"]), W_IN_PIECES)
    w_ffn_gate, w_ffn_up = _regroup_cols(list(w["ffn_w_in"]), (FFN_HIDDEN, FFN_HIDDEN))
    w_xk, w_xv = _regroup_cols(list(w["xattn_w_kv"]), (D_MODEL, D_MODEL))
    q, kv = w["mla_w_q_b"], w["mla_w_kv_b"]

    def row(v):
        return v.reshape(1, -1)

    def norm_pair(g):
        return _pad_cols(row(g), 2 * LANES)

    return {
        "mix_norm_g": row(w["mix_norm_g"]),
        "w_z": w_z, "w_xbc": w_xbc, "w_dt": _pad_cols(w_dt, LANES), "w_glu": w_glu, "w_q": w_q, "w_ckv": w_ckv,
        "w_kr": _pad_cols(w_kr, LANES), "w_gate": w_gate,
        "ssd_conv_w": w["ssd_conv_w"], "ssd_conv_b": row(w["ssd_conv_b"]),
        "ssd_dt_bias": _pad_cols(row(w["ssd_dt_bias"]), LANES), "ssd_a_log": _pad_cols(row(w["ssd_a_log"]), LANES),
        "ssd_d": _pad_cols(row(w["ssd_d"]), LANES), "ssd_norm_g": row(w["ssd_norm_g"]), "ssd_w_out": w["ssd_w_out"],
        "conv_dw_w": w["conv_dw_w"], "conv_dw_b": row(w["conv_dw_b"]), "conv_ln_g": row(w["conv_ln_g"]),
        "conv_ln_b": row(w["conv_ln_b"]), "conv_w_out": w["conv_w_out"],
        "mla_q_a_g": row(w["mla_q_a_g"]), "mla_kv_a_g": row(w["mla_kv_a_g"]),
        "w_qn": jnp.concatenate([q[h, :, :MLA_NOPE] for h in range(MLA_HEADS)], axis=1),
        "w_qr": jnp.concatenate([_pad_cols(q[h, :, MLA_NOPE:], LANES) for h in range(MLA_HEADS)], axis=1),
        "w_kn": jnp.concatenate([kv[h, :, :MLA_NOPE] for h in range(MLA_HEADS)], axis=1),
        "w_v": jnp.concatenate([kv[h, :, MLA_NOPE:] for h in range(MLA_HEADS)], axis=1),
        "mla_q_norm_g": norm_pair(w["mla_q_norm_g"]), "mla_k_norm_g": norm_pair(w["mla_k_norm_g"]),
        "mla_w_o": w["mla_w_o"], "gate_b": row(w["gate_b"]), "w_out": w["w_out"],
        "xattn_norm_g": row(w["xattn_norm_g"]), "mem_norm_g": row(w["mem_norm_g"]), "xattn_w_q": w["xattn_w_q"],
        "w_xk": w_xk, "w_xv": w_xv,
        "xattn_q_norm_g": row(w["xattn_q_norm_g"]), "xattn_k_norm_g": row(w["xattn_k_norm_g"]),
        "xattn_w_o": w["xattn_w_o"], "ffn_norm_g": row(w["ffn_norm_g"]),
        "w_ffn_gate": w_ffn_gate, "w_ffn_up": w_ffn_up, "ffn_w_out": w["ffn_w_out"],
    }


def _unprep_grads(g):
    n_dt, n_kr = IN_SIZES[2], MLA_ROPE
    flat = lambda v: v.reshape(-1)

    def blocks(srcs):
        total = sum(s.shape[1] for s in srcs)
        return jnp.stack(_regroup_cols(srcs, (total // N_DEV,) * N_DEV))

    def head(a, h, n=LANES):
        return a[:, h * LANES:h * LANES + n]

    return {
        "mix_norm_g": flat(g["mix_norm_g"]),
        "w_in": blocks([g["w_z"], g["w_xbc"], g["w_dt"][:, :n_dt], g["w_glu"], g["w_q"], g["w_ckv"],
                        g["w_kr"][:, :n_kr], g["w_gate"]]),
        "ssd_conv_w": g["ssd_conv_w"], "ssd_conv_b": flat(g["ssd_conv_b"]),
        "ssd_dt_bias": flat(g["ssd_dt_bias"])[:SSD_HEADS], "ssd_a_log": flat(g["ssd_a_log"])[:SSD_HEADS],
        "ssd_d": flat(g["ssd_d"])[:SSD_HEADS], "ssd_norm_g": flat(g["ssd_norm_g"]), "ssd_w_out": g["ssd_w_out"],
        "conv_dw_w": g["conv_dw_w"], "conv_dw_b": flat(g["conv_dw_b"]), "conv_ln_g": flat(g["conv_ln_g"]),
        "conv_ln_b": flat(g["conv_ln_b"]), "conv_w_out": g["conv_w_out"],
        "mla_q_a_g": flat(g["mla_q_a_g"]),
        "mla_w_q_b": jnp.stack([jnp.concatenate([head(g["w_qn"], h), head(g["w_qr"], h, MLA_ROPE)], axis=1)
                                for h in range(MLA_HEADS)]),
        "mla_kv_a_g": flat(g["mla_kv_a_g"]),
        "mla_w_kv_b": jnp.stack([jnp.concatenate([head(g["w_kn"], h), head(g["w_v"], h)], axis=1)
                                 for h in range(MLA_HEADS)]),
        "mla_q_norm_g": flat(g["mla_q_norm_g"])[:MLA_NOPE + MLA_ROPE],
        "mla_k_norm_g": flat(g["mla_k_norm_g"])[:MLA_NOPE + MLA_ROPE],
        "mla_w_o": g["mla_w_o"], "gate_b": g["gate_b"].reshape(3, D_MODEL), "w_out": g["w_out"],
        "xattn_norm_g": flat(g["xattn_norm_g"]), "mem_norm_g": flat(g["mem_norm_g"]), "xattn_w_q": g["xattn_w_q"],
        "xattn_w_kv": blocks([g["w_xk"], g["w_xv"]]),
        "xattn_q_norm_g": flat(g["xattn_q_norm_g"]), "xattn_k_norm_g": flat(g["xattn_k_norm_g"]),
        "xattn_w_o": g["xattn_w_o"], "ffn_norm_g": flat(g["ffn_norm_g"]),
        "ffn_w_in": blocks([g["w_ffn_gate"], g["w_ffn_up"]]), "ffn_w_out": g["ffn_w_out"],
    }


def _layer(l, x, mem, cosf, sinf, w, hosted=(), carriers=()):
    t = x.shape[0]
    n = lambda s: f"l{l}_{s}"
    tile = min(ROW_TILE, t)
    grid = (1, t // tile)

    def rowwise(name, f, ins, width, to_matmul=False):
        return tmap(n(name), f, grid, ins, [_row_out(t, width, tile)], narrow=(0,) if to_matmul else ())[0]

    u, x = rms_norm_through(n("mix_norm"), x, w["mix_norm_g"])
    in_keys = ["z", "xbc", "dt", "glu", "q", "ckv", "kr", "gate"]
    z, xbc, dtr, glu, q_lat, c_kv, kr_raw, gate_logits = multi_matmul(n("in"), u, [w["w_" + k] for k in in_keys],
                                                                      in_keys)

    xc = dwconv(n("ssd_conv"), xbc, w["ssd_conv_w"], w["ssd_conv_b"])
    y_scan = ssd_scan(n("ssd_scan"), xc, dtr, w["ssd_dt_bias"], w["ssd_a_log"], w["ssd_d"])
    y_norm = rowwise("ssd_gate_norm", _ssd_gate_norm_f, [_rows(y_scan, tile), _rows(z, tile), _whole(w["ssd_norm_g"])],
                     SSD_INNER, to_matmul=True)
    y_ssd = matmul(n("ssd_out"), y_norm, w["ssd_w_out"])

    v = rowwise("glu", _glu_f, [_rows(glu, tile)], D_MODEL)
    v = dwconv(n("conv_dw"), v, w["conv_dw_w"], w["conv_dw_b"])
    v = rowwise("conv_ln_silu", _ln_silu_f, [_rows(v, tile), _whole(w["conv_ln_g"]), _whole(w["conv_ln_b"])], D_MODEL,
                to_matmul=True)
    y_conv = matmul(n("conv_out"), v, w["conv_w_out"])

    q_n = rms_norm(n("q_a_norm"), q_lat, w["mla_q_a_g"])
    qn_raw, qr_raw = multi_matmul(n("q"), q_n, [w["w_qn"], w["w_qr"]], ["nope", "rope"])
    c_n = rms_norm(n("kv_a_norm"), c_kv, w["mla_kv_a_g"])
    kn_raw, val = multi_matmul(n("kv"), c_n, [w["w_kn"], w["w_v"]], ["nope", "v"])
    tables = [_rows(cosf, tile, "n"), _rows(sinf, tile, "n")]
    kn = rowwise("k_nope_norm", _k_nope_f, [_rows(kn_raw, tile), _whole(w["mla_k_norm_g"])], MLA_HEADS * MLA_NOPE)
    kr = rowwise("k_rope", _k_rope_f, [_rows(kr_raw, tile)] + tables + [_whole(w["mla_k_norm_g"])], LANES)
    wide = _row_out(t, MLA_HEADS * LANES, tile)
    qn, qr = tmap(n("q_prep"), _q_prep_f, grid,
                  [_rows(qn_raw, tile), _rows(qr_raw, tile)] + tables + [_whole(w["mla_q_norm_g"])], [wide, wide])
    att, gathered, handles = mla_attention(n("mla_attn"), qn, qr, kn, kr, val, hosted, carriers)
    y_mla = matmul(n("mla_out"), att, w["mla_w_o"])

    merged = rowwise("merge", _merge_f, [_rows(gate_logits, tile), _whole(w["gate_b"]), _rows(y_ssd, tile),
                                         _rows(y_conv, tile), _rows(y_mla, tile)], D_MODEL, to_matmul=True)
    x = matmul(n("mix_out"), merged, w["w_out"], res=x)

    h, x = rms_norm_through(n("xattn_norm"), x, w["xattn_norm_g"])
    mem_n = rms_norm_nograd_x(n("mem_norm"), mem, w["mem_norm_g"])
    xq = matmul(n("xattn_q"), h, w["xattn_w_q"])
    xk, xv = multi_matmul(n("xattn_kv"), mem_n, [w["w_xk"], w["w_xv"]], ["k", "v"])
    m = mem.shape[0]
    txq = min(XATT_Q_TILE, t)
    kv_head = lambda arr: (arr, (m, X_HEAD_DIM), lambda o, i: (0, o), "ai")
    xo = tmap(n("xattn"), _xattn_f, (X_HEADS, t // txq),
              [(xq, (txq, X_HEAD_DIM), lambda o, i: (i, o), "t"), kv_head(xk), kv_head(xv),
               _whole(w["xattn_q_norm_g"]), _whole(w["xattn_k_norm_g"])],
              [((t, D_MODEL), (txq, X_HEAD_DIM), lambda o, i: (i, o))], narrow=(0,))[0]
    x = matmul(n("xattn_out"), xo, w["xattn_w_o"], res=x)

    h, x = rms_norm_through(n("ffn_norm"), x, w["ffn_norm_g"])
    gate, up = multi_matmul(n("ffn_in"), h, [w["w_ffn_gate"], w["w_ffn_up"]], ["gate", "up"])
    act = rowwise("swiglu", _swiglu_f, [_rows(gate, tile), _rows(up, tile)], FFN_HIDDEN, to_matmul=True)
    return matmul(n("ffn_out"), act, w["ffn_w_out"], res=x), gathered, handles


def _rope_tables(positions):
    inv = ROPE_THETA ** (-jnp.arange(0, MLA_ROPE, 2, dtype=F32) / MLA_ROPE)
    ang = positions.astype(F32)[:, None] * inv
    pad = jnp.zeros((positions.shape[0], LANES - MLA_ROPE), F32)
    cos, sin = jnp.cos(ang), jnp.sin(ang)
    return jnp.concatenate([cos, cos, pad], axis=1), jnp.concatenate([sin, sin, pad], axis=1)


def local_step(x, mem, positions, target, weights, gathered0, later, shard_shapes, exchange=None):
    cosf, sinf = _rope_tables(positions)

    def layer_weights(l, big):
        w = {k: v[:, l] if k in COL_BLOCKED else v[l] for k, v in weights.items()}
        for (_, names, _, width), stack in zip(HOSTED, big):
            for n, b in zip(names, _unstack_rows(stack, [shard_shapes[n] for n in names], width, 1)):
                w[n] = b if n in COL_BLOCKED else b.reshape((-1,) + b.shape[2:])
        return w

    diff = [{k: jnp.zeros(v.shape, F32) if k in MATRICES else v
             for k, v in _prep_layer(layer_weights(l, gathered0)).items()} for l in range(DEPTH)]

    def layer_fn(l, big, hosted):
        def f(x, d, carriers):
            mats = _prep_layer(layer_weights(l, big))
            w = {k: Mat(mats[k], s) if k in MATRICES else s for k, s in d.items()}
            y, gathered, handles = _layer(l, x, mem, cosf, sinf, w, hosted, carriers)
            return (y, handles), gathered
        return f

    pulls, big = [], gathered0
    for l in range(DEPTH):
        inner = l + 1 < DEPTH
        carriers = tuple(jnp.zeros((3,) + s.shape, s.dtype) for s in later[l]) if inner and exchange else ()
        (x, _), pull, big = jax.vjp(layer_fn(l, big, later[l] if inner else ()), x, diff[l], carriers, has_aux=True)
        pulls.append(pull)
    sq, g = loss_head(x, target)

    per_layer, reduced, sent, mine = [None] * DEPTH, [None] * DEPTH, (), None
    for l in reversed(range(DEPTH)):
        g, gd, got = pulls[l]((g, tuple(sent)))
        if mine is not None:
            reduced[l + 1] = exchange.finish(mine, got)
        per_layer[l] = _unprep_grads(gd)
        if exchange:
            sent, mine = exchange.begin(per_layer[l])
    if exchange:
        reduced[0] = exchange.finish(mine, [chip_exchange(f"chip_exchange_first_layer_{k}", p)
                                            for k, p in enumerate(sent)])
    grads = {k: jnp.stack([pl_[k] for pl_ in per_layer], axis=1 if k in COL_BLOCKED else 0) for k in WEIGHTS}
    shards = {k: jnp.stack([r[k] for r in reduced]) for k in reduced[0]} if exchange else None
    return sq, g, grads, shards


def _step(x, mem, positions, loss_target, w, m, v):
    xi, yi, ci = _position()

    full = {n: w[n] for n in REPLICATED}
    for gname, names, wire, width in UPFRONT:
        shapes = [w[n].shape for n in names]
        stacked = _stack_rows([w[n] for n in names], width, 0).astype(wire)
        gathered = all_gather("gather_" + gname, stacked)
        for n, b in zip(names, _unstack_rows(gathered, shapes, width, 1)):
            full[n] = b if n in COL_BLOCKED else _join_shards(b, SHARDED[n])
    shards = [[_stack_rows([w[n][l] for n in names], width, 0).astype(wire) for _, names, wire, width in HOSTED]
              for l in range(DEPTH)]
    gathered0 = [all_gather("gather0_" + g[0], s) for g, s in zip(HOSTED, shards[0])]
    shard_shapes = {n: w[n].shape[1:] for g in HOSTED for n in g[1]}

    def to_sibling(groups, grads):
        sent, mine = [], []
        for gname, names, wire, width in groups:
            by_owner = _stack_rows([_blocks_by_owner(grads[n]) if n in COL_BLOCKED else
                                    _split_by_owner(grads[n], SHARDED[n]) for n in names], width, 2).astype(wire)
            from_sibling = pair_exchange("pair_exchange_" + gname, by_owner)
            p, own = pair_reduce("pair_reduce_" + gname, by_owner, from_sibling, ci, 2 * xi + yi, wire)
            sent.append(p)
            mine.append(own)
        return sent, mine

    def from_chips(groups, shapes, mine, got):
        out = {}
        for (gname, names, _, width), own, arrived in zip(groups, mine, got):
            reduced = chip_reduce("chip_reduce_" + gname, own, arrived)
            out.update(zip(names, _unstack_rows(reduced, [shapes[n] for n in names], width, 0)))
        return out

    class LayerExchange:
        @staticmethod
        def begin(layer_grads):
            return to_sibling(HOSTED, {n: g[:, None] if n in COL_BLOCKED else g[None] for n, g in layer_grads.items()
                                       if n in shard_shapes})

        @staticmethod
        def finish(mine, got):
            return from_chips(HOSTED, shard_shapes, mine, got)

    sq, gx, grads, g_shard = local_step(x[0], mem[0], positions[0], loss_target[0], full, gathered0, shards[1:],
                                        shard_shapes, LayerExchange)
    loss = lax.psum(0.5 * jnp.sum(sq) / D_MODEL, ("x", "y", "c"))

    sent, mine = to_sibling(UPFRONT, grads)
    got = [chip_exchange("chip_exchange_" + g[0], p) for g, p in zip(UPFRONT, sent)]
    g_shard.update(from_chips(UPFRONT, {n: w[n].shape for g in UPFRONT for n in g[1]}, mine, got))

    rep_shapes = [w[n].shape for n in REPLICATED]
    rep_all = all_gather("small_grads_all_gather", _pack([grads[n] for n in REPLICATED]))
    g_rep = dict(zip(REPLICATED, _unpack(sum_blocks("small_grads_sum", rep_all), rep_shapes)))

    out_g, out_d, out_m, out_v = [], [], [], []
    for n in WEIGHTS:
        g = g_shard[n] if n in SHARDED else g_rep[n]
        d, nm, nv = adamw("adamw_" + n, w[n], g, m[n], v[n])
        out_g.append(g)
        out_d.append(d)
        out_m.append(nm)
        out_v.append(nv)
    return (loss, gx[None], *out_g, *out_d, *out_m, *out_v)


def kernel(x, mem, positions, mix_norm_g, w_in, ssd_conv_w, ssd_conv_b, ssd_dt_bias, ssd_a_log, ssd_d, ssd_norm_g, ssd_w_out, conv_dw_w, conv_dw_b, conv_ln_g, conv_ln_b, conv_w_out, mla_q_a_g, mla_w_q_b, mla_kv_a_g, mla_w_kv_b, mla_q_norm_g, mla_k_norm_g, mla_w_o, gate_b, w_out, xattn_norm_g, mem_norm_g, xattn_w_q, xattn_w_kv, xattn_q_norm_g, xattn_k_norm_g, xattn_w_o, ffn_norm_g, ffn_w_in, ffn_w_out, loss_target, m_mix_norm_g, m_w_in, m_ssd_conv_w, m_ssd_conv_b, m_ssd_dt_bias, m_ssd_a_log, m_ssd_d, m_ssd_norm_g, m_ssd_w_out, m_conv_dw_w, m_conv_dw_b, m_conv_ln_g, m_conv_ln_b, m_conv_w_out, m_mla_q_a_g, m_mla_w_q_b, m_mla_kv_a_g, m_mla_w_kv_b, m_mla_q_norm_g, m_mla_k_norm_g, m_mla_w_o, m_gate_b, m_w_out, m_xattn_norm_g, m_mem_norm_g, m_xattn_w_q, m_xattn_w_kv, m_xattn_q_norm_g, m_xattn_k_norm_g, m_xattn_w_o, m_ffn_norm_g, m_ffn_w_in, m_ffn_w_out, v_mix_norm_g, v_w_in, v_ssd_conv_w, v_ssd_conv_b, v_ssd_dt_bias, v_ssd_a_log, v_ssd_d, v_ssd_norm_g, v_ssd_w_out, v_conv_dw_w, v_conv_dw_b, v_conv_ln_g, v_conv_ln_b, v_conv_w_out, v_mla_q_a_g, v_mla_w_q_b, v_mla_kv_a_g, v_mla_w_kv_b, v_mla_q_norm_g, v_mla_k_norm_g, v_mla_w_o, v_gate_b, v_w_out, v_xattn_norm_g, v_mem_norm_g, v_xattn_w_q, v_xattn_w_kv, v_xattn_q_norm_g, v_xattn_k_norm_g, v_xattn_w_o, v_ffn_norm_g, v_ffn_w_in, v_ffn_w_out):
    args = locals()
    w = {n: args[n] for n in WEIGHTS}
    m = {n: args["m_" + n] for n in WEIGHTS}
    v = {n: args["v_" + n] for n in WEIGHTS}
    return _step(x, mem, positions, loss_target, w, m, v)
```

```python
from typing import NamedTuple

import jax
import jax.numpy as jnp
from jax import lax
from jax.experimental import pallas as pl
from jax.experimental.pallas import tpu as pltpu

F32 = jnp.float32
BF16 = jnp.bfloat16
HIGHEST = lax.Precision.HIGHEST
MESH_ID = pl.DeviceIdType.MESH

VMEM_LIMIT_BYTES = 56 * 1024 * 1024
LANES = 128

EPS = 1e-6
DEPTH = 4
D_MODEL = 1024
N_DEV = 8
SSD_HEADS = 16
SSD_HEAD_DIM = 64
SSD_STATE = 128
SSD_GROUPS = 4
SSD_INNER = 1024
SSD_TILE = 256
CONV_K = 31
SSD_CONV_K = 4
MLA_HEADS = 8
MLA_NOPE = 128
MLA_ROPE = 64
MLA_V = 128
MLA_Q_RANK = 384
MLA_KV_RANK = 256
ATT_CHUNK = 64
ROPE_THETA = 10000.0
X_HEADS = 4
X_HEAD_DIM = 256
FFN_HIDDEN = 2816
IN_SIZES = (1024, 2048, 16, 2048, 384, 320, 3072)

ADAM_LR = 0.001
ADAM_B1 = 0.9
ADAM_B2 = 0.999
ADAM_EPS = 1e-08
ADAM_WD = 0.01
ADAM_STEP = 10

MM_FULL_K = 3072
ROW_TILE = 256
ATT_BLOCK = 512
XATT_Q_TILE = 512

SHARDED = {
    "w_in": 1, "ssd_conv_w": 1, "ssd_w_out": 0, "conv_dw_w": 1, "conv_w_out": 0, "mla_w_q_b": 1, "mla_w_kv_b": 1,
    "mla_w_o": 0, "gate_b": 1, "w_out": 0, "xattn_w_q": 0, "xattn_w_kv": 1, "xattn_w_o": 0, "ffn_w_in": 1,
    "ffn_w_out": 0,
}
WEIGHTS = ["mix_norm_g", "w_in", "ssd_conv_w", "ssd_conv_b", "ssd_dt_bias", "ssd_a_log", "ssd_d", "ssd_norm_g",
           "ssd_w_out", "conv_dw_w", "conv_dw_b", "conv_ln_g", "conv_ln_b", "conv_w_out", "mla_q_a_g", "mla_w_q_b",
           "mla_kv_a_g", "mla_w_kv_b", "mla_q_norm_g", "mla_k_norm_g", "mla_w_o", "gate_b", "w_out", "xattn_norm_g",
           "mem_norm_g", "xattn_w_q", "xattn_w_kv", "xattn_q_norm_g", "xattn_k_norm_g", "xattn_w_o", "ffn_norm_g",
           "ffn_w_in", "ffn_w_out"]
REPLICATED = [n for n in WEIGHTS if n not in SHARDED]


def _params(sem=None):
    return pltpu.CompilerParams(dimension_semantics=sem, vmem_limit_bytes=VMEM_LIMIT_BYTES)


def _divisor(n, cap, mult):
    if n <= cap:
        return n
    for d in range(cap - cap % mult, 0, -mult):
        if n % d == 0:
            return d
    raise ValueError(f"no tile for {n}")


def _dg(a, b, ca, cb):
    return lax.dot_general(a.astype(BF16), b.astype(BF16), (((ca,), (cb,)), ((), ())), preferred_element_type=F32)


@jax.custom_vjp
def bdot(a, b):
    return _dg(a, b, 1, 0)


bdot.defvjp(lambda a, b: (_dg(a, b, 1, 0), (a, b)), lambda r, g: (_dg(g, r[1], 1, 1), _dg(r[0], g, 0, 0)))


@jax.custom_vjp
def bdot_nt(a, b):
    return _dg(a, b, 1, 1)


bdot_nt.defvjp(lambda a, b: (_dg(a, b, 1, 1), (a, b)), lambda r, g: (_dg(g, r[1], 1, 0), _dg(g, r[0], 0, 0)))


@jax.custom_vjp
def bdot_tn(a, b):
    return _dg(a, b, 0, 0)


bdot_tn.defvjp(lambda a, b: (_dg(a, b, 0, 0), (a, b)), lambda r, g: (_dg(r[1], g, 1, 1), _dg(r[0], g, 1, 0)))


def hdot(a, b):
    return jnp.dot(a, b, precision=HIGHEST, preferred_element_type=F32)


def _iota(shape, dim):
    return lax.broadcasted_iota(jnp.int32, shape, dim)


def _mm(name, a, b, ta=False, tb=False, res=None):
    m, k = (a.shape[1], a.shape[0]) if ta else a.shape
    n = b.shape[0] if tb else b.shape[1]
    tm = _divisor(m, 1536, LANES) if ta else _divisor(m, 1024, 8)
    tn = _divisor(n, 1536, LANES)
    tk = k if k <= MM_FULL_K else _divisor(k, 1024, LANES)
    if tk == k and k > 1024:
        tm = _divisor(m, 512, LANES if ta else 8)
    nk = k // tk
    dims = (((0 if ta else 1,), (1 if tb else 0,)), ((), ()))

    def body(*refs):
        if res is None:
            a_ref, b_ref, o_ref = refs
        else:
            a_ref, b_ref, r_ref, o_ref = refs
        part = lax.dot_general(a_ref[...].astype(BF16), b_ref[...].astype(BF16), dims, preferred_element_type=F32)
        if nk == 1:
            o_ref[...] = part if res is None else part + r_ref[...]
        else:
            kk = pl.program_id(2)

            @pl.when(kk == 0)
            def _():
                o_ref[...] = part if res is None else part + r_ref[...]

            @pl.when(kk != 0)
            def _():
                o_ref[...] += part

    a_spec = pl.BlockSpec((tk, tm), lambda i, j, kk: (kk, i)) if ta else pl.BlockSpec((tm, tk), lambda i, j, kk: (i, kk))
    b_spec = pl.BlockSpec((tn, tk), lambda i, j, kk: (j, kk)) if tb else pl.BlockSpec((tk, tn), lambda i, j, kk: (kk, j))
    o_spec = pl.BlockSpec((tm, tn), lambda i, j, kk: (i, j))
    in_specs = [a_spec, b_spec] + ([] if res is None else [o_spec])
    args = (a, b) + (() if res is None else (res,))
    return pl.pallas_call(
        body, grid=(m // tm, n // tn, nk), in_specs=in_specs, out_specs=o_spec,
        out_shape=jax.ShapeDtypeStruct((m, n), F32),
        compiler_params=_params(("parallel", "parallel", "arbitrary")), name=name)(*args)


class Mat(NamedTuple):
    value: jax.Array
    slot: jax.Array


MATRICES = frozenset([
    "w_z", "w_xbc", "w_dt", "w_glu", "w_q", "w_ckv", "w_kr", "w_gate", "ssd_w_out", "conv_w_out", "w_qn", "w_qr",
    "w_kn", "w_v", "mla_w_o", "w_out", "xattn_w_q", "w_xk", "w_xv", "xattn_w_o", "w_ffn_gate", "w_ffn_up",
    "ffn_w_out"])


class Act(NamedTuple):
    value: jax.Array
    slot: jax.Array


def _operand(a):
    return (a.value, a.slot) if isinstance(a, Act) else (a, a)


class Out(NamedTuple):
    value: jax.Array
    handle: jax.Array


def _handle(out):
    return jnp.zeros(out.shape, BF16)


def matmul(name, a, mat, res=None, to_tmap=False):
    w, slot = mat
    a, a_slot = _operand(a)
    if res is None:
        @jax.custom_vjp
        def run(a, a_slot, w, slot):
            out = _mm(name, a, w)
            return Out(out, _handle(out)) if to_tmap else out

        def fwd(a, a_slot, w, slot):
            return run(a, a_slot, w, slot), (a, w)

        def bwd(r, g):
            g = g.handle if to_tmap else g
            return None, _mm(name + "_da", g, r[1], tb=True), None, _mm(name + "_dw", r[0], g, ta=True)

        run.defvjp(fwd, bwd)
        return run(a, a_slot, w, slot)

    @jax.custom_vjp
    def run_res(a, a_slot, w, slot, res):
        return _mm(name, a, w, res=res)

    def fwd_res(a, a_slot, w, slot, res):
        return run_res(a, a_slot, w, slot, res), (a, w)

    def bwd_res(r, g):
        return None, _mm(name + "_da", g, r[1], tb=True), None, _mm(name + "_dw", r[0], g, ta=True), g

    run_res.defvjp(fwd_res, bwd_res)
    return run_res(a, a_slot, w, slot, res)


def multi_matmul(name, a, mats, keys, to_tmap=()):
    ws, slots = tuple(m.value for m in mats), tuple(m.slot for m in mats)
    a, a_slot = _operand(a)

    @jax.custom_vjp
    def run(a, a_slot, ws, slots):
        outs = [_mm(f"{name}_{k}", a, w) for k, w in zip(keys, ws)]
        return tuple(Out(o, _handle(o)) if k in to_tmap else o for k, o in zip(keys, outs))

    def fwd(a, a_slot, ws, slots):
        return run(a, a_slot, ws, slots), (a, ws)

    def bwd(r, gs):
        a, ws = r
        gs = [g.handle if k in to_tmap else g for k, g in zip(keys, gs)]
        da = None
        for k, w, g in zip(keys, ws, gs):
            da = _mm(f"{name}_{k}_da", g, w, tb=True, res=da)
        dws = tuple(_mm(f"{name}_{k}_dw", a, g, ta=True) for k, g in zip(keys, gs))
        return None, da, tuple(None for _ in ws), dws

    run.defvjp(fwd, bwd)
    return run(a, a_slot, ws, slots)


def tmap(name, f, grid, ins, outs, through=None, narrow=()):
    handled = [k for k, x in enumerate(ins) if isinstance(x[0], Out)]
    assert all(ins[k][3] == "t" and k != through for k in handled)
    arrays = [x[0].value if isinstance(x[0], Out) else x[0] for x in ins] + [ins[k][0].handle for k in handled]
    kinds = [x[3] for x in ins]
    in_specs = [pl.BlockSpec(x[1], x[2]) for x in ins]
    out_specs = [pl.BlockSpec(x[1], x[2]) for x in outs]
    out_shape = [jax.ShapeDtypeStruct(x[0], BF16 if k in narrow else F32) for k, x in enumerate(outs)]
    n_in, n_out = len(ins), len(outs)
    n_through = 0 if through is None else 1
    assert through is None or kinds[through] == "t"
    didx = [k for k, kd in enumerate(kinds) if kd != "n"]

    def fwd_call(*arrs):
        def body(*refs):
            pids = (pl.program_id(0), pl.program_id(1))
            vals = f(pids, *[r[...] for r in refs[:n_in]])
            for r, v in zip(refs[n_in:], vals):
                r[...] = v.astype(r.dtype)

        return pl.pallas_call(body, grid=grid, in_specs=in_specs, out_specs=out_specs, out_shape=out_shape,
                              compiler_params=_params(("arbitrary", "arbitrary")), name=name)(*arrs)

    def bwd_call(arrs, cts):
        def body(*refs):
            o, i = pl.program_id(0), pl.program_id(1)
            vals = [r[...] for r in refs[:n_in]]

            def g(*dv):
                full = list(vals)
                for k, v in zip(didx, dv):
                    full[k] = v
                return tuple(f((o, i), *full))

            _, vjp = jax.vjp(g, *[vals[k] for k in didx])
            grads = vjp(tuple(r[...] for r in refs[n_in:n_in + n_out]))
            for k, gr, r in zip(didx, grads, refs[n_in + n_out + n_through:]):
                if k == through:
                    r[...] = gr + refs[n_in + n_out][...]
                elif kinds[k] == "t":
                    r[...] = gr.astype(r.dtype)
                else:
                    first = (i == 0) if kinds[k] == "ai" else jnp.logical_and(o == 0, i == 0)

                    @pl.when(first)
                    def _(r=r, gr=gr):
                        r[...] = gr

                    @pl.when(jnp.logical_not(first))
                    def _(r=r, gr=gr):
                        r[...] += gr

        g_specs = [in_specs[k] for k in didx]
        g_shape = [jax.ShapeDtypeStruct(arrs[k].shape, BF16 if k in handled else F32) for k in didx]
        ct_specs = out_specs + ([in_specs[through]] if n_through else [])
        return pl.pallas_call(body, grid=grid, in_specs=in_specs + ct_specs, out_specs=g_specs, out_shape=g_shape,
                              compiler_params=_params(("arbitrary", "arbitrary")), name=name + "_bwd")(*arrs, *cts)

    @jax.custom_vjp
    def run(*arrs):
        res = [Act(o, jnp.zeros(o.shape, F32)) if k in narrow else o for k, o in enumerate(fwd_call(*arrs[:n_in]))]
        return tuple(res) + ((arrs[through],) if n_through else ())

    def run_fwd(*arrs):
        return run(*arrs), arrs[:n_in]

    def run_bwd(arrs, cts):
        cts = [c.slot if isinstance(c, Act) else c for c in cts]
        gs = bwd_call(arrs, cts)
        full = [None] * (n_in + len(handled))
        for k, g in zip(didx, gs):
            full[n_in + handled.index(k) if k in handled else k] = g
        return tuple(full)

    run.defvjp(run_fwd, run_bwd)
    return run(*arrays)


def _rows(arr, tile, kind="t"):
    width = (arr.value if isinstance(arr, Out) else arr).shape[1]
    return (arr, (tile, width), lambda o, i: (i, 0), kind)


def _whole(arr, kind="ag"):
    return (arr, arr.shape, lambda o, i: (0, 0), kind)


def _row_out(t, n, tile):
    return ((t, n), (tile, n), lambda o, i: (i, 0))


def _rms(x, g, n=None):
    ms = jnp.sum(x * x, axis=-1, keepdims=True) / (x.shape[-1] if n is None else n)
    return x * lax.rsqrt(ms + EPS) * g


def rms_norm(name, x, g):
    t, n = (x.value if isinstance(x, Out) else x).shape
    tile = min(ROW_TILE, t)
    return tmap(name, lambda p, x, g: (_rms(x, g),), (1, t // tile), [_rows(x, tile), _whole(g)],
                [_row_out(t, n, tile)], narrow=(0,))[0]


def rms_norm_through(name, x, g):
    t, n = (x.value if isinstance(x, Out) else x).shape
    tile = min(ROW_TILE, t)
    return tmap(name, lambda p, x, g: (_rms(x, g),), (1, t // tile), [_rows(x, tile), _whole(g)],
                [_row_out(t, n, tile)], through=0, narrow=(0,))


def rms_norm_nograd_x(name, x, g):
    t, n = (x.value if isinstance(x, Out) else x).shape
    tile = min(ROW_TILE, t)
    return tmap(name, lambda p, x, g: (_rms(x, g),), (1, t // tile), [_rows(x, tile, "n"), _whole(g)],
                [_row_out(t, n, tile)], narrow=(0,))[0]


def _glu_f(p, glu):
    h = glu.shape[1] // 2
    return (glu[:, :h] * jax.nn.sigmoid(glu[:, h:]),)


def _ln_silu_f(p, v, g, b):
    mu = jnp.mean(v, axis=-1, keepdims=True)
    xc = v - mu
    var = jnp.mean(xc * xc, axis=-1, keepdims=True)
    return (jax.nn.silu(xc * lax.rsqrt(var + EPS) * g + b),)


def _ssd_gate_norm_f(p, y, z, g):
    v = y * jax.nn.silu(z)
    w = SSD_INNER // SSD_GROUPS
    parts = []
    for k in range(SSD_GROUPS):
        vg = v[:, k * w:(k + 1) * w]
        parts.append(vg * lax.rsqrt(jnp.mean(vg * vg, axis=-1, keepdims=True) + EPS))
    return (jnp.concatenate(parts, axis=1) * g,)


def _merge_f(p, gl, gb, y0, y1, y2):
    g = jax.nn.sigmoid(gl + gb)
    d = D_MODEL
    return (g[:, :d] * y0 + g[:, d:2 * d] * y1 + g[:, 2 * d:] * y2,)


def _swiglu_f(p, gate, up):
    return (jax.nn.silu(gate) * up,)


def _rot_matrix():
    r, c = _iota((LANES, LANES), 0), _iota((LANES, LANES), 1)
    h = MLA_ROPE // 2
    plus = jnp.logical_and(c >= h, jnp.logical_and(c < 2 * h, r == c - h))
    minus = jnp.logical_and(c < h, r == c + h)
    return plus.astype(F32) - minus.astype(F32)


def _rope(x, cosf, sinf):
    return x * cosf + hdot(x, _rot_matrix()) * sinf


def _per_head(f, x):
    return jnp.concatenate([f(x[:, h * LANES:(h + 1) * LANES]) for h in range(x.shape[1] // LANES)], axis=1)


def _k_nope_f(p, kn_raw, kg):
    return (_per_head(lambda x: _rms(x, kg[:, :MLA_NOPE]), kn_raw),)


def _k_rope_f(p, kr_raw, cosf, sinf, kg):
    return (_rope(_rms(kr_raw, kg[:, MLA_NOPE:], n=MLA_ROPE), cosf, sinf),)


def _q_prep_f(p, qn_raw, qr_raw, cosf, sinf, qg):
    qn = _per_head(lambda x: _rms(x, qg[:, :MLA_NOPE]), qn_raw)
    qr = _per_head(lambda x: _rope(_rms(x, qg[:, MLA_NOPE:], n=MLA_ROPE), cosf, sinf), qr_raw)
    return qn, qr


def _softmax(s):
    m = jnp.max(s, axis=-1, keepdims=True)
    e = jnp.exp(s - m)
    return e / jnp.sum(e, axis=-1, keepdims=True)


def _xattn_f(p, q, k, v, qg, kg):
    s = bdot_nt(_rms(q, qg), _rms(k, kg)) * (X_HEAD_DIM ** -0.5)
    return (bdot(_softmax(s), v),)


ATT_SCALE = (MLA_NOPE + MLA_ROPE) ** -0.5
NT_DIMS = (((1,), (1,)), ((), ()))
NN_DIMS = (((1,), (0,)), ((), ()))
TN_DIMS = (((0,), (0,)), ((), ()))


def _att_specs(t, blk):
    q_spec = pl.BlockSpec((blk, LANES), lambda h, i: (i, h))
    k_spec = pl.BlockSpec((t, LANES), lambda h, i: (0, h))
    shared = pl.BlockSpec((t, LANES), lambda h, i: (0, 0))
    lse_spec = pl.BlockSpec((None, blk, 1), lambda h, i: (h, i, 0))
    return q_spec, k_spec, shared, lse_spec


def _diagonal_mask(blk):
    return (_iota((blk, blk), 1) // ATT_CHUNK) <= (_iota((blk, blk), 0) // ATT_CHUNK)


def _att_keys(kn_ref, kr_ref, j, blk):
    ks = pl.ds(pl.multiple_of(j * blk, blk), blk)
    return ks, jnp.concatenate([kn_ref[ks, :], kr_ref[ks, :]], axis=1).astype(BF16)


def _att_fwd(name, qn, qr, kn, kr, v, hosted=()):
    t, width = qn.shape
    heads = width // LANES
    blk = min(ATT_BLOCK, t)
    nh = len(hosted)

    def body(qn_ref, qr_ref, kn_ref, kr_ref, v_ref, *rest):
        shard_refs, (o_ref, lse_ref), rest = rest[:nh], rest[nh:nh + 2], rest[nh + 2:]
        full_refs, sems = rest[:nh], rest[nh:]
        i = pl.program_id(1)
        gathers = [_gather_copies(shard_refs[k], full_refs[k], *sems[3 * k:3 * k + 3]) for k in range(nh)]
        if nh:
            @pl.when(jnp.logical_and(pl.program_id(0) == 0, i == 0))
            def _():
                for start, _ in gathers:
                    start()
        q = jnp.concatenate([qn_ref[...], qr_ref[...]], axis=1).astype(BF16)

        def scores(j):
            _, k = _att_keys(kn_ref, kr_ref, j, blk)
            return lax.dot_general(q, k, NT_DIMS, preferred_element_type=F32)

        def weighted_values(p, j):
            ks = pl.ds(pl.multiple_of(j * blk, blk), blk)
            return lax.dot_general(p, v_ref[ks, :].astype(BF16), NN_DIMS, preferred_element_type=F32)

        def softmax_step(s, m, l):
            m_new = jnp.maximum(m, jnp.max(s, axis=1, keepdims=True))
            alpha = jnp.exp(m - m_new)
            p = jnp.exp(s - m_new)
            return m_new, alpha, alpha * l + jnp.sum(p, axis=1, keepdims=True), p.astype(BF16)

        def step(j, carry):
            s, p_prev, m, l, acc = carry
            s_next = scores(j + 1)
            pv_prev = weighted_values(p_prev, jnp.maximum(j - 1, 0))
            m, alpha, l, p = softmax_step(s * ATT_SCALE, m, l)
            return s_next, p, m, l, alpha * (acc + pv_prev)

        init = (scores(0), jnp.zeros((blk, blk), BF16), jnp.full((blk, 1), -1e30, F32), jnp.zeros((blk, 1), F32),
                jnp.zeros((blk, LANES), F32))
        s, p_prev, m, l, acc = lax.fori_loop(0, i, step, init)
        pv_prev = weighted_values(p_prev, jnp.maximum(i - 1, 0))
        s = jnp.where(_diagonal_mask(blk), s * ATT_SCALE, -1e30)
        m, alpha, l, p = softmax_step(s, m, l)
        acc = alpha * (acc + pv_prev) + weighted_values(p, i)
        o_ref[...] = acc / l
        lse_ref[...] = m + jnp.log(l)
        if nh:
            @pl.when(jnp.logical_and(pl.program_id(0) == heads - 1, i == t // blk - 1))
            def _():
                for _, finish in gathers:
                    finish()

    q_spec, k_spec, shared, lse_spec = _att_specs(t, blk)
    sem_shapes = [pltpu.SemaphoreType.DMA((7,)), pltpu.SemaphoreType.DMA((7,)), pltpu.SemaphoreType.DMA] * nh
    return pl.pallas_call(
        body, grid=(heads, t // blk), in_specs=[q_spec, q_spec, k_spec, shared, k_spec] + [HBM_SPEC] * nh,
        out_specs=[q_spec, lse_spec] + [HBM_SPEC] * nh,
        out_shape=[jax.ShapeDtypeStruct((t, width), F32), jax.ShapeDtypeStruct((heads, t, 1), F32)] +
                  [jax.ShapeDtypeStruct((N_DEV,) + s.shape, s.dtype) for s in hosted],
        scratch_shapes=sem_shapes, compiler_params=_params(("arbitrary", "arbitrary")), name=name)(
        qn, qr, kn, kr, v, *hosted)


def _att_bwd(name, qn, qr, kn, kr, v, o, lse, do, hosted=()):
    t, width = qn.shape
    heads = width // LANES
    blk = min(ATT_BLOCK, t)
    nh = len(hosted)

    def body(qn_ref, qr_ref, kn_ref, kr_ref, v_ref, o_ref, lse_ref, do_ref, *rest):
        sent_refs, (dqn_ref, dqr_ref, dkn_ref, dkr_ref, dv_ref), rest = rest[:nh], rest[nh:nh + 5], rest[nh + 5:]
        got_refs, sems = rest[:nh], rest[nh:]
        h, i = pl.program_id(0), pl.program_id(1)
        exchanges = [_chip_copies(sent_refs[k], got_refs[k], *sems[2 * k:2 * k + 2]) for k in range(nh)]
        if nh:
            @pl.when(jnp.logical_and(h == 0, i == 0))
            def _():
                for start, _ in exchanges:
                    start()

        @pl.when(i == 0)
        def _():
            dkn_ref[...] = jnp.zeros_like(dkn_ref)
            dv_ref[...] = jnp.zeros_like(dv_ref)

        @pl.when(jnp.logical_and(h == 0, i == 0))
        def _():
            dkr_ref[...] = jnp.zeros_like(dkr_ref)

        q = jnp.concatenate([qn_ref[...], qr_ref[...]], axis=1).astype(BF16)
        do = do_ref[...]
        do16 = do.astype(BF16)
        delta = jnp.sum(do * o_ref[...], axis=1, keepdims=True)
        lse = lse_ref[...]

        def issue(j):
            ks, k = _att_keys(kn_ref, kr_ref, j, blk)
            s = lax.dot_general(q, k, NT_DIMS, preferred_element_type=F32)
            dp = lax.dot_general(do16, v_ref[ks, :].astype(BF16), NT_DIMS, preferred_element_type=F32)
            return s, dp

        def retire(p, ds, j, dq):
            ks, k = _att_keys(kn_ref, kr_ref, j, blk)
            dv_ref[ks, :] += lax.dot_general(p, do16, TN_DIMS, preferred_element_type=F32)
            dk = lax.dot_general(ds, q, TN_DIMS, preferred_element_type=F32)
            dkn_ref[ks, :] += dk[:, :LANES]
            dkr_ref[ks, :] += dk[:, LANES:]
            return dq + lax.dot_general(ds, k, NN_DIMS, preferred_element_type=F32)

        def probs(s, dp, masked):
            s = s * ATT_SCALE
            if masked:
                s = jnp.where(_diagonal_mask(blk), s, -1e30)
            p = jnp.exp(s - lse)
            return p.astype(BF16), (p * (dp - delta) * ATT_SCALE).astype(BF16)

        def step(j, carry):
            s, dp, p_prev, ds_prev, dq = carry
            s_next, dp_next = issue(j + 1)
            dq = retire(p_prev, ds_prev, jnp.maximum(j - 1, 0), dq)
            p, ds = probs(s, dp, False)
            return s_next, dp_next, p, ds, dq

        none = jnp.zeros((blk, blk), BF16)
        s, dp, p_prev, ds_prev, dq = lax.fori_loop(0, i, step,
                                                   issue(0) + (none, none, jnp.zeros((blk, 2 * LANES), F32)))
        dq = retire(p_prev, ds_prev, jnp.maximum(i - 1, 0), dq)
        p, ds = probs(s, dp, True)
        dq = retire(p, ds, i, dq)
        dqn_ref[...] = dq[:, :LANES]
        dqr_ref[...] = dq[:, LANES:]
        if nh:
            @pl.when(jnp.logical_and(h == heads - 1, i == t // blk - 1))
            def _():
                for _, finish in exchanges:
                    finish()

    q_spec, k_spec, shared, lse_spec = _att_specs(t, blk)
    big, one = jax.ShapeDtypeStruct((t, width), F32), jax.ShapeDtypeStruct((t, LANES), F32)
    sem_shapes = [pltpu.SemaphoreType.DMA((3,)), pltpu.SemaphoreType.DMA((3,))] * nh
    return pl.pallas_call(
        body, grid=(heads, t // blk),
        in_specs=[q_spec, q_spec, k_spec, shared, k_spec, q_spec, lse_spec, q_spec] + [HBM_SPEC] * nh,
        out_specs=[q_spec, q_spec, k_spec, shared, k_spec] + [HBM_SPEC] * nh,
        out_shape=[big, big, big, one, big] + [jax.ShapeDtypeStruct((3,) + p.shape[1:], p.dtype) for p in hosted],
        scratch_shapes=sem_shapes, compiler_params=_params(("arbitrary", "arbitrary")), name=name)(
        qn, qr, kn, kr, v, o, lse, do, *hosted)


def mla_attention(name, qn, qr, kn, kr, v, hosted=(), carriers=()):
    @jax.custom_vjp
    def run(qn, qr, kn, kr, v, hosted, carriers):
        o, _, *gathered = _att_fwd(name, qn, qr, kn, kr, v, hosted)
        return o, tuple(gathered), tuple(jnp.zeros((4,) + c.shape[1:], c.dtype) for c in carriers)

    def fwd(qn, qr, kn, kr, v, hosted, carriers):
        o, lse, *gathered = _att_fwd(name, qn, qr, kn, kr, v, hosted)
        handles = tuple(jnp.zeros((4,) + c.shape[1:], c.dtype) for c in carriers)
        return (o, tuple(gathered), handles), (qn, qr, kn, kr, v, o, lse)

    def bwd(r, g):
        do, _, sent = g
        dqn, dqr, dkn, dkr, dv, *got = _att_bwd(name + "_bwd", *r, do, tuple(sent))
        return dqn, dqr, dkn, dkr, dv, (None,) * len(hosted), tuple(got)

    run.defvjp(fwd, bwd)
    return run(qn, qr, kn, kr, v, tuple(hosted), tuple(carriers))


def _shift_down(v, s, rows):
    return v if s == 0 else jnp.where(rows >= s, pltpu.roll(v, s, 0), 0.0)


def _shift_up(v, s, rows):
    t = v.shape[0]
    return v if s == 0 else jnp.where(rows < t - s, pltpu.roll(v, t - s, 0), 0.0)


def _dwconv_fwd(name, x, w, b):
    t, c = x.shape
    kw = w.shape[0]

    def body(x_ref, w_ref, b_ref, y_ref):
        x = x_ref[...]
        rows = _iota(x.shape, 0)
        acc = jnp.zeros_like(x) + b_ref[...]
        for k in range(kw):
            acc = acc + w_ref[k:k + 1, :] * _shift_down(x, kw - 1 - k, rows)
        y_ref[...] = acc

    col = lambda i: (0, i)
    return pl.pallas_call(
        body, grid=(c // LANES,),
        in_specs=[pl.BlockSpec((t, LANES), col), pl.BlockSpec((kw, LANES), col), pl.BlockSpec((1, LANES), col)],
        out_specs=pl.BlockSpec((t, LANES), col), out_shape=jax.ShapeDtypeStruct((t, c), F32),
        compiler_params=_params(("arbitrary",)), name=name)(x, w, b)


def _dwconv_bwd(name, x, w, dy):
    t, c = x.shape
    kw = w.shape[0]

    def body(x_ref, w_ref, dy_ref, dx_ref, dw_ref, db_ref):
        x, dy = x_ref[...], dy_ref[...]
        rows = _iota(x.shape, 0)
        dx = jnp.zeros_like(x)
        for k in range(kw):
            s = kw - 1 - k
            dx = dx + w_ref[k:k + 1, :] * _shift_up(dy, s, rows)
            dw_ref[k:k + 1, :] = jnp.sum(dy * _shift_down(x, s, rows), axis=0, keepdims=True)
        dx_ref[...] = dx
        db_ref[...] = jnp.sum(dy, axis=0, keepdims=True)

    col = lambda i: (0, i)
    big, wsp, bsp = pl.BlockSpec((t, LANES), col), pl.BlockSpec((kw, LANES), col), pl.BlockSpec((1, LANES), col)
    return pl.pallas_call(
        body, grid=(c // LANES,), in_specs=[big, wsp, big], out_specs=[big, wsp, bsp],
        out_shape=[jax.ShapeDtypeStruct((t, c), F32), jax.ShapeDtypeStruct((kw, c), F32),
                   jax.ShapeDtypeStruct((1, c), F32)],
        compiler_params=_params(("arbitrary",)), name=name)(x, w, dy)


def dwconv(name, x, w, b):
    @jax.custom_vjp
    def run(x, w, b):
        return _dwconv_fwd(name, x, w, b)

    def fwd(x, w, b):
        return run(x, w, b), (x, w)

    def bwd(r, g):
        return tuple(_dwconv_bwd(name + "_bwd", r[0], r[1], g))

    run.defvjp(fwd, bwd)
    return run(x, w, b)


def _ssd_tile(xc, dtr, dtb, alog, dsk, prev):
    ln = xc.shape[0]
    gw = SSD_INNER // SSD_GROUPS
    ns = SSD_STATE
    xs = jax.nn.silu(xc[:, :SSD_INNER])
    bm = jax.nn.silu(xc[:, SSD_INNER:SSD_INNER + SSD_GROUPS * ns])
    cm = jax.nn.silu(xc[:, SSD_INNER + SSD_GROUPS * ns:])
    dt = jax.nn.softplus(dtr + dtb)
    a = dt * (-jnp.exp(alog))
    expand = (_iota((LANES, SSD_INNER), 0) == _iota((LANES, SSD_INNER), 1) // SSD_HEAD_DIM).astype(F32)
    causal = _iota((ln, ln), 0) >= _iota((ln, ln), 1)
    acs_h = hdot(causal.astype(F32), a)
    acs_c = hdot(acs_h, expand)
    dt_c = hdot(dt, expand)

    def row_per_column(v):
        return jnp.mean(hdot(jnp.broadcast_to(v, (8, LANES)), expand), axis=0, keepdims=True)

    aend_c = row_per_column(jnp.sum(a, axis=0, keepdims=True))
    xdt = xs * dt_c
    to_end = xdt * jnp.exp(aend_c - acs_c)
    from_start = jnp.exp(acs_c)
    acs_ht = acs_h.T
    lane_h, sub_h = _iota((1, LANES), 1), _iota((LANES, 1), 0)
    head_of_col = _iota((1, gw), 1) // SSD_HEAD_DIM
    ys, states = [], []
    for g in range(SSD_GROUPS):
        cg = cm[:, g * ns:(g + 1) * ns]
        bg = bm[:, g * ns:(g + 1) * ns]
        cols = slice(g * gw, (g + 1) * gw)
        y = bdot(cg, prev[:, cols]) * from_start[:, cols]
        states.append(bdot_tn(bg, to_end[:, cols]))
        cb = bdot_nt(cg, bg)
        for r in range(gw // SSD_HEAD_DIM):
            h = g * (gw // SSD_HEAD_DIM) + r
            col = jnp.sum(jnp.where(lane_h == h, acs_h, 0.0), axis=1, keepdims=True)
            row = jnp.sum(jnp.where(sub_h == h, acs_ht, 0.0), axis=0, keepdims=True)
            decay = jnp.exp(jnp.where(causal, col - row, -1e30))
            y = y + jnp.where(head_of_col == r, bdot(cb * decay, xdt[:, cols]), 0.0)
        ys.append(y)
    y = jnp.concatenate(ys, axis=1) + row_per_column(dsk) * xs
    new = prev * jnp.exp(aend_c) + jnp.concatenate(states, axis=1)
    return y, new


def _ssd_fwd(name, xc, dtr, dtb, alog, dsk):
    t = xc.shape[0]
    ln = min(SSD_TILE, t)
    nt = t // ln

    def body(xc_ref, dtr_ref, dtb_ref, alog_ref, dsk_ref, y_ref, prev_ref, carry):
        @pl.when(pl.program_id(0) == 0)
        def _():
            carry[...] = jnp.zeros_like(carry)

        prev = carry[...]
        prev_ref[...] = prev
        y, new = _ssd_tile(xc_ref[...], dtr_ref[...], dtb_ref[...], alog_ref[...], dsk_ref[...], prev)
        y_ref[...] = y
        carry[...] = new

    row = lambda i: (i, 0)
    par = pl.BlockSpec((1, LANES), lambda i: (0, 0))
    return pl.pallas_call(
        body, grid=(nt,),
        in_specs=[pl.BlockSpec((ln, xc.shape[1]), row), pl.BlockSpec((ln, LANES), row), par, par, par],
        out_specs=[pl.BlockSpec((ln, SSD_INNER), row), pl.BlockSpec((None, SSD_STATE, SSD_INNER), lambda i: (i, 0, 0))],
        out_shape=[jax.ShapeDtypeStruct((t, SSD_INNER), F32), jax.ShapeDtypeStruct((nt, SSD_STATE, SSD_INNER), F32)],
        scratch_shapes=[pltpu.VMEM((SSD_STATE, SSD_INNER), F32)],
        compiler_params=_params(("arbitrary",)), name=name)(xc, dtr, dtb, alog, dsk)


def _ssd_bwd(name, xc, dtr, dtb, alog, dsk, prevs, dy):
    t = xc.shape[0]
    ln = min(SSD_TILE, t)
    nt = t // ln

    def body(xc_ref, dtr_ref, dtb_ref, alog_ref, dsk_ref, prev_ref, dy_ref, dxc_ref, ddtr_ref, ddtb_ref, dalog_ref,
             ddsk_ref, dcarry):
        i = pl.program_id(0)

        @pl.when(i == 0)
        def _():
            dcarry[...] = jnp.zeros_like(dcarry)

        _, vjp = jax.vjp(_ssd_tile, xc_ref[...], dtr_ref[...], dtb_ref[...], alog_ref[...], dsk_ref[...], prev_ref[...])
        dxc, ddtr, ddtb, dalog, ddsk, dprev = vjp((dy_ref[...], dcarry[...]))
        dxc_ref[...] = dxc
        ddtr_ref[...] = ddtr
        dcarry[...] = dprev
        for r, gr in ((ddtb_ref, ddtb), (dalog_ref, dalog), (ddsk_ref, ddsk)):
            @pl.when(i == 0)
            def _(r=r, gr=gr):
                r[...] = gr

            @pl.when(i != 0)
            def _(r=r, gr=gr):
                r[...] += gr

    row = lambda i: (nt - 1 - i, 0)
    par = pl.BlockSpec((1, LANES), lambda i: (0, 0))
    big, dts = pl.BlockSpec((ln, xc.shape[1]), row), pl.BlockSpec((ln, LANES), row)
    par_shape = jax.ShapeDtypeStruct((1, LANES), F32)
    return pl.pallas_call(
        body, grid=(nt,),
        in_specs=[big, dts, par, par, par, pl.BlockSpec((None, SSD_STATE, SSD_INNER), lambda i: (nt - 1 - i, 0, 0)),
                  pl.BlockSpec((ln, SSD_INNER), row)],
        out_specs=[big, dts, par, par, par],
        out_shape=[jax.ShapeDtypeStruct(xc.shape, F32), jax.ShapeDtypeStruct(dtr.shape, F32), par_shape, par_shape,
                   par_shape],
        scratch_shapes=[pltpu.VMEM((SSD_STATE, SSD_INNER), F32)],
        compiler_params=_params(("arbitrary",)), name=name)(xc, dtr, dtb, alog, dsk, prevs, dy)


def ssd_scan(name, xc, dtr, dtb, alog, dsk):
    @jax.custom_vjp
    def run(xc, dtr, dtb, alog, dsk):
        return _ssd_fwd(name, xc, dtr, dtb, alog, dsk)[0]

    def fwd(xc, dtr, dtb, alog, dsk):
        y, prevs = _ssd_fwd(name, xc, dtr, dtb, alog, dsk)
        return y, (xc, dtr, dtb, alog, dsk, prevs)

    def bwd(r, g):
        return tuple(_ssd_bwd(name + "_bwd", *r, g))

    run.defvjp(fwd, bwd)
    return run(xc, dtr, dtb, alog, dsk)


def loss_head(y, target):
    t, n = y.shape
    tile = min(ROW_TILE, t)

    def body(y_ref, t_ref, dy_ref, acc_ref):
        d = y_ref[...] - t_ref[...]
        dy_ref[...] = d * (1.0 / n)

        @pl.when(pl.program_id(0) == 0)
        def _():
            acc_ref[...] = jnp.zeros_like(acc_ref)

        acc_ref[...] += jnp.sum(d * d, axis=0, keepdims=True)

    row = pl.BlockSpec((tile, n), lambda i: (i, 0))
    dy, acc = pl.pallas_call(
        body, grid=(t // tile,), in_specs=[row, row], out_specs=[row, pl.BlockSpec((1, n), lambda i: (0, 0))],
        out_shape=[jax.ShapeDtypeStruct((t, n), F32), jax.ShapeDtypeStruct((1, n), F32)],
        compiler_params=_params(("arbitrary",)), name="loss_head")(y, target)
    return acc, dy


def adamw(name, w, g, m, v):
    shape = w.shape
    cols = shape[-1]
    rows = w.size // cols
    tile = _divisor(rows, 512, 8) if rows % 8 == 0 else rows

    def body(w_ref, g_ref, m_ref, v_ref, d_ref, nm_ref, nv_ref):
        g = g_ref[...]
        m = ADAM_B1 * m_ref[...] + (1.0 - ADAM_B1) * g
        v = ADAM_B2 * v_ref[...] + (1.0 - ADAM_B2) * jnp.square(g)
        m_hat = m / (1.0 - ADAM_B1 ** ADAM_STEP)
        v_hat = v / (1.0 - ADAM_B2 ** ADAM_STEP)
        d_ref[...] = -ADAM_LR * (m_hat / (jnp.sqrt(v_hat) + ADAM_EPS) + ADAM_WD * w_ref[...])
        nm_ref[...] = m
        nv_ref[...] = v

    spec = pl.BlockSpec((tile, cols), lambda i: (i, 0))
    two_d = jax.ShapeDtypeStruct((rows, cols), F32)
    outs = pl.pallas_call(body, grid=(rows // tile,), in_specs=[spec] * 4, out_specs=[spec] * 3, out_shape=[two_d] * 3,
                          compiler_params=_params(("arbitrary",)), name=name)(
        *[a.reshape(rows, cols) for a in (w, g, m, v)])
    return [o.reshape(shape) for o in outs]


HBM_SPEC = pl.BlockSpec(memory_space=pl.ANY)


def _position():
    return lax.axis_index("x"), lax.axis_index("y"), lax.axis_index("c")


def _gather_copies(x_ref, out_ref, send_sems, recv_sems, local_sem):
    x, y, c = _position()
    me, sibling = (x, y, c), (x, y, 1 - c)
    chips = [(1 - x, y), (x, 1 - y), (1 - x, 1 - y)]

    def block(px, py, pc):
        return out_ref.at[4 * px + 2 * py + pc]

    def copy(k, blk, to, src=None):
        return pltpu.make_async_remote_copy(
            src_ref=block(*blk) if src is None else src, dst_ref=block(*blk), send_sem=send_sems.at[k],
            recv_sem=recv_sems.at[k], device_id=to, device_id_type=MESH_ID)

    mine = pltpu.make_async_copy(x_ref, block(*me), local_sem)
    first = [copy(0, me, sibling, src=x_ref)]
    first += [copy(1 + j, me, (*chip, c), src=x_ref) for j, chip in enumerate(chips)]
    passed = [copy(4 + j, (*chip, c), sibling) for j, chip in enumerate(chips)]

    def start():
        mine.start()
        for cp in first:
            cp.start()

    def finish():
        for j, chip in enumerate(chips):
            copy(1 + j, (*chip, c), me).wait_recv()
            passed[j].start()
        copy(0, sibling, me).wait_recv()
        for j, chip in enumerate(chips):
            copy(4 + j, (*chip, 1 - c), me).wait_recv()
        for cp in first + passed:
            cp.wait_send()
        mine.wait()

    return start, finish


def all_gather(name, shard):
    def body(x_ref, out_ref, send_sems, recv_sems, local_sem):
        start, finish = _gather_copies(x_ref, out_ref, send_sems, recv_sems, local_sem)
        start()
        finish()

    return pl.pallas_call(
        body, in_specs=[HBM_SPEC], out_specs=HBM_SPEC,
        out_shape=jax.ShapeDtypeStruct((N_DEV,) + shard.shape, shard.dtype),
        scratch_shapes=[pltpu.SemaphoreType.DMA((7,)), pltpu.SemaphoreType.DMA((7,)), pltpu.SemaphoreType.DMA],
        name=name)(shard)


def pair_exchange(name, g):
    def body(g_ref, out_ref, send_sem, recv_sem):
        x, y, c = _position()
        cp = pltpu.make_async_remote_copy(src_ref=g_ref.at[1 - c], dst_ref=out_ref, send_sem=send_sem,
                                          recv_sem=recv_sem, device_id=(x, y, 1 - c), device_id_type=MESH_ID)
        cp.start()
        cp.wait()

    return pl.pallas_call(
        body, in_specs=[HBM_SPEC], out_specs=HBM_SPEC, out_shape=jax.ShapeDtypeStruct(g.shape[1:], g.dtype),
        scratch_shapes=[pltpu.SemaphoreType.DMA, pltpu.SemaphoreType.DMA], name=name)(g)


def _chip_copies(p_ref, out_ref, send_sems, recv_sems):
    x, y, c = _position()
    copies = [pltpu.make_async_remote_copy(
        src_ref=p_ref.at[2 * px + py], dst_ref=out_ref.at[j], send_sem=send_sems.at[j], recv_sem=recv_sems.at[j],
        device_id=(px, py, c), device_id_type=MESH_ID) for j, (px, py) in enumerate([(1 - x, y), (x, 1 - y), (1 - x, 1 - y)])]

    def start():
        for cp in copies:
            cp.start()

    def finish():
        for cp in copies:
            cp.wait()

    return start, finish


def chip_exchange(name, p):
    def body(p_ref, out_ref, send_sems, recv_sems):
        start, finish = _chip_copies(p_ref, out_ref, send_sems, recv_sems)
        start()
        finish()

    return pl.pallas_call(
        body, in_specs=[HBM_SPEC], out_specs=HBM_SPEC, out_shape=jax.ShapeDtypeStruct((3,) + p.shape[1:], p.dtype),
        scratch_shapes=[pltpu.SemaphoreType.DMA((3,)), pltpu.SemaphoreType.DMA((3,))], name=name)(p)


def _sum_tile(r):
    return _divisor(r, 512, 16) if r % 16 == 0 else r


def pair_reduce(name, g, got, my_c, my_chip, wire):
    _, nchip, r, c_ = g.shape
    tile = _sum_tile(r)

    def body(ids, g_ref, got_ref, p_ref, mine_ref):
        s = g_ref[...].astype(F32) + got_ref[...].astype(F32)
        p_ref[...] = s.astype(wire)

        @pl.when(pl.program_id(1) == ids[1])
        def _():
            mine_ref[...] = s

    return pl.pallas_call(
        body,
        grid_spec=pltpu.PrefetchScalarGridSpec(
            num_scalar_prefetch=1, grid=(r // tile, nchip),
            in_specs=[pl.BlockSpec((None, None, tile, c_), lambda i, k, ids: (ids[0], k, i, 0)),
                      pl.BlockSpec((None, tile, c_), lambda i, k, ids: (k, i, 0))],
            out_specs=[pl.BlockSpec((None, tile, c_), lambda i, k, ids: (k, i, 0)),
                       pl.BlockSpec((tile, c_), lambda i, k, ids: (i, 0))]),
        out_shape=[jax.ShapeDtypeStruct((nchip, r, c_), wire), jax.ShapeDtypeStruct((r, c_), F32)],
        compiler_params=_params(("arbitrary", "arbitrary")), name=name)(
        jnp.stack([my_c, my_chip]).astype(jnp.int32), g, got)


def chip_reduce(name, mine, got):
    r, c_ = mine.shape
    tile = _sum_tile(r)

    def body(m_ref, got_ref, o_ref):
        o_ref[...] = ((m_ref[...] + got_ref[0].astype(F32)) + got_ref[1].astype(F32)) + got_ref[2].astype(F32)

    return pl.pallas_call(
        body, grid=(r // tile,),
        in_specs=[pl.BlockSpec((tile, c_), lambda i: (i, 0)), pl.BlockSpec((3, tile, c_), lambda i: (0, i, 0))],
        out_specs=pl.BlockSpec((tile, c_), lambda i: (i, 0)), out_shape=jax.ShapeDtypeStruct((r, c_), F32),
        compiler_params=_params(("arbitrary",)), name=name)(mine, got)


def sum_blocks(name, a):
    n, r, c_ = a.shape
    tile = _sum_tile(r)

    def body(a_ref, o_ref):
        acc = a_ref[0]
        for k in range(1, n):
            acc = acc + a_ref[k]
        o_ref[...] = acc

    return pl.pallas_call(
        body, grid=(r // tile,), in_specs=[pl.BlockSpec((n, tile, c_), lambda i: (0, i, 0))],
        out_specs=pl.BlockSpec((tile, c_), lambda i: (i, 0)), out_shape=jax.ShapeDtypeStruct((r, c_), F32),
        compiler_params=_params(("arbitrary",)), name=name)(a)


GROUPS = [
    ("rows1024", ["ssd_w_out", "conv_w_out", "mla_w_o", "w_out", "xattn_w_q", "xattn_w_o", "ffn_w_out"], BF16, 1024),
    ("w_in", ["w_in"], BF16, 1114),
    ("ffn_w_in", ["ffn_w_in"], BF16, 704),
    ("cols256", ["xattn_w_kv", "mla_w_kv_b"], BF16, 256),
    ("w_q_b", ["mla_w_q_b"], BF16, 192),
    ("small", ["ssd_conv_w", "conv_dw_w", "gate_b"], F32, 128),
]
HOSTED, UPFRONT = GROUPS[:3], GROUPS[3:]
PACK_COLS = 1024


def _pack(arrays):
    flat = jnp.concatenate([a.reshape(-1) for a in arrays])
    rows = -(-flat.shape[0] // PACK_COLS)
    rows += -rows % 8
    return jnp.pad(flat, (0, rows * PACK_COLS - flat.shape[0])).reshape(rows, PACK_COLS)


def _unpack(buf, shapes):
    flat = buf.reshape(-1)
    out, off = [], 0
    for s in shapes:
        n = 1
        for d in s:
            n *= d
        out.append(flat[off:off + n].reshape(tuple(s)))
        off += n
    return out


def _stack_rows(arrays, width, lead):
    return jnp.concatenate([a.reshape(a.shape[:lead] + (-1, width)) for a in arrays], axis=lead)


def _unstack_rows(buf, shapes, width, lead):
    out, off = [], 0
    for s in shapes:
        n = 1
        for d in s:
            n *= d
        rows = n // width
        idx = (slice(None),) * lead + (slice(off, off + rows),)
        out.append(buf[idx].reshape(buf.shape[:lead] + tuple(s)))
        off += rows
    return out


def _join_shards(blocks, axis):
    ax = axis + 1
    moved = jnp.moveaxis(blocks, 0, ax)
    s = moved.shape
    return moved.reshape(s[:ax] + (s[ax] * s[ax + 1],) + s[ax + 2:])


def _split_by_owner(full, axis):
    ax = axis + 1
    s = full.shape
    cut = full.reshape(s[:ax] + (2, 2, 2, s[ax] // N_DEV) + s[ax + 1:])
    cut = jnp.moveaxis(cut, (ax + 2, ax, ax + 1), (0, 1, 2))
    return cut.reshape((2, 4) + cut.shape[3:])


def _blocks_by_owner(blocks):
    cut = blocks.reshape((2, 2, 2) + blocks.shape[1:])
    return jnp.moveaxis(cut, 2, 0).reshape((2, 4) + blocks.shape[1:])


def _pad_cols(w, n):
    return jnp.pad(w, ((0, 0), (0, n - w.shape[1])))


def _regroup_cols(srcs, widths):
    starts = [0]
    for s in srcs:
        starts.append(starts[-1] + s.shape[1])
    assert starts[-1] == sum(widths)
    out, lo = [], 0
    for wd in widths:
        hi = lo + wd
        parts = []
        for s, a, b in zip(srcs, starts[:-1], starts[1:]):
            u, v = max(lo, a), min(hi, b)
            if u < v:
                parts.append(s[:, u - a:v - a])
        out.append(parts[0] if len(parts) == 1 else jnp.concatenate(parts, axis=1))
        lo = hi
    return out


COL_BLOCKED = ("w_in", "ffn_w_in", "xattn_w_kv", "mla_w_kv_b", "mla_w_q_b")
W_IN_PIECES = (1024, 2048, 16, 2048, 384, MLA_KV_RANK, MLA_ROPE, 3072)


def _prep_layer(w):
    w_z, w_xbc, w_dt, w_glu, w_q, w_ckv, w_kr, w_gate = _regroup_cols(list(w["w_in"]), W_IN_PIECES)
    w_ffn_gate, w_ffn_up = _regroup_cols(list(w["ffn_w_in"]), (FFN_HIDDEN, FFN_HIDDEN))
    w_xk, w_xv = _regroup_cols(list(w["xattn_w_kv"]), (D_MODEL, D_MODEL))
    q, kv = w["mla_w_q_b"], w["mla_w_kv_b"]

    def row(v):
        return v.reshape(1, -1)

    def norm_pair(g):
        return _pad_cols(row(g), 2 * LANES)

    return {
        "mix_norm_g": row(w["mix_norm_g"]),
        "w_z": w_z, "w_xbc": w_xbc, "w_dt": _pad_cols(w_dt, LANES), "w_glu": w_glu, "w_q": w_q, "w_ckv": w_ckv,
        "w_kr": _pad_cols(w_kr, LANES), "w_gate": w_gate,
        "ssd_conv_w": w["ssd_conv_w"], "ssd_conv_b": row(w["ssd_conv_b"]),
        "ssd_dt_bias": _pad_cols(row(w["ssd_dt_bias"]), LANES), "ssd_a_log": _pad_cols(row(w["ssd_a_log"]), LANES),
        "ssd_d": _pad_cols(row(w["ssd_d"]), LANES), "ssd_norm_g": row(w["ssd_norm_g"]), "ssd_w_out": w["ssd_w_out"],
        "conv_dw_w": w["conv_dw_w"], "conv_dw_b": row(w["conv_dw_b"]), "conv_ln_g": row(w["conv_ln_g"]),
        "conv_ln_b": row(w["conv_ln_b"]), "conv_w_out": w["conv_w_out"],
        "mla_q_a_g": row(w["mla_q_a_g"]), "mla_kv_a_g": row(w["mla_kv_a_g"]),
        "w_qn": jnp.concatenate([q[h, :, :MLA_NOPE] for h in range(MLA_HEADS)], axis=1),
        "w_qr": jnp.concatenate([_pad_cols(q[h, :, MLA_NOPE:], LANES) for h in range(MLA_HEADS)], axis=1),
        "w_kn": jnp.concatenate([kv[h, :, :MLA_NOPE] for h in range(MLA_HEADS)], axis=1),
        "w_v": jnp.concatenate([kv[h, :, MLA_NOPE:] for h in range(MLA_HEADS)], axis=1),
        "mla_q_norm_g": norm_pair(w["mla_q_norm_g"]), "mla_k_norm_g": norm_pair(w["mla_k_norm_g"]),
        "mla_w_o": w["mla_w_o"], "gate_b": row(w["gate_b"]), "w_out": w["w_out"],
        "xattn_norm_g": row(w["xattn_norm_g"]), "mem_norm_g": row(w["mem_norm_g"]), "xattn_w_q": w["xattn_w_q"],
        "w_xk": w_xk, "w_xv": w_xv,
        "xattn_q_norm_g": row(w["xattn_q_norm_g"]), "xattn_k_norm_g": row(w["xattn_k_norm_g"]),
        "xattn_w_o": w["xattn_w_o"], "ffn_norm_g": row(w["ffn_norm_g"]),
        "w_ffn_gate": w_ffn_gate, "w_ffn_up": w_ffn_up, "ffn_w_out": w["ffn_w_out"],
    }


def _unprep_grads(g):
    n_dt, n_kr = IN_SIZES[2], MLA_ROPE
    flat = lambda v: v.reshape(-1)

    def blocks(srcs):
        total = sum(s.shape[1] for s in srcs)
        return jnp.stack(_regroup_cols(srcs, (total // N_DEV,) * N_DEV))

    def head(a, h, n=LANES):
        return a[:, h * LANES:h * LANES + n]

    return {
        "mix_norm_g": flat(g["mix_norm_g"]),
        "w_in": blocks([g["w_z"], g["w_xbc"], g["w_dt"][:, :n_dt], g["w_glu"], g["w_q"], g["w_ckv"],
                        g["w_kr"][:, :n_kr], g["w_gate"]]),
        "ssd_conv_w": g["ssd_conv_w"], "ssd_conv_b": flat(g["ssd_conv_b"]),
        "ssd_dt_bias": flat(g["ssd_dt_bias"])[:SSD_HEADS], "ssd_a_log": flat(g["ssd_a_log"])[:SSD_HEADS],
        "ssd_d": flat(g["ssd_d"])[:SSD_HEADS], "ssd_norm_g": flat(g["ssd_norm_g"]), "ssd_w_out": g["ssd_w_out"],
        "conv_dw_w": g["conv_dw_w"], "conv_dw_b": flat(g["conv_dw_b"]), "conv_ln_g": flat(g["conv_ln_g"]),
        "conv_ln_b": flat(g["conv_ln_b"]), "conv_w_out": g["conv_w_out"],
        "mla_q_a_g": flat(g["mla_q_a_g"]),
        "mla_w_q_b": jnp.stack([jnp.concatenate([head(g["w_qn"], h), head(g["w_qr"], h, MLA_ROPE)], axis=1)
                                for h in range(MLA_HEADS)]),
        "mla_kv_a_g": flat(g["mla_kv_a_g"]),
        "mla_w_kv_b": jnp.stack([jnp.concatenate([head(g["w_kn"], h), head(g["w_v"], h)], axis=1)
                                 for h in range(MLA_HEADS)]),
        "mla_q_norm_g": flat(g["mla_q_norm_g"])[:MLA_NOPE + MLA_ROPE],
        "mla_k_norm_g": flat(g["mla_k_norm_g"])[:MLA_NOPE + MLA_ROPE],
        "mla_w_o": g["mla_w_o"], "gate_b": g["gate_b"].reshape(3, D_MODEL), "w_out": g["w_out"],
        "xattn_norm_g": flat(g["xattn_norm_g"]), "mem_norm_g": flat(g["mem_norm_g"]), "xattn_w_q": g["xattn_w_q"],
        "xattn_w_kv": blocks([g["w_xk"], g["w_xv"]]),
        "xattn_q_norm_g": flat(g["xattn_q_norm_g"]), "xattn_k_norm_g": flat(g["xattn_k_norm_g"]),
        "xattn_w_o": g["xattn_w_o"], "ffn_norm_g": flat(g["ffn_norm_g"]),
        "ffn_w_in": blocks([g["w_ffn_gate"], g["w_ffn_up"]]), "ffn_w_out": g["ffn_w_out"],
    }


def _layer(l, x, mem, cosf, sinf, w, hosted=(), carriers=()):
    t = x.shape[0]
    n = lambda s: f"l{l}_{s}"
    tile = min(ROW_TILE, t)
    grid = (1, t // tile)

    def rowwise(name, f, ins, width, to_matmul=False):
        return tmap(n(name), f, grid, ins, [_row_out(t, width, tile)], narrow=(0,) if to_matmul else ())[0]

    u, x = rms_norm_through(n("mix_norm"), x, w["mix_norm_g"])
    in_keys = ["z", "xbc", "dt", "glu", "q", "ckv", "kr", "gate"]
    z, xbc, dtr, glu, q_lat, c_kv, kr_raw, gate_logits = multi_matmul(
        n("in"), u, [w["w_" + k] for k in in_keys], in_keys, to_tmap=("z", "glu", "q", "ckv", "kr", "gate"))

    xc = dwconv(n("ssd_conv"), xbc, w["ssd_conv_w"], w["ssd_conv_b"])
    y_scan = ssd_scan(n("ssd_scan"), xc, dtr, w["ssd_dt_bias"], w["ssd_a_log"], w["ssd_d"])
    y_norm = rowwise("ssd_gate_norm", _ssd_gate_norm_f, [_rows(y_scan, tile), _rows(z, tile), _whole(w["ssd_norm_g"])],
                     SSD_INNER, to_matmul=True)
    y_ssd = matmul(n("ssd_out"), y_norm, w["ssd_w_out"], to_tmap=True)

    v = rowwise("glu", _glu_f, [_rows(glu, tile)], D_MODEL)
    v = dwconv(n("conv_dw"), v, w["conv_dw_w"], w["conv_dw_b"])
    v = rowwise("conv_ln_silu", _ln_silu_f, [_rows(v, tile), _whole(w["conv_ln_g"]), _whole(w["conv_ln_b"])], D_MODEL,
                to_matmul=True)
    y_conv = matmul(n("conv_out"), v, w["conv_w_out"], to_tmap=True)

    q_n = rms_norm(n("q_a_norm"), q_lat, w["mla_q_a_g"])
    qn_raw, qr_raw = multi_matmul(n("q"), q_n, [w["w_qn"], w["w_qr"]], ["nope", "rope"], to_tmap=("nope", "rope"))
    c_n = rms_norm(n("kv_a_norm"), c_kv, w["mla_kv_a_g"])
    kn_raw, val = multi_matmul(n("kv"), c_n, [w["w_kn"], w["w_v"]], ["nope", "v"], to_tmap=("nope",))
    tables = [_rows(cosf, tile, "n"), _rows(sinf, tile, "n")]
    kn = rowwise("k_nope_norm", _k_nope_f, [_rows(kn_raw, tile), _whole(w["mla_k_norm_g"])], MLA_HEADS * MLA_NOPE)
    kr = rowwise("k_rope", _k_rope_f, [_rows(kr_raw, tile)] + tables + [_whole(w["mla_k_norm_g"])], LANES)
    wide = _row_out(t, MLA_HEADS * LANES, tile)
    qn, qr = tmap(n("q_prep"), _q_prep_f, grid,
                  [_rows(qn_raw, tile), _rows(qr_raw, tile)] + tables + [_whole(w["mla_q_norm_g"])], [wide, wide])
    att, gathered, handles = mla_attention(n("mla_attn"), qn, qr, kn, kr, val, hosted, carriers)
    y_mla = matmul(n("mla_out"), att, w["mla_w_o"], to_tmap=True)

    merged = rowwise("merge", _merge_f, [_rows(gate_logits, tile), _whole(w["gate_b"]), _rows(y_ssd, tile),
                                         _rows(y_conv, tile), _rows(y_mla, tile)], D_MODEL, to_matmul=True)
    x = matmul(n("mix_out"), merged, w["w_out"], res=x)

    h, x = rms_norm_through(n("xattn_norm"), x, w["xattn_norm_g"])
    mem_n = rms_norm_nograd_x(n("mem_norm"), mem, w["mem_norm_g"])
    xq = matmul(n("xattn_q"), h, w["xattn_w_q"], to_tmap=True)
    xk, xv = multi_matmul(n("xattn_kv"), mem_n, [w["w_xk"], w["w_xv"]], ["k", "v"])
    m = mem.shape[0]
    txq = min(XATT_Q_TILE, t)
    kv_head = lambda arr: (arr, (m, X_HEAD_DIM), lambda o, i: (0, o), "ai")
    xo = tmap(n("xattn"), _xattn_f, (X_HEADS, t // txq),
              [(xq, (txq, X_HEAD_DIM), lambda o, i: (i, o), "t"), kv_head(xk), kv_head(xv),
               _whole(w["xattn_q_norm_g"]), _whole(w["xattn_k_norm_g"])],
              [((t, D_MODEL), (txq, X_HEAD_DIM), lambda o, i: (i, o))], narrow=(0,))[0]
    x = matmul(n("xattn_out"), xo, w["xattn_w_o"], res=x)

    h, x = rms_norm_through(n("ffn_norm"), x, w["ffn_norm_g"])
    gate, up = multi_matmul(n("ffn_in"), h, [w["w_ffn_gate"], w["w_ffn_up"]], ["gate", "up"], to_tmap=("gate", "up"))
    act = rowwise("swiglu", _swiglu_f, [_rows(gate, tile), _rows(up, tile)], FFN_HIDDEN, to_matmul=True)
    return matmul(n("ffn_out"), act, w["ffn_w_out"], res=x), gathered, handles


def _rope_tables(positions):
    inv = ROPE_THETA ** (-jnp.arange(0, MLA_ROPE, 2, dtype=F32) / MLA_ROPE)
    ang = positions.astype(F32)[:, None] * inv
    pad = jnp.zeros((positions.shape[0], LANES - MLA_ROPE), F32)
    cos, sin = jnp.cos(ang), jnp.sin(ang)
    return jnp.concatenate([cos, cos, pad], axis=1), jnp.concatenate([sin, sin, pad], axis=1)


def local_step(x, mem, positions, target, weights, gathered0, later, shard_shapes, exchange=None):
    cosf, sinf = _rope_tables(positions)

    def layer_weights(l, big):
        w = {k: v[:, l] if k in COL_BLOCKED else v[l] for k, v in weights.items()}
        for (_, names, _, width), stack in zip(HOSTED, big):
            for n, b in zip(names, _unstack_rows(stack, [shard_shapes[n] for n in names], width, 1)):
                w[n] = b if n in COL_BLOCKED else b.reshape((-1,) + b.shape[2:])
        return w

    diff = [{k: jnp.zeros(v.shape, F32) if k in MATRICES else v
             for k, v in _prep_layer(layer_weights(l, gathered0)).items()} for l in range(DEPTH)]

    def layer_fn(l, big, hosted):
        def f(x, d, carriers):
            mats = _prep_layer(layer_weights(l, big))
            w = {k: Mat(mats[k], s) if k in MATRICES else s for k, s in d.items()}
            y, gathered, handles = _layer(l, x, mem, cosf, sinf, w, hosted, carriers)
            return (y, handles), gathered
        return f

    pulls, big = [], gathered0
    for l in range(DEPTH):
        inner = l + 1 < DEPTH
        carriers = tuple(jnp.zeros((3,) + s.shape, s.dtype) for s in later[l]) if inner and exchange else ()
        (x, _), pull, big = jax.vjp(layer_fn(l, big, later[l] if inner else ()), x, diff[l], carriers, has_aux=True)
        pulls.append(pull)
    sq, g = loss_head(x, target)

    per_layer, reduced, sent, mine = [None] * DEPTH, [None] * DEPTH, (), None
    for l in reversed(range(DEPTH)):
        g, gd, got = pulls[l]((g, tuple(sent)))
        if mine is not None:
            reduced[l + 1] = exchange.finish(mine, got)
        per_layer[l] = _unprep_grads(gd)
        if exchange:
            sent, mine = exchange.begin(per_layer[l])
    if exchange:
        reduced[0] = exchange.finish(mine, [chip_exchange(f"chip_exchange_first_layer_{k}", p)
                                            for k, p in enumerate(sent)])
    grads = {k: jnp.stack([pl_[k] for pl_ in per_layer], axis=1 if k in COL_BLOCKED else 0) for k in WEIGHTS}
    shards = {k: jnp.stack([r[k] for r in reduced]) for k in reduced[0]} if exchange else None
    return sq, g, grads, shards


def _step(x, mem, positions, loss_target, w, m, v):
    xi, yi, ci = _position()

    full = {n: w[n] for n in REPLICATED}
    for gname, names, wire, width in UPFRONT:
        shapes = [w[n].shape for n in names]
        stacked = _stack_rows([w[n] for n in names], width, 0).astype(wire)
        gathered = all_gather("gather_" + gname, stacked)
        for n, b in zip(names, _unstack_rows(gathered, shapes, width, 1)):
            full[n] = b if n in COL_BLOCKED else _join_shards(b, SHARDED[n])
    shards = [[_stack_rows([w[n][l] for n in names], width, 0).astype(wire) for _, names, wire, width in HOSTED]
              for l in range(DEPTH)]
    gathered0 = [all_gather("gather0_" + g[0], s) for g, s in zip(HOSTED, shards[0])]
    shard_shapes = {n: w[n].shape[1:] for g in HOSTED for n in g[1]}

    def to_sibling(groups, grads):
        sent, mine = [], []
        for gname, names, wire, width in groups:
            by_owner = _stack_rows([_blocks_by_owner(grads[n]) if n in COL_BLOCKED else
                                    _split_by_owner(grads[n], SHARDED[n]) for n in names], width, 2).astype(wire)
            from_sibling = pair_exchange("pair_exchange_" + gname, by_owner)
            p, own = pair_reduce("pair_reduce_" + gname, by_owner, from_sibling, ci, 2 * xi + yi, wire)
            sent.append(p)
            mine.append(own)
        return sent, mine

    def from_chips(groups, shapes, mine, got):
        out = {}
        for (gname, names, _, width), own, arrived in zip(groups, mine, got):
            reduced = chip_reduce("chip_reduce_" + gname, own, arrived)
            out.update(zip(names, _unstack_rows(reduced, [shapes[n] for n in names], width, 0)))
        return out

    class LayerExchange:
        @staticmethod
        def begin(layer_grads):
            return to_sibling(HOSTED, {n: g[:, None] if n in COL_BLOCKED else g[None] for n, g in layer_grads.items()
                                       if n in shard_shapes})

        @staticmethod
        def finish(mine, got):
            return from_chips(HOSTED, shard_shapes, mine, got)

    sq, gx, grads, g_shard = local_step(x[0], mem[0], positions[0], loss_target[0], full, gathered0, shards[1:],
                                        shard_shapes, LayerExchange)
    loss = lax.psum(0.5 * jnp.sum(sq) / D_MODEL, ("x", "y", "c"))

    sent, mine = to_sibling(UPFRONT, grads)
    got = [chip_exchange("chip_exchange_" + g[0], p) for g, p in zip(UPFRONT, sent)]
    g_shard.update(from_chips(UPFRONT, {n: w[n].shape for g in UPFRONT for n in g[1]}, mine, got))

    rep_shapes = [w[n].shape for n in REPLICATED]
    rep_all = all_gather("small_grads_all_gather", _pack([grads[n] for n in REPLICATED]))
    g_rep = dict(zip(REPLICATED, _unpack(sum_blocks("small_grads_sum", rep_all), rep_shapes)))

    out_g, out_d, out_m, out_v = [], [], [], []
    for n in WEIGHTS:
        g = g_shard[n] if n in SHARDED else g_rep[n]
        d, nm, nv = adamw("adamw_" + n, w[n], g, m[n], v[n])
        out_g.append(g)
        out_d.append(d)
        out_m.append(nm)
        out_v.append(nv)
    return (loss, gx[None], *out_g, *out_d, *out_m, *out_v)


def kernel(x, mem, positions, mix_norm_g, w_in, ssd_conv_w, ssd_conv_b, ssd_dt_bias, ssd_a_log, ssd_d, ssd_norm_g, ssd_w_out, conv_dw_w, conv_dw_b, conv_ln_g, conv_ln_b, conv_w_out, mla_q_a_g, mla_w_q_b, mla_kv_a_g, mla_w_kv_b, mla_q_norm_g, mla_k_norm_g, mla_w_o, gate_b, w_out, xattn_norm_g, mem_norm_g, xattn_w_q, xattn_w_kv, xattn_q_norm_g, xattn_k_norm_g, xattn_w_o, ffn_norm_g, ffn_w_in, ffn_w_out, loss_target, m_mix_norm_g, m_w_in, m_ssd_conv_w, m_ssd_conv_b, m_ssd_dt_bias, m_ssd_a_log, m_ssd_d, m_ssd_norm_g, m_ssd_w_out, m_conv_dw_w, m_conv_dw_b, m_conv_ln_g, m_conv_ln_b, m_conv_w_out, m_mla_q_a_g, m_mla_w_q_b, m_mla_kv_a_g, m_mla_w_kv_b, m_mla_q_norm_g, m_mla_k_norm_g, m_mla_w_o, m_gate_b, m_w_out, m_xattn_norm_g, m_mem_norm_g, m_xattn_w_q, m_xattn_w_kv, m_xattn_q_norm_g, m_xattn_k_norm_g, m_xattn_w_o, m_ffn_norm_g, m_ffn_w_in, m_ffn_w_out, v_mix_norm_g, v_w_in, v_ssd_conv_w, v_ssd_conv_b, v_ssd_dt_bias, v_ssd_a_log, v_ssd_d, v_ssd_norm_g, v_ssd_w_out, v_conv_dw_w, v_conv_dw_b, v_conv_ln_g, v_conv_ln_b, v_conv_w_out, v_mla_q_a_g, v_mla_w_q_b, v_mla_kv_a_g, v_mla_w_kv_b, v_mla_q_norm_g, v_mla_k_norm_g, v_mla_w_o, v_gate_b, v_w_out, v_xattn_norm_g, v_mem_norm_g, v_xattn_w_q, v_xattn_w_kv, v_xattn_q_norm_g, v_xattn_k_norm_g, v_xattn_w_o, v_ffn_norm_g, v_ffn_w_in, v_ffn_w_out):
    args = locals()
    w = {n: args[n] for n in WEIGHTS}
    m = {n: args["m_" + n] for n in WEIGHTS}
    v = {n: args["v_" + n] for n in WEIGHTS}
    return _step(x, mem, positions, loss_target, w, m, v)
```

```python
from typing import NamedTuple

import jax
import jax.numpy as jnp
from jax import lax
from jax.experimental import pallas as pl
from jax.experimental.pallas import tpu as pltpu

F32 = jnp.float32
BF16 = jnp.bfloat16
HIGHEST = lax.Precision.HIGHEST
MESH_ID = pl.DeviceIdType.MESH

VMEM_LIMIT_BYTES = 56 * 1024 * 1024
LANES = 128

EPS = 1e-6
DEPTH = 4
D_MODEL = 1024
N_DEV = 8
SSD_HEADS = 16
SSD_HEAD_DIM = 64
SSD_STATE = 128
SSD_GROUPS = 4
SSD_INNER = 1024
SSD_TILE = 256
CONV_K = 31
SSD_CONV_K = 4
MLA_HEADS = 8
MLA_NOPE = 128
MLA_ROPE = 64
MLA_V = 128
MLA_Q_RANK = 384
MLA_KV_RANK = 256
ATT_CHUNK = 64
ROPE_THETA = 10000.0
X_HEADS = 4
X_HEAD_DIM = 256
FFN_HIDDEN = 2816
IN_SIZES = (1024, 2048, 16, 2048, 384, 320, 3072)

ADAM_LR = 0.001
ADAM_B1 = 0.9
ADAM_B2 = 0.999
ADAM_EPS = 1e-08
ADAM_WD = 0.01
ADAM_STEP = 10

MM_FULL_K = 3072
ROW_TILE = 256
ATT_BLOCK = 512
XATT_Q_TILE = 512

SHARDED = {
    "w_in": 1, "ssd_conv_w": 1, "ssd_w_out": 0, "conv_dw_w": 1, "conv_w_out": 0, "mla_w_q_b": 1, "mla_w_kv_b": 1,
    "mla_w_o": 0, "gate_b": 1, "w_out": 0, "xattn_w_q": 0, "xattn_w_kv": 1, "xattn_w_o": 0, "ffn_w_in": 1,
    "ffn_w_out": 0,
}
WEIGHTS = ["mix_norm_g", "w_in", "ssd_conv_w", "ssd_conv_b", "ssd_dt_bias", "ssd_a_log", "ssd_d", "ssd_norm_g",
           "ssd_w_out", "conv_dw_w", "conv_dw_b", "conv_ln_g", "conv_ln_b", "conv_w_out", "mla_q_a_g", "mla_w_q_b",
           "mla_kv_a_g", "mla_w_kv_b", "mla_q_norm_g", "mla_k_norm_g", "mla_w_o", "gate_b", "w_out", "xattn_norm_g",
           "mem_norm_g", "xattn_w_q", "xattn_w_kv", "xattn_q_norm_g", "xattn_k_norm_g", "xattn_w_o", "ffn_norm_g",
           "ffn_w_in", "ffn_w_out"]
REPLICATED = [n for n in WEIGHTS if n not in SHARDED]


def _params(sem=None):
    return pltpu.CompilerParams(dimension_semantics=sem, vmem_limit_bytes=VMEM_LIMIT_BYTES)


def _divisor(n, cap, mult):
    if n <= cap:
        return n
    for d in range(cap - cap % mult, 0, -mult):
        if n % d == 0:
            return d
    raise ValueError(f"no tile for {n}")


def _dg(a, b, ca, cb):
    return lax.dot_general(a.astype(BF16), b.astype(BF16), (((ca,), (cb,)), ((), ())), preferred_element_type=F32)


@jax.custom_vjp
def bdot(a, b):
    return _dg(a, b, 1, 0)


bdot.defvjp(lambda a, b: (_dg(a, b, 1, 0), (a, b)), lambda r, g: (_dg(g, r[1], 1, 1), _dg(r[0], g, 0, 0)))


@jax.custom_vjp
def bdot_nt(a, b):
    return _dg(a, b, 1, 1)


bdot_nt.defvjp(lambda a, b: (_dg(a, b, 1, 1), (a, b)), lambda r, g: (_dg(g, r[1], 1, 0), _dg(g, r[0], 0, 0)))


@jax.custom_vjp
def bdot_tn(a, b):
    return _dg(a, b, 0, 0)


bdot_tn.defvjp(lambda a, b: (_dg(a, b, 0, 0), (a, b)), lambda r, g: (_dg(r[1], g, 1, 1), _dg(r[0], g, 1, 0)))


def hdot(a, b):
    return jnp.dot(a, b, precision=HIGHEST, preferred_element_type=F32)


def _iota(shape, dim):
    return lax.broadcasted_iota(jnp.int32, shape, dim)


def _mm(name, a, b, ta=False, tb=False, res=None, narrow_out=False):
    m, k = (a.shape[1], a.shape[0]) if ta else a.shape
    n = b.shape[0] if tb else b.shape[1]
    tm = _divisor(m, 1536, LANES) if ta else _divisor(m, 1024, 8)
    tn = _divisor(n, 1536, LANES)
    tk = k if k <= MM_FULL_K else _divisor(k, 1024, LANES)
    if tk == k and k > 1024:
        tm = _divisor(m, 512, LANES if ta else 8)
    nk = k // tk
    dims = (((0 if ta else 1,), (1 if tb else 0,)), ((), ()))

    out_dtype = BF16 if narrow_out else F32
    own_acc = narrow_out and nk > 1
    assert res is None or not narrow_out

    def body(*refs):
        if res is None:
            a_ref, b_ref, o_ref = refs[:3]
        else:
            a_ref, b_ref, r_ref, o_ref = refs[:4]
        acc = refs[-1] if own_acc else o_ref
        part = lax.dot_general(a_ref[...].astype(BF16), b_ref[...].astype(BF16), dims, preferred_element_type=F32)
        if nk == 1:
            o_ref[...] = (part if res is None else part + r_ref[...]).astype(out_dtype)
        else:
            kk = pl.program_id(2)

            @pl.when(kk == 0)
            def _():
                acc[...] = part if res is None else part + r_ref[...]

            @pl.when(kk != 0)
            def _():
                acc[...] += part

            if own_acc:
                @pl.when(kk == nk - 1)
                def _():
                    o_ref[...] = acc[...].astype(out_dtype)

    a_spec = pl.BlockSpec((tk, tm), lambda i, j, kk: (kk, i)) if ta else pl.BlockSpec((tm, tk), lambda i, j, kk: (i, kk))
    b_spec = pl.BlockSpec((tn, tk), lambda i, j, kk: (j, kk)) if tb else pl.BlockSpec((tk, tn), lambda i, j, kk: (kk, j))
    o_spec = pl.BlockSpec((tm, tn), lambda i, j, kk: (i, j))
    in_specs = [a_spec, b_spec] + ([] if res is None else [o_spec])
    args = (a, b) + (() if res is None else (res,))
    return pl.pallas_call(
        body, grid=(m // tm, n // tn, nk), in_specs=in_specs, out_specs=o_spec,
        out_shape=jax.ShapeDtypeStruct((m, n), out_dtype),
        scratch_shapes=[pltpu.VMEM((tm, tn), F32)] if own_acc else [],
        compiler_params=_params(("parallel", "parallel", "arbitrary")), name=name)(*args)


class Mat(NamedTuple):
    value: jax.Array
    slot: jax.Array


MATRICES = frozenset([
    "w_z", "w_xbc", "w_dt", "w_glu", "w_q", "w_ckv", "w_kr", "w_gate", "ssd_w_out", "conv_w_out", "w_qn", "w_qr",
    "w_kn", "w_v", "mla_w_o", "w_out", "xattn_w_q", "w_xk", "w_xv", "xattn_w_o", "w_ffn_gate", "w_ffn_up",
    "ffn_w_out"])


class Act(NamedTuple):
    value: jax.Array
    slot: jax.Array


def _operand(a):
    return (a.value, a.slot) if isinstance(a, Act) else (a, a)


class Out(NamedTuple):
    value: jax.Array
    handle: jax.Array


def _handle(out):
    return jnp.zeros(out.shape, BF16)


def matmul(name, a, mat, res=None, to_tmap=False):
    w, slot = mat
    a, a_slot = _operand(a)
    if res is None:
        @jax.custom_vjp
        def run(a, a_slot, w, slot):
            out = _mm(name, a, w)
            return Out(out, _handle(out)) if to_tmap else out

        def fwd(a, a_slot, w, slot):
            return run(a, a_slot, w, slot), (a, w)

        def bwd(r, g):
            g = g.handle if to_tmap else g
            return None, _mm(name + "_da", g, r[1], tb=True), None, _mm(name + "_dw", r[0], g, ta=True, narrow_out=True)

        run.defvjp(fwd, bwd)
        return run(a, a_slot, w, slot)

    @jax.custom_vjp
    def run_res(a, a_slot, w, slot, res):
        return _mm(name, a, w, res=res)

    def fwd_res(a, a_slot, w, slot, res):
        return run_res(a, a_slot, w, slot, res), (a, w)

    def bwd_res(r, g):
        return None, _mm(name + "_da", g, r[1], tb=True), None, _mm(name + "_dw", r[0], g, ta=True, narrow_out=True), g

    run_res.defvjp(fwd_res, bwd_res)
    return run_res(a, a_slot, w, slot, res)


def multi_matmul(name, a, mats, keys, to_tmap=()):
    ws, slots = tuple(m.value for m in mats), tuple(m.slot for m in mats)
    a, a_slot = _operand(a)

    @jax.custom_vjp
    def run(a, a_slot, ws, slots):
        outs = [_mm(f"{name}_{k}", a, w) for k, w in zip(keys, ws)]
        return tuple(Out(o, _handle(o)) if k in to_tmap else o for k, o in zip(keys, outs))

    def fwd(a, a_slot, ws, slots):
        return run(a, a_slot, ws, slots), (a, ws)

    def bwd(r, gs):
        a, ws = r
        gs = [g.handle if k in to_tmap else g for k, g in zip(keys, gs)]
        da = None
        for k, w, g in zip(keys, ws, gs):
            da = _mm(f"{name}_{k}_da", g, w, tb=True, res=da)
        dws = tuple(_mm(f"{name}_{k}_dw", a, g, ta=True, narrow_out=True) for k, g in zip(keys, gs))
        return None, da, tuple(None for _ in ws), dws

    run.defvjp(fwd, bwd)
    return run(a, a_slot, ws, slots)


def tmap(name, f, grid, ins, outs, through=None, narrow=()):
    handled = [k for k, x in enumerate(ins) if isinstance(x[0], Out)]
    assert all(ins[k][3] == "t" and k != through for k in handled)
    arrays = [x[0].value if isinstance(x[0], Out) else x[0] for x in ins] + [ins[k][0].handle for k in handled]
    kinds = [x[3] for x in ins]
    in_specs = [pl.BlockSpec(x[1], x[2]) for x in ins]
    out_specs = [pl.BlockSpec(x[1], x[2]) for x in outs]
    out_shape = [jax.ShapeDtypeStruct(x[0], BF16 if k in narrow else F32) for k, x in enumerate(outs)]
    n_in, n_out = len(ins), len(outs)
    n_through = 0 if through is None else 1
    assert through is None or kinds[through] == "t"
    didx = [k for k, kd in enumerate(kinds) if kd != "n"]

    def fwd_call(*arrs):
        def body(*refs):
            pids = (pl.program_id(0), pl.program_id(1))
            vals = f(pids, *[r[...] for r in refs[:n_in]])
            for r, v in zip(refs[n_in:], vals):
                r[...] = v.astype(r.dtype)

        return pl.pallas_call(body, grid=grid, in_specs=in_specs, out_specs=out_specs, out_shape=out_shape,
                              compiler_params=_params(("arbitrary", "arbitrary")), name=name)(*arrs)

    def bwd_call(arrs, cts):
        def body(*refs):
            o, i = pl.program_id(0), pl.program_id(1)
            vals = [r[...] for r in refs[:n_in]]

            def g(*dv):
                full = list(vals)
                for k, v in zip(didx, dv):
                    full[k] = v
                return tuple(f((o, i), *full))

            _, vjp = jax.vjp(g, *[vals[k] for k in didx])
            grads = vjp(tuple(r[...] for r in refs[n_in:n_in + n_out]))
            for k, gr, r in zip(didx, grads, refs[n_in + n_out + n_through:]):
                if k == through:
                    r[...] = gr + refs[n_in + n_out][...]
                elif kinds[k] == "t":
                    r[...] = gr.astype(r.dtype)
                else:
                    first = (i == 0) if kinds[k] == "ai" else jnp.logical_and(o == 0, i == 0)

                    @pl.when(first)
                    def _(r=r, gr=gr):
                        r[...] = gr

                    @pl.when(jnp.logical_not(first))
                    def _(r=r, gr=gr):
                        r[...] += gr

        g_specs = [in_specs[k] for k in didx]
        g_shape = [jax.ShapeDtypeStruct(arrs[k].shape, BF16 if k in handled else F32) for k in didx]
        ct_specs = out_specs + ([in_specs[through]] if n_through else [])
        return pl.pallas_call(body, grid=grid, in_specs=in_specs + ct_specs, out_specs=g_specs, out_shape=g_shape,
                              compiler_params=_params(("arbitrary", "arbitrary")), name=name + "_bwd")(*arrs, *cts)

    @jax.custom_vjp
    def run(*arrs):
        res = [Act(o, jnp.zeros(o.shape, F32)) if k in narrow else o for k, o in enumerate(fwd_call(*arrs[:n_in]))]
        return tuple(res) + ((arrs[through],) if n_through else ())

    def run_fwd(*arrs):
        return run(*arrs), arrs[:n_in]

    def run_bwd(arrs, cts):
        cts = [c.slot if isinstance(c, Act) else c for c in cts]
        gs = bwd_call(arrs, cts)
        full = [None] * (n_in + len(handled))
        for k, g in zip(didx, gs):
            full[n_in + handled.index(k) if k in handled else k] = g
        return tuple(full)

    run.defvjp(run_fwd, run_bwd)
    return run(*arrays)


def _rows(arr, tile, kind="t"):
    width = (arr.value if isinstance(arr, Out) else arr).shape[1]
    return (arr, (tile, width), lambda o, i: (i, 0), kind)


def _whole(arr, kind="ag"):
    return (arr, arr.shape, lambda o, i: (0, 0), kind)


def _row_out(t, n, tile):
    return ((t, n), (tile, n), lambda o, i: (i, 0))


def _rms(x, g, n=None):
    ms = jnp.sum(x * x, axis=-1, keepdims=True) / (x.shape[-1] if n is None else n)
    return x * lax.rsqrt(ms + EPS) * g


def rms_norm(name, x, g):
    t, n = (x.value if isinstance(x, Out) else x).shape
    tile = min(ROW_TILE, t)
    return tmap(name, lambda p, x, g: (_rms(x, g),), (1, t // tile), [_rows(x, tile), _whole(g)],
                [_row_out(t, n, tile)], narrow=(0,))[0]


def rms_norm_through(name, x, g):
    t, n = (x.value if isinstance(x, Out) else x).shape
    tile = min(ROW_TILE, t)
    return tmap(name, lambda p, x, g: (_rms(x, g),), (1, t // tile), [_rows(x, tile), _whole(g)],
                [_row_out(t, n, tile)], through=0, narrow=(0,))


def rms_norm_nograd_x(name, x, g):
    t, n = (x.value if isinstance(x, Out) else x).shape
    tile = min(ROW_TILE, t)
    return tmap(name, lambda p, x, g: (_rms(x, g),), (1, t // tile), [_rows(x, tile, "n"), _whole(g)],
                [_row_out(t, n, tile)], narrow=(0,))[0]


def _glu_f(p, glu):
    h = glu.shape[1] // 2
    return (glu[:, :h] * jax.nn.sigmoid(glu[:, h:]),)


def _ln_silu_f(p, v, g, b):
    mu = jnp.mean(v, axis=-1, keepdims=True)
    xc = v - mu
    var = jnp.mean(xc * xc, axis=-1, keepdims=True)
    return (jax.nn.silu(xc * lax.rsqrt(var + EPS) * g + b),)


def _ssd_gate_norm_f(p, y, z, g):
    v = y * jax.nn.silu(z)
    w = SSD_INNER // SSD_GROUPS
    parts = []
    for k in range(SSD_GROUPS):
        vg = v[:, k * w:(k + 1) * w]
        parts.append(vg * lax.rsqrt(jnp.mean(vg * vg, axis=-1, keepdims=True) + EPS))
    return (jnp.concatenate(parts, axis=1) * g,)


def _merge_f(p, gl, gb, y0, y1, y2):
    g = jax.nn.sigmoid(gl + gb)
    d = D_MODEL
    return (g[:, :d] * y0 + g[:, d:2 * d] * y1 + g[:, 2 * d:] * y2,)


def _swiglu_f(p, gate, up):
    return (jax.nn.silu(gate) * up,)


def _rot_matrix():
    r, c = _iota((LANES, LANES), 0), _iota((LANES, LANES), 1)
    h = MLA_ROPE // 2
    plus = jnp.logical_and(c >= h, jnp.logical_and(c < 2 * h, r == c - h))
    minus = jnp.logical_and(c < h, r == c + h)
    return plus.astype(F32) - minus.astype(F32)


def _rope(x, cosf, sinf):
    return x * cosf + hdot(x, _rot_matrix()) * sinf


def _per_head(f, x):
    return jnp.concatenate([f(x[:, h * LANES:(h + 1) * LANES]) for h in range(x.shape[1] // LANES)], axis=1)


def _k_nope_f(p, kn_raw, kg):
    return (_per_head(lambda x: _rms(x, kg[:, :MLA_NOPE]), kn_raw),)


def _k_rope_f(p, kr_raw, cosf, sinf, kg):
    return (_rope(_rms(kr_raw, kg[:, MLA_NOPE:], n=MLA_ROPE), cosf, sinf),)


def _q_prep_f(p, qn_raw, qr_raw, cosf, sinf, qg):
    qn = _per_head(lambda x: _rms(x, qg[:, :MLA_NOPE]), qn_raw)
    qr = _per_head(lambda x: _rope(_rms(x, qg[:, MLA_NOPE:], n=MLA_ROPE), cosf, sinf), qr_raw)
    return qn, qr


def _softmax(s):
    m = jnp.max(s, axis=-1, keepdims=True)
    e = jnp.exp(s - m)
    return e / jnp.sum(e, axis=-1, keepdims=True)


def _xattn_f(p, q, k, v, qg, kg):
    s = bdot_nt(_rms(q, qg), _rms(k, kg)) * (X_HEAD_DIM ** -0.5)
    return (bdot(_softmax(s), v),)


ATT_SCALE = (MLA_NOPE + MLA_ROPE) ** -0.5
NT_DIMS = (((1,), (1,)), ((), ()))
NN_DIMS = (((1,), (0,)), ((), ()))
TN_DIMS = (((0,), (0,)), ((), ()))


def _att_specs(t, blk):
    q_spec = pl.BlockSpec((blk, LANES), lambda h, i: (i, h))
    k_spec = pl.BlockSpec((t, LANES), lambda h, i: (0, h))
    shared = pl.BlockSpec((t, LANES), lambda h, i: (0, 0))
    lse_spec = pl.BlockSpec((None, blk, 1), lambda h, i: (h, i, 0))
    return q_spec, k_spec, shared, lse_spec


def _diagonal_mask(blk):
    return (_iota((blk, blk), 1) // ATT_CHUNK) <= (_iota((blk, blk), 0) // ATT_CHUNK)


def _att_keys(kn_ref, kr_ref, j, blk):
    ks = pl.ds(pl.multiple_of(j * blk, blk), blk)
    return ks, jnp.concatenate([kn_ref[ks, :], kr_ref[ks, :]], axis=1).astype(BF16)


def _att_fwd(name, qn, qr, kn, kr, v, hosted=()):
    t, width = qn.shape
    heads = width // LANES
    blk = min(ATT_BLOCK, t)
    nh = len(hosted)

    def body(qn_ref, qr_ref, kn_ref, kr_ref, v_ref, *rest):
        shard_refs, (o_ref, lse_ref), rest = rest[:nh], rest[nh:nh + 2], rest[nh + 2:]
        full_refs, sems = rest[:nh], rest[nh:]
        i = pl.program_id(1)
        gathers = [_gather_copies(shard_refs[k], full_refs[k], *sems[3 * k:3 * k + 3]) for k in range(nh)]
        if nh:
            @pl.when(jnp.logical_and(pl.program_id(0) == 0, i == 0))
            def _():
                for start, _ in gathers:
                    start()
        q = jnp.concatenate([qn_ref[...], qr_ref[...]], axis=1).astype(BF16)

        def scores(j):
            _, k = _att_keys(kn_ref, kr_ref, j, blk)
            return lax.dot_general(q, k, NT_DIMS, preferred_element_type=F32)

        def weighted_values(p, j):
            ks = pl.ds(pl.multiple_of(j * blk, blk), blk)
            return lax.dot_general(p, v_ref[ks, :].astype(BF16), NN_DIMS, preferred_element_type=F32)

        def softmax_step(s, m, l):
            m_new = jnp.maximum(m, jnp.max(s, axis=1, keepdims=True))
            alpha = jnp.exp(m - m_new)
            p = jnp.exp(s - m_new)
            return m_new, alpha, alpha * l + jnp.sum(p, axis=1, keepdims=True), p.astype(BF16)

        def step(j, carry):
            s, p_prev, m, l, acc = carry
            s_next = scores(j + 1)
            pv_prev = weighted_values(p_prev, jnp.maximum(j - 1, 0))
            m, alpha, l, p = softmax_step(s * ATT_SCALE, m, l)
            return s_next, p, m, l, alpha * (acc + pv_prev)

        init = (scores(0), jnp.zeros((blk, blk), BF16), jnp.full((blk, 1), -1e30, F32), jnp.zeros((blk, 1), F32),
                jnp.zeros((blk, LANES), F32))
        s, p_prev, m, l, acc = lax.fori_loop(0, i, step, init)
        pv_prev = weighted_values(p_prev, jnp.maximum(i - 1, 0))
        s = jnp.where(_diagonal_mask(blk), s * ATT_SCALE, -1e30)
        m, alpha, l, p = softmax_step(s, m, l)
        acc = alpha * (acc + pv_prev) + weighted_values(p, i)
        o_ref[...] = acc / l
        lse_ref[...] = m + jnp.log(l)
        if nh:
            @pl.when(jnp.logical_and(pl.program_id(0) == heads - 1, i == t // blk - 1))
            def _():
                for _, finish in gathers:
                    finish()

    q_spec, k_spec, shared, lse_spec = _att_specs(t, blk)
    sem_shapes = [pltpu.SemaphoreType.DMA((7,)), pltpu.SemaphoreType.DMA((7,)), pltpu.SemaphoreType.DMA] * nh
    return pl.pallas_call(
        body, grid=(heads, t // blk), in_specs=[q_spec, q_spec, k_spec, shared, k_spec] + [HBM_SPEC] * nh,
        out_specs=[q_spec, lse_spec] + [HBM_SPEC] * nh,
        out_shape=[jax.ShapeDtypeStruct((t, width), F32), jax.ShapeDtypeStruct((heads, t, 1), F32)] +
                  [jax.ShapeDtypeStruct((N_DEV,) + s.shape, s.dtype) for s in hosted],
        scratch_shapes=sem_shapes, compiler_params=_params(("arbitrary", "arbitrary")), name=name)(
        qn, qr, kn, kr, v, *hosted)


def _att_bwd(name, qn, qr, kn, kr, v, o, lse, do, hosted=()):
    t, width = qn.shape
    heads = width // LANES
    blk = min(ATT_BLOCK, t)
    nh = len(hosted)

    def body(qn_ref, qr_ref, kn_ref, kr_ref, v_ref, o_ref, lse_ref, do_ref, *rest):
        sent_refs, (dqn_ref, dqr_ref, dkn_ref, dkr_ref, dv_ref), rest = rest[:nh], rest[nh:nh + 5], rest[nh + 5:]
        got_refs, sems = rest[:nh], rest[nh:]
        h, i = pl.program_id(0), pl.program_id(1)
        exchanges = [_chip_copies(sent_refs[k], got_refs[k], *sems[2 * k:2 * k + 2]) for k in range(nh)]
        if nh:
            @pl.when(jnp.logical_and(h == 0, i == 0))
            def _():
                for start, _ in exchanges:
                    start()

        @pl.when(i == 0)
        def _():
            dkn_ref[...] = jnp.zeros_like(dkn_ref)
            dv_ref[...] = jnp.zeros_like(dv_ref)

        @pl.when(jnp.logical_and(h == 0, i == 0))
        def _():
            dkr_ref[...] = jnp.zeros_like(dkr_ref)

        q = jnp.concatenate([qn_ref[...], qr_ref[...]], axis=1).astype(BF16)
        do = do_ref[...]
        do16 = do.astype(BF16)
        delta = jnp.sum(do * o_ref[...], axis=1, keepdims=True)
        lse = lse_ref[...]

        def issue(j):
            ks, k = _att_keys(kn_ref, kr_ref, j, blk)
            s = lax.dot_general(q, k, NT_DIMS, preferred_element_type=F32)
            dp = lax.dot_general(do16, v_ref[ks, :].astype(BF16), NT_DIMS, preferred_element_type=F32)
            return s, dp

        def retire(p, ds, j, dq):
            ks, k = _att_keys(kn_ref, kr_ref, j, blk)
            dv_ref[ks, :] += lax.dot_general(p, do16, TN_DIMS, preferred_element_type=F32)
            dk = lax.dot_general(ds, q, TN_DIMS, preferred_element_type=F32)
            dkn_ref[ks, :] += dk[:, :LANES]
            dkr_ref[ks, :] += dk[:, LANES:]
            return dq + lax.dot_general(ds, k, NN_DIMS, preferred_element_type=F32)

        def probs(s, dp, masked):
            s = s * ATT_SCALE
            if masked:
                s = jnp.where(_diagonal_mask(blk), s, -1e30)
            p = jnp.exp(s - lse)
            return p.astype(BF16), (p * (dp - delta) * ATT_SCALE).astype(BF16)

        def step(j, carry):
            s, dp, p_prev, ds_prev, dq = carry
            s_next, dp_next = issue(j + 1)
            dq = retire(p_prev, ds_prev, jnp.maximum(j - 1, 0), dq)
            p, ds = probs(s, dp, False)
            return s_next, dp_next, p, ds, dq

        none = jnp.zeros((blk, blk), BF16)
        s, dp, p_prev, ds_prev, dq = lax.fori_loop(0, i, step,
                                                   issue(0) + (none, none, jnp.zeros((blk, 2 * LANES), F32)))
        dq = retire(p_prev, ds_prev, jnp.maximum(i - 1, 0), dq)
        p, ds = probs(s, dp, True)
        dq = retire(p, ds, i, dq)
        dqn_ref[...] = dq[:, :LANES]
        dqr_ref[...] = dq[:, LANES:]
        if nh:
            @pl.when(jnp.logical_and(h == heads - 1, i == t // blk - 1))
            def _():
                for _, finish in exchanges:
                    finish()

    q_spec, k_spec, shared, lse_spec = _att_specs(t, blk)
    big, one = jax.ShapeDtypeStruct((t, width), F32), jax.ShapeDtypeStruct((t, LANES), F32)
    sem_shapes = [pltpu.SemaphoreType.DMA((3,)), pltpu.SemaphoreType.DMA((3,))] * nh
    return pl.pallas_call(
        body, grid=(heads, t // blk),
        in_specs=[q_spec, q_spec, k_spec, shared, k_spec, q_spec, lse_spec, q_spec] + [HBM_SPEC] * nh,
        out_specs=[q_spec, q_spec, k_spec, shared, k_spec] + [HBM_SPEC] * nh,
        out_shape=[big, big, big, one, big] + [jax.ShapeDtypeStruct((3,) + p.shape[1:], p.dtype) for p in hosted],
        scratch_shapes=sem_shapes, compiler_params=_params(("arbitrary", "arbitrary")), name=name)(
        qn, qr, kn, kr, v, o, lse, do, *hosted)


def mla_attention(name, qn, qr, kn, kr, v, hosted=(), carriers=()):
    @jax.custom_vjp
    def run(qn, qr, kn, kr, v, hosted, carriers):
        o, _, *gathered = _att_fwd(name, qn, qr, kn, kr, v, hosted)
        return o, tuple(gathered), tuple(jnp.zeros((4,) + c.shape[1:], c.dtype) for c in carriers)

    def fwd(qn, qr, kn, kr, v, hosted, carriers):
        o, lse, *gathered = _att_fwd(name, qn, qr, kn, kr, v, hosted)
        handles = tuple(jnp.zeros((4,) + c.shape[1:], c.dtype) for c in carriers)
        return (o, tuple(gathered), handles), (qn, qr, kn, kr, v, o, lse)

    def bwd(r, g):
        do, _, sent = g
        dqn, dqr, dkn, dkr, dv, *got = _att_bwd(name + "_bwd", *r, do, tuple(sent))
        return dqn, dqr, dkn, dkr, dv, (None,) * len(hosted), tuple(got)

    run.defvjp(fwd, bwd)
    return run(qn, qr, kn, kr, v, tuple(hosted), tuple(carriers))


def _shift_down(v, s, rows):
    return v if s == 0 else jnp.where(rows >= s, pltpu.roll(v, s, 0), 0.0)


def _shift_up(v, s, rows):
    t = v.shape[0]
    return v if s == 0 else jnp.where(rows < t - s, pltpu.roll(v, t - s, 0), 0.0)


def _dwconv_fwd(name, x, w, b):
    t, c = x.shape
    kw = w.shape[0]

    def body(x_ref, w_ref, b_ref, y_ref):
        x = x_ref[...]
        rows = _iota(x.shape, 0)
        acc = jnp.zeros_like(x) + b_ref[...]
        for k in range(kw):
            acc = acc + w_ref[k:k + 1, :] * _shift_down(x, kw - 1 - k, rows)
        y_ref[...] = acc

    col = lambda i: (0, i)
    return pl.pallas_call(
        body, grid=(c // LANES,),
        in_specs=[pl.BlockSpec((t, LANES), col), pl.BlockSpec((kw, LANES), col), pl.BlockSpec((1, LANES), col)],
        out_specs=pl.BlockSpec((t, LANES), col), out_shape=jax.ShapeDtypeStruct((t, c), F32),
        compiler_params=_params(("arbitrary",)), name=name)(x, w, b)


def _dwconv_bwd(name, x, w, dy):
    t, c = x.shape
    kw = w.shape[0]

    def body(x_ref, w_ref, dy_ref, dx_ref, dw_ref, db_ref):
        x, dy = x_ref[...], dy_ref[...]
        rows = _iota(x.shape, 0)
        dx = jnp.zeros_like(x)
        for k in range(kw):
            s = kw - 1 - k
            dx = dx + w_ref[k:k + 1, :] * _shift_up(dy, s, rows)
            dw_ref[k:k + 1, :] = jnp.sum(dy * _shift_down(x, s, rows), axis=0, keepdims=True)
        dx_ref[...] = dx
        db_ref[...] = jnp.sum(dy, axis=0, keepdims=True)

    col = lambda i: (0, i)
    big, wsp, bsp = pl.BlockSpec((t, LANES), col), pl.BlockSpec((kw, LANES), col), pl.BlockSpec((1, LANES), col)
    return pl.pallas_call(
        body, grid=(c // LANES,), in_specs=[big, wsp, big], out_specs=[big, wsp, bsp],
        out_shape=[jax.ShapeDtypeStruct((t, c), F32), jax.ShapeDtypeStruct((kw, c), F32),
                   jax.ShapeDtypeStruct((1, c), F32)],
        compiler_params=_params(("arbitrary",)), name=name)(x, w, dy)


def dwconv(name, x, w, b):
    @jax.custom_vjp
    def run(x, w, b):
        return _dwconv_fwd(name, x, w, b)

    def fwd(x, w, b):
        return run(x, w, b), (x, w)

    def bwd(r, g):
        return tuple(_dwconv_bwd(name + "_bwd", r[0], r[1], g))

    run.defvjp(fwd, bwd)
    return run(x, w, b)


def _ssd_tile(xc, dtr, dtb, alog, dsk, prev):
    ln = xc.shape[0]
    gw = SSD_INNER // SSD_GROUPS
    ns = SSD_STATE
    xs = jax.nn.silu(xc[:, :SSD_INNER])
    bm = jax.nn.silu(xc[:, SSD_INNER:SSD_INNER + SSD_GROUPS * ns])
    cm = jax.nn.silu(xc[:, SSD_INNER + SSD_GROUPS * ns:])
    dt = jax.nn.softplus(dtr + dtb)
    a = dt * (-jnp.exp(alog))
    expand = (_iota((LANES, SSD_INNER), 0) == _iota((LANES, SSD_INNER), 1) // SSD_HEAD_DIM).astype(F32)
    causal = _iota((ln, ln), 0) >= _iota((ln, ln), 1)
    acs_h = hdot(causal.astype(F32), a)
    acs_c = hdot(acs_h, expand)
    dt_c = hdot(dt, expand)

    def row_per_column(v):
        return jnp.mean(hdot(jnp.broadcast_to(v, (8, LANES)), expand), axis=0, keepdims=True)

    aend_c = row_per_column(jnp.sum(a, axis=0, keepdims=True))
    xdt = xs * dt_c
    to_end = xdt * jnp.exp(aend_c - acs_c)
    from_start = jnp.exp(acs_c)
    acs_ht = acs_h.T
    lane_h, sub_h = _iota((1, LANES), 1), _iota((LANES, 1), 0)
    head_of_col = _iota((1, gw), 1) // SSD_HEAD_DIM
    ys, states = [], []
    for g in range(SSD_GROUPS):
        cg = cm[:, g * ns:(g + 1) * ns]
        bg = bm[:, g * ns:(g + 1) * ns]
        cols = slice(g * gw, (g + 1) * gw)
        y = bdot(cg, prev[:, cols]) * from_start[:, cols]
        states.append(bdot_tn(bg, to_end[:, cols]))
        cb = bdot_nt(cg, bg)
        for r in range(gw // SSD_HEAD_DIM):
            h = g * (gw // SSD_HEAD_DIM) + r
            col = jnp.sum(jnp.where(lane_h == h, acs_h, 0.0), axis=1, keepdims=True)
            row = jnp.sum(jnp.where(sub_h == h, acs_ht, 0.0), axis=0, keepdims=True)
            decay = jnp.exp(jnp.where(causal, col - row, -1e30))
            y = y + jnp.where(head_of_col == r, bdot(cb * decay, xdt[:, cols]), 0.0)
        ys.append(y)
    y = jnp.concatenate(ys, axis=1) + row_per_column(dsk) * xs
    new = prev * jnp.exp(aend_c) + jnp.concatenate(states, axis=1)
    return y, new


def _ssd_fwd(name, xc, dtr, dtb, alog, dsk):
    t = xc.shape[0]
    ln = min(SSD_TILE, t)
    nt = t // ln

    def body(xc_ref, dtr_ref, dtb_ref, alog_ref, dsk_ref, y_ref, prev_ref, carry):
        @pl.when(pl.program_id(0) == 0)
        def _():
            carry[...] = jnp.zeros_like(carry)

        prev = carry[...]
        prev_ref[...] = prev
        y, new = _ssd_tile(xc_ref[...], dtr_ref[...], dtb_ref[...], alog_ref[...], dsk_ref[...], prev)
        y_ref[...] = y
        carry[...] = new

    row = lambda i: (i, 0)
    par = pl.BlockSpec((1, LANES), lambda i: (0, 0))
    return pl.pallas_call(
        body, grid=(nt,),
        in_specs=[pl.BlockSpec((ln, xc.shape[1]), row), pl.BlockSpec((ln, LANES), row), par, par, par],
        out_specs=[pl.BlockSpec((ln, SSD_INNER), row), pl.BlockSpec((None, SSD_STATE, SSD_INNER), lambda i: (i, 0, 0))],
        out_shape=[jax.ShapeDtypeStruct((t, SSD_INNER), F32), jax.ShapeDtypeStruct((nt, SSD_STATE, SSD_INNER), F32)],
        scratch_shapes=[pltpu.VMEM((SSD_STATE, SSD_INNER), F32)],
        compiler_params=_params(("arbitrary",)), name=name)(xc, dtr, dtb, alog, dsk)


def _ssd_bwd(name, xc, dtr, dtb, alog, dsk, prevs, dy):
    t = xc.shape[0]
    ln = min(SSD_TILE, t)
    nt = t // ln

    def body(xc_ref, dtr_ref, dtb_ref, alog_ref, dsk_ref, prev_ref, dy_ref, dxc_ref, ddtr_ref, ddtb_ref, dalog_ref,
             ddsk_ref, dcarry):
        i = pl.program_id(0)

        @pl.when(i == 0)
        def _():
            dcarry[...] = jnp.zeros_like(dcarry)

        _, vjp = jax.vjp(_ssd_tile, xc_ref[...], dtr_ref[...], dtb_ref[...], alog_ref[...], dsk_ref[...], prev_ref[...])
        dxc, ddtr, ddtb, dalog, ddsk, dprev = vjp((dy_ref[...], dcarry[...]))
        dxc_ref[...] = dxc
        ddtr_ref[...] = ddtr
        dcarry[...] = dprev
        for r, gr in ((ddtb_ref, ddtb), (dalog_ref, dalog), (ddsk_ref, ddsk)):
            @pl.when(i == 0)
            def _(r=r, gr=gr):
                r[...] = gr

            @pl.when(i != 0)
            def _(r=r, gr=gr):
                r[...] += gr

    row = lambda i: (nt - 1 - i, 0)
    par = pl.BlockSpec((1, LANES), lambda i: (0, 0))
    big, dts = pl.BlockSpec((ln, xc.shape[1]), row), pl.BlockSpec((ln, LANES), row)
    par_shape = jax.ShapeDtypeStruct((1, LANES), F32)
    return pl.pallas_call(
        body, grid=(nt,),
        in_specs=[big, dts, par, par, par, pl.BlockSpec((None, SSD_STATE, SSD_INNER), lambda i: (nt - 1 - i, 0, 0)),
                  pl.BlockSpec((ln, SSD_INNER), row)],
        out_specs=[big, dts, par, par, par],
        out_shape=[jax.ShapeDtypeStruct(xc.shape, F32), jax.ShapeDtypeStruct(dtr.shape, F32), par_shape, par_shape,
                   par_shape],
        scratch_shapes=[pltpu.VMEM((SSD_STATE, SSD_INNER), F32)],
        compiler_params=_params(("arbitrary",)), name=name)(xc, dtr, dtb, alog, dsk, prevs, dy)


def ssd_scan(name, xc, dtr, dtb, alog, dsk):
    @jax.custom_vjp
    def run(xc, dtr, dtb, alog, dsk):
        return _ssd_fwd(name, xc, dtr, dtb, alog, dsk)[0]

    def fwd(xc, dtr, dtb, alog, dsk):
        y, prevs = _ssd_fwd(name, xc, dtr, dtb, alog, dsk)
        return y, (xc, dtr, dtb, alog, dsk, prevs)

    def bwd(r, g):
        return tuple(_ssd_bwd(name + "_bwd", *r, g))

    run.defvjp(fwd, bwd)
    return run(xc, dtr, dtb, alog, dsk)


def loss_head(y, target):
    t, n = y.shape
    tile = min(ROW_TILE, t)

    def body(y_ref, t_ref, dy_ref, acc_ref):
        d = y_ref[...] - t_ref[...]
        dy_ref[...] = d * (1.0 / n)

        @pl.when(pl.program_id(0) == 0)
        def _():
            acc_ref[...] = jnp.zeros_like(acc_ref)

        acc_ref[...] += jnp.sum(d * d, axis=0, keepdims=True)

    row = pl.BlockSpec((tile, n), lambda i: (i, 0))
    dy, acc = pl.pallas_call(
        body, grid=(t // tile,), in_specs=[row, row], out_specs=[row, pl.BlockSpec((1, n), lambda i: (0, 0))],
        out_shape=[jax.ShapeDtypeStruct((t, n), F32), jax.ShapeDtypeStruct((1, n), F32)],
        compiler_params=_params(("arbitrary",)), name="loss_head")(y, target)
    return acc, dy


def adamw(name, w, g, m, v):
    shape = w.shape
    cols = shape[-1]
    rows = w.size // cols
    tile = _divisor(rows, 512, 8) if rows % 8 == 0 else rows

    def body(w_ref, g_ref, m_ref, v_ref, d_ref, nm_ref, nv_ref):
        g = g_ref[...]
        m = ADAM_B1 * m_ref[...] + (1.0 - ADAM_B1) * g
        v = ADAM_B2 * v_ref[...] + (1.0 - ADAM_B2) * jnp.square(g)
        m_hat = m / (1.0 - ADAM_B1 ** ADAM_STEP)
        v_hat = v / (1.0 - ADAM_B2 ** ADAM_STEP)
        d_ref[...] = -ADAM_LR * (m_hat / (jnp.sqrt(v_hat) + ADAM_EPS) + ADAM_WD * w_ref[...])
        nm_ref[...] = m
        nv_ref[...] = v

    spec = pl.BlockSpec((tile, cols), lambda i: (i, 0))
    two_d = jax.ShapeDtypeStruct((rows, cols), F32)
    outs = pl.pallas_call(body, grid=(rows // tile,), in_specs=[spec] * 4, out_specs=[spec] * 3, out_shape=[two_d] * 3,
                          compiler_params=_params(("arbitrary",)), name=name)(
        *[a.reshape(rows, cols) for a in (w, g, m, v)])
    return [o.reshape(shape) for o in outs]


HBM_SPEC = pl.BlockSpec(memory_space=pl.ANY)


def _position():
    return lax.axis_index("x"), lax.axis_index("y"), lax.axis_index("c")


def _gather_copies(x_ref, out_ref, send_sems, recv_sems, local_sem):
    x, y, c = _position()
    me, sibling = (x, y, c), (x, y, 1 - c)
    chips = [(1 - x, y), (x, 1 - y), (1 - x, 1 - y)]

    def block(px, py, pc):
        return out_ref.at[4 * px + 2 * py + pc]

    def copy(k, blk, to, src=None):
        return pltpu.make_async_remote_copy(
            src_ref=block(*blk) if src is None else src, dst_ref=block(*blk), send_sem=send_sems.at[k],
            recv_sem=recv_sems.at[k], device_id=to, device_id_type=MESH_ID)

    mine = pltpu.make_async_copy(x_ref, block(*me), local_sem)
    first = [copy(0, me, sibling, src=x_ref)]
    first += [copy(1 + j, me, (*chip, c), src=x_ref) for j, chip in enumerate(chips)]
    passed = [copy(4 + j, (*chip, c), sibling) for j, chip in enumerate(chips)]

    def start():
        mine.start()
        for cp in first:
            cp.start()

    def finish():
        for j, chip in enumerate(chips):
            copy(1 + j, (*chip, c), me).wait_recv()
            passed[j].start()
        copy(0, sibling, me).wait_recv()
        for j, chip in enumerate(chips):
            copy(4 + j, (*chip, 1 - c), me).wait_recv()
        for cp in first + passed:
            cp.wait_send()
        mine.wait()

    return start, finish


def all_gather(name, shard):
    def body(x_ref, out_ref, send_sems, recv_sems, local_sem):
        start, finish = _gather_copies(x_ref, out_ref, send_sems, recv_sems, local_sem)
        start()
        finish()

    return pl.pallas_call(
        body, in_specs=[HBM_SPEC], out_specs=HBM_SPEC,
        out_shape=jax.ShapeDtypeStruct((N_DEV,) + shard.shape, shard.dtype),
        scratch_shapes=[pltpu.SemaphoreType.DMA((7,)), pltpu.SemaphoreType.DMA((7,)), pltpu.SemaphoreType.DMA],
        name=name)(shard)


def pair_exchange(name, g):
    def body(g_ref, out_ref, send_sem, recv_sem):
        x, y, c = _position()
        cp = pltpu.make_async_remote_copy(src_ref=g_ref.at[1 - c], dst_ref=out_ref, send_sem=send_sem,
                                          recv_sem=recv_sem, device_id=(x, y, 1 - c), device_id_type=MESH_ID)
        cp.start()
        cp.wait()

    return pl.pallas_call(
        body, in_specs=[HBM_SPEC], out_specs=HBM_SPEC, out_shape=jax.ShapeDtypeStruct(g.shape[1:], g.dtype),
        scratch_shapes=[pltpu.SemaphoreType.DMA, pltpu.SemaphoreType.DMA], name=name)(g)


def _chip_copies(p_ref, out_ref, send_sems, recv_sems):
    x, y, c = _position()
    copies = [pltpu.make_async_remote_copy(
        src_ref=p_ref.at[2 * px + py], dst_ref=out_ref.at[j], send_sem=send_sems.at[j], recv_sem=recv_sems.at[j],
        device_id=(px, py, c), device_id_type=MESH_ID) for j, (px, py) in enumerate([(1 - x, y), (x, 1 - y), (1 - x, 1 - y)])]

    def start():
        for cp in copies:
            cp.start()

    def finish():
        for cp in copies:
            cp.wait()

    return start, finish


def chip_exchange(name, p):
    def body(p_ref, out_ref, send_sems, recv_sems):
        start, finish = _chip_copies(p_ref, out_ref, send_sems, recv_sems)
        start()
        finish()

    return pl.pallas_call(
        body, in_specs=[HBM_SPEC], out_specs=HBM_SPEC, out_shape=jax.ShapeDtypeStruct((3,) + p.shape[1:], p.dtype),
        scratch_shapes=[pltpu.SemaphoreType.DMA((3,)), pltpu.SemaphoreType.DMA((3,))], name=name)(p)


def _sum_tile(r):
    return _divisor(r, 512, 16) if r % 16 == 0 else r


def pair_reduce(name, g, got, my_c, my_chip, wire):
    _, nchip, r, c_ = g.shape
    tile = _sum_tile(r)

    def body(ids, g_ref, got_ref, p_ref, mine_ref):
        s = g_ref[...].astype(F32) + got_ref[...].astype(F32)
        p_ref[...] = s.astype(wire)

        @pl.when(pl.program_id(1) == ids[1])
        def _():
            mine_ref[...] = s

    return pl.pallas_call(
        body,
        grid_spec=pltpu.PrefetchScalarGridSpec(
            num_scalar_prefetch=1, grid=(r // tile, nchip),
            in_specs=[pl.BlockSpec((None, None, tile, c_), lambda i, k, ids: (ids[0], k, i, 0)),
                      pl.BlockSpec((None, tile, c_), lambda i, k, ids: (k, i, 0))],
            out_specs=[pl.BlockSpec((None, tile, c_), lambda i, k, ids: (k, i, 0)),
                       pl.BlockSpec((tile, c_), lambda i, k, ids: (i, 0))]),
        out_shape=[jax.ShapeDtypeStruct((nchip, r, c_), wire), jax.ShapeDtypeStruct((r, c_), F32)],
        compiler_params=_params(("arbitrary", "arbitrary")), name=name)(
        jnp.stack([my_c, my_chip]).astype(jnp.int32), g, got)


def chip_reduce(name, mine, got):
    r, c_ = mine.shape
    tile = _sum_tile(r)

    def body(m_ref, got_ref, o_ref):
        o_ref[...] = ((m_ref[...] + got_ref[0].astype(F32)) + got_ref[1].astype(F32)) + got_ref[2].astype(F32)

    return pl.pallas_call(
        body, grid=(r // tile,),
        in_specs=[pl.BlockSpec((tile, c_), lambda i: (i, 0)), pl.BlockSpec((3, tile, c_), lambda i: (0, i, 0))],
        out_specs=pl.BlockSpec((tile, c_), lambda i: (i, 0)), out_shape=jax.ShapeDtypeStruct((r, c_), F32),
        compiler_params=_params(("arbitrary",)), name=name)(mine, got)


def sum_blocks(name, a):
    n, r, c_ = a.shape
    tile = _sum_tile(r)

    def body(a_ref, o_ref):
        acc = a_ref[0]
        for k in range(1, n):
            acc = acc + a_ref[k]
        o_ref[...] = acc

    return pl.pallas_call(
        body, grid=(r // tile,), in_specs=[pl.BlockSpec((n, tile, c_), lambda i: (0, i, 0))],
        out_specs=pl.BlockSpec((tile, c_), lambda i: (i, 0)), out_shape=jax.ShapeDtypeStruct((r, c_), F32),
        compiler_params=_params(("arbitrary",)), name=name)(a)


GROUPS = [
    ("rows1024", ["ssd_w_out", "conv_w_out", "mla_w_o", "w_out", "xattn_w_q", "xattn_w_o", "ffn_w_out"], BF16, 1024),
    ("w_in", ["w_in"], BF16, 1114),
    ("ffn_w_in", ["ffn_w_in"], BF16, 704),
    ("cols256", ["xattn_w_kv", "mla_w_kv_b"], BF16, 256),
    ("w_q_b", ["mla_w_q_b"], BF16, 192),
    ("small", ["ssd_conv_w", "conv_dw_w", "gate_b"], F32, 128),
]
HOSTED, UPFRONT = GROUPS[:3], GROUPS[3:]
PACK_COLS = 1024


def _pack(arrays):
    flat = jnp.concatenate([a.reshape(-1) for a in arrays])
    rows = -(-flat.shape[0] // PACK_COLS)
    rows += -rows % 8
    return jnp.pad(flat, (0, rows * PACK_COLS - flat.shape[0])).reshape(rows, PACK_COLS)


def _unpack(buf, shapes):
    flat = buf.reshape(-1)
    out, off = [], 0
    for s in shapes:
        n = 1
        for d in s:
            n *= d
        out.append(flat[off:off + n].reshape(tuple(s)))
        off += n
    return out


def _stack_rows(arrays, width, lead):
    return jnp.concatenate([a.reshape(a.shape[:lead] + (-1, width)) for a in arrays], axis=lead)


def _unstack_rows(buf, shapes, width, lead):
    out, off = [], 0
    for s in shapes:
        n = 1
        for d in s:
            n *= d
        rows = n // width
        idx = (slice(None),) * lead + (slice(off, off + rows),)
        out.append(buf[idx].reshape(buf.shape[:lead] + tuple(s)))
        off += rows
    return out


def _join_shards(blocks, axis):
    ax = axis + 1
    moved = jnp.moveaxis(blocks, 0, ax)
    s = moved.shape
    return moved.reshape(s[:ax] + (s[ax] * s[ax + 1],) + s[ax + 2:])


def _split_by_owner(full, axis):
    ax = axis + 1
    s = full.shape
    cut = full.reshape(s[:ax] + (2, 2, 2, s[ax] // N_DEV) + s[ax + 1:])
    cut = jnp.moveaxis(cut, (ax + 2, ax, ax + 1), (0, 1, 2))
    return cut.reshape((2, 4) + cut.shape[3:])


def _blocks_by_owner(blocks):
    cut = blocks.reshape((2, 2, 2) + blocks.shape[1:])
    return jnp.moveaxis(cut, 2, 0).reshape((2, 4) + blocks.shape[1:])


def _pad_cols(w, n):
    return jnp.pad(w, ((0, 0), (0, n - w.shape[1])))


def _regroup_cols(srcs, widths):
    starts = [0]
    for s in srcs:
        starts.append(starts[-1] + s.shape[1])
    assert starts[-1] == sum(widths)
    out, lo = [], 0
    for wd in widths:
        hi = lo + wd
        parts = []
        for s, a, b in zip(srcs, starts[:-1], starts[1:]):
            u, v = max(lo, a), min(hi, b)
            if u < v:
                parts.append(s[:, u - a:v - a])
        out.append(parts[0] if len(parts) == 1 else jnp.concatenate(parts, axis=1))
        lo = hi
    return out


COL_BLOCKED = ("w_in", "ffn_w_in", "xattn_w_kv", "mla_w_kv_b", "mla_w_q_b")
W_IN_PIECES = (1024, 2048, 16, 2048, 384, MLA_KV_RANK, MLA_ROPE, 3072)


def _prep_layer(w):
    w_z, w_xbc, w_dt, w_glu, w_q, w_ckv, w_kr, w_gate = _regroup_cols(list(w["w_in"]), W_IN_PIECES)
    w_ffn_gate, w_ffn_up = _regroup_cols(list(w["ffn_w_in"]), (FFN_HIDDEN, FFN_HIDDEN))
    w_xk, w_xv = _regroup_cols(list(w["xattn_w_kv"]), (D_MODEL, D_MODEL))
    q, kv = w["mla_w_q_b"], w["mla_w_kv_b"]

    def row(v):
        return v.reshape(1, -1)

    def norm_pair(g):
        return _pad_cols(row(g), 2 * LANES)

    return {
        "mix_norm_g": row(w["mix_norm_g"]),
        "w_z": w_z, "w_xbc": w_xbc, "w_dt": _pad_cols(w_dt, LANES), "w_glu": w_glu, "w_q": w_q, "w_ckv": w_ckv,
        "w_kr": _pad_cols(w_kr, LANES), "w_gate": w_gate,
        "ssd_conv_w": w["ssd_conv_w"], "ssd_conv_b": row(w["ssd_conv_b"]),
        "ssd_dt_bias": _pad_cols(row(w["ssd_dt_bias"]), LANES), "ssd_a_log": _pad_cols(row(w["ssd_a_log"]), LANES),
        "ssd_d": _pad_cols(row(w["ssd_d"]), LANES), "ssd_norm_g": row(w["ssd_norm_g"]), "ssd_w_out": w["ssd_w_out"],
        "conv_dw_w": w["conv_dw_w"], "conv_dw_b": row(w["conv_dw_b"]), "conv_ln_g": row(w["conv_ln_g"]),
        "conv_ln_b": row(w["conv_ln_b"]), "conv_w_out": w["conv_w_out"],
        "mla_q_a_g": row(w["mla_q_a_g"]), "mla_kv_a_g": row(w["mla_kv_a_g"]),
        "w_qn": jnp.concatenate([q[h, :, :MLA_NOPE] for h in range(MLA_HEADS)], axis=1),
        "w_qr": jnp.concatenate([_pad_cols(q[h, :, MLA_NOPE:], LANES) for h in range(MLA_HEADS)], axis=1),
        "w_kn": jnp.concatenate([kv[h, :, :MLA_NOPE] for h in range(MLA_HEADS)], axis=1),
        "w_v": jnp.concatenate([kv[h, :, MLA_NOPE:] for h in range(MLA_HEADS)], axis=1),
        "mla_q_norm_g": norm_pair(w["mla_q_norm_g"]), "mla_k_norm_g": norm_pair(w["mla_k_norm_g"]),
        "mla_w_o": w["mla_w_o"], "gate_b": row(w["gate_b"]), "w_out": w["w_out"],
        "xattn_norm_g": row(w["xattn_norm_g"]), "mem_norm_g": row(w["mem_norm_g"]), "xattn_w_q": w["xattn_w_q"],
        "w_xk": w_xk, "w_xv": w_xv,
        "xattn_q_norm_g": row(w["xattn_q_norm_g"]), "xattn_k_norm_g": row(w["xattn_k_norm_g"]),
        "xattn_w_o": w["xattn_w_o"], "ffn_norm_g": row(w["ffn_norm_g"]),
        "w_ffn_gate": w_ffn_gate, "w_ffn_up": w_ffn_up, "ffn_w_out": w["ffn_w_out"],
    }


def _unprep_grads(g):
    n_dt, n_kr = IN_SIZES[2], MLA_ROPE
    flat = lambda v: v.reshape(-1)

    def blocks(srcs):
        total = sum(s.shape[1] for s in srcs)
        return jnp.stack(_regroup_cols(srcs, (total // N_DEV,) * N_DEV))

    def head(a, h, n=LANES):
        return a[:, h * LANES:h * LANES + n]

    return {
        "mix_norm_g": flat(g["mix_norm_g"]),
        "w_in": blocks([g["w_z"], g["w_xbc"], g["w_dt"][:, :n_dt], g["w_glu"], g["w_q"], g["w_ckv"],
                        g["w_kr"][:, :n_kr], g["w_gate"]]),
        "ssd_conv_w": g["ssd_conv_w"], "ssd_conv_b": flat(g["ssd_conv_b"]),
        "ssd_dt_bias": flat(g["ssd_dt_bias"])[:SSD_HEADS], "ssd_a_log": flat(g["ssd_a_log"])[:SSD_HEADS],
        "ssd_d": flat(g["ssd_d"])[:SSD_HEADS], "ssd_norm_g": flat(g["ssd_norm_g"]), "ssd_w_out": g["ssd_w_out"],
        "conv_dw_w": g["conv_dw_w"], "conv_dw_b": flat(g["conv_dw_b"]), "conv_ln_g": flat(g["conv_ln_g"]),
        "conv_ln_b": flat(g["conv_ln_b"]), "conv_w_out": g["conv_w_out"],
        "mla_q_a_g": flat(g["mla_q_a_g"]),
        "mla_w_q_b": jnp.stack([jnp.concatenate([head(g["w_qn"], h), head(g["w_qr"], h, MLA_ROPE)], axis=1)
                                for h in range(MLA_HEADS)]),
        "mla_kv_a_g": flat(g["mla_kv_a_g"]),
        "mla_w_kv_b": jnp.stack([jnp.concatenate([head(g["w_kn"], h), head(g["w_v"], h)], axis=1)
                                 for h in range(MLA_HEADS)]),
        "mla_q_norm_g": flat(g["mla_q_norm_g"])[:MLA_NOPE + MLA_ROPE],
        "mla_k_norm_g": flat(g["mla_k_norm_g"])[:MLA_NOPE + MLA_ROPE],
        "mla_w_o": g["mla_w_o"], "gate_b": g["gate_b"].reshape(3, D_MODEL), "w_out": g["w_out"],
        "xattn_norm_g": flat(g["xattn_norm_g"]), "mem_norm_g": flat(g["mem_norm_g"]), "xattn_w_q": g["xattn_w_q"],
        "xattn_w_kv": blocks([g["w_xk"], g["w_xv"]]),
        "xattn_q_norm_g": flat(g["xattn_q_norm_g"]), "xattn_k_norm_g": flat(g["xattn_k_norm_g"]),
        "xattn_w_o": g["xattn_w_o"], "ffn_norm_g": flat(g["ffn_norm_g"]),
        "ffn_w_in": blocks([g["w_ffn_gate"], g["w_ffn_up"]]), "ffn_w_out": g["ffn_w_out"],
    }


def _layer(l, x, mem, cosf, sinf, w, hosted=(), carriers=()):
    t = x.shape[0]
    n = lambda s: f"l{l}_{s}"
    tile = min(ROW_TILE, t)
    grid = (1, t // tile)

    def rowwise(name, f, ins, width, to_matmul=False):
        return tmap(n(name), f, grid, ins, [_row_out(t, width, tile)], narrow=(0,) if to_matmul else ())[0]

    u, x = rms_norm_through(n("mix_norm"), x, w["mix_norm_g"])
    in_keys = ["z", "xbc", "dt", "glu", "q", "ckv", "kr", "gate"]
    z, xbc, dtr, glu, q_lat, c_kv, kr_raw, gate_logits = multi_matmul(
        n("in"), u, [w["w_" + k] for k in in_keys], in_keys, to_tmap=("z", "glu", "q", "ckv", "kr", "gate"))

    xc = dwconv(n("ssd_conv"), xbc, w["ssd_conv_w"], w["ssd_conv_b"])
    y_scan = ssd_scan(n("ssd_scan"), xc, dtr, w["ssd_dt_bias"], w["ssd_a_log"], w["ssd_d"])
    y_norm = rowwise("ssd_gate_norm", _ssd_gate_norm_f, [_rows(y_scan, tile), _rows(z, tile), _whole(w["ssd_norm_g"])],
                     SSD_INNER, to_matmul=True)
    y_ssd = matmul(n("ssd_out"), y_norm, w["ssd_w_out"], to_tmap=True)

    v = rowwise("glu", _glu_f, [_rows(glu, tile)], D_MODEL)
    v = dwconv(n("conv_dw"), v, w["conv_dw_w"], w["conv_dw_b"])
    v = rowwise("conv_ln_silu", _ln_silu_f, [_rows(v, tile), _whole(w["conv_ln_g"]), _whole(w["conv_ln_b"])], D_MODEL,
                to_matmul=True)
    y_conv = matmul(n("conv_out"), v, w["conv_w_out"], to_tmap=True)

    q_n = rms_norm(n("q_a_norm"), q_lat, w["mla_q_a_g"])
    qn_raw, qr_raw = multi_matmul(n("q"), q_n, [w["w_qn"], w["w_qr"]], ["nope", "rope"], to_tmap=("nope", "rope"))
    c_n = rms_norm(n("kv_a_norm"), c_kv, w["mla_kv_a_g"])
    kn_raw, val = multi_matmul(n("kv"), c_n, [w["w_kn"], w["w_v"]], ["nope", "v"], to_tmap=("nope",))
    tables = [_rows(cosf, tile, "n"), _rows(sinf, tile, "n")]
    kn = rowwise("k_nope_norm", _k_nope_f, [_rows(kn_raw, tile), _whole(w["mla_k_norm_g"])], MLA_HEADS * MLA_NOPE)
    kr = rowwise("k_rope", _k_rope_f, [_rows(kr_raw, tile)] + tables + [_whole(w["mla_k_norm_g"])], LANES)
    wide = _row_out(t, MLA_HEADS * LANES, tile)
    qn, qr = tmap(n("q_prep"), _q_prep_f, grid,
                  [_rows(qn_raw, tile), _rows(qr_raw, tile)] + tables + [_whole(w["mla_q_norm_g"])], [wide, wide])
    att, gathered, handles = mla_attention(n("mla_attn"), qn, qr, kn, kr, val, hosted, carriers)
    y_mla = matmul(n("mla_out"), att, w["mla_w_o"], to_tmap=True)

    merged = rowwise("merge", _merge_f, [_rows(gate_logits, tile), _whole(w["gate_b"]), _rows(y_ssd, tile),
                                         _rows(y_conv, tile), _rows(y_mla, tile)], D_MODEL, to_matmul=True)
    x = matmul(n("mix_out"), merged, w["w_out"], res=x)

    h, x = rms_norm_through(n("xattn_norm"), x, w["xattn_norm_g"])
    mem_n = rms_norm_nograd_x(n("mem_norm"), mem, w["mem_norm_g"])
    xq = matmul(n("xattn_q"), h, w["xattn_w_q"], to_tmap=True)
    xk, xv = multi_matmul(n("xattn_kv"), mem_n, [w["w_xk"], w["w_xv"]], ["k", "v"])
    m = mem.shape[0]
    txq = min(XATT_Q_TILE, t)
    kv_head = lambda arr: (arr, (m, X_HEAD_DIM), lambda o, i: (0, o), "ai")
    xo = tmap(n("xattn"), _xattn_f, (X_HEADS, t // txq),
              [(xq, (txq, X_HEAD_DIM), lambda o, i: (i, o), "t"), kv_head(xk), kv_head(xv),
               _whole(w["xattn_q_norm_g"]), _whole(w["xattn_k_norm_g"])],
              [((t, D_MODEL), (txq, X_HEAD_DIM), lambda o, i: (i, o))], narrow=(0,))[0]
    x = matmul(n("xattn_out"), xo, w["xattn_w_o"], res=x)

    h, x = rms_norm_through(n("ffn_norm"), x, w["ffn_norm_g"])
    gate, up = multi_matmul(n("ffn_in"), h, [w["w_ffn_gate"], w["w_ffn_up"]], ["gate", "up"], to_tmap=("gate", "up"))
    act = rowwise("swiglu", _swiglu_f, [_rows(gate, tile), _rows(up, tile)], FFN_HIDDEN, to_matmul=True)
    return matmul(n("ffn_out"), act, w["ffn_w_out"], res=x), gathered, handles


def _rope_tables(positions):
    inv = ROPE_THETA ** (-jnp.arange(0, MLA_ROPE, 2, dtype=F32) / MLA_ROPE)
    ang = positions.astype(F32)[:, None] * inv
    pad = jnp.zeros((positions.shape[0], LANES - MLA_ROPE), F32)
    cos, sin = jnp.cos(ang), jnp.sin(ang)
    return jnp.concatenate([cos, cos, pad], axis=1), jnp.concatenate([sin, sin, pad], axis=1)


def local_step(x, mem, positions, target, weights, gathered0, later, shard_shapes, exchange=None):
    cosf, sinf = _rope_tables(positions)

    def layer_weights(l, big):
        w = {k: v[:, l] if k in COL_BLOCKED else v[l] for k, v in weights.items()}
        for (_, names, _, width), stack in zip(HOSTED, big):
            for n, b in zip(names, _unstack_rows(stack, [shard_shapes[n] for n in names], width, 1)):
                w[n] = b if n in COL_BLOCKED else b.reshape((-1,) + b.shape[2:])
        return w

    diff = [{k: jnp.zeros(v.shape, BF16) if k in MATRICES else v
             for k, v in _prep_layer(layer_weights(l, gathered0)).items()} for l in range(DEPTH)]

    def layer_fn(l, big, hosted):
        def f(x, d, carriers):
            mats = _prep_layer(layer_weights(l, big))
            w = {k: Mat(mats[k], s) if k in MATRICES else s for k, s in d.items()}
            y, gathered, handles = _layer(l, x, mem, cosf, sinf, w, hosted, carriers)
            return (y, handles), gathered
        return f

    pulls, big = [], gathered0
    for l in range(DEPTH):
        inner = l + 1 < DEPTH
        carriers = tuple(jnp.zeros((3,) + s.shape, s.dtype) for s in later[l]) if inner and exchange else ()
        (x, _), pull, big = jax.vjp(layer_fn(l, big, later[l] if inner else ()), x, diff[l], carriers, has_aux=True)
        pulls.append(pull)
    sq, g = loss_head(x, target)

    per_layer, reduced, sent, mine = [None] * DEPTH, [None] * DEPTH, (), None
    for l in reversed(range(DEPTH)):
        g, gd, got = pulls[l]((g, tuple(sent)))
        if mine is not None:
            reduced[l + 1] = exchange.finish(mine, got)
        per_layer[l] = _unprep_grads(gd)
        if exchange:
            sent, mine = exchange.begin(per_layer[l])
    if exchange:
        reduced[0] = exchange.finish(mine, [chip_exchange(f"chip_exchange_first_layer_{k}", p)
                                            for k, p in enumerate(sent)])
    grads = {k: jnp.stack([pl_[k] for pl_ in per_layer], axis=1 if k in COL_BLOCKED else 0) for k in WEIGHTS}
    shards = {k: jnp.stack([r[k] for r in reduced]) for k in reduced[0]} if exchange else None
    return sq, g, grads, shards


def _step(x, mem, positions, loss_target, w, m, v):
    xi, yi, ci = _position()

    full = {n: w[n] for n in REPLICATED}
    for gname, names, wire, width in UPFRONT:
        shapes = [w[n].shape for n in names]
        stacked = _stack_rows([w[n] for n in names], width, 0).astype(wire)
        gathered = all_gather("gather_" + gname, stacked)
        for n, b in zip(names, _unstack_rows(gathered, shapes, width, 1)):
            full[n] = b if n in COL_BLOCKED else _join_shards(b, SHARDED[n])
    shards = [[_stack_rows([w[n][l] for n in names], width, 0).astype(wire) for _, names, wire, width in HOSTED]
              for l in range(DEPTH)]
    gathered0 = [all_gather("gather0_" + g[0], s) for g, s in zip(HOSTED, shards[0])]
    shard_shapes = {n: w[n].shape[1:] for g in HOSTED for n in g[1]}

    def to_sibling(groups, grads):
        sent, mine = [], []
        for gname, names, wire, width in groups:
            by_owner = _stack_rows([_blocks_by_owner(grads[n]) if n in COL_BLOCKED else
                                    _split_by_owner(grads[n], SHARDED[n]) for n in names], width, 2).astype(wire)
            from_sibling = pair_exchange("pair_exchange_" + gname, by_owner)
            p, own = pair_reduce("pair_reduce_" + gname, by_owner, from_sibling, ci, 2 * xi + yi, wire)
            sent.append(p)
            mine.append(own)
        return sent, mine

    def from_chips(groups, shapes, mine, got):
        out = {}
        for (gname, names, _, width), own, arrived in zip(groups, mine, got):
            reduced = chip_reduce("chip_reduce_" + gname, own, arrived)
            out.update(zip(names, _unstack_rows(reduced, [shapes[n] for n in names], width, 0)))
        return out

    class LayerExchange:
        @staticmethod
        def begin(layer_grads):
            return to_sibling(HOSTED, {n: g[:, None] if n in COL_BLOCKED else g[None] for n, g in layer_grads.items()
                                       if n in shard_shapes})

        @staticmethod
        def finish(mine, got):
            return from_chips(HOSTED, shard_shapes, mine, got)

    sq, gx, grads, g_shard = local_step(x[0], mem[0], positions[0], loss_target[0], full, gathered0, shards[1:],
                                        shard_shapes, LayerExchange)
    loss = lax.psum(0.5 * jnp.sum(sq) / D_MODEL, ("x", "y", "c"))

    sent, mine = to_sibling(UPFRONT, grads)
    got = [chip_exchange("chip_exchange_" + g[0], p) for g, p in zip(UPFRONT, sent)]
    g_shard.update(from_chips(UPFRONT, {n: w[n].shape for g in UPFRONT for n in g[1]}, mine, got))

    rep_shapes = [w[n].shape for n in REPLICATED]
    rep_all = all_gather("small_grads_all_gather", _pack([grads[n] for n in REPLICATED]))
    g_rep = dict(zip(REPLICATED, _unpack(sum_blocks("small_grads_sum", rep_all), rep_shapes)))

    out_g, out_d, out_m, out_v = [], [], [], []
    for n in WEIGHTS:
        g = g_shard[n] if n in SHARDED else g_rep[n]
        d, nm, nv = adamw("adamw_" + n, w[n], g, m[n], v[n])
        out_g.append(g)
        out_d.append(d)
        out_m.append(nm)
        out_v.append(nv)
    return (loss, gx[None], *out_g, *out_d, *out_m, *out_v)


def kernel(x, mem, positions, mix_norm_g, w_in, ssd_conv_w, ssd_conv_b, ssd_dt_bias, ssd_a_log, ssd_d, ssd_norm_g, ssd_w_out, conv_dw_w, conv_dw_b, conv_ln_g, conv_ln_b, conv_w_out, mla_q_a_g, mla_w_q_b, mla_kv_a_g, mla_w_kv_b, mla_q_norm_g, mla_k_norm_g, mla_w_o, gate_b, w_out, xattn_norm_g, mem_norm_g, xattn_w_q, xattn_w_kv, xattn_q_norm_g, xattn_k_norm_g, xattn_w_o, ffn_norm_g, ffn_w_in, ffn_w_out, loss_target, m_mix_norm_g, m_w_in, m_ssd_conv_w, m_ssd_conv_b, m_ssd_dt_bias, m_ssd_a_log, m_ssd_d, m_ssd_norm_g, m_ssd_w_out, m_conv_dw_w, m_conv_dw_b, m_conv_ln_g, m_conv_ln_b, m_conv_w_out, m_mla_q_a_g, m_mla_w_q_b, m_mla_kv_a_g, m_mla_w_kv_b, m_mla_q_norm_g, m_mla_k_norm_g, m_mla_w_o, m_gate_b, m_w_out, m_xattn_norm_g, m_mem_norm_g, m_xattn_w_q, m_xattn_w_kv, m_xattn_q_norm_g, m_xattn_k_norm_g, m_xattn_w_o, m_ffn_norm_g, m_ffn_w_in, m_ffn_w_out, v_mix_norm_g, v_w_in, v_ssd_conv_w, v_ssd_conv_b, v_ssd_dt_bias, v_ssd_a_log, v_ssd_d, v_ssd_norm_g, v_ssd_w_out, v_conv_dw_w, v_conv_dw_b, v_conv_ln_g, v_conv_ln_b, v_conv_w_out, v_mla_q_a_g, v_mla_w_q_b, v_mla_kv_a_g, v_mla_w_kv_b, v_mla_q_norm_g, v_mla_k_norm_g, v_mla_w_o, v_gate_b, v_w_out, v_xattn_norm_g, v_mem_norm_g, v_xattn_w_q, v_xattn_w_kv, v_xattn_q_norm_g, v_xattn_k_norm_g, v_xattn_w_o, v_ffn_norm_g, v_ffn_w_in, v_ffn_w_out):
    args = locals()
    w = {n: args[n] for n in WEIGHTS}
    m = {n: args["m_" + n] for n in WEIGHTS}
    v = {n: args["v_" + n] for n in WEIGHTS}
    return _step(x, mem, positions, loss_target, w, m, v)
```

```python
from typing import NamedTuple

import jax
import jax.numpy as jnp
from jax import lax
from jax.experimental import pallas as pl
from jax.experimental.pallas import tpu as pltpu

F32 = jnp.float32
BF16 = jnp.bfloat16
HIGHEST = lax.Precision.HIGHEST
MESH_ID = pl.DeviceIdType.MESH

VMEM_LIMIT_BYTES = 56 * 1024 * 1024
LANES = 128

EPS = 1e-6
DEPTH = 4
D_MODEL = 1024
N_DEV = 8
SSD_HEADS = 16
SSD_HEAD_DIM = 64
SSD_STATE = 128
SSD_GROUPS = 4
SSD_INNER = 1024
SSD_TILE = 256
CONV_K = 31
SSD_CONV_K = 4
MLA_HEADS = 8
MLA_NOPE = 128
MLA_ROPE = 64
MLA_V = 128
MLA_Q_RANK = 384
MLA_KV_RANK = 256
ATT_CHUNK = 64
ROPE_THETA = 10000.0
X_HEADS = 4
X_HEAD_DIM = 256
FFN_HIDDEN = 2816
IN_SIZES = (1024, 2048, 16, 2048, 384, 320, 3072)

ADAM_LR = 0.001
ADAM_B1 = 0.9
ADAM_B2 = 0.999
ADAM_EPS = 1e-08
ADAM_WD = 0.01
ADAM_STEP = 10

MM_FULL_K = 3072
ROW_TILE = 256
ATT_BLOCK = 512
XATT_Q_TILE = 512

SHARDED = {
    "w_in": 1, "ssd_conv_w": 1, "ssd_w_out": 0, "conv_dw_w": 1, "conv_w_out": 0, "mla_w_q_b": 1, "mla_w_kv_b": 1,
    "mla_w_o": 0, "gate_b": 1, "w_out": 0, "xattn_w_q": 0, "xattn_w_kv": 1, "xattn_w_o": 0, "ffn_w_in": 1,
    "ffn_w_out": 0,
}
WEIGHTS = ["mix_norm_g", "w_in", "ssd_conv_w", "ssd_conv_b", "ssd_dt_bias", "ssd_a_log", "ssd_d", "ssd_norm_g",
           "ssd_w_out", "conv_dw_w", "conv_dw_b", "conv_ln_g", "conv_ln_b", "conv_w_out", "mla_q_a_g", "mla_w_q_b",
           "mla_kv_a_g", "mla_w_kv_b", "mla_q_norm_g", "mla_k_norm_g", "mla_w_o", "gate_b", "w_out", "xattn_norm_g",
           "mem_norm_g", "xattn_w_q", "xattn_w_kv", "xattn_q_norm_g", "xattn_k_norm_g", "xattn_w_o", "ffn_norm_g",
           "ffn_w_in", "ffn_w_out"]
REPLICATED = [n for n in WEIGHTS if n not in SHARDED]


def _params(sem=None):
    return pltpu.CompilerParams(dimension_semantics=sem, vmem_limit_bytes=VMEM_LIMIT_BYTES)


def _divisor(n, cap, mult):
    if n <= cap:
        return n
    for d in range(cap - cap % mult, 0, -mult):
        if n % d == 0:
            return d
    raise ValueError(f"no tile for {n}")


def _dg(a, b, ca, cb):
    return lax.dot_general(a.astype(BF16), b.astype(BF16), (((ca,), (cb,)), ((), ())), preferred_element_type=F32)


@jax.custom_vjp
def bdot(a, b):
    return _dg(a, b, 1, 0)


bdot.defvjp(lambda a, b: (_dg(a, b, 1, 0), (a, b)), lambda r, g: (_dg(g, r[1], 1, 1), _dg(r[0], g, 0, 0)))


@jax.custom_vjp
def bdot_nt(a, b):
    return _dg(a, b, 1, 1)


bdot_nt.defvjp(lambda a, b: (_dg(a, b, 1, 1), (a, b)), lambda r, g: (_dg(g, r[1], 1, 0), _dg(g, r[0], 0, 0)))


@jax.custom_vjp
def bdot_tn(a, b):
    return _dg(a, b, 0, 0)


bdot_tn.defvjp(lambda a, b: (_dg(a, b, 0, 0), (a, b)), lambda r, g: (_dg(r[1], g, 1, 1), _dg(r[0], g, 1, 0)))


def hdot(a, b):
    return jnp.dot(a, b, precision=HIGHEST, preferred_element_type=F32)


def _iota(shape, dim):
    return lax.broadcasted_iota(jnp.int32, shape, dim)


def _mm(name, a, b, ta=False, tb=False, res=None, narrow_out=False):
    m, k = (a.shape[1], a.shape[0]) if ta else a.shape
    n = b.shape[0] if tb else b.shape[1]
    tm = _divisor(m, 1536, LANES) if ta else _divisor(m, 1024, 8)
    tn = _divisor(n, 1536, LANES)
    tk = k if k <= MM_FULL_K else _divisor(k, 1024, LANES)
    if tk == k and k > 1024:
        tm = _divisor(m, 512, LANES if ta else 8)
    nk = k // tk
    dims = (((0 if ta else 1,), (1 if tb else 0,)), ((), ()))

    out_dtype = BF16 if narrow_out else F32
    own_acc = narrow_out and nk > 1
    assert res is None or not narrow_out

    def body(*refs):
        if res is None:
            a_ref, b_ref, o_ref = refs[:3]
        else:
            a_ref, b_ref, r_ref, o_ref = refs[:4]
        acc = refs[-1] if own_acc else o_ref
        part = lax.dot_general(a_ref[...].astype(BF16), b_ref[...].astype(BF16), dims, preferred_element_type=F32)
        if nk == 1:
            o_ref[...] = (part if res is None else part + r_ref[...]).astype(out_dtype)
        else:
            kk = pl.program_id(2)

            @pl.when(kk == 0)
            def _():
                acc[...] = part if res is None else part + r_ref[...]

            @pl.when(kk != 0)
            def _():
                acc[...] += part

            if own_acc:
                @pl.when(kk == nk - 1)
                def _():
                    o_ref[...] = acc[...].astype(out_dtype)

    a_spec = pl.BlockSpec((tk, tm), lambda i, j, kk: (kk, i)) if ta else pl.BlockSpec((tm, tk), lambda i, j, kk: (i, kk))
    b_spec = pl.BlockSpec((tn, tk), lambda i, j, kk: (j, kk)) if tb else pl.BlockSpec((tk, tn), lambda i, j, kk: (kk, j))
    o_spec = pl.BlockSpec((tm, tn), lambda i, j, kk: (i, j))
    in_specs = [a_spec, b_spec] + ([] if res is None else [o_spec])
    args = (a, b) + (() if res is None else (res,))
    return pl.pallas_call(
        body, grid=(m // tm, n // tn, nk), in_specs=in_specs, out_specs=o_spec,
        out_shape=jax.ShapeDtypeStruct((m, n), out_dtype),
        scratch_shapes=[pltpu.VMEM((tm, tn), F32)] if own_acc else [],
        compiler_params=_params(("parallel", "parallel", "arbitrary")), name=name)(*args)


class Mat(NamedTuple):
    value: jax.Array
    slot: jax.Array


MATRICES = frozenset([
    "w_z", "w_xbc", "w_dt", "w_glu", "w_q", "w_ckv", "w_kr", "w_gate", "ssd_w_out", "conv_w_out", "w_qn", "w_qr",
    "w_kn", "w_v", "mla_w_o", "w_out", "xattn_w_q", "w_xk", "w_xv", "xattn_w_o", "w_ffn_gate", "w_ffn_up",
    "ffn_w_out"])


class Act(NamedTuple):
    value: jax.Array
    slot: jax.Array


def _operand(a):
    return (a.value, a.slot) if isinstance(a, Act) else (a, a)


class Out(NamedTuple):
    value: jax.Array
    handle: jax.Array


def _handle(out):
    return jnp.zeros(out.shape, BF16)


def matmul(name, a, mat, res=None, to_tmap=False):
    w, slot = mat
    a, a_slot = _operand(a)
    if res is None:
        @jax.custom_vjp
        def run(a, a_slot, w, slot):
            out = _mm(name, a, w)
            return Out(out, _handle(out)) if to_tmap else out

        def fwd(a, a_slot, w, slot):
            return run(a, a_slot, w, slot), (a, w)

        def bwd(r, g):
            g = g.handle if to_tmap else g
            return None, _mm(name + "_da", g, r[1], tb=True), None, _mm(name + "_dw", r[0], g, ta=True, narrow_out=True)

        run.defvjp(fwd, bwd)
        return run(a, a_slot, w, slot)

    @jax.custom_vjp
    def run_res(a, a_slot, w, slot, res):
        return _mm(name, a, w, res=res)

    def fwd_res(a, a_slot, w, slot, res):
        return run_res(a, a_slot, w, slot, res), (a, w)

    def bwd_res(r, g):
        return None, _mm(name + "_da", g, r[1], tb=True), None, _mm(name + "_dw", r[0], g, ta=True, narrow_out=True), g

    run_res.defvjp(fwd_res, bwd_res)
    return run_res(a, a_slot, w, slot, res)


def multi_matmul(name, a, mats, keys, to_tmap=()):
    ws, slots = tuple(m.value for m in mats), tuple(m.slot for m in mats)
    a, a_slot = _operand(a)

    @jax.custom_vjp
    def run(a, a_slot, ws, slots):
        outs = [_mm(f"{name}_{k}", a, w) for k, w in zip(keys, ws)]
        return tuple(Out(o, _handle(o)) if k in to_tmap else o for k, o in zip(keys, outs))

    def fwd(a, a_slot, ws, slots):
        return run(a, a_slot, ws, slots), (a, ws)

    def bwd(r, gs):
        a, ws = r
        gs = [g.handle if k in to_tmap else g for k, g in zip(keys, gs)]
        da = None
        for k, w, g in zip(keys, ws, gs):
            da = _mm(f"{name}_{k}_da", g, w, tb=True, res=da)
        dws = tuple(_mm(f"{name}_{k}_dw", a, g, ta=True, narrow_out=True) for k, g in zip(keys, gs))
        return None, da, tuple(None for _ in ws), dws

    run.defvjp(fwd, bwd)
    return run(a, a_slot, ws, slots)


def tmap(name, f, grid, ins, outs, through=None, narrow=()):
    handled = [k for k, x in enumerate(ins) if isinstance(x[0], Out)]
    assert all(ins[k][3] == "t" and k != through for k in handled)
    arrays = [x[0].value if isinstance(x[0], Out) else x[0] for x in ins] + [ins[k][0].handle for k in handled]
    kinds = [x[3] for x in ins]
    in_specs = [pl.BlockSpec(x[1], x[2]) for x in ins]
    out_specs = [pl.BlockSpec(x[1], x[2]) for x in outs]
    out_shape = [jax.ShapeDtypeStruct(x[0], BF16 if k in narrow else F32) for k, x in enumerate(outs)]
    n_in, n_out = len(ins), len(outs)
    n_through = 0 if through is None else 1
    assert through is None or kinds[through] == "t"
    didx = [k for k, kd in enumerate(kinds) if kd != "n"]

    def fwd_call(*arrs):
        def body(*refs):
            pids = (pl.program_id(0), pl.program_id(1))
            vals = f(pids, *[r[...] for r in refs[:n_in]])
            for r, v in zip(refs[n_in:], vals):
                r[...] = v.astype(r.dtype)

        return pl.pallas_call(body, grid=grid, in_specs=in_specs, out_specs=out_specs, out_shape=out_shape,
                              compiler_params=_params(("arbitrary", "arbitrary")), name=name)(*arrs)

    def bwd_call(arrs, cts):
        def body(*refs):
            o, i = pl.program_id(0), pl.program_id(1)
            vals = [r[...] for r in refs[:n_in]]

            def g(*dv):
                full = list(vals)
                for k, v in zip(didx, dv):
                    full[k] = v
                return tuple(f((o, i), *full))

            _, vjp = jax.vjp(g, *[vals[k] for k in didx])
            grads = vjp(tuple(r[...] for r in refs[n_in:n_in + n_out]))
            for k, gr, r in zip(didx, grads, refs[n_in + n_out + n_through:]):
                if k == through:
                    r[...] = gr + refs[n_in + n_out][...]
                elif kinds[k] == "t":
                    r[...] = gr.astype(r.dtype)
                else:
                    first = (i == 0) if kinds[k] == "ai" else jnp.logical_and(o == 0, i == 0)

                    @pl.when(first)
                    def _(r=r, gr=gr):
                        r[...] = gr

                    @pl.when(jnp.logical_not(first))
                    def _(r=r, gr=gr):
                        r[...] += gr

        g_specs = [in_specs[k] for k in didx]
        g_shape = [jax.ShapeDtypeStruct(arrs[k].shape, BF16 if k in handled else F32) for k in didx]
        ct_specs = out_specs + ([in_specs[through]] if n_through else [])
        return pl.pallas_call(body, grid=grid, in_specs=in_specs + ct_specs, out_specs=g_specs, out_shape=g_shape,
                              compiler_params=_params(("arbitrary", "arbitrary")), name=name + "_bwd")(*arrs, *cts)

    @jax.custom_vjp
    def run(*arrs):
        res = [Act(o, jnp.zeros(o.shape, F32)) if k in narrow else o for k, o in enumerate(fwd_call(*arrs[:n_in]))]
        return tuple(res) + ((arrs[through],) if n_through else ())

    def run_fwd(*arrs):
        return run(*arrs), arrs[:n_in]

    def run_bwd(arrs, cts):
        cts = [c.slot if isinstance(c, Act) else c for c in cts]
        gs = bwd_call(arrs, cts)
        full = [None] * (n_in + len(handled))
        for k, g in zip(didx, gs):
            full[n_in + handled.index(k) if k in handled else k] = g
        return tuple(full)

    run.defvjp(run_fwd, run_bwd)
    return run(*arrays)


def _rows(arr, tile, kind="t"):
    width = (arr.value if isinstance(arr, Out) else arr).shape[1]
    return (arr, (tile, width), lambda o, i: (i, 0), kind)


def _whole(arr, kind="ag"):
    return (arr, arr.shape, lambda o, i: (0, 0), kind)


def _row_out(t, n, tile):
    return ((t, n), (tile, n), lambda o, i: (i, 0))


def _rms(x, g, n=None):
    ms = jnp.sum(x * x, axis=-1, keepdims=True) / (x.shape[-1] if n is None else n)
    return x * lax.rsqrt(ms + EPS) * g


def rms_norm(name, x, g):
    t, n = (x.value if isinstance(x, Out) else x).shape
    tile = min(ROW_TILE, t)
    return tmap(name, lambda p, x, g: (_rms(x, g),), (1, t // tile), [_rows(x, tile), _whole(g)],
                [_row_out(t, n, tile)], narrow=(0,))[0]


def rms_norm_through(name, x, g):
    t, n = (x.value if isinstance(x, Out) else x).shape
    tile = min(ROW_TILE, t)
    return tmap(name, lambda p, x, g: (_rms(x, g),), (1, t // tile), [_rows(x, tile), _whole(g)],
                [_row_out(t, n, tile)], through=0, narrow=(0,))


def rms_norm_nograd_x(name, x, g):
    t, n = (x.value if isinstance(x, Out) else x).shape
    tile = min(ROW_TILE, t)
    return tmap(name, lambda p, x, g: (_rms(x, g),), (1, t // tile), [_rows(x, tile, "n"), _whole(g)],
                [_row_out(t, n, tile)], narrow=(0,))[0]


def _glu_f(p, glu):
    h = glu.shape[1] // 2
    return (glu[:, :h] * jax.nn.sigmoid(glu[:, h:]),)


def _ln_silu_f(p, v, g, b):
    mu = jnp.mean(v, axis=-1, keepdims=True)
    xc = v - mu
    var = jnp.mean(xc * xc, axis=-1, keepdims=True)
    return (jax.nn.silu(xc * lax.rsqrt(var + EPS) * g + b),)


def _ssd_gate_norm_f(p, y, z, g):
    v = y * jax.nn.silu(z)
    w = SSD_INNER // SSD_GROUPS
    parts = []
    for k in range(SSD_GROUPS):
        vg = v[:, k * w:(k + 1) * w]
        parts.append(vg * lax.rsqrt(jnp.mean(vg * vg, axis=-1, keepdims=True) + EPS))
    return (jnp.concatenate(parts, axis=1) * g,)


def _merge_f(p, gl, gb, y0, y1, y2):
    g = jax.nn.sigmoid(gl + gb)
    d = D_MODEL
    return (g[:, :d] * y0 + g[:, d:2 * d] * y1 + g[:, 2 * d:] * y2,)


def _swiglu_f(p, gate, up):
    return (jax.nn.silu(gate) * up,)


def _rot_matrix():
    r, c = _iota((LANES, LANES), 0), _iota((LANES, LANES), 1)
    h = MLA_ROPE // 2
    plus = jnp.logical_and(c >= h, jnp.logical_and(c < 2 * h, r == c - h))
    minus = jnp.logical_and(c < h, r == c + h)
    return plus.astype(F32) - minus.astype(F32)


def _rope(x, cosf, sinf):
    return x * cosf + hdot(x, _rot_matrix()) * sinf


def _per_head(f, x):
    return jnp.concatenate([f(x[:, h * LANES:(h + 1) * LANES]) for h in range(x.shape[1] // LANES)], axis=1)


def _k_nope_f(p, kn_raw, kg):
    return (_per_head(lambda x: _rms(x, kg[:, :MLA_NOPE]), kn_raw),)


def _k_rope_f(p, kr_raw, cosf, sinf, kg):
    return (_rope(_rms(kr_raw, kg[:, MLA_NOPE:], n=MLA_ROPE), cosf, sinf),)


def _q_prep_f(p, qn_raw, qr_raw, cosf, sinf, qg):
    qn = _per_head(lambda x: _rms(x, qg[:, :MLA_NOPE]), qn_raw)
    qr = _per_head(lambda x: _rope(_rms(x, qg[:, MLA_NOPE:], n=MLA_ROPE), cosf, sinf), qr_raw)
    return qn, qr


def _softmax(s):
    m = jnp.max(s, axis=-1, keepdims=True)
    e = jnp.exp(s - m)
    return e / jnp.sum(e, axis=-1, keepdims=True)


def _xattn_f(p, q, k, v, qg, kg):
    s = bdot_nt(_rms(q, qg), _rms(k, kg)) * (X_HEAD_DIM ** -0.5)
    return (bdot(_softmax(s), v),)


ATT_SCALE = (MLA_NOPE + MLA_ROPE) ** -0.5
NT_DIMS = (((1,), (1,)), ((), ()))
NN_DIMS = (((1,), (0,)), ((), ()))
TN_DIMS = (((0,), (0,)), ((), ()))


def _att_specs(t, blk):
    q_spec = pl.BlockSpec((blk, LANES), lambda h, i: (i, h))
    k_spec = pl.BlockSpec((t, LANES), lambda h, i: (0, h))
    shared = pl.BlockSpec((t, LANES), lambda h, i: (0, 0))
    lse_spec = pl.BlockSpec((None, blk, 1), lambda h, i: (h, i, 0))
    return q_spec, k_spec, shared, lse_spec


def _diagonal_mask(blk):
    return (_iota((blk, blk), 1) // ATT_CHUNK) <= (_iota((blk, blk), 0) // ATT_CHUNK)


def _att_keys(kn_ref, kr_ref, j, blk):
    ks = pl.ds(pl.multiple_of(j * blk, blk), blk)
    return ks, jnp.concatenate([kn_ref[ks, :], kr_ref[ks, :]], axis=1).astype(BF16)


def _att_fwd(name, qn, qr, kn, kr, v, hosted=()):
    t, width = qn.shape
    heads = width // LANES
    blk = min(ATT_BLOCK, t)
    nh = len(hosted)

    def body(qn_ref, qr_ref, kn_ref, kr_ref, v_ref, *rest):
        shard_refs, (o_ref, lse_ref), rest = rest[:nh], rest[nh:nh + 2], rest[nh + 2:]
        full_refs, sems = rest[:nh], rest[nh:]
        i = pl.program_id(1)
        gathers = [_gather_copies(shard_refs[k], full_refs[k], *sems[3 * k:3 * k + 3]) for k in range(nh)]
        if nh:
            @pl.when(jnp.logical_and(pl.program_id(0) == 0, i == 0))
            def _():
                for start, _, _ in gathers:
                    start()

            @pl.when(jnp.logical_and(pl.program_id(0) == heads - 1, i == 0))
            def _():
                for _, pass_on, _ in gathers:
                    pass_on()
        q = jnp.concatenate([qn_ref[...], qr_ref[...]], axis=1).astype(BF16)

        def scores(j):
            _, k = _att_keys(kn_ref, kr_ref, j, blk)
            return lax.dot_general(q, k, NT_DIMS, preferred_element_type=F32)

        def weighted_values(p, j):
            ks = pl.ds(pl.multiple_of(j * blk, blk), blk)
            return lax.dot_general(p, v_ref[ks, :].astype(BF16), NN_DIMS, preferred_element_type=F32)

        def softmax_step(s, m, l):
            m_new = jnp.maximum(m, jnp.max(s, axis=1, keepdims=True))
            alpha = jnp.exp(m - m_new)
            p = jnp.exp(s - m_new)
            return m_new, alpha, alpha * l + jnp.sum(p, axis=1, keepdims=True), p.astype(BF16)

        def step(j, carry):
            s, p_prev, m, l, acc = carry
            s_next = scores(j + 1)
            pv_prev = weighted_values(p_prev, jnp.maximum(j - 1, 0))
            m, alpha, l, p = softmax_step(s * ATT_SCALE, m, l)
            return s_next, p, m, l, alpha * (acc + pv_prev)

        init = (scores(0), jnp.zeros((blk, blk), BF16), jnp.full((blk, 1), -1e30, F32), jnp.zeros((blk, 1), F32),
                jnp.zeros((blk, LANES), F32))
        s, p_prev, m, l, acc = lax.fori_loop(0, i, step, init)
        pv_prev = weighted_values(p_prev, jnp.maximum(i - 1, 0))
        s = jnp.where(_diagonal_mask(blk), s * ATT_SCALE, -1e30)
        m, alpha, l, p = softmax_step(s, m, l)
        acc = alpha * (acc + pv_prev) + weighted_values(p, i)
        o_ref[...] = acc / l
        lse_ref[...] = m + jnp.log(l)
        if nh:
            @pl.when(jnp.logical_and(pl.program_id(0) == heads - 1, i == t // blk - 1))
            def _():
                for _, _, finish in gathers:
                    finish()

    q_spec, k_spec, shared, lse_spec = _att_specs(t, blk)
    sem_shapes = [pltpu.SemaphoreType.DMA((7,)), pltpu.SemaphoreType.DMA((7,)), pltpu.SemaphoreType.DMA] * nh
    return pl.pallas_call(
        body, grid=(heads, t // blk), in_specs=[q_spec, q_spec, k_spec, shared, k_spec] + [HBM_SPEC] * nh,
        out_specs=[q_spec, lse_spec] + [HBM_SPEC] * nh,
        out_shape=[jax.ShapeDtypeStruct((t, width), F32), jax.ShapeDtypeStruct((heads, t, 1), F32)] +
                  [jax.ShapeDtypeStruct((N_DEV,) + s.shape, s.dtype) for s in hosted],
        scratch_shapes=sem_shapes, compiler_params=_params(("arbitrary", "arbitrary")), name=name)(
        qn, qr, kn, kr, v, *hosted)


def _att_bwd(name, qn, qr, kn, kr, v, o, lse, do, hosted=()):
    t, width = qn.shape
    heads = width // LANES
    blk = min(ATT_BLOCK, t)
    nh = len(hosted)

    def body(qn_ref, qr_ref, kn_ref, kr_ref, v_ref, o_ref, lse_ref, do_ref, *rest):
        sent_refs, (dqn_ref, dqr_ref, dkn_ref, dkr_ref, dv_ref), rest = rest[:nh], rest[nh:nh + 5], rest[nh + 5:]
        got_refs, sems = rest[:nh], rest[nh:]
        h, i = pl.program_id(0), pl.program_id(1)
        exchanges = [_chip_copies(sent_refs[k], got_refs[k], *sems[2 * k:2 * k + 2]) for k in range(nh)]
        if nh:
            @pl.when(jnp.logical_and(h == 0, i == 0))
            def _():
                for start, _ in exchanges:
                    start()

        @pl.when(i == 0)
        def _():
            dkn_ref[...] = jnp.zeros_like(dkn_ref)
            dv_ref[...] = jnp.zeros_like(dv_ref)

        @pl.when(jnp.logical_and(h == 0, i == 0))
        def _():
            dkr_ref[...] = jnp.zeros_like(dkr_ref)

        q = jnp.concatenate([qn_ref[...], qr_ref[...]], axis=1).astype(BF16)
        do = do_ref[...]
        do16 = do.astype(BF16)
        delta = jnp.sum(do * o_ref[...], axis=1, keepdims=True)
        lse = lse_ref[...]

        def issue(j):
            ks, k = _att_keys(kn_ref, kr_ref, j, blk)
            s = lax.dot_general(q, k, NT_DIMS, preferred_element_type=F32)
            dp = lax.dot_general(do16, v_ref[ks, :].astype(BF16), NT_DIMS, preferred_element_type=F32)
            return s, dp

        def retire(p, ds, j, dq):
            ks, k = _att_keys(kn_ref, kr_ref, j, blk)
            dv_ref[ks, :] += lax.dot_general(p, do16, TN_DIMS, preferred_element_type=F32)
            dk = lax.dot_general(ds, q, TN_DIMS, preferred_element_type=F32)
            dkn_ref[ks, :] += dk[:, :LANES]
            dkr_ref[ks, :] += dk[:, LANES:]
            return dq + lax.dot_general(ds, k, NN_DIMS, preferred_element_type=F32)

        def probs(s, dp, masked):
            s = s * ATT_SCALE
            if masked:
                s = jnp.where(_diagonal_mask(blk), s, -1e30)
            p = jnp.exp(s - lse)
            return p.astype(BF16), (p * (dp - delta) * ATT_SCALE).astype(BF16)

        def step(j, carry):
            s, dp, p_prev, ds_prev, dq = carry
            s_next, dp_next = issue(j + 1)
            dq = retire(p_prev, ds_prev, jnp.maximum(j - 1, 0), dq)
            p, ds = probs(s, dp, False)
            return s_next, dp_next, p, ds, dq

        none = jnp.zeros((blk, blk), BF16)
        s, dp, p_prev, ds_prev, dq = lax.fori_loop(0, i, step,
                                                   issue(0) + (none, none, jnp.zeros((blk, 2 * LANES), F32)))
        dq = retire(p_prev, ds_prev, jnp.maximum(i - 1, 0), dq)
        p, ds = probs(s, dp, True)
        dq = retire(p, ds, i, dq)
        dqn_ref[...] = dq[:, :LANES]
        dqr_ref[...] = dq[:, LANES:]
        if nh:
            @pl.when(jnp.logical_and(h == heads - 1, i == t // blk - 1))
            def _():
                for _, finish in exchanges:
                    finish()

    q_spec, k_spec, shared, lse_spec = _att_specs(t, blk)
    big, one = jax.ShapeDtypeStruct((t, width), F32), jax.ShapeDtypeStruct((t, LANES), F32)
    sem_shapes = [pltpu.SemaphoreType.DMA((3,)), pltpu.SemaphoreType.DMA((3,))] * nh
    return pl.pallas_call(
        body, grid=(heads, t // blk),
        in_specs=[q_spec, q_spec, k_spec, shared, k_spec, q_spec, lse_spec, q_spec] + [HBM_SPEC] * nh,
        out_specs=[q_spec, q_spec, k_spec, shared, k_spec] + [HBM_SPEC] * nh,
        out_shape=[big, big, big, one, big] + [jax.ShapeDtypeStruct((3,) + p.shape[1:], p.dtype) for p in hosted],
        scratch_shapes=sem_shapes, compiler_params=_params(("arbitrary", "arbitrary")), name=name)(
        qn, qr, kn, kr, v, o, lse, do, *hosted)


def mla_attention(name, qn, qr, kn, kr, v, hosted=(), carriers=()):
    @jax.custom_vjp
    def run(qn, qr, kn, kr, v, hosted, carriers):
        o, _, *gathered = _att_fwd(name, qn, qr, kn, kr, v, hosted)
        return o, tuple(gathered), tuple(jnp.zeros((4,) + c.shape[1:], c.dtype) for c in carriers)

    def fwd(qn, qr, kn, kr, v, hosted, carriers):
        o, lse, *gathered = _att_fwd(name, qn, qr, kn, kr, v, hosted)
        handles = tuple(jnp.zeros((4,) + c.shape[1:], c.dtype) for c in carriers)
        return (o, tuple(gathered), handles), (qn, qr, kn, kr, v, o, lse)

    def bwd(r, g):
        do, _, sent = g
        dqn, dqr, dkn, dkr, dv, *got = _att_bwd(name + "_bwd", *r, do, tuple(sent))
        return dqn, dqr, dkn, dkr, dv, (None,) * len(hosted), tuple(got)

    run.defvjp(fwd, bwd)
    return run(qn, qr, kn, kr, v, tuple(hosted), tuple(carriers))


def _shift_down(v, s, rows):
    return v if s == 0 else jnp.where(rows >= s, pltpu.roll(v, s, 0), 0.0)


def _shift_up(v, s, rows):
    t = v.shape[0]
    return v if s == 0 else jnp.where(rows < t - s, pltpu.roll(v, t - s, 0), 0.0)


def _dwconv_fwd(name, x, w, b):
    t, c = x.shape
    kw = w.shape[0]

    def body(x_ref, w_ref, b_ref, y_ref):
        x = x_ref[...]
        rows = _iota(x.shape, 0)
        acc = jnp.zeros_like(x) + b_ref[...]
        for k in range(kw):
            acc = acc + w_ref[k:k + 1, :] * _shift_down(x, kw - 1 - k, rows)
        y_ref[...] = acc

    col = lambda i: (0, i)
    return pl.pallas_call(
        body, grid=(c // LANES,),
        in_specs=[pl.BlockSpec((t, LANES), col), pl.BlockSpec((kw, LANES), col), pl.BlockSpec((1, LANES), col)],
        out_specs=pl.BlockSpec((t, LANES), col), out_shape=jax.ShapeDtypeStruct((t, c), F32),
        compiler_params=_params(("arbitrary",)), name=name)(x, w, b)


def _dwconv_bwd(name, x, w, dy):
    t, c = x.shape
    kw = w.shape[0]

    def body(x_ref, w_ref, dy_ref, dx_ref, dw_ref, db_ref):
        x, dy = x_ref[...], dy_ref[...]
        rows = _iota(x.shape, 0)
        dx = jnp.zeros_like(x)
        for k in range(kw):
            s = kw - 1 - k
            dx = dx + w_ref[k:k + 1, :] * _shift_up(dy, s, rows)
            dw_ref[k:k + 1, :] = jnp.sum(dy * _shift_down(x, s, rows), axis=0, keepdims=True)
        dx_ref[...] = dx
        db_ref[...] = jnp.sum(dy, axis=0, keepdims=True)

    col = lambda i: (0, i)
    big, wsp, bsp = pl.BlockSpec((t, LANES), col), pl.BlockSpec((kw, LANES), col), pl.BlockSpec((1, LANES), col)
    return pl.pallas_call(
        body, grid=(c // LANES,), in_specs=[big, wsp, big], out_specs=[big, wsp, bsp],
        out_shape=[jax.ShapeDtypeStruct((t, c), F32), jax.ShapeDtypeStruct((kw, c), F32),
                   jax.ShapeDtypeStruct((1, c), F32)],
        compiler_params=_params(("arbitrary",)), name=name)(x, w, dy)


def dwconv(name, x, w, b):
    @jax.custom_vjp
    def run(x, w, b):
        return _dwconv_fwd(name, x, w, b)

    def fwd(x, w, b):
        return run(x, w, b), (x, w)

    def bwd(r, g):
        return tuple(_dwconv_bwd(name + "_bwd", r[0], r[1], g))

    run.defvjp(fwd, bwd)
    return run(x, w, b)


def _ssd_tile(xc, dtr, dtb, alog, dsk, prev):
    ln = xc.shape[0]
    gw = SSD_INNER // SSD_GROUPS
    ns = SSD_STATE
    xs = jax.nn.silu(xc[:, :SSD_INNER])
    bm = jax.nn.silu(xc[:, SSD_INNER:SSD_INNER + SSD_GROUPS * ns])
    cm = jax.nn.silu(xc[:, SSD_INNER + SSD_GROUPS * ns:])
    dt = jax.nn.softplus(dtr + dtb)
    a = dt * (-jnp.exp(alog))
    expand = (_iota((LANES, SSD_INNER), 0) == _iota((LANES, SSD_INNER), 1) // SSD_HEAD_DIM).astype(F32)
    causal = _iota((ln, ln), 0) >= _iota((ln, ln), 1)
    acs_h = hdot(causal.astype(F32), a)
    acs_c = hdot(acs_h, expand)
    dt_c = hdot(dt, expand)

    def row_per_column(v):
        return jnp.mean(hdot(jnp.broadcast_to(v, (8, LANES)), expand), axis=0, keepdims=True)

    aend_c = row_per_column(jnp.sum(a, axis=0, keepdims=True))
    xdt = xs * dt_c
    to_end = xdt * jnp.exp(aend_c - acs_c)
    from_start = jnp.exp(acs_c)
    acs_ht = acs_h.T
    lane_h, sub_h = _iota((1, LANES), 1), _iota((LANES, 1), 0)
    head_of_col = _iota((1, gw), 1) // SSD_HEAD_DIM
    ys, states = [], []
    for g in range(SSD_GROUPS):
        cg = cm[:, g * ns:(g + 1) * ns]
        bg = bm[:, g * ns:(g + 1) * ns]
        cols = slice(g * gw, (g + 1) * gw)
        y = bdot(cg, prev[:, cols]) * from_start[:, cols]
        states.append(bdot_tn(bg, to_end[:, cols]))
        cb = bdot_nt(cg, bg)
        for r in range(gw // SSD_HEAD_DIM):
            h = g * (gw // SSD_HEAD_DIM) + r
            col = jnp.sum(jnp.where(lane_h == h, acs_h, 0.0), axis=1, keepdims=True)
            row = jnp.sum(jnp.where(sub_h == h, acs_ht, 0.0), axis=0, keepdims=True)
            decay = jnp.exp(jnp.where(causal, col - row, -1e30))
            y = y + jnp.where(head_of_col == r, bdot(cb * decay, xdt[:, cols]), 0.0)
        ys.append(y)
    y = jnp.concatenate(ys, axis=1) + row_per_column(dsk) * xs
    new = prev * jnp.exp(aend_c) + jnp.concatenate(states, axis=1)
    return y, new


def _ssd_fwd(name, xc, dtr, dtb, alog, dsk):
    t = xc.shape[0]
    ln = min(SSD_TILE, t)
    nt = t // ln

    def body(xc_ref, dtr_ref, dtb_ref, alog_ref, dsk_ref, y_ref, prev_ref, carry):
        @pl.when(pl.program_id(0) == 0)
        def _():
            carry[...] = jnp.zeros_like(carry)

        prev = carry[...]
        prev_ref[...] = prev
        y, new = _ssd_tile(xc_ref[...], dtr_ref[...], dtb_ref[...], alog_ref[...], dsk_ref[...], prev)
        y_ref[...] = y
        carry[...] = new

    row = lambda i: (i, 0)
    par = pl.BlockSpec((1, LANES), lambda i: (0, 0))
    return pl.pallas_call(
        body, grid=(nt,),
        in_specs=[pl.BlockSpec((ln, xc.shape[1]), row), pl.BlockSpec((ln, LANES), row), par, par, par],
        out_specs=[pl.BlockSpec((ln, SSD_INNER), row), pl.BlockSpec((None, SSD_STATE, SSD_INNER), lambda i: (i, 0, 0))],
        out_shape=[jax.ShapeDtypeStruct((t, SSD_INNER), F32), jax.ShapeDtypeStruct((nt, SSD_STATE, SSD_INNER), F32)],
        scratch_shapes=[pltpu.VMEM((SSD_STATE, SSD_INNER), F32)],
        compiler_params=_params(("arbitrary",)), name=name)(xc, dtr, dtb, alog, dsk)


def _ssd_bwd(name, xc, dtr, dtb, alog, dsk, prevs, dy):
    t = xc.shape[0]
    ln = min(SSD_TILE, t)
    nt = t // ln

    def body(xc_ref, dtr_ref, dtb_ref, alog_ref, dsk_ref, prev_ref, dy_ref, dxc_ref, ddtr_ref, ddtb_ref, dalog_ref,
             ddsk_ref, dcarry):
        i = pl.program_id(0)

        @pl.when(i == 0)
        def _():
            dcarry[...] = jnp.zeros_like(dcarry)

        _, vjp = jax.vjp(_ssd_tile, xc_ref[...], dtr_ref[...], dtb_ref[...], alog_ref[...], dsk_ref[...], prev_ref[...])
        dxc, ddtr, ddtb, dalog, ddsk, dprev = vjp((dy_ref[...], dcarry[...]))
        dxc_ref[...] = dxc
        ddtr_ref[...] = ddtr
        dcarry[...] = dprev
        for r, gr in ((ddtb_ref, ddtb), (dalog_ref, dalog), (ddsk_ref, ddsk)):
            @pl.when(i == 0)
            def _(r=r, gr=gr):
                r[...] = gr

            @pl.when(i != 0)
            def _(r=r, gr=gr):
                r[...] += gr

    row = lambda i: (nt - 1 - i, 0)
    par = pl.BlockSpec((1, LANES), lambda i: (0, 0))
    big, dts = pl.BlockSpec((ln, xc.shape[1]), row), pl.BlockSpec((ln, LANES), row)
    par_shape = jax.ShapeDtypeStruct((1, LANES), F32)
    return pl.pallas_call(
        body, grid=(nt,),
        in_specs=[big, dts, par, par, par, pl.BlockSpec((None, SSD_STATE, SSD_INNER), lambda i: (nt - 1 - i, 0, 0)),
                  pl.BlockSpec((ln, SSD_INNER), row)],
        out_specs=[big, dts, par, par, par],
        out_shape=[jax.ShapeDtypeStruct(xc.shape, F32), jax.ShapeDtypeStruct(dtr.shape, F32), par_shape, par_shape,
                   par_shape],
        scratch_shapes=[pltpu.VMEM((SSD_STATE, SSD_INNER), F32)],
        compiler_params=_params(("arbitrary",)), name=name)(xc, dtr, dtb, alog, dsk, prevs, dy)


def ssd_scan(name, xc, dtr, dtb, alog, dsk):
    @jax.custom_vjp
    def run(xc, dtr, dtb, alog, dsk):
        return _ssd_fwd(name, xc, dtr, dtb, alog, dsk)[0]

    def fwd(xc, dtr, dtb, alog, dsk):
        y, prevs = _ssd_fwd(name, xc, dtr, dtb, alog, dsk)
        return y, (xc, dtr, dtb, alog, dsk, prevs)

    def bwd(r, g):
        return tuple(_ssd_bwd(name + "_bwd", *r, g))

    run.defvjp(fwd, bwd)
    return run(xc, dtr, dtb, alog, dsk)


def loss_head(y, target):
    t, n = y.shape
    tile = min(ROW_TILE, t)

    def body(y_ref, t_ref, dy_ref, acc_ref):
        d = y_ref[...] - t_ref[...]
        dy_ref[...] = d * (1.0 / n)

        @pl.when(pl.program_id(0) == 0)
        def _():
            acc_ref[...] = jnp.zeros_like(acc_ref)

        acc_ref[...] += jnp.sum(d * d, axis=0, keepdims=True)

    row = pl.BlockSpec((tile, n), lambda i: (i, 0))
    dy, acc = pl.pallas_call(
        body, grid=(t // tile,), in_specs=[row, row], out_specs=[row, pl.BlockSpec((1, n), lambda i: (0, 0))],
        out_shape=[jax.ShapeDtypeStruct((t, n), F32), jax.ShapeDtypeStruct((1, n), F32)],
        compiler_params=_params(("arbitrary",)), name="loss_head")(y, target)
    return acc, dy


def adamw(name, w, g, m, v):
    shape = w.shape
    cols = shape[-1]
    rows = w.size // cols
    tile = _divisor(rows, 512, 8) if rows % 8 == 0 else rows

    def body(w_ref, g_ref, m_ref, v_ref, d_ref, nm_ref, nv_ref):
        g = g_ref[...]
        m = ADAM_B1 * m_ref[...] + (1.0 - ADAM_B1) * g
        v = ADAM_B2 * v_ref[...] + (1.0 - ADAM_B2) * jnp.square(g)
        m_hat = m / (1.0 - ADAM_B1 ** ADAM_STEP)
        v_hat = v / (1.0 - ADAM_B2 ** ADAM_STEP)
        d_ref[...] = -ADAM_LR * (m_hat / (jnp.sqrt(v_hat) + ADAM_EPS) + ADAM_WD * w_ref[...])
        nm_ref[...] = m
        nv_ref[...] = v

    spec = pl.BlockSpec((tile, cols), lambda i: (i, 0))
    two_d = jax.ShapeDtypeStruct((rows, cols), F32)
    outs = pl.pallas_call(body, grid=(rows // tile,), in_specs=[spec] * 4, out_specs=[spec] * 3, out_shape=[two_d] * 3,
                          compiler_params=_params(("arbitrary",)), name=name)(
        *[a.reshape(rows, cols) for a in (w, g, m, v)])
    return [o.reshape(shape) for o in outs]


HBM_SPEC = pl.BlockSpec(memory_space=pl.ANY)


def _position():
    return lax.axis_index("x"), lax.axis_index("y"), lax.axis_index("c")


def _gather_copies(x_ref, out_ref, send_sems, recv_sems, local_sem):
    x, y, c = _position()
    me, sibling = (x, y, c), (x, y, 1 - c)
    chips = [(1 - x, y), (x, 1 - y), (1 - x, 1 - y)]

    def block(px, py, pc):
        return out_ref.at[4 * px + 2 * py + pc]

    def copy(k, blk, to, src=None):
        return pltpu.make_async_remote_copy(
            src_ref=block(*blk) if src is None else src, dst_ref=block(*blk), send_sem=send_sems.at[k],
            recv_sem=recv_sems.at[k], device_id=to, device_id_type=MESH_ID)

    mine = pltpu.make_async_copy(x_ref, block(*me), local_sem)
    first = [copy(0, me, sibling, src=x_ref)]
    first += [copy(1 + j, me, (*chip, c), src=x_ref) for j, chip in enumerate(chips)]
    passed = [copy(4 + j, (*chip, c), sibling) for j, chip in enumerate(chips)]

    def start():
        mine.start()
        for cp in first:
            cp.start()

    def pass_on():
        for j, chip in enumerate(chips):
            copy(1 + j, (*chip, c), me).wait_recv()
            passed[j].start()

    def finish():
        copy(0, sibling, me).wait_recv()
        for j, chip in enumerate(chips):
            copy(4 + j, (*chip, 1 - c), me).wait_recv()
        for cp in first + passed:
            cp.wait_send()
        mine.wait()

    return start, pass_on, finish


def all_gather(name, shard):
    def body(x_ref, out_ref, send_sems, recv_sems, local_sem):
        start, pass_on, finish = _gather_copies(x_ref, out_ref, send_sems, recv_sems, local_sem)
        start()
        pass_on()
        finish()

    return pl.pallas_call(
        body, in_specs=[HBM_SPEC], out_specs=HBM_SPEC,
        out_shape=jax.ShapeDtypeStruct((N_DEV,) + shard.shape, shard.dtype),
        scratch_shapes=[pltpu.SemaphoreType.DMA((7,)), pltpu.SemaphoreType.DMA((7,)), pltpu.SemaphoreType.DMA],
        name=name)(shard)


def pair_exchange(name, g):
    def body(g_ref, out_ref, send_sem, recv_sem):
        x, y, c = _position()
        cp = pltpu.make_async_remote_copy(src_ref=g_ref.at[1 - c], dst_ref=out_ref, send_sem=send_sem,
                                          recv_sem=recv_sem, device_id=(x, y, 1 - c), device_id_type=MESH_ID)
        cp.start()
        cp.wait()

    return pl.pallas_call(
        body, in_specs=[HBM_SPEC], out_specs=HBM_SPEC, out_shape=jax.ShapeDtypeStruct(g.shape[1:], g.dtype),
        scratch_shapes=[pltpu.SemaphoreType.DMA, pltpu.SemaphoreType.DMA], name=name)(g)


def _chip_copies(p_ref, out_ref, send_sems, recv_sems):
    x, y, c = _position()
    copies = [pltpu.make_async_remote_copy(
        src_ref=p_ref.at[2 * px + py], dst_ref=out_ref.at[j], send_sem=send_sems.at[j], recv_sem=recv_sems.at[j],
        device_id=(px, py, c), device_id_type=MESH_ID) for j, (px, py) in enumerate([(1 - x, y), (x, 1 - y), (1 - x, 1 - y)])]

    def start():
        for cp in copies:
            cp.start()

    def finish():
        for cp in copies:
            cp.wait()

    return start, finish


def chip_exchange(name, p):
    def body(p_ref, out_ref, send_sems, recv_sems):
        start, finish = _chip_copies(p_ref, out_ref, send_sems, recv_sems)
        start()
        finish()

    return pl.pallas_call(
        body, in_specs=[HBM_SPEC], out_specs=HBM_SPEC, out_shape=jax.ShapeDtypeStruct((3,) + p.shape[1:], p.dtype),
        scratch_shapes=[pltpu.SemaphoreType.DMA((3,)), pltpu.SemaphoreType.DMA((3,))], name=name)(p)


def _sum_tile(r):
    return _divisor(r, 512, 16) if r % 16 == 0 else r


def pair_reduce(name, g, got, my_c, my_chip, wire):
    _, nchip, r, c_ = g.shape
    tile = _sum_tile(r)

    def body(ids, g_ref, got_ref, p_ref, mine_ref):
        s = g_ref[...].astype(F32) + got_ref[...].astype(F32)
        p_ref[...] = s.astype(wire)

        @pl.when(pl.program_id(1) == ids[1])
        def _():
            mine_ref[...] = s

    return pl.pallas_call(
        body,
        grid_spec=pltpu.PrefetchScalarGridSpec(
            num_scalar_prefetch=1, grid=(r // tile, nchip),
            in_specs=[pl.BlockSpec((None, None, tile, c_), lambda i, k, ids: (ids[0], k, i, 0)),
                      pl.BlockSpec((None, tile, c_), lambda i, k, ids: (k, i, 0))],
            out_specs=[pl.BlockSpec((None, tile, c_), lambda i, k, ids: (k, i, 0)),
                       pl.BlockSpec((tile, c_), lambda i, k, ids: (i, 0))]),
        out_shape=[jax.ShapeDtypeStruct((nchip, r, c_), wire), jax.ShapeDtypeStruct((r, c_), F32)],
        compiler_params=_params(("arbitrary", "arbitrary")), name=name)(
        jnp.stack([my_c, my_chip]).astype(jnp.int32), g, got)


def chip_reduce(name, mine, got):
    r, c_ = mine.shape
    tile = _sum_tile(r)

    def body(m_ref, got_ref, o_ref):
        o_ref[...] = ((m_ref[...] + got_ref[0].astype(F32)) + got_ref[1].astype(F32)) + got_ref[2].astype(F32)

    return pl.pallas_call(
        body, grid=(r // tile,),
        in_specs=[pl.BlockSpec((tile, c_), lambda i: (i, 0)), pl.BlockSpec((3, tile, c_), lambda i: (0, i, 0))],
        out_specs=pl.BlockSpec((tile, c_), lambda i: (i, 0)), out_shape=jax.ShapeDtypeStruct((r, c_), F32),
        compiler_params=_params(("arbitrary",)), name=name)(mine, got)


def sum_blocks(name, a):
    n, r, c_ = a.shape
    tile = _sum_tile(r)

    def body(a_ref, o_ref):
        acc = a_ref[0]
        for k in range(1, n):
            acc = acc + a_ref[k]
        o_ref[...] = acc

    return pl.pallas_call(
        body, grid=(r // tile,), in_specs=[pl.BlockSpec((n, tile, c_), lambda i: (0, i, 0))],
        out_specs=pl.BlockSpec((tile, c_), lambda i: (i, 0)), out_shape=jax.ShapeDtypeStruct((r, c_), F32),
        compiler_params=_params(("arbitrary",)), name=name)(a)


GROUPS = [
    ("rows1024", ["ssd_w_out", "conv_w_out", "mla_w_o", "w_out", "xattn_w_q", "xattn_w_o", "ffn_w_out"], BF16, 1024),
    ("w_in", ["w_in"], BF16, 1114),
    ("ffn_w_in", ["ffn_w_in"], BF16, 704),
    ("cols256", ["xattn_w_kv", "mla_w_kv_b"], BF16, 256),
    ("w_q_b", ["mla_w_q_b"], BF16, 192),
    ("small", ["ssd_conv_w", "conv_dw_w", "gate_b"], F32, 128),
]
HOSTED, UPFRONT = GROUPS[:3], GROUPS[3:]
PACK_COLS = 1024


def _pack(arrays):
    flat = jnp.concatenate([a.reshape(-1) for a in arrays])
    rows = -(-flat.shape[0] // PACK_COLS)
    rows += -rows % 8
    return jnp.pad(flat, (0, rows * PACK_COLS - flat.shape[0])).reshape(rows, PACK_COLS)


def _unpack(buf, shapes):
    flat = buf.reshape(-1)
    out, off = [], 0
    for s in shapes:
        n = 1
        for d in s:
            n *= d
        out.append(flat[off:off + n].reshape(tuple(s)))
        off += n
    return out


def _stack_rows(arrays, width, lead):
    return jnp.concatenate([a.reshape(a.shape[:lead] + (-1, width)) for a in arrays], axis=lead)


def _unstack_rows(buf, shapes, width, lead):
    out, off = [], 0
    for s in shapes:
        n = 1
        for d in s:
            n *= d
        rows = n // width
        idx = (slice(None),) * lead + (slice(off, off + rows),)
        out.append(buf[idx].reshape(buf.shape[:lead] + tuple(s)))
        off += rows
    return out


def _join_shards(blocks, axis):
    ax = axis + 1
    moved = jnp.moveaxis(blocks, 0, ax)
    s = moved.shape
    return moved.reshape(s[:ax] + (s[ax] * s[ax + 1],) + s[ax + 2:])


def _split_by_owner(full, axis):
    ax = axis + 1
    s = full.shape
    cut = full.reshape(s[:ax] + (2, 2, 2, s[ax] // N_DEV) + s[ax + 1:])
    cut = jnp.moveaxis(cut, (ax + 2, ax, ax + 1), (0, 1, 2))
    return cut.reshape((2, 4) + cut.shape[3:])


def _blocks_by_owner(blocks):
    cut = blocks.reshape((2, 2, 2) + blocks.shape[1:])
    return jnp.moveaxis(cut, 2, 0).reshape((2, 4) + blocks.shape[1:])


def _pad_cols(w, n):
    return jnp.pad(w, ((0, 0), (0, n - w.shape[1])))


def _regroup_cols(srcs, widths):
    starts = [0]
    for s in srcs:
        starts.append(starts[-1] + s.shape[1])
    assert starts[-1] == sum(widths)
    out, lo = [], 0
    for wd in widths:
        hi = lo + wd
        parts = []
        for s, a, b in zip(srcs, starts[:-1], starts[1:]):
            u, v = max(lo, a), min(hi, b)
            if u < v:
                parts.append(s[:, u - a:v - a])
        out.append(parts[0] if len(parts) == 1 else jnp.concatenate(parts, axis=1))
        lo = hi
    return out


COL_BLOCKED = ("w_in", "ffn_w_in", "xattn_w_kv", "mla_w_kv_b", "mla_w_q_b")
W_IN_PIECES = (1024, 2048, 16, 2048, 384, MLA_KV_RANK, MLA_ROPE, 3072)


def _prep_layer(w):
    w_z, w_xbc, w_dt, w_glu, w_q, w_ckv, w_kr, w_gate = _regroup_cols(list(w["w_in"]), W_IN_PIECES)
    w_ffn_gate, w_ffn_up = _regroup_cols(list(w["ffn_w_in"]), (FFN_HIDDEN, FFN_HIDDEN))
    w_xk, w_xv = _regroup_cols(list(w["xattn_w_kv"]), (D_MODEL, D_MODEL))
    q, kv = w["mla_w_q_b"], w["mla_w_kv_b"]

    def row(v):
        return v.reshape(1, -1)

    def norm_pair(g):
        return _pad_cols(row(g), 2 * LANES)

    return {
        "mix_norm_g": row(w["mix_norm_g"]),
        "w_z": w_z, "w_xbc": w_xbc, "w_dt": _pad_cols(w_dt, LANES), "w_glu": w_glu, "w_q": w_q, "w_ckv": w_ckv,
        "w_kr": _pad_cols(w_kr, LANES), "w_gate": w_gate,
        "ssd_conv_w": w["ssd_conv_w"], "ssd_conv_b": row(w["ssd_conv_b"]),
        "ssd_dt_bias": _pad_cols(row(w["ssd_dt_bias"]), LANES), "ssd_a_log": _pad_cols(row(w["ssd_a_log"]), LANES),
        "ssd_d": _pad_cols(row(w["ssd_d"]), LANES), "ssd_norm_g": row(w["ssd_norm_g"]), "ssd_w_out": w["ssd_w_out"],
        "conv_dw_w": w["conv_dw_w"], "conv_dw_b": row(w["conv_dw_b"]), "conv_ln_g": row(w["conv_ln_g"]),
        "conv_ln_b": row(w["conv_ln_b"]), "conv_w_out": w["conv_w_out"],
        "mla_q_a_g": row(w["mla_q_a_g"]), "mla_kv_a_g": row(w["mla_kv_a_g"]),
        "w_qn": jnp.concatenate([q[h, :, :MLA_NOPE] for h in range(MLA_HEADS)], axis=1),
        "w_qr": jnp.concatenate([_pad_cols(q[h, :, MLA_NOPE:], LANES) for h in range(MLA_HEADS)], axis=1),
        "w_kn": jnp.concatenate([kv[h, :, :MLA_NOPE] for h in range(MLA_HEADS)], axis=1),
        "w_v": jnp.concatenate([kv[h, :, MLA_NOPE:] for h in range(MLA_HEADS)], axis=1),
        "mla_q_norm_g": norm_pair(w["mla_q_norm_g"]), "mla_k_norm_g": norm_pair(w["mla_k_norm_g"]),
        "mla_w_o": w["mla_w_o"], "gate_b": row(w["gate_b"]), "w_out": w["w_out"],
        "xattn_norm_g": row(w["xattn_norm_g"]), "mem_norm_g": row(w["mem_norm_g"]), "xattn_w_q": w["xattn_w_q"],
        "w_xk": w_xk, "w_xv": w_xv,
        "xattn_q_norm_g": row(w["xattn_q_norm_g"]), "xattn_k_norm_g": row(w["xattn_k_norm_g"]),
        "xattn_w_o": w["xattn_w_o"], "ffn_norm_g": row(w["ffn_norm_g"]),
        "w_ffn_gate": w_ffn_gate, "w_ffn_up": w_ffn_up, "ffn_w_out": w["ffn_w_out"],
    }


def _unprep_grads(g):
    n_dt, n_kr = IN_SIZES[2], MLA_ROPE
    flat = lambda v: v.reshape(-1)

    def blocks(srcs):
        total = sum(s.shape[1] for s in srcs)
        return jnp.stack(_regroup_cols(srcs, (total // N_DEV,) * N_DEV))

    def head(a, h, n=LANES):
        return a[:, h * LANES:h * LANES + n]

    return {
        "mix_norm_g": flat(g["mix_norm_g"]),
        "w_in": blocks([g["w_z"], g["w_xbc"], g["w_dt"][:, :n_dt], g["w_glu"], g["w_q"], g["w_ckv"],
                        g["w_kr"][:, :n_kr], g["w_gate"]]),
        "ssd_conv_w": g["ssd_conv_w"], "ssd_conv_b": flat(g["ssd_conv_b"]),
        "ssd_dt_bias": flat(g["ssd_dt_bias"])[:SSD_HEADS], "ssd_a_log": flat(g["ssd_a_log"])[:SSD_HEADS],
        "ssd_d": flat(g["ssd_d"])[:SSD_HEADS], "ssd_norm_g": flat(g["ssd_norm_g"]), "ssd_w_out": g["ssd_w_out"],
        "conv_dw_w": g["conv_dw_w"], "conv_dw_b": flat(g["conv_dw_b"]), "conv_ln_g": flat(g["conv_ln_g"]),
        "conv_ln_b": flat(g["conv_ln_b"]), "conv_w_out": g["conv_w_out"],
        "mla_q_a_g": flat(g["mla_q_a_g"]),
        "mla_w_q_b": jnp.stack([jnp.concatenate([head(g["w_qn"], h), head(g["w_qr"], h, MLA_ROPE)], axis=1)
                                for h in range(MLA_HEADS)]),
        "mla_kv_a_g": flat(g["mla_kv_a_g"]),
        "mla_w_kv_b": jnp.stack([jnp.concatenate([head(g["w_kn"], h), head(g["w_v"], h)], axis=1)
                                 for h in range(MLA_HEADS)]),
        "mla_q_norm_g": flat(g["mla_q_norm_g"])[:MLA_NOPE + MLA_ROPE],
        "mla_k_norm_g": flat(g["mla_k_norm_g"])[:MLA_NOPE + MLA_ROPE],
        "mla_w_o": g["mla_w_o"], "gate_b": g["gate_b"].reshape(3, D_MODEL), "w_out": g["w_out"],
        "xattn_norm_g": flat(g["xattn_norm_g"]), "mem_norm_g": flat(g["mem_norm_g"]), "xattn_w_q": g["xattn_w_q"],
        "xattn_w_kv": blocks([g["w_xk"], g["w_xv"]]),
        "xattn_q_norm_g": flat(g["xattn_q_norm_g"]), "xattn_k_norm_g": flat(g["xattn_k_norm_g"]),
        "xattn_w_o": g["xattn_w_o"], "ffn_norm_g": flat(g["ffn_norm_g"]),
        "ffn_w_in": blocks([g["w_ffn_gate"], g["w_ffn_up"]]), "ffn_w_out": g["ffn_w_out"],
    }


def _layer(l, x, mem, cosf, sinf, w, hosted=(), carriers=()):
    t = x.shape[0]
    n = lambda s: f"l{l}_{s}"
    tile = min(ROW_TILE, t)
    grid = (1, t // tile)

    def rowwise(name, f, ins, width, to_matmul=False):
        return tmap(n(name), f, grid, ins, [_row_out(t, width, tile)], narrow=(0,) if to_matmul else ())[0]

    u, x = rms_norm_through(n("mix_norm"), x, w["mix_norm_g"])
    in_keys = ["z", "xbc", "dt", "glu", "q", "ckv", "kr", "gate"]
    z, xbc, dtr, glu, q_lat, c_kv, kr_raw, gate_logits = multi_matmul(
        n("in"), u, [w["w_" + k] for k in in_keys], in_keys, to_tmap=("z", "glu", "q", "ckv", "kr", "gate"))

    xc = dwconv(n("ssd_conv"), xbc, w["ssd_conv_w"], w["ssd_conv_b"])
    y_scan = ssd_scan(n("ssd_scan"), xc, dtr, w["ssd_dt_bias"], w["ssd_a_log"], w["ssd_d"])
    y_norm = rowwise("ssd_gate_norm", _ssd_gate_norm_f, [_rows(y_scan, tile), _rows(z, tile), _whole(w["ssd_norm_g"])],
                     SSD_INNER, to_matmul=True)
    y_ssd = matmul(n("ssd_out"), y_norm, w["ssd_w_out"], to_tmap=True)

    v = rowwise("glu", _glu_f, [_rows(glu, tile)], D_MODEL)
    v = dwconv(n("conv_dw"), v, w["conv_dw_w"], w["conv_dw_b"])
    v = rowwise("conv_ln_silu", _ln_silu_f, [_rows(v, tile), _whole(w["conv_ln_g"]), _whole(w["conv_ln_b"])], D_MODEL,
                to_matmul=True)
    y_conv = matmul(n("conv_out"), v, w["conv_w_out"], to_tmap=True)

    q_n = rms_norm(n("q_a_norm"), q_lat, w["mla_q_a_g"])
    qn_raw, qr_raw = multi_matmul(n("q"), q_n, [w["w_qn"], w["w_qr"]], ["nope", "rope"], to_tmap=("nope", "rope"))
    c_n = rms_norm(n("kv_a_norm"), c_kv, w["mla_kv_a_g"])
    kn_raw, val = multi_matmul(n("kv"), c_n, [w["w_kn"], w["w_v"]], ["nope", "v"], to_tmap=("nope",))
    tables = [_rows(cosf, tile, "n"), _rows(sinf, tile, "n")]
    kn = rowwise("k_nope_norm", _k_nope_f, [_rows(kn_raw, tile), _whole(w["mla_k_norm_g"])], MLA_HEADS * MLA_NOPE)
    kr = rowwise("k_rope", _k_rope_f, [_rows(kr_raw, tile)] + tables + [_whole(w["mla_k_norm_g"])], LANES)
    wide = _row_out(t, MLA_HEADS * LANES, tile)
    qn, qr = tmap(n("q_prep"), _q_prep_f, grid,
                  [_rows(qn_raw, tile), _rows(qr_raw, tile)] + tables + [_whole(w["mla_q_norm_g"])], [wide, wide])
    att, gathered, handles = mla_attention(n("mla_attn"), qn, qr, kn, kr, val, hosted, carriers)
    y_mla = matmul(n("mla_out"), att, w["mla_w_o"], to_tmap=True)

    merged = rowwise("merge", _merge_f, [_rows(gate_logits, tile), _whole(w["gate_b"]), _rows(y_ssd, tile),
                                         _rows(y_conv, tile), _rows(y_mla, tile)], D_MODEL, to_matmul=True)
    x = matmul(n("mix_out"), merged, w["w_out"], res=x)

    h, x = rms_norm_through(n("xattn_norm"), x, w["xattn_norm_g"])
    mem_n = rms_norm_nograd_x(n("mem_norm"), mem, w["mem_norm_g"])
    xq = matmul(n("xattn_q"), h, w["xattn_w_q"], to_tmap=True)
    xk, xv = multi_matmul(n("xattn_kv"), mem_n, [w["w_xk"], w["w_xv"]], ["k", "v"])
    m = mem.shape[0]
    txq = min(XATT_Q_TILE, t)
    kv_head = lambda arr: (arr, (m, X_HEAD_DIM), lambda o, i: (0, o), "ai")
    xo = tmap(n("xattn"), _xattn_f, (X_HEADS, t // txq),
              [(xq, (txq, X_HEAD_DIM), lambda o, i: (i, o), "t"), kv_head(xk), kv_head(xv),
               _whole(w["xattn_q_norm_g"]), _whole(w["xattn_k_norm_g"])],
              [((t, D_MODEL), (txq, X_HEAD_DIM), lambda o, i: (i, o))], narrow=(0,))[0]
    x = matmul(n("xattn_out"), xo, w["xattn_w_o"], res=x)

    h, x = rms_norm_through(n("ffn_norm"), x, w["ffn_norm_g"])
    gate, up = multi_matmul(n("ffn_in"), h, [w["w_ffn_gate"], w["w_ffn_up"]], ["gate", "up"], to_tmap=("gate", "up"))
    act = rowwise("swiglu", _swiglu_f, [_rows(gate, tile), _rows(up, tile)], FFN_HIDDEN, to_matmul=True)
    return matmul(n("ffn_out"), act, w["ffn_w_out"], res=x), gathered, handles


def _rope_tables(positions):
    inv = ROPE_THETA ** (-jnp.arange(0, MLA_ROPE, 2, dtype=F32) / MLA_ROPE)
    ang = positions.astype(F32)[:, None] * inv
    pad = jnp.zeros((positions.shape[0], LANES - MLA_ROPE), F32)
    cos, sin = jnp.cos(ang), jnp.sin(ang)
    return jnp.concatenate([cos, cos, pad], axis=1), jnp.concatenate([sin, sin, pad], axis=1)


def local_step(x, mem, positions, target, weights, gathered0, later, shard_shapes, exchange=None):
    cosf, sinf = _rope_tables(positions)

    def layer_weights(l, big):
        w = {k: v[:, l] if k in COL_BLOCKED else v[l] for k, v in weights.items()}
        for (_, names, _, width), stack in zip(HOSTED, big):
            for n, b in zip(names, _unstack_rows(stack, [shard_shapes[n] for n in names], width, 1)):
                w[n] = b if n in COL_BLOCKED else b.reshape((-1,) + b.shape[2:])
        return w

    diff = [{k: jnp.zeros(v.shape, BF16) if k in MATRICES else v
             for k, v in _prep_layer(layer_weights(l, gathered0)).items()} for l in range(DEPTH)]

    def layer_fn(l, big, hosted):
        def f(x, d, carriers):
            mats = _prep_layer(layer_weights(l, big))
            w = {k: Mat(mats[k], s) if k in MATRICES else s for k, s in d.items()}
            y, gathered, handles = _layer(l, x, mem, cosf, sinf, w, hosted, carriers)
            return (y, handles), gathered
        return f

    pulls, big = [], gathered0
    for l in range(DEPTH):
        inner = l + 1 < DEPTH
        carriers = tuple(jnp.zeros((3,) + s.shape, s.dtype) for s in later[l]) if inner and exchange else ()
        (x, _), pull, big = jax.vjp(layer_fn(l, big, later[l] if inner else ()), x, diff[l], carriers, has_aux=True)
        pulls.append(pull)
    sq, g = loss_head(x, target)

    per_layer, reduced, sent, mine = [None] * DEPTH, [None] * DEPTH, (), None
    for l in reversed(range(DEPTH)):
        g, gd, got = pulls[l]((g, tuple(sent)))
        if mine is not None:
            reduced[l + 1] = exchange.finish(mine, got)
        per_layer[l] = _unprep_grads(gd)
        if exchange:
            sent, mine = exchange.begin(per_layer[l])
    if exchange:
        reduced[0] = exchange.finish(mine, [chip_exchange(f"chip_exchange_first_layer_{k}", p)
                                            for k, p in enumerate(sent)])
    grads = {k: jnp.stack([pl_[k] for pl_ in per_layer], axis=1 if k in COL_BLOCKED else 0) for k in WEIGHTS}
    shards = {k: jnp.stack([r[k] for r in reduced]) for k in reduced[0]} if exchange else None
    return sq, g, grads, shards


def _step(x, mem, positions, loss_target, w, m, v):
    xi, yi, ci = _position()

    full = {n: w[n] for n in REPLICATED}
    for gname, names, wire, width in UPFRONT:
        shapes = [w[n].shape for n in names]
        stacked = _stack_rows([w[n] for n in names], width, 0).astype(wire)
        gathered = all_gather("gather_" + gname, stacked)
        for n, b in zip(names, _unstack_rows(gathered, shapes, width, 1)):
            full[n] = b if n in COL_BLOCKED else _join_shards(b, SHARDED[n])
    shards = [[_stack_rows([w[n][l] for n in names], width, 0).astype(wire) for _, names, wire, width in HOSTED]
              for l in range(DEPTH)]
    gathered0 = [all_gather("gather0_" + g[0], s) for g, s in zip(HOSTED, shards[0])]
    shard_shapes = {n: w[n].shape[1:] for g in HOSTED for n in g[1]}

    def to_sibling(groups, grads):
        sent, mine = [], []
        for gname, names, wire, width in groups:
            by_owner = _stack_rows([_blocks_by_owner(grads[n]) if n in COL_BLOCKED else
                                    _split_by_owner(grads[n], SHARDED[n]) for n in names], width, 2).astype(wire)
            from_sibling = pair_exchange("pair_exchange_" + gname, by_owner)
            p, own = pair_reduce("pair_reduce_" + gname, by_owner, from_sibling, ci, 2 * xi + yi, wire)
            sent.append(p)
            mine.append(own)
        return sent, mine

    def from_chips(groups, shapes, mine, got):
        out = {}
        for (gname, names, _, width), own, arrived in zip(groups, mine, got):
            reduced = chip_reduce("chip_reduce_" + gname, own, arrived)
            out.update(zip(names, _unstack_rows(reduced, [shapes[n] for n in names], width, 0)))
        return out

    class LayerExchange:
        @staticmethod
        def begin(layer_grads):
            return to_sibling(HOSTED, {n: g[:, None] if n in COL_BLOCKED else g[None] for n, g in layer_grads.items()
                                       if n in shard_shapes})

        @staticmethod
        def finish(mine, got):
            return from_chips(HOSTED, shard_shapes, mine, got)

    sq, gx, grads, g_shard = local_step(x[0], mem[0], positions[0], loss_target[0], full, gathered0, shards[1:],
                                        shard_shapes, LayerExchange)
    loss = lax.psum(0.5 * jnp.sum(sq) / D_MODEL, ("x", "y", "c"))

    sent, mine = to_sibling(UPFRONT, grads)
    got = [chip_exchange("chip_exchange_" + g[0], p) for g, p in zip(UPFRONT, sent)]
    g_shard.update(from_chips(UPFRONT, {n: w[n].shape for g in UPFRONT for n in g[1]}, mine, got))

    rep_shapes = [w[n].shape for n in REPLICATED]
    rep_all = all_gather("small_grads_all_gather", _pack([grads[n] for n in REPLICATED]))
    g_rep = dict(zip(REPLICATED, _unpack(sum_blocks("small_grads_sum", rep_all), rep_shapes)))

    out_g, out_d, out_m, out_v = [], [], [], []
    for n in WEIGHTS:
        g = g_shard[n] if n in SHARDED else g_rep[n]
        d, nm, nv = adamw("adamw_" + n, w[n], g, m[n], v[n])
        out_g.append(g)
        out_d.append(d)
        out_m.append(nm)
        out_v.append(nv)
    return (loss, gx[None], *out_g, *out_d, *out_m, *out_v)


def kernel(x, mem, positions, mix_norm_g, w_in, ssd_conv_w, ssd_conv_b, ssd_dt_bias, ssd_a_log, ssd_d, ssd_norm_g, ssd_w_out, conv_dw_w, conv_dw_b, conv_ln_g, conv_ln_b, conv_w_out, mla_q_a_g, mla_w_q_b, mla_kv_a_g, mla_w_kv_b, mla_q_norm_g, mla_k_norm_g, mla_w_o, gate_b, w_out, xattn_norm_g, mem_norm_g, xattn_w_q, xattn_w_kv, xattn_q_norm_g, xattn_k_norm_g, xattn_w_o, ffn_norm_g, ffn_w_in, ffn_w_out, loss_target, m_mix_norm_g, m_w_in, m_ssd_conv_w, m_ssd_conv_b, m_ssd_dt_bias, m_ssd_a_log, m_ssd_d, m_ssd_norm_g, m_ssd_w_out, m_conv_dw_w, m_conv_dw_b, m_conv_ln_g, m_conv_ln_b, m_conv_w_out, m_mla_q_a_g, m_mla_w_q_b, m_mla_kv_a_g, m_mla_w_kv_b, m_mla_q_norm_g, m_mla_k_norm_g, m_mla_w_o, m_gate_b, m_w_out, m_xattn_norm_g, m_mem_norm_g, m_xattn_w_q, m_xattn_w_kv, m_xattn_q_norm_g, m_xattn_k_norm_g, m_xattn_w_o, m_ffn_norm_g, m_ffn_w_in, m_ffn_w_out, v_mix_norm_g, v_w_in, v_ssd_conv_w, v_ssd_conv_b, v_ssd_dt_bias, v_ssd_a_log, v_ssd_d, v_ssd_norm_g, v_ssd_w_out, v_conv_dw_w, v_conv_dw_b, v_conv_ln_g, v_conv_ln_b, v_conv_w_out, v_mla_q_a_g, v_mla_w_q_b, v_mla_kv_a_g, v_mla_w_kv_b, v_mla_q_norm_g, v_mla_k_norm_g, v_mla_w_o, v_gate_b, v_w_out, v_xattn_norm_g, v_mem_norm_g, v_xattn_w_q, v_xattn_w_kv, v_xattn_q_norm_g, v_xattn_k_norm_g, v_xattn_w_o, v_ffn_norm_g, v_ffn_w_in, v_ffn_w_out):
    args = locals()
    w = {n: args[n] for n in WEIGHTS}
    m = {n: args["m_" + n] for n in WEIGHTS}
    v = {n: args["v_" + n] for n in WEIGHTS}
    return _step(x, mem, positions, loss_target, w, m, v)
```

```python
from typing import NamedTuple

import jax
import jax.numpy as jnp
from jax import lax
from jax.experimental import pallas as pl
from jax.experimental.pallas import tpu as pltpu

F32 = jnp.float32
BF16 = jnp.bfloat16
HIGHEST = lax.Precision.HIGHEST
MESH_ID = pl.DeviceIdType.MESH

VMEM_LIMIT_BYTES = 56 * 1024 * 1024
LANES = 128

EPS = 1e-6
DEPTH = 4
D_MODEL = 1024
N_DEV = 8
SSD_HEADS = 16
SSD_HEAD_DIM = 64
SSD_STATE = 128
SSD_GROUPS = 4
SSD_INNER = 1024
SSD_TILE = 256
CONV_K = 31
SSD_CONV_K = 4
MLA_HEADS = 8
MLA_NOPE = 128
MLA_ROPE = 64
MLA_V = 128
MLA_Q_RANK = 384
MLA_KV_RANK = 256
ATT_CHUNK = 64
ROPE_THETA = 10000.0
X_HEADS = 4
X_HEAD_DIM = 256
FFN_HIDDEN = 2816
IN_SIZES = (1024, 2048, 16, 2048, 384, 320, 3072)

ADAM_LR = 0.001
ADAM_B1 = 0.9
ADAM_B2 = 0.999
ADAM_EPS = 1e-08
ADAM_WD = 0.01
ADAM_STEP = 10

MM_FULL_K = 3072
ROW_TILE = 256
ATT_BLOCK = 512
XATT_Q_TILE = 512

SHARDED = {
    "w_in": 1, "ssd_conv_w": 1, "ssd_w_out": 0, "conv_dw_w": 1, "conv_w_out": 0, "mla_w_q_b": 1, "mla_w_kv_b": 1,
    "mla_w_o": 0, "gate_b": 1, "w_out": 0, "xattn_w_q": 0, "xattn_w_kv": 1, "xattn_w_o": 0, "ffn_w_in": 1,
    "ffn_w_out": 0,
}
WEIGHTS = ["mix_norm_g", "w_in", "ssd_conv_w", "ssd_conv_b", "ssd_dt_bias", "ssd_a_log", "ssd_d", "ssd_norm_g",
           "ssd_w_out", "conv_dw_w", "conv_dw_b", "conv_ln_g", "conv_ln_b", "conv_w_out", "mla_q_a_g", "mla_w_q_b",
           "mla_kv_a_g", "mla_w_kv_b", "mla_q_norm_g", "mla_k_norm_g", "mla_w_o", "gate_b", "w_out", "xattn_norm_g",
           "mem_norm_g", "xattn_w_q", "xattn_w_kv", "xattn_q_norm_g", "xattn_k_norm_g", "xattn_w_o", "ffn_norm_g",
           "ffn_w_in", "ffn_w_out"]
REPLICATED = [n for n in WEIGHTS if n not in SHARDED]


def _params(sem=None):
    return pltpu.CompilerParams(dimension_semantics=sem, vmem_limit_bytes=VMEM_LIMIT_BYTES)


def _divisor(n, cap, mult):
    if n <= cap:
        return n
    for d in range(cap - cap % mult, 0, -mult):
        if n % d == 0:
            return d
    raise ValueError(f"no tile for {n}")


def _dg(a, b, ca, cb):
    return lax.dot_general(a.astype(BF16), b.astype(BF16), (((ca,), (cb,)), ((), ())), preferred_element_type=F32)


@jax.custom_vjp
def bdot(a, b):
    return _dg(a, b, 1, 0)


bdot.defvjp(lambda a, b: (_dg(a, b, 1, 0), (a, b)), lambda r, g: (_dg(g, r[1], 1, 1), _dg(r[0], g, 0, 0)))


@jax.custom_vjp
def bdot_nt(a, b):
    return _dg(a, b, 1, 1)


bdot_nt.defvjp(lambda a, b: (_dg(a, b, 1, 1), (a, b)), lambda r, g: (_dg(g, r[1], 1, 0), _dg(g, r[0], 0, 0)))


@jax.custom_vjp
def bdot_tn(a, b):
    return _dg(a, b, 0, 0)


bdot_tn.defvjp(lambda a, b: (_dg(a, b, 0, 0), (a, b)), lambda r, g: (_dg(r[1], g, 1, 1), _dg(r[0], g, 1, 0)))


def hdot(a, b):
    return jnp.dot(a, b, precision=HIGHEST, preferred_element_type=F32)


def _iota(shape, dim):
    return lax.broadcasted_iota(jnp.int32, shape, dim)


def _mm(name, a, b, ta=False, tb=False, res=None, narrow_out=False):
    m, k = (a.shape[1], a.shape[0]) if ta else a.shape
    n = b.shape[0] if tb else b.shape[1]
    tm = _divisor(m, 1536, LANES) if ta else _divisor(m, 1024, 8)
    tn = _divisor(n, 1536, LANES)
    tk = k if k <= MM_FULL_K else _divisor(k, 1024, LANES)
    if tk == k and k > 1024:
        tm = _divisor(m, 512, LANES if ta else 8)
    nk = k // tk
    dims = (((0 if ta else 1,), (1 if tb else 0,)), ((), ()))

    out_dtype = BF16 if narrow_out else F32
    own_acc = narrow_out and nk > 1
    assert res is None or not narrow_out

    def body(*refs):
        if res is None:
            a_ref, b_ref, o_ref = refs[:3]
        else:
            a_ref, b_ref, r_ref, o_ref = refs[:4]
        acc = refs[-1] if own_acc else o_ref
        part = lax.dot_general(a_ref[...].astype(BF16), b_ref[...].astype(BF16), dims, preferred_element_type=F32)
        if nk == 1:
            o_ref[...] = (part if res is None else part + r_ref[...]).astype(out_dtype)
        else:
            kk = pl.program_id(2)

            @pl.when(kk == 0)
            def _():
                acc[...] = part if res is None else part + r_ref[...]

            @pl.when(kk != 0)
            def _():
                acc[...] += part

            if own_acc:
                @pl.when(kk == nk - 1)
                def _():
                    o_ref[...] = acc[...].astype(out_dtype)

    a_spec = pl.BlockSpec((tk, tm), lambda i, j, kk: (kk, i)) if ta else pl.BlockSpec((tm, tk), lambda i, j, kk: (i, kk))
    b_spec = pl.BlockSpec((tn, tk), lambda i, j, kk: (j, kk)) if tb else pl.BlockSpec((tk, tn), lambda i, j, kk: (kk, j))
    o_spec = pl.BlockSpec((tm, tn), lambda i, j, kk: (i, j))
    in_specs = [a_spec, b_spec] + ([] if res is None else [o_spec])
    args = (a, b) + (() if res is None else (res,))
    return pl.pallas_call(
        body, grid=(m // tm, n // tn, nk), in_specs=in_specs, out_specs=o_spec,
        out_shape=jax.ShapeDtypeStruct((m, n), out_dtype),
        scratch_shapes=[pltpu.VMEM((tm, tn), F32)] if own_acc else [],
        compiler_params=_params(("parallel", "parallel", "arbitrary")), name=name)(*args)


class Mat(NamedTuple):
    value: jax.Array
    slot: jax.Array


MATRICES = frozenset([
    "w_z", "w_xbc", "w_dt", "w_glu", "w_q", "w_ckv", "w_kr", "w_gate", "ssd_w_out", "conv_w_out", "w_qn", "w_qr",
    "w_kn", "w_v", "mla_w_o", "w_out", "xattn_w_q", "w_xk", "w_xv", "xattn_w_o", "w_ffn_gate", "w_ffn_up",
    "ffn_w_out"])


class Act(NamedTuple):
    value: jax.Array
    slot: jax.Array


def _operand(a):
    return (a.value, a.slot) if isinstance(a, Act) else (a, a)


class Out(NamedTuple):
    value: jax.Array
    handle: jax.Array


def _handle(out):
    return jnp.zeros(out.shape, BF16)


def matmul(name, a, mat, res=None, to_tmap=False):
    w, slot = mat
    a, a_slot = _operand(a)
    if res is None:
        @jax.custom_vjp
        def run(a, a_slot, w, slot):
            out = _mm(name, a, w)
            return Out(out, _handle(out)) if to_tmap else out

        def fwd(a, a_slot, w, slot):
            return run(a, a_slot, w, slot), (a, w)

        def bwd(r, g):
            g = g.handle if to_tmap else g
            return None, _mm(name + "_da", g, r[1], tb=True), None, _mm(name + "_dw", r[0], g, ta=True, narrow_out=True)

        run.defvjp(fwd, bwd)
        return run(a, a_slot, w, slot)

    @jax.custom_vjp
    def run_res(a, a_slot, w, slot, res):
        return _mm(name, a, w, res=res)

    def fwd_res(a, a_slot, w, slot, res):
        return run_res(a, a_slot, w, slot, res), (a, w)

    def bwd_res(r, g):
        return None, _mm(name + "_da", g, r[1], tb=True), None, _mm(name + "_dw", r[0], g, ta=True, narrow_out=True), g

    run_res.defvjp(fwd_res, bwd_res)
    return run_res(a, a_slot, w, slot, res)


def multi_matmul(name, a, mats, keys, to_tmap=()):
    ws, slots = tuple(m.value for m in mats), tuple(m.slot for m in mats)
    a, a_slot = _operand(a)

    @jax.custom_vjp
    def run(a, a_slot, ws, slots):
        outs = [_mm(f"{name}_{k}", a, w) for k, w in zip(keys, ws)]
        return tuple(Out(o, _handle(o)) if k in to_tmap else o for k, o in zip(keys, outs))

    def fwd(a, a_slot, ws, slots):
        return run(a, a_slot, ws, slots), (a, ws)

    def bwd(r, gs):
        a, ws = r
        gs = [g.handle if k in to_tmap else g for k, g in zip(keys, gs)]
        da = None
        for k, w, g in zip(keys, ws, gs):
            da = _mm(f"{name}_{k}_da", g, w, tb=True, res=da)
        dws = tuple(_mm(f"{name}_{k}_dw", a, g, ta=True, narrow_out=True) for k, g in zip(keys, gs))
        return None, da, tuple(None for _ in ws), dws

    run.defvjp(fwd, bwd)
    return run(a, a_slot, ws, slots)


def tmap(name, f, grid, ins, outs, through=None, narrow=()):
    handled = [k for k, x in enumerate(ins) if isinstance(x[0], Out)]
    assert all(ins[k][3] == "t" and k != through for k in handled)
    arrays = [x[0].value if isinstance(x[0], Out) else x[0] for x in ins] + [ins[k][0].handle for k in handled]
    kinds = [x[3] for x in ins]
    in_specs = [pl.BlockSpec(x[1], x[2]) for x in ins]
    out_specs = [pl.BlockSpec(x[1], x[2]) for x in outs]
    out_shape = [jax.ShapeDtypeStruct(x[0], BF16 if k in narrow else F32) for k, x in enumerate(outs)]
    n_in, n_out = len(ins), len(outs)
    n_through = 0 if through is None else 1
    assert through is None or kinds[through] == "t"
    didx = [k for k, kd in enumerate(kinds) if kd != "n"]

    def fwd_call(*arrs):
        def body(*refs):
            pids = (pl.program_id(0), pl.program_id(1))
            vals = f(pids, *[r[...] for r in refs[:n_in]])
            for r, v in zip(refs[n_in:], vals):
                r[...] = v.astype(r.dtype)

        return pl.pallas_call(body, grid=grid, in_specs=in_specs, out_specs=out_specs, out_shape=out_shape,
                              compiler_params=_params(("arbitrary", "arbitrary")), name=name)(*arrs)

    def bwd_call(arrs, cts):
        def body(*refs):
            o, i = pl.program_id(0), pl.program_id(1)
            vals = [r[...] for r in refs[:n_in]]

            def g(*dv):
                full = list(vals)
                for k, v in zip(didx, dv):
                    full[k] = v
                return tuple(f((o, i), *full))

            _, vjp = jax.vjp(g, *[vals[k] for k in didx])
            grads = vjp(tuple(r[...] for r in refs[n_in:n_in + n_out]))
            for k, gr, r in zip(didx, grads, refs[n_in + n_out + n_through:]):
                if k == through:
                    r[...] = gr + refs[n_in + n_out][...]
                elif kinds[k] == "t":
                    r[...] = gr.astype(r.dtype)
                else:
                    first = (i == 0) if kinds[k] == "ai" else jnp.logical_and(o == 0, i == 0)

                    @pl.when(first)
                    def _(r=r, gr=gr):
                        r[...] = gr

                    @pl.when(jnp.logical_not(first))
                    def _(r=r, gr=gr):
                        r[...] += gr

        g_specs = [in_specs[k] for k in didx]
        g_shape = [jax.ShapeDtypeStruct(arrs[k].shape, BF16 if k in handled else F32) for k in didx]
        ct_specs = out_specs + ([in_specs[through]] if n_through else [])
        return pl.pallas_call(body, grid=grid, in_specs=in_specs + ct_specs, out_specs=g_specs, out_shape=g_shape,
                              compiler_params=_params(("arbitrary", "arbitrary")), name=name + "_bwd")(*arrs, *cts)

    @jax.custom_vjp
    def run(*arrs):
        res = [Act(o, jnp.zeros(o.shape, F32)) if k in narrow else o for k, o in enumerate(fwd_call(*arrs[:n_in]))]
        return tuple(res) + ((arrs[through],) if n_through else ())

    def run_fwd(*arrs):
        return run(*arrs), arrs[:n_in]

    def run_bwd(arrs, cts):
        cts = [c.slot if isinstance(c, Act) else c for c in cts]
        gs = bwd_call(arrs, cts)
        full = [None] * (n_in + len(handled))
        for k, g in zip(didx, gs):
            full[n_in + handled.index(k) if k in handled else k] = g
        return tuple(full)

    run.defvjp(run_fwd, run_bwd)
    return run(*arrays)


def _rows(arr, tile, kind="t"):
    width = (arr.value if isinstance(arr, Out) else arr).shape[1]
    return (arr, (tile, width), lambda o, i: (i, 0), kind)


def _whole(arr, kind="ag"):
    return (arr, arr.shape, lambda o, i: (0, 0), kind)


def _row_out(t, n, tile):
    return ((t, n), (tile, n), lambda o, i: (i, 0))


def _rms(x, g, n=None):
    ms = jnp.sum(x * x, axis=-1, keepdims=True) / (x.shape[-1] if n is None else n)
    return x * lax.rsqrt(ms + EPS) * g


def rms_norm(name, x, g):
    t, n = (x.value if isinstance(x, Out) else x).shape
    tile = min(ROW_TILE, t)
    return tmap(name, lambda p, x, g: (_rms(x, g),), (1, t // tile), [_rows(x, tile), _whole(g)],
                [_row_out(t, n, tile)], narrow=(0,))[0]


def rms_norm_through(name, x, g):
    t, n = (x.value if isinstance(x, Out) else x).shape
    tile = min(ROW_TILE, t)
    return tmap(name, lambda p, x, g: (_rms(x, g),), (1, t // tile), [_rows(x, tile), _whole(g)],
                [_row_out(t, n, tile)], through=0, narrow=(0,))


def rms_norm_nograd_x(name, x, g):
    t, n = (x.value if isinstance(x, Out) else x).shape
    tile = min(ROW_TILE, t)
    return tmap(name, lambda p, x, g: (_rms(x, g),), (1, t // tile), [_rows(x, tile, "n"), _whole(g)],
                [_row_out(t, n, tile)], narrow=(0,))[0]


def _glu_f(p, glu):
    h = glu.shape[1] // 2
    return (glu[:, :h] * jax.nn.sigmoid(glu[:, h:]),)


def _ln_silu_f(p, v, g, b):
    mu = jnp.mean(v, axis=-1, keepdims=True)
    xc = v - mu
    var = jnp.mean(xc * xc, axis=-1, keepdims=True)
    return (jax.nn.silu(xc * lax.rsqrt(var + EPS) * g + b),)


def _ssd_gate_norm_f(p, y, z, g):
    v = y * jax.nn.silu(z)
    w = SSD_INNER // SSD_GROUPS
    parts = []
    for k in range(SSD_GROUPS):
        vg = v[:, k * w:(k + 1) * w]
        parts.append(vg * lax.rsqrt(jnp.mean(vg * vg, axis=-1, keepdims=True) + EPS))
    return (jnp.concatenate(parts, axis=1) * g,)


def _merge_f(p, gl, gb, y0, y1, y2):
    g = jax.nn.sigmoid(gl + gb)
    d = D_MODEL
    return (g[:, :d] * y0 + g[:, d:2 * d] * y1 + g[:, 2 * d:] * y2,)


def _swiglu_f(p, gate, up):
    return (jax.nn.silu(gate) * up,)


def _rot_matrix():
    r, c = _iota((LANES, LANES), 0), _iota((LANES, LANES), 1)
    h = MLA_ROPE // 2
    plus = jnp.logical_and(c >= h, jnp.logical_and(c < 2 * h, r == c - h))
    minus = jnp.logical_and(c < h, r == c + h)
    return plus.astype(F32) - minus.astype(F32)


def _rope(x, cosf, sinf):
    return x * cosf + hdot(x, _rot_matrix()) * sinf


def _per_head(f, x):
    return jnp.concatenate([f(x[:, h * LANES:(h + 1) * LANES]) for h in range(x.shape[1] // LANES)], axis=1)


def _k_nope_f(p, kn_raw, kg):
    return (_per_head(lambda x: _rms(x, kg[:, :MLA_NOPE]), kn_raw),)


def _k_rope_f(p, kr_raw, cosf, sinf, kg):
    return (_rope(_rms(kr_raw, kg[:, MLA_NOPE:], n=MLA_ROPE), cosf, sinf),)


def _q_prep_f(p, qn_raw, qr_raw, cosf, sinf, qg):
    qn = _per_head(lambda x: _rms(x, qg[:, :MLA_NOPE]), qn_raw)
    qr = _per_head(lambda x: _rope(_rms(x, qg[:, MLA_NOPE:], n=MLA_ROPE), cosf, sinf), qr_raw)
    return qn, qr


def _softmax(s):
    m = jnp.max(s, axis=-1, keepdims=True)
    e = jnp.exp(s - m)
    return e / jnp.sum(e, axis=-1, keepdims=True)


def _xattn_f(p, q, k, v, qg, kg):
    s = bdot_nt(_rms(q, qg), _rms(k, kg)) * (X_HEAD_DIM ** -0.5)
    return (bdot(_softmax(s), v),)


ATT_SCALE = (MLA_NOPE + MLA_ROPE) ** -0.5
NT_DIMS = (((1,), (1,)), ((), ()))
NN_DIMS = (((1,), (0,)), ((), ()))
TN_DIMS = (((0,), (0,)), ((), ()))


def _att_specs(t, blk):
    q_spec = pl.BlockSpec((blk, LANES), lambda h, i: (i, h))
    k_spec = pl.BlockSpec((t, LANES), lambda h, i: (0, h))
    shared = pl.BlockSpec((t, LANES), lambda h, i: (0, 0))
    lse_spec = pl.BlockSpec((None, blk, 1), lambda h, i: (h, i, 0))
    return q_spec, k_spec, shared, lse_spec


def _diagonal_mask(blk):
    return (_iota((blk, blk), 1) // ATT_CHUNK) <= (_iota((blk, blk), 0) // ATT_CHUNK)


def _att_keys(kn_ref, kr_ref, j, blk):
    ks = pl.ds(pl.multiple_of(j * blk, blk), blk)
    return ks, jnp.concatenate([kn_ref[ks, :], kr_ref[ks, :]], axis=1).astype(BF16)


def _att_fwd(name, qn, qr, kn, kr, v, hosted=()):
    t, width = qn.shape
    heads = width // LANES
    blk = min(ATT_BLOCK, t)
    nh = len(hosted)

    def body(qn_ref, qr_ref, kn_ref, kr_ref, v_ref, *rest):
        shard_refs, (o_ref, lse_ref), rest = rest[:nh], rest[nh:nh + 2], rest[nh + 2:]
        full_refs, sems = rest[:nh], rest[nh:]
        i = pl.program_id(1)
        gathers = [_gather_copies(shard_refs[k], full_refs[k], *sems[3 * k:3 * k + 3]) for k in range(nh)]
        if nh:
            @pl.when(jnp.logical_and(pl.program_id(0) == 0, i == 0))
            def _():
                for start, _, _ in gathers:
                    start()

            @pl.when(jnp.logical_and(pl.program_id(0) == heads - 1, i == 0))
            def _():
                for _, pass_on, _ in gathers:
                    pass_on()
        q = jnp.concatenate([qn_ref[...], qr_ref[...]], axis=1).astype(BF16)

        def scores(j):
            _, k = _att_keys(kn_ref, kr_ref, j, blk)
            return lax.dot_general(q, k, NT_DIMS, preferred_element_type=F32)

        def weighted_values(p, j):
            ks = pl.ds(pl.multiple_of(j * blk, blk), blk)
            return lax.dot_general(p, v_ref[ks, :].astype(BF16), NN_DIMS, preferred_element_type=F32)

        def softmax_step(s, m, l):
            m_new = jnp.maximum(m, jnp.max(s, axis=1, keepdims=True))
            alpha = jnp.exp(m - m_new)
            p = jnp.exp(s - m_new)
            return m_new, alpha, alpha * l + jnp.sum(p, axis=1, keepdims=True), p.astype(BF16)

        def step(j, carry):
            s, p_prev, m, l, acc = carry
            s_next = scores(j + 1)
            pv_prev = weighted_values(p_prev, jnp.maximum(j - 1, 0))
            m, alpha, l, p = softmax_step(s * ATT_SCALE, m, l)
            return s_next, p, m, l, alpha * (acc + pv_prev)

        init = (scores(0), jnp.zeros((blk, blk), BF16), jnp.full((blk, 1), -1e30, F32), jnp.zeros((blk, 1), F32),
                jnp.zeros((blk, LANES), F32))
        s, p_prev, m, l, acc = lax.fori_loop(0, i, step, init)
        pv_prev = weighted_values(p_prev, jnp.maximum(i - 1, 0))
        s = jnp.where(_diagonal_mask(blk), s * ATT_SCALE, -1e30)
        m, alpha, l, p = softmax_step(s, m, l)
        acc = alpha * (acc + pv_prev) + weighted_values(p, i)
        o_ref[...] = acc / l
        lse_ref[...] = m + jnp.log(l)
        if nh:
            @pl.when(jnp.logical_and(pl.program_id(0) == heads - 1, i == t // blk - 1))
            def _():
                for _, _, finish in gathers:
                    finish()

    q_spec, k_spec, shared, lse_spec = _att_specs(t, blk)
    sem_shapes = [pltpu.SemaphoreType.DMA((7,)), pltpu.SemaphoreType.DMA((7,)), pltpu.SemaphoreType.DMA] * nh
    return pl.pallas_call(
        body, grid=(heads, t // blk), in_specs=[q_spec, q_spec, k_spec, shared, k_spec] + [HBM_SPEC] * nh,
        out_specs=[q_spec, lse_spec] + [HBM_SPEC] * nh,
        out_shape=[jax.ShapeDtypeStruct((t, width), F32), jax.ShapeDtypeStruct((heads, t, 1), F32)] +
                  [jax.ShapeDtypeStruct((N_DEV,) + s.shape, s.dtype) for s in hosted],
        scratch_shapes=sem_shapes, compiler_params=_params(("arbitrary", "arbitrary")), name=name)(
        qn, qr, kn, kr, v, *hosted)


def _att_bwd(name, qn, qr, kn, kr, v, o, lse, do, hosted=()):
    t, width = qn.shape
    heads = width // LANES
    blk = min(ATT_BLOCK, t)
    nh = len(hosted)

    def body(qn_ref, qr_ref, kn_ref, kr_ref, v_ref, o_ref, lse_ref, do_ref, *rest):
        sent_refs, (dqn_ref, dqr_ref, dkn_ref, dkr_ref, dv_ref), rest = rest[:nh], rest[nh:nh + 5], rest[nh + 5:]
        got_refs, sems = rest[:nh], rest[nh:]
        h, i = pl.program_id(0), pl.program_id(1)
        exchanges = [_chip_copies(sent_refs[k], got_refs[k], *sems[2 * k:2 * k + 2]) for k in range(nh)]
        if nh:
            @pl.when(jnp.logical_and(h == 0, i == 0))
            def _():
                for start, _ in exchanges:
                    start()

        @pl.when(i == 0)
        def _():
            dkn_ref[...] = jnp.zeros_like(dkn_ref)
            dv_ref[...] = jnp.zeros_like(dv_ref)

        @pl.when(jnp.logical_and(h == 0, i == 0))
        def _():
            dkr_ref[...] = jnp.zeros_like(dkr_ref)

        q = jnp.concatenate([qn_ref[...], qr_ref[...]], axis=1).astype(BF16)
        do = do_ref[...]
        do16 = do.astype(BF16)
        delta = jnp.sum(do * o_ref[...], axis=1, keepdims=True)
        lse = lse_ref[...]

        def issue(j):
            ks, k = _att_keys(kn_ref, kr_ref, j, blk)
            s = lax.dot_general(q, k, NT_DIMS, preferred_element_type=F32)
            dp = lax.dot_general(do16, v_ref[ks, :].astype(BF16), NT_DIMS, preferred_element_type=F32)
            return s, dp

        def retire(p, ds, j, dq):
            ks, k = _att_keys(kn_ref, kr_ref, j, blk)
            dv_ref[ks, :] += lax.dot_general(p, do16, TN_DIMS, preferred_element_type=F32)
            dk = lax.dot_general(ds, q, TN_DIMS, preferred_element_type=F32)
            dkn_ref[ks, :] += dk[:, :LANES]
            dkr_ref[ks, :] += dk[:, LANES:]
            return dq + lax.dot_general(ds, k, NN_DIMS, preferred_element_type=F32)

        def probs(s, dp, masked):
            s = s * ATT_SCALE
            if masked:
                s = jnp.where(_diagonal_mask(blk), s, -1e30)
            p = jnp.exp(s - lse)
            return p.astype(BF16), (p * (dp - delta) * ATT_SCALE).astype(BF16)

        def step(j, carry):
            s, dp, p_prev, ds_prev, dq = carry
            s_next, dp_next = issue(j + 1)
            dq = retire(p_prev, ds_prev, jnp.maximum(j - 1, 0), dq)
            p, ds = probs(s, dp, False)
            return s_next, dp_next, p, ds, dq

        none = jnp.zeros((blk, blk), BF16)
        s, dp, p_prev, ds_prev, dq = lax.fori_loop(0, i, step,
                                                   issue(0) + (none, none, jnp.zeros((blk, 2 * LANES), F32)))
        dq = retire(p_prev, ds_prev, jnp.maximum(i - 1, 0), dq)
        p, ds = probs(s, dp, True)
        dq = retire(p, ds, i, dq)
        dqn_ref[...] = dq[:, :LANES]
        dqr_ref[...] = dq[:, LANES:]
        if nh:
            @pl.when(jnp.logical_and(h == heads - 1, i == t // blk - 1))
            def _():
                for _, finish in exchanges:
                    finish()

    q_spec, k_spec, shared, lse_spec = _att_specs(t, blk)
    big, one = jax.ShapeDtypeStruct((t, width), F32), jax.ShapeDtypeStruct((t, LANES), F32)
    sem_shapes = [pltpu.SemaphoreType.DMA((3,)), pltpu.SemaphoreType.DMA((3,))] * nh
    return pl.pallas_call(
        body, grid=(heads, t // blk),
        in_specs=[q_spec, q_spec, k_spec, shared, k_spec, q_spec, lse_spec, q_spec] + [HBM_SPEC] * nh,
        out_specs=[q_spec, q_spec, k_spec, shared, k_spec] + [HBM_SPEC] * nh,
        out_shape=[big, big, big, one, big] + [jax.ShapeDtypeStruct((3,) + p.shape[1:], p.dtype) for p in hosted],
        scratch_shapes=sem_shapes, compiler_params=_params(("arbitrary", "arbitrary")), name=name)(
        qn, qr, kn, kr, v, o, lse, do, *hosted)


def mla_attention(name, qn, qr, kn, kr, v, hosted=(), carriers=()):
    @jax.custom_vjp
    def run(qn, qr, kn, kr, v, hosted, carriers):
        o, _, *gathered = _att_fwd(name, qn, qr, kn, kr, v, hosted)
        return o, tuple(gathered), tuple(jnp.zeros((4,) + c.shape[1:], c.dtype) for c in carriers)

    def fwd(qn, qr, kn, kr, v, hosted, carriers):
        o, lse, *gathered = _att_fwd(name, qn, qr, kn, kr, v, hosted)
        handles = tuple(jnp.zeros((4,) + c.shape[1:], c.dtype) for c in carriers)
        return (o, tuple(gathered), handles), (qn, qr, kn, kr, v, o, lse)

    def bwd(r, g):
        do, _, sent = g
        dqn, dqr, dkn, dkr, dv, *got = _att_bwd(name + "_bwd", *r, do, tuple(sent))
        return dqn, dqr, dkn, dkr, dv, (None,) * len(hosted), tuple(got)

    run.defvjp(fwd, bwd)
    return run(qn, qr, kn, kr, v, tuple(hosted), tuple(carriers))


CONV_ROWS = 64
CONV_PAD = 32


def _dwconv_fwd(name, x, w, b):
    t, c = x.shape
    kw = w.shape[0]
    rows = min(CONV_ROWS, t)
    assert kw - 1 <= CONV_PAD

    def body(x_ref, w_ref, b_ref, y_ref, buf):
        buf[0:CONV_PAD, :] = jnp.zeros((CONV_PAD, LANES), F32)
        buf[CONV_PAD:CONV_PAD + t, :] = x_ref[...]
        taps = [w_ref[k:k + 1, :] for k in range(kw)]
        bias = b_ref[...]
        for r0 in range(0, t, rows):
            acc = jnp.zeros((rows, LANES), F32) + bias
            for k in range(kw):
                off = r0 + CONV_PAD - (kw - 1 - k)
                acc = acc + taps[k] * buf[off:off + rows, :]
            y_ref[r0:r0 + rows, :] = acc

    col = lambda i: (0, i)
    return pl.pallas_call(
        body, grid=(c // LANES,),
        in_specs=[pl.BlockSpec((t, LANES), col), pl.BlockSpec((kw, LANES), col), pl.BlockSpec((1, LANES), col)],
        out_specs=pl.BlockSpec((t, LANES), col), out_shape=jax.ShapeDtypeStruct((t, c), F32),
        scratch_shapes=[pltpu.VMEM((t + CONV_PAD, LANES), F32)],
        compiler_params=_params(("arbitrary",)), name=name)(x, w, b)


def _dwconv_bwd(name, x, w, dy):
    t, c = x.shape
    kw = w.shape[0]
    rows = min(CONV_ROWS, t)
    assert kw - 1 <= CONV_PAD

    def fold(v):
        return jnp.sum(v.reshape(rows // 8, 8, LANES), axis=0)

    def body(x_ref, w_ref, dy_ref, dx_ref, dw_ref, db_ref, xbuf, gbuf):
        xbuf[0:CONV_PAD, :] = jnp.zeros((CONV_PAD, LANES), F32)
        xbuf[CONV_PAD:CONV_PAD + t, :] = x_ref[...]
        gbuf[t:t + CONV_PAD, :] = jnp.zeros((CONV_PAD, LANES), F32)
        gbuf[0:t, :] = dy_ref[...]
        taps = [w_ref[k:k + 1, :] for k in range(kw)]
        dw_part = [jnp.zeros((8, LANES), F32) for _ in range(kw)]
        db_part = jnp.zeros((8, LANES), F32)
        for r0 in range(0, t, rows):
            g = gbuf[r0:r0 + rows, :]
            dx = jnp.zeros((rows, LANES), F32)
            for k in range(kw):
                s = kw - 1 - k
                dx = dx + taps[k] * gbuf[r0 + s:r0 + s + rows, :]
                dw_part[k] = dw_part[k] + fold(g * xbuf[r0 + CONV_PAD - s:r0 + CONV_PAD - s + rows, :])
            dx_ref[r0:r0 + rows, :] = dx
            db_part = db_part + fold(g)
        for k in range(kw):
            dw_ref[k:k + 1, :] = jnp.sum(dw_part[k], axis=0, keepdims=True)
        db_ref[...] = jnp.sum(db_part, axis=0, keepdims=True)

    col = lambda i: (0, i)
    big, wsp, bsp = pl.BlockSpec((t, LANES), col), pl.BlockSpec((kw, LANES), col), pl.BlockSpec((1, LANES), col)
    pad = pltpu.VMEM((t + CONV_PAD, LANES), F32)
    return pl.pallas_call(
        body, grid=(c // LANES,), in_specs=[big, wsp, big], out_specs=[big, wsp, bsp],
        out_shape=[jax.ShapeDtypeStruct((t, c), F32), jax.ShapeDtypeStruct((kw, c), F32),
                   jax.ShapeDtypeStruct((1, c), F32)],
        scratch_shapes=[pad, pad], compiler_params=_params(("arbitrary",)), name=name)(x, w, dy)


def dwconv(name, x, w, b):
    @jax.custom_vjp
    def run(x, w, b):
        return _dwconv_fwd(name, x, w, b)

    def fwd(x, w, b):
        return run(x, w, b), (x, w)

    def bwd(r, g):
        return tuple(_dwconv_bwd(name + "_bwd", r[0], r[1], g))

    run.defvjp(fwd, bwd)
    return run(x, w, b)


def _ssd_tile(xc, dtr, dtb, alog, dsk, prev):
    ln = xc.shape[0]
    gw = SSD_INNER // SSD_GROUPS
    ns = SSD_STATE
    xs = jax.nn.silu(xc[:, :SSD_INNER])
    bm = jax.nn.silu(xc[:, SSD_INNER:SSD_INNER + SSD_GROUPS * ns])
    cm = jax.nn.silu(xc[:, SSD_INNER + SSD_GROUPS * ns:])
    dt = jax.nn.softplus(dtr + dtb)
    a = dt * (-jnp.exp(alog))
    expand = (_iota((LANES, SSD_INNER), 0) == _iota((LANES, SSD_INNER), 1) // SSD_HEAD_DIM).astype(F32)
    causal = _iota((ln, ln), 0) >= _iota((ln, ln), 1)
    acs_h = hdot(causal.astype(F32), a)
    acs_c = hdot(acs_h, expand)
    dt_c = hdot(dt, expand)

    def row_per_column(v):
        return jnp.mean(hdot(jnp.broadcast_to(v, (8, LANES)), expand), axis=0, keepdims=True)

    aend_c = row_per_column(jnp.sum(a, axis=0, keepdims=True))
    xdt = xs * dt_c
    to_end = xdt * jnp.exp(aend_c - acs_c)
    from_start = jnp.exp(acs_c)
    acs_ht = acs_h.T
    lane_h, sub_h = _iota((1, LANES), 1), _iota((LANES, 1), 0)
    head_of_col = _iota((1, gw), 1) // SSD_HEAD_DIM
    ys, states = [], []
    for g in range(SSD_GROUPS):
        cg = cm[:, g * ns:(g + 1) * ns]
        bg = bm[:, g * ns:(g + 1) * ns]
        cols = slice(g * gw, (g + 1) * gw)
        y = bdot(cg, prev[:, cols]) * from_start[:, cols]
        states.append(bdot_tn(bg, to_end[:, cols]))
        cb = bdot_nt(cg, bg)
        for r in range(gw // SSD_HEAD_DIM):
            h = g * (gw // SSD_HEAD_DIM) + r
            col = jnp.sum(jnp.where(lane_h == h, acs_h, 0.0), axis=1, keepdims=True)
            row = jnp.sum(jnp.where(sub_h == h, acs_ht, 0.0), axis=0, keepdims=True)
            decay = jnp.exp(jnp.where(causal, col - row, -1e30))
            y = y + jnp.where(head_of_col == r, bdot(cb * decay, xdt[:, cols]), 0.0)
        ys.append(y)
    y = jnp.concatenate(ys, axis=1) + row_per_column(dsk) * xs
    new = prev * jnp.exp(aend_c) + jnp.concatenate(states, axis=1)
    return y, new


def _ssd_fwd(name, xc, dtr, dtb, alog, dsk):
    t = xc.shape[0]
    ln = min(SSD_TILE, t)
    nt = t // ln

    def body(xc_ref, dtr_ref, dtb_ref, alog_ref, dsk_ref, y_ref, prev_ref, carry):
        @pl.when(pl.program_id(0) == 0)
        def _():
            carry[...] = jnp.zeros_like(carry)

        prev = carry[...]
        prev_ref[...] = prev
        y, new = _ssd_tile(xc_ref[...], dtr_ref[...], dtb_ref[...], alog_ref[...], dsk_ref[...], prev)
        y_ref[...] = y
        carry[...] = new

    row = lambda i: (i, 0)
    par = pl.BlockSpec((1, LANES), lambda i: (0, 0))
    return pl.pallas_call(
        body, grid=(nt,),
        in_specs=[pl.BlockSpec((ln, xc.shape[1]), row), pl.BlockSpec((ln, LANES), row), par, par, par],
        out_specs=[pl.BlockSpec((ln, SSD_INNER), row), pl.BlockSpec((None, SSD_STATE, SSD_INNER), lambda i: (i, 0, 0))],
        out_shape=[jax.ShapeDtypeStruct((t, SSD_INNER), F32), jax.ShapeDtypeStruct((nt, SSD_STATE, SSD_INNER), F32)],
        scratch_shapes=[pltpu.VMEM((SSD_STATE, SSD_INNER), F32)],
        compiler_params=_params(("arbitrary",)), name=name)(xc, dtr, dtb, alog, dsk)


def _ssd_bwd(name, xc, dtr, dtb, alog, dsk, prevs, dy):
    t = xc.shape[0]
    ln = min(SSD_TILE, t)
    nt = t // ln

    def body(xc_ref, dtr_ref, dtb_ref, alog_ref, dsk_ref, prev_ref, dy_ref, dxc_ref, ddtr_ref, ddtb_ref, dalog_ref,
             ddsk_ref, dcarry):
        i = pl.program_id(0)

        @pl.when(i == 0)
        def _():
            dcarry[...] = jnp.zeros_like(dcarry)

        _, vjp = jax.vjp(_ssd_tile, xc_ref[...], dtr_ref[...], dtb_ref[...], alog_ref[...], dsk_ref[...], prev_ref[...])
        dxc, ddtr, ddtb, dalog, ddsk, dprev = vjp((dy_ref[...], dcarry[...]))
        dxc_ref[...] = dxc
        ddtr_ref[...] = ddtr
        dcarry[...] = dprev
        for r, gr in ((ddtb_ref, ddtb), (dalog_ref, dalog), (ddsk_ref, ddsk)):
            @pl.when(i == 0)
            def _(r=r, gr=gr):
                r[...] = gr

            @pl.when(i != 0)
            def _(r=r, gr=gr):
                r[...] += gr

    row = lambda i: (nt - 1 - i, 0)
    par = pl.BlockSpec((1, LANES), lambda i: (0, 0))
    big, dts = pl.BlockSpec((ln, xc.shape[1]), row), pl.BlockSpec((ln, LANES), row)
    par_shape = jax.ShapeDtypeStruct((1, LANES), F32)
    return pl.pallas_call(
        body, grid=(nt,),
        in_specs=[big, dts, par, par, par, pl.BlockSpec((None, SSD_STATE, SSD_INNER), lambda i: (nt - 1 - i, 0, 0)),
                  pl.BlockSpec((ln, SSD_INNER), row)],
        out_specs=[big, dts, par, par, par],
        out_shape=[jax.ShapeDtypeStruct(xc.shape, F32), jax.ShapeDtypeStruct(dtr.shape, F32), par_shape, par_shape,
                   par_shape],
        scratch_shapes=[pltpu.VMEM((SSD_STATE, SSD_INNER), F32)],
        compiler_params=_params(("arbitrary",)), name=name)(xc, dtr, dtb, alog, dsk, prevs, dy)


def ssd_scan(name, xc, dtr, dtb, alog, dsk):
    @jax.custom_vjp
    def run(xc, dtr, dtb, alog, dsk):
        return _ssd_fwd(name, xc, dtr, dtb, alog, dsk)[0]

    def fwd(xc, dtr, dtb, alog, dsk):
        y, prevs = _ssd_fwd(name, xc, dtr, dtb, alog, dsk)
        return y, (xc, dtr, dtb, alog, dsk, prevs)

    def bwd(r, g):
        return tuple(_ssd_bwd(name + "_bwd", *r, g))

    run.defvjp(fwd, bwd)
    return run(xc, dtr, dtb, alog, dsk)


def loss_head(y, target):
    t, n = y.shape
    tile = min(ROW_TILE, t)

    def body(y_ref, t_ref, dy_ref, acc_ref):
        d = y_ref[...] - t_ref[...]
        dy_ref[...] = d * (1.0 / n)

        @pl.when(pl.program_id(0) == 0)
        def _():
            acc_ref[...] = jnp.zeros_like(acc_ref)

        acc_ref[...] += jnp.sum(d * d, axis=0, keepdims=True)

    row = pl.BlockSpec((tile, n), lambda i: (i, 0))
    dy, acc = pl.pallas_call(
        body, grid=(t // tile,), in_specs=[row, row], out_specs=[row, pl.BlockSpec((1, n), lambda i: (0, 0))],
        out_shape=[jax.ShapeDtypeStruct((t, n), F32), jax.ShapeDtypeStruct((1, n), F32)],
        compiler_params=_params(("arbitrary",)), name="loss_head")(y, target)
    return acc, dy


def adamw(name, w, g, m, v):
    shape = w.shape
    cols = shape[-1]
    rows = w.size // cols
    tile = _divisor(rows, 512, 8) if rows % 8 == 0 else rows

    def body(w_ref, g_ref, m_ref, v_ref, d_ref, nm_ref, nv_ref):
        g = g_ref[...]
        m = ADAM_B1 * m_ref[...] + (1.0 - ADAM_B1) * g
        v = ADAM_B2 * v_ref[...] + (1.0 - ADAM_B2) * jnp.square(g)
        m_hat = m / (1.0 - ADAM_B1 ** ADAM_STEP)
        v_hat = v / (1.0 - ADAM_B2 ** ADAM_STEP)
        d_ref[...] = -ADAM_LR * (m_hat / (jnp.sqrt(v_hat) + ADAM_EPS) + ADAM_WD * w_ref[...])
        nm_ref[...] = m
        nv_ref[...] = v

    spec = pl.BlockSpec((tile, cols), lambda i: (i, 0))
    two_d = jax.ShapeDtypeStruct((rows, cols), F32)
    outs = pl.pallas_call(body, grid=(rows // tile,), in_specs=[spec] * 4, out_specs=[spec] * 3, out_shape=[two_d] * 3,
                          compiler_params=_params(("arbitrary",)), name=name)(
        *[a.reshape(rows, cols) for a in (w, g, m, v)])
    return [o.reshape(shape) for o in outs]


HBM_SPEC = pl.BlockSpec(memory_space=pl.ANY)


def _position():
    return lax.axis_index("x"), lax.axis_index("y"), lax.axis_index("c")


def _gather_copies(x_ref, out_ref, send_sems, recv_sems, local_sem):
    x, y, c = _position()
    me, sibling = (x, y, c), (x, y, 1 - c)
    chips = [(1 - x, y), (x, 1 - y), (1 - x, 1 - y)]

    def block(px, py, pc):
        return out_ref.at[4 * px + 2 * py + pc]

    def copy(k, blk, to, src=None):
        return pltpu.make_async_remote_copy(
            src_ref=block(*blk) if src is None else src, dst_ref=block(*blk), send_sem=send_sems.at[k],
            recv_sem=recv_sems.at[k], device_id=to, device_id_type=MESH_ID)

    mine = pltpu.make_async_copy(x_ref, block(*me), local_sem)
    first = [copy(0, me, sibling, src=x_ref)]
    first += [copy(1 + j, me, (*chip, c), src=x_ref) for j, chip in enumerate(chips)]
    passed = [copy(4 + j, (*chip, c), sibling) for j, chip in enumerate(chips)]

    def start():
        mine.start()
        for cp in first:
            cp.start()

    def pass_on():
        for j, chip in enumerate(chips):
            copy(1 + j, (*chip, c), me).wait_recv()
            passed[j].start()

    def finish():
        copy(0, sibling, me).wait_recv()
        for j, chip in enumerate(chips):
            copy(4 + j, (*chip, 1 - c), me).wait_recv()
        for cp in first + passed:
            cp.wait_send()
        mine.wait()

    return start, pass_on, finish


def all_gather(name, shard):
    def body(x_ref, out_ref, send_sems, recv_sems, local_sem):
        start, pass_on, finish = _gather_copies(x_ref, out_ref, send_sems, recv_sems, local_sem)
        start()
        pass_on()
        finish()

    return pl.pallas_call(
        body, in_specs=[HBM_SPEC], out_specs=HBM_SPEC,
        out_shape=jax.ShapeDtypeStruct((N_DEV,) + shard.shape, shard.dtype),
        scratch_shapes=[pltpu.SemaphoreType.DMA((7,)), pltpu.SemaphoreType.DMA((7,)), pltpu.SemaphoreType.DMA],
        name=name)(shard)


def pair_exchange(name, g):
    def body(g_ref, out_ref, send_sem, recv_sem):
        x, y, c = _position()
        cp = pltpu.make_async_remote_copy(src_ref=g_ref.at[1 - c], dst_ref=out_ref, send_sem=send_sem,
                                          recv_sem=recv_sem, device_id=(x, y, 1 - c), device_id_type=MESH_ID)
        cp.start()
        cp.wait()

    return pl.pallas_call(
        body, in_specs=[HBM_SPEC], out_specs=HBM_SPEC, out_shape=jax.ShapeDtypeStruct(g.shape[1:], g.dtype),
        scratch_shapes=[pltpu.SemaphoreType.DMA, pltpu.SemaphoreType.DMA], name=name)(g)


def _chip_copies(p_ref, out_ref, send_sems, recv_sems):
    x, y, c = _position()
    copies = [pltpu.make_async_remote_copy(
        src_ref=p_ref.at[2 * px + py], dst_ref=out_ref.at[j], send_sem=send_sems.at[j], recv_sem=recv_sems.at[j],
        device_id=(px, py, c), device_id_type=MESH_ID) for j, (px, py) in enumerate([(1 - x, y), (x, 1 - y), (1 - x, 1 - y)])]

    def start():
        for cp in copies:
            cp.start()

    def finish():
        for cp in copies:
            cp.wait()

    return start, finish


def chip_exchange(name, p):
    def body(p_ref, out_ref, send_sems, recv_sems):
        start, finish = _chip_copies(p_ref, out_ref, send_sems, recv_sems)
        start()
        finish()

    return pl.pallas_call(
        body, in_specs=[HBM_SPEC], out_specs=HBM_SPEC, out_shape=jax.ShapeDtypeStruct((3,) + p.shape[1:], p.dtype),
        scratch_shapes=[pltpu.SemaphoreType.DMA((3,)), pltpu.SemaphoreType.DMA((3,))], name=name)(p)


def _sum_tile(r):
    return _divisor(r, 512, 16) if r % 16 == 0 else r


def pair_reduce(name, g, got, my_c, my_chip, wire):
    _, nchip, r, c_ = g.shape
    tile = _sum_tile(r)

    def body(ids, g_ref, got_ref, p_ref, mine_ref):
        s = g_ref[...].astype(F32) + got_ref[...].astype(F32)
        p_ref[...] = s.astype(wire)

        @pl.when(pl.program_id(1) == ids[1])
        def _():
            mine_ref[...] = s

    return pl.pallas_call(
        body,
        grid_spec=pltpu.PrefetchScalarGridSpec(
            num_scalar_prefetch=1, grid=(r // tile, nchip),
            in_specs=[pl.BlockSpec((None, None, tile, c_), lambda i, k, ids: (ids[0], k, i, 0)),
                      pl.BlockSpec((None, tile, c_), lambda i, k, ids: (k, i, 0))],
            out_specs=[pl.BlockSpec((None, tile, c_), lambda i, k, ids: (k, i, 0)),
                       pl.BlockSpec((tile, c_), lambda i, k, ids: (i, 0))]),
        out_shape=[jax.ShapeDtypeStruct((nchip, r, c_), wire), jax.ShapeDtypeStruct((r, c_), F32)],
        compiler_params=_params(("arbitrary", "arbitrary")), name=name)(
        jnp.stack([my_c, my_chip]).astype(jnp.int32), g, got)


def chip_reduce(name, mine, got):
    r, c_ = mine.shape
    tile = _sum_tile(r)

    def body(m_ref, got_ref, o_ref):
        o_ref[...] = ((m_ref[...] + got_ref[0].astype(F32)) + got_ref[1].astype(F32)) + got_ref[2].astype(F32)

    return pl.pallas_call(
        body, grid=(r // tile,),
        in_specs=[pl.BlockSpec((tile, c_), lambda i: (i, 0)), pl.BlockSpec((3, tile, c_), lambda i: (0, i, 0))],
        out_specs=pl.BlockSpec((tile, c_), lambda i: (i, 0)), out_shape=jax.ShapeDtypeStruct((r, c_), F32),
        compiler_params=_params(("arbitrary",)), name=name)(mine, got)


def sum_blocks(name, a):
    n, r, c_ = a.shape
    tile = _sum_tile(r)

    def body(a_ref, o_ref):
        acc = a_ref[0]
        for k in range(1, n):
            acc = acc + a_ref[k]
        o_ref[...] = acc

    return pl.pallas_call(
        body, grid=(r // tile,), in_specs=[pl.BlockSpec((n, tile, c_), lambda i: (0, i, 0))],
        out_specs=pl.BlockSpec((tile, c_), lambda i: (i, 0)), out_shape=jax.ShapeDtypeStruct((r, c_), F32),
        compiler_params=_params(("arbitrary",)), name=name)(a)


GROUPS = [
    ("rows1024", ["ssd_w_out", "conv_w_out", "mla_w_o", "w_out", "xattn_w_q", "xattn_w_o", "ffn_w_out"], BF16, 1024),
    ("w_in", ["w_in"], BF16, 1114),
    ("ffn_w_in", ["ffn_w_in"], BF16, 704),
    ("cols256", ["xattn_w_kv", "mla_w_kv_b"], BF16, 256),
    ("w_q_b", ["mla_w_q_b"], BF16, 192),
    ("small", ["ssd_conv_w", "conv_dw_w", "gate_b"], F32, 128),
]
HOSTED, UPFRONT = GROUPS[:3], GROUPS[3:]
PACK_COLS = 1024


def _pack(arrays):
    flat = jnp.concatenate([a.reshape(-1) for a in arrays])
    rows = -(-flat.shape[0] // PACK_COLS)
    rows += -rows % 8
    return jnp.pad(flat, (0, rows * PACK_COLS - flat.shape[0])).reshape(rows, PACK_COLS)


def _unpack(buf, shapes):
    flat = buf.reshape(-1)
    out, off = [], 0
    for s in shapes:
        n = 1
        for d in s:
            n *= d
        out.append(flat[off:off + n].reshape(tuple(s)))
        off += n
    return out


def _stack_rows(arrays, width, lead):
    return jnp.concatenate([a.reshape(a.shape[:lead] + (-1, width)) for a in arrays], axis=lead)


def _unstack_rows(buf, shapes, width, lead):
    out, off = [], 0
    for s in shapes:
        n = 1
        for d in s:
            n *= d
        rows = n // width
        idx = (slice(None),) * lead + (slice(off, off + rows),)
        out.append(buf[idx].reshape(buf.shape[:lead] + tuple(s)))
        off += rows
    return out


def _join_shards(blocks, axis):
    ax = axis + 1
    moved = jnp.moveaxis(blocks, 0, ax)
    s = moved.shape
    return moved.reshape(s[:ax] + (s[ax] * s[ax + 1],) + s[ax + 2:])


def _split_by_owner(full, axis):
    ax = axis + 1
    s = full.shape
    cut = full.reshape(s[:ax] + (2, 2, 2, s[ax] // N_DEV) + s[ax + 1:])
    cut = jnp.moveaxis(cut, (ax + 2, ax, ax + 1), (0, 1, 2))
    return cut.reshape((2, 4) + cut.shape[3:])


def _blocks_by_owner(blocks):
    cut = blocks.reshape((2, 2, 2) + blocks.shape[1:])
    return jnp.moveaxis(cut, 2, 0).reshape((2, 4) + blocks.shape[1:])


def _pad_cols(w, n):
    return jnp.pad(w, ((0, 0), (0, n - w.shape[1])))


def _regroup_cols(srcs, widths):
    starts = [0]
    for s in srcs:
        starts.append(starts[-1] + s.shape[1])
    assert starts[-1] == sum(widths)
    out, lo = [], 0
    for wd in widths:
        hi = lo + wd
        parts = []
        for s, a, b in zip(srcs, starts[:-1], starts[1:]):
            u, v = max(lo, a), min(hi, b)
            if u < v:
                parts.append(s[:, u - a:v - a])
        out.append(parts[0] if len(parts) == 1 else jnp.concatenate(parts, axis=1))
        lo = hi
    return out


COL_BLOCKED = ("w_in", "ffn_w_in", "xattn_w_kv", "mla_w_kv_b", "mla_w_q_b")
W_IN_PIECES = (1024, 2048, 16, 2048, 384, MLA_KV_RANK, MLA_ROPE, 3072)


def _prep_layer(w):
    w_z, w_xbc, w_dt, w_glu, w_q, w_ckv, w_kr, w_gate = _regroup_cols(list(w["w_in"]), W_IN_PIECES)
    w_ffn_gate, w_ffn_up = _regroup_cols(list(w["ffn_w_in"]), (FFN_HIDDEN, FFN_HIDDEN))
    w_xk, w_xv = _regroup_cols(list(w["xattn_w_kv"]), (D_MODEL, D_MODEL))
    q, kv = w["mla_w_q_b"], w["mla_w_kv_b"]

    def row(v):
        return v.reshape(1, -1)

    def norm_pair(g):
        return _pad_cols(row(g), 2 * LANES)

    return {
        "mix_norm_g": row(w["mix_norm_g"]),
        "w_z": w_z, "w_xbc": w_xbc, "w_dt": _pad_cols(w_dt, LANES), "w_glu": w_glu, "w_q": w_q, "w_ckv": w_ckv,
        "w_kr": _pad_cols(w_kr, LANES), "w_gate": w_gate,
        "ssd_conv_w": w["ssd_conv_w"], "ssd_conv_b": row(w["ssd_conv_b"]),
        "ssd_dt_bias": _pad_cols(row(w["ssd_dt_bias"]), LANES), "ssd_a_log": _pad_cols(row(w["ssd_a_log"]), LANES),
        "ssd_d": _pad_cols(row(w["ssd_d"]), LANES), "ssd_norm_g": row(w["ssd_norm_g"]), "ssd_w_out": w["ssd_w_out"],
        "conv_dw_w": w["conv_dw_w"], "conv_dw_b": row(w["conv_dw_b"]), "conv_ln_g": row(w["conv_ln_g"]),
        "conv_ln_b": row(w["conv_ln_b"]), "conv_w_out": w["conv_w_out"],
        "mla_q_a_g": row(w["mla_q_a_g"]), "mla_kv_a_g": row(w["mla_kv_a_g"]),
        "w_qn": jnp.concatenate([q[h, :, :MLA_NOPE] for h in range(MLA_HEADS)], axis=1),
        "w_qr": jnp.concatenate([_pad_cols(q[h, :, MLA_NOPE:], LANES) for h in range(MLA_HEADS)], axis=1),
        "w_kn": jnp.concatenate([kv[h, :, :MLA_NOPE] for h in range(MLA_HEADS)], axis=1),
        "w_v": jnp.concatenate([kv[h, :, MLA_NOPE:] for h in range(MLA_HEADS)], axis=1),
        "mla_q_norm_g": norm_pair(w["mla_q_norm_g"]), "mla_k_norm_g": norm_pair(w["mla_k_norm_g"]),
        "mla_w_o": w["mla_w_o"], "gate_b": row(w["gate_b"]), "w_out": w["w_out"],
        "xattn_norm_g": row(w["xattn_norm_g"]), "mem_norm_g": row(w["mem_norm_g"]), "xattn_w_q": w["xattn_w_q"],
        "w_xk": w_xk, "w_xv": w_xv,
        "xattn_q_norm_g": row(w["xattn_q_norm_g"]), "xattn_k_norm_g": row(w["xattn_k_norm_g"]),
        "xattn_w_o": w["xattn_w_o"], "ffn_norm_g": row(w["ffn_norm_g"]),
        "w_ffn_gate": w_ffn_gate, "w_ffn_up": w_ffn_up, "ffn_w_out": w["ffn_w_out"],
    }


def _unprep_grads(g):
    n_dt, n_kr = IN_SIZES[2], MLA_ROPE
    flat = lambda v: v.reshape(-1)

    def blocks(srcs):
        total = sum(s.shape[1] for s in srcs)
        return jnp.stack(_regroup_cols(srcs, (total // N_DEV,) * N_DEV))

    def head(a, h, n=LANES):
        return a[:, h * LANES:h * LANES + n]

    return {
        "mix_norm_g": flat(g["mix_norm_g"]),
        "w_in": blocks([g["w_z"], g["w_xbc"], g["w_dt"][:, :n_dt], g["w_glu"], g["w_q"], g["w_ckv"],
                        g["w_kr"][:, :n_kr], g["w_gate"]]),
        "ssd_conv_w": g["ssd_conv_w"], "ssd_conv_b": flat(g["ssd_conv_b"]),
        "ssd_dt_bias": flat(g["ssd_dt_bias"])[:SSD_HEADS], "ssd_a_log": flat(g["ssd_a_log"])[:SSD_HEADS],
        "ssd_d": flat(g["ssd_d"])[:SSD_HEADS], "ssd_norm_g": flat(g["ssd_norm_g"]), "ssd_w_out": g["ssd_w_out"],
        "conv_dw_w": g["conv_dw_w"], "conv_dw_b": flat(g["conv_dw_b"]), "conv_ln_g": flat(g["conv_ln_g"]),
        "conv_ln_b": flat(g["conv_ln_b"]), "conv_w_out": g["conv_w_out"],
        "mla_q_a_g": flat(g["mla_q_a_g"]),
        "mla_w_q_b": jnp.stack([jnp.concatenate([head(g["w_qn"], h), head(g["w_qr"], h, MLA_ROPE)], axis=1)
                                for h in range(MLA_HEADS)]),
        "mla_kv_a_g": flat(g["mla_kv_a_g"]),
        "mla_w_kv_b": jnp.stack([jnp.concatenate([head(g["w_kn"], h), head(g["w_v"], h)], axis=1)
                                 for h in range(MLA_HEADS)]),
        "mla_q_norm_g": flat(g["mla_q_norm_g"])[:MLA_NOPE + MLA_ROPE],
        "mla_k_norm_g": flat(g["mla_k_norm_g"])[:MLA_NOPE + MLA_ROPE],
        "mla_w_o": g["mla_w_o"], "gate_b": g["gate_b"].reshape(3, D_MODEL), "w_out": g["w_out"],
        "xattn_norm_g": flat(g["xattn_norm_g"]), "mem_norm_g": flat(g["mem_norm_g"]), "xattn_w_q": g["xattn_w_q"],
        "xattn_w_kv": blocks([g["w_xk"], g["w_xv"]]),
        "xattn_q_norm_g": flat(g["xattn_q_norm_g"]), "xattn_k_norm_g": flat(g["xattn_k_norm_g"]),
        "xattn_w_o": g["xattn_w_o"], "ffn_norm_g": flat(g["ffn_norm_g"]),
        "ffn_w_in": blocks([g["w_ffn_gate"], g["w_ffn_up"]]), "ffn_w_out": g["ffn_w_out"],
    }


def _layer(l, x, mem, cosf, sinf, w, hosted=(), carriers=()):
    t = x.shape[0]
    n = lambda s: f"l{l}_{s}"
    tile = min(ROW_TILE, t)
    grid = (1, t // tile)

    def rowwise(name, f, ins, width, to_matmul=False):
        return tmap(n(name), f, grid, ins, [_row_out(t, width, tile)], narrow=(0,) if to_matmul else ())[0]

    u, x = rms_norm_through(n("mix_norm"), x, w["mix_norm_g"])
    in_keys = ["z", "xbc", "dt", "glu", "q", "ckv", "kr", "gate"]
    z, xbc, dtr, glu, q_lat, c_kv, kr_raw, gate_logits = multi_matmul(
        n("in"), u, [w["w_" + k] for k in in_keys], in_keys, to_tmap=("z", "glu", "q", "ckv", "kr", "gate"))

    xc = dwconv(n("ssd_conv"), xbc, w["ssd_conv_w"], w["ssd_conv_b"])
    y_scan = ssd_scan(n("ssd_scan"), xc, dtr, w["ssd_dt_bias"], w["ssd_a_log"], w["ssd_d"])
    y_norm = rowwise("ssd_gate_norm", _ssd_gate_norm_f, [_rows(y_scan, tile), _rows(z, tile), _whole(w["ssd_norm_g"])],
                     SSD_INNER, to_matmul=True)
    y_ssd = matmul(n("ssd_out"), y_norm, w["ssd_w_out"], to_tmap=True)

    v = rowwise("glu", _glu_f, [_rows(glu, tile)], D_MODEL)
    v = dwconv(n("conv_dw"), v, w["conv_dw_w"], w["conv_dw_b"])
    v = rowwise("conv_ln_silu", _ln_silu_f, [_rows(v, tile), _whole(w["conv_ln_g"]), _whole(w["conv_ln_b"])], D_MODEL,
                to_matmul=True)
    y_conv = matmul(n("conv_out"), v, w["conv_w_out"], to_tmap=True)

    q_n = rms_norm(n("q_a_norm"), q_lat, w["mla_q_a_g"])
    qn_raw, qr_raw = multi_matmul(n("q"), q_n, [w["w_qn"], w["w_qr"]], ["nope", "rope"], to_tmap=("nope", "rope"))
    c_n = rms_norm(n("kv_a_norm"), c_kv, w["mla_kv_a_g"])
    kn_raw, val = multi_matmul(n("kv"), c_n, [w["w_kn"], w["w_v"]], ["nope", "v"], to_tmap=("nope",))
    tables = [_rows(cosf, tile, "n"), _rows(sinf, tile, "n")]
    kn = rowwise("k_nope_norm", _k_nope_f, [_rows(kn_raw, tile), _whole(w["mla_k_norm_g"])], MLA_HEADS * MLA_NOPE)
    kr = rowwise("k_rope", _k_rope_f, [_rows(kr_raw, tile)] + tables + [_whole(w["mla_k_norm_g"])], LANES)
    wide = _row_out(t, MLA_HEADS * LANES, tile)
    qn, qr = tmap(n("q_prep"), _q_prep_f, grid,
                  [_rows(qn_raw, tile), _rows(qr_raw, tile)] + tables + [_whole(w["mla_q_norm_g"])], [wide, wide])
    att, gathered, handles = mla_attention(n("mla_attn"), qn, qr, kn, kr, val, hosted, carriers)
    y_mla = matmul(n("mla_out"), att, w["mla_w_o"], to_tmap=True)

    merged = rowwise("merge", _merge_f, [_rows(gate_logits, tile), _whole(w["gate_b"]), _rows(y_ssd, tile),
                                         _rows(y_conv, tile), _rows(y_mla, tile)], D_MODEL, to_matmul=True)
    x = matmul(n("mix_out"), merged, w["w_out"], res=x)

    h, x = rms_norm_through(n("xattn_norm"), x, w["xattn_norm_g"])
    mem_n = rms_norm_nograd_x(n("mem_norm"), mem, w["mem_norm_g"])
    xq = matmul(n("xattn_q"), h, w["xattn_w_q"], to_tmap=True)
    xk, xv = multi_matmul(n("xattn_kv"), mem_n, [w["w_xk"], w["w_xv"]], ["k", "v"])
    m = mem.shape[0]
    txq = min(XATT_Q_TILE, t)
    kv_head = lambda arr: (arr, (m, X_HEAD_DIM), lambda o, i: (0, o), "ai")
    xo = tmap(n("xattn"), _xattn_f, (X_HEADS, t // txq),
              [(xq, (txq, X_HEAD_DIM), lambda o, i: (i, o), "t"), kv_head(xk), kv_head(xv),
               _whole(w["xattn_q_norm_g"]), _whole(w["xattn_k_norm_g"])],
              [((t, D_MODEL), (txq, X_HEAD_DIM), lambda o, i: (i, o))], narrow=(0,))[0]
    x = matmul(n("xattn_out"), xo, w["xattn_w_o"], res=x)

    h, x = rms_norm_through(n("ffn_norm"), x, w["ffn_norm_g"])
    gate, up = multi_matmul(n("ffn_in"), h, [w["w_ffn_gate"], w["w_ffn_up"]], ["gate", "up"], to_tmap=("gate", "up"))
    act = rowwise("swiglu", _swiglu_f, [_rows(gate, tile), _rows(up, tile)], FFN_HIDDEN, to_matmul=True)
    return matmul(n("ffn_out"), act, w["ffn_w_out"], res=x), gathered, handles


def _rope_tables(positions):
    inv = ROPE_THETA ** (-jnp.arange(0, MLA_ROPE, 2, dtype=F32) / MLA_ROPE)
    ang = positions.astype(F32)[:, None] * inv
    pad = jnp.zeros((positions.shape[0], LANES - MLA_ROPE), F32)
    cos, sin = jnp.cos(ang), jnp.sin(ang)
    return jnp.concatenate([cos, cos, pad], axis=1), jnp.concatenate([sin, sin, pad], axis=1)


def local_step(x, mem, positions, target, weights, gathered0, later, shard_shapes, exchange=None):
    cosf, sinf = _rope_tables(positions)

    def layer_weights(l, big):
        w = {k: v[:, l] if k in COL_BLOCKED else v[l] for k, v in weights.items()}
        for (_, names, _, width), stack in zip(HOSTED, big):
            for n, b in zip(names, _unstack_rows(stack, [shard_shapes[n] for n in names], width, 1)):
                w[n] = b if n in COL_BLOCKED else b.reshape((-1,) + b.shape[2:])
        return w

    diff = [{k: jnp.zeros(v.shape, BF16) if k in MATRICES else v
             for k, v in _prep_layer(layer_weights(l, gathered0)).items()} for l in range(DEPTH)]

    def layer_fn(l, big, hosted):
        def f(x, d, carriers):
            mats = _prep_layer(layer_weights(l, big))
            w = {k: Mat(mats[k], s) if k in MATRICES else s for k, s in d.items()}
            y, gathered, handles = _layer(l, x, mem, cosf, sinf, w, hosted, carriers)
            return (y, handles), gathered
        return f

    pulls, big = [], gathered0
    for l in range(DEPTH):
        inner = l + 1 < DEPTH
        carriers = tuple(jnp.zeros((3,) + s.shape, s.dtype) for s in later[l]) if inner and exchange else ()
        (x, _), pull, big = jax.vjp(layer_fn(l, big, later[l] if inner else ()), x, diff[l], carriers, has_aux=True)
        pulls.append(pull)
    sq, g = loss_head(x, target)

    per_layer, reduced, sent, mine = [None] * DEPTH, [None] * DEPTH, (), None
    for l in reversed(range(DEPTH)):
        g, gd, got = pulls[l]((g, tuple(sent)))
        if mine is not None:
            reduced[l + 1] = exchange.finish(mine, got)
        per_layer[l] = _unprep_grads(gd)
        if exchange:
            sent, mine = exchange.begin(per_layer[l])
    if exchange:
        reduced[0] = exchange.finish(mine, [chip_exchange(f"chip_exchange_first_layer_{k}", p)
                                            for k, p in enumerate(sent)])
    grads = {k: jnp.stack([pl_[k] for pl_ in per_layer], axis=1 if k in COL_BLOCKED else 0) for k in WEIGHTS}
    shards = {k: jnp.stack([r[k] for r in reduced]) for k in reduced[0]} if exchange else None
    return sq, g, grads, shards


def _step(x, mem, positions, loss_target, w, m, v):
    xi, yi, ci = _position()

    full = {n: w[n] for n in REPLICATED}
    for gname, names, wire, width in UPFRONT:
        shapes = [w[n].shape for n in names]
        stacked = _stack_rows([w[n] for n in names], width, 0).astype(wire)
        gathered = all_gather("gather_" + gname, stacked)
        for n, b in zip(names, _unstack_rows(gathered, shapes, width, 1)):
            full[n] = b if n in COL_BLOCKED else _join_shards(b, SHARDED[n])
    shards = [[_stack_rows([w[n][l] for n in names], width, 0).astype(wire) for _, names, wire, width in HOSTED]
              for l in range(DEPTH)]
    gathered0 = [all_gather("gather0_" + g[0], s) for g, s in zip(HOSTED, shards[0])]
    shard_shapes = {n: w[n].shape[1:] for g in HOSTED for n in g[1]}

    def to_sibling(groups, grads):
        sent, mine = [], []
        for gname, names, wire, width in groups:
            by_owner = _stack_rows([_blocks_by_owner(grads[n]) if n in COL_BLOCKED else
                                    _split_by_owner(grads[n], SHARDED[n]) for n in names], width, 2).astype(wire)
            from_sibling = pair_exchange("pair_exchange_" + gname, by_owner)
            p, own = pair_reduce("pair_reduce_" + gname, by_owner, from_sibling, ci, 2 * xi + yi, wire)
            sent.append(p)
            mine.append(own)
        return sent, mine

    def from_chips(groups, shapes, mine, got):
        out = {}
        for (gname, names, _, width), own, arrived in zip(groups, mine, got):
            reduced = chip_reduce("chip_reduce_" + gname, own, arrived)
            out.update(zip(names, _unstack_rows(reduced, [shapes[n] for n in names], width, 0)))
        return out

    class LayerExchange:
        @staticmethod
        def begin(layer_grads):
            return to_sibling(HOSTED, {n: g[:, None] if n in COL_BLOCKED else g[None] for n, g in layer_grads.items()
                                       if n in shard_shapes})

        @staticmethod
        def finish(mine, got):
            return from_chips(HOSTED, shard_shapes, mine, got)

    sq, gx, grads, g_shard = local_step(x[0], mem[0], positions[0], loss_target[0], full, gathered0, shards[1:],
                                        shard_shapes, LayerExchange)
    loss = lax.psum(0.5 * jnp.sum(sq) / D_MODEL, ("x", "y", "c"))

    sent, mine = to_sibling(UPFRONT, grads)
    got = [chip_exchange("chip_exchange_" + g[0], p) for g, p in zip(UPFRONT, sent)]
    g_shard.update(from_chips(UPFRONT, {n: w[n].shape for g in UPFRONT for n in g[1]}, mine, got))

    rep_shapes = [w[n].shape for n in REPLICATED]
    rep_all = all_gather("small_grads_all_gather", _pack([grads[n] for n in REPLICATED]))
    g_rep = dict(zip(REPLICATED, _unpack(sum_blocks("small_grads_sum", rep_all), rep_shapes)))

    out_g, out_d, out_m, out_v = [], [], [], []
    for n in WEIGHTS:
        g = g_shard[n] if n in SHARDED else g_rep[n]
        d, nm, nv = adamw("adamw_" + n, w[n], g, m[n], v[n])
        out_g.append(g)
        out_d.append(d)
        out_m.append(nm)
        out_v.append(nv)
    return (loss, gx[None], *out_g, *out_d, *out_m, *out_v)


def kernel(x, mem, positions, mix_norm_g, w_in, ssd_conv_w, ssd_conv_b, ssd_dt_bias, ssd_a_log, ssd_d, ssd_norm_g, ssd_w_out, conv_dw_w, conv_dw_b, conv_ln_g, conv_ln_b, conv_w_out, mla_q_a_g, mla_w_q_b, mla_kv_a_g, mla_w_kv_b, mla_q_norm_g, mla_k_norm_g, mla_w_o, gate_b, w_out, xattn_norm_g, mem_norm_g, xattn_w_q, xattn_w_kv, xattn_q_norm_g, xattn_k_norm_g, xattn_w_o, ffn_norm_g, ffn_w_in, ffn_w_out, loss_target, m_mix_norm_g, m_w_in, m_ssd_conv_w, m_ssd_conv_b, m_ssd_dt_bias, m_ssd_a_log, m_ssd_d, m_ssd_norm_g, m_ssd_w_out, m_conv_dw_w, m_conv_dw_b, m_conv_ln_g, m_conv_ln_b, m_conv_w_out, m_mla_q_a_g, m_mla_w_q_b, m_mla_kv_a_g, m_mla_w_kv_b, m_mla_q_norm_g, m_mla_k_norm_g, m_mla_w_o, m_gate_b, m_w_out, m_xattn_norm_g, m_mem_norm_g, m_xattn_w_q, m_xattn_w_kv, m_xattn_q_norm_g, m_xattn_k_norm_g, m_xattn_w_o, m_ffn_norm_g, m_ffn_w_in, m_ffn_w_out, v_mix_norm_g, v_w_in, v_ssd_conv_w, v_ssd_conv_b, v_ssd_dt_bias, v_ssd_a_log, v_ssd_d, v_ssd_norm_g, v_ssd_w_out, v_conv_dw_w, v_conv_dw_b, v_conv_ln_g, v_conv_ln_b, v_conv_w_out, v_mla_q_a_g, v_mla_w_q_b, v_mla_kv_a_g, v_mla_w_kv_b, v_mla_q_norm_g, v_mla_k_norm_g, v_mla_w_o, v_gate_b, v_w_out, v_xattn_norm_g, v_mem_norm_g, v_xattn_w_q, v_xattn_w_kv, v_xattn_q_norm_g, v_xattn_k_norm_g, v_xattn_w_o, v_ffn_norm_g, v_ffn_w_in, v_ffn_w_out):
    args = locals()
    w = {n: args[n] for n in WEIGHTS}
    m = {n: args["m_" + n] for n in WEIGHTS}
    v = {n: args["v_" + n] for n in WEIGHTS}
    return _step(x, mem, positions, loss_target, w, m, v)
```
